```python
import jax, jax.numpy as jnp
from jax import lax
import numpy as np

D_MODEL = 2048
BATCH = 8
SEQ = 8192
DEPTH = 2

N_MIXERS = 2
MIX_WIDTH = D_MODEL
FOX_HEADS = 16
FOX_HEAD_DIM = MIX_WIDTH // FOX_HEADS
Q_BLOCK = 128
HGRN_HEADS = 16
HGRN_KEY_DIM = MIX_WIDTH // HGRN_HEADS
HGRN_VAL_DIM = MIX_WIDTH // HGRN_HEADS
HGRN_CHUNK = 64
N_FOX_LAYERS = (DEPTH + 1) // 2
N_HGRN_LAYERS = DEPTH // 2
FOX_IN = 4 * MIX_WIDTH + FOX_HEADS
HGRN_IN = 4 * MIX_WIDTH
EPS = 1e-6

kernel_name = "fox_hgrn2_interleaved_hybrid"


def rms_norm(x, gain):
    xf = x.astype(jnp.float32)
    y = xf * lax.rsqrt(jnp.mean(xf * xf, axis=-1, keepdims=True) + EPS)
    return y.astype(x.dtype) * gain


def split_heads(t, n_heads):
    b, s, _ = t.shape
    return t.reshape(b, s, n_heads, -1).transpose(0, 2, 1, 3)


def fox_mixer(h, w_in, b_f):
    B, S, _ = h.shape
    W, H, dh = MIX_WIDTH, FOX_HEADS, FOX_HEAD_DIM
    proj = h @ w_in
    q = split_heads(proj[..., :W], H)
    k = split_heads(proj[..., W:2 * W], H)
    v = split_heads(proj[..., 2 * W:3 * W], H)
    f_logit = proj[..., 3 * W:3 * W + H]
    gate = proj[..., 3 * W + H:]
    log_f = jax.nn.log_sigmoid((f_logit + b_f).astype(jnp.float32)).transpose(0, 2, 1)
    c = jnp.cumsum(log_f, axis=-1)
    nq = S // Q_BLOCK
    qb = q.reshape(B, H, nq, Q_BLOCK, dh).transpose(2, 0, 1, 3, 4)
    cb = c.reshape(B, H, nq, Q_BLOCK).transpose(2, 0, 1, 3)
    starts = jnp.arange(nq) * Q_BLOCK
    key_pos = jnp.arange(S)
    scale = dh ** -0.5

    def block(args):
        q_blk, c_blk, start = args
        logits = jnp.einsum('bhqd,bhkd->bhqk', q_blk, k).astype(jnp.float32) * scale
        logits = logits + (c_blk[..., :, None] - c[:, :, None, :])
        q_pos = start + jnp.arange(Q_BLOCK)
        causal = key_pos[None, :] <= q_pos[:, None]
        logits = jnp.where(causal, logits, -jnp.inf)
        p = jax.nn.softmax(logits, axis=-1).astype(v.dtype)
        return jnp.einsum('bhqk,bhkd->bhqd', p, v)

    o = lax.map(block, (qb, cb, starts))
    o = o.transpose(1, 0, 3, 2, 4).reshape(B, S, W)
    return o * jax.nn.silu(gate)


def hgrn2_mixer(h, w_in, lb, onorm_gain):
    B, S, _ = h.shape
    H, dk, dv, C = HGRN_HEADS, HGRN_KEY_DIM, HGRN_VAL_DIM, HGRN_CHUNK
    proj = h @ w_in
    q_raw, f_raw, i_raw, gate = jnp.split(proj, 4, axis=-1)
    q = split_heads(jax.nn.silu(q_raw), H).astype(jnp.float32)
    fz = split_heads(f_raw, H).astype(jnp.float32)
    v = split_heads(i_raw, H).astype(jnp.float32)
    lb_h = lb.astype(jnp.float32).reshape(H, 1, dk)
    log_f = jnp.log(lb_h + (1.0 - lb_h) * jax.nn.sigmoid(fz))
    k = (1.0 - lb_h) * jax.nn.sigmoid(-fz)
    nc = S // C

    def to_chunks(t):
        return t.reshape(B, H, nc, C, t.shape[-1]).transpose(2, 0, 1, 3, 4)

    tri = jnp.tril(jnp.ones((C, C), dtype=bool))

    def step(state, inp):
        q_c, k_c, lf_c, v_c = inp
        b = jnp.cumsum(lf_c, axis=-2)
        b_last = b[:, :, -1, :]
        inter = jnp.einsum('bhtd,bhde->bhte', q_c * jnp.exp(b), state)
        rel = jnp.where(tri[:, :, None], b[:, :, :, None, :] - b[:, :, None, :, :], -jnp.inf)
        A = jnp.einsum('bhtd,bhsd,bhtsd->bhts', q_c, k_c, jnp.exp(rel))
        intra = jnp.einsum('bhts,bhse->bhte', A, v_c)
        new_state = jnp.exp(b_last)[..., None] * state + jnp.einsum(
            'bhsd,bhse->bhde', k_c * jnp.exp(b_last[:, :, None, :] - b), v_c)
        return new_state, inter + intra

    state0 = jnp.zeros((B, H, dk, dv), jnp.float32)
    _, o = lax.scan(step, state0, (to_chunks(q), to_chunks(k), to_chunks(log_f), to_chunks(v)))
    o = o.transpose(1, 2, 0, 3, 4).reshape(B, H, S, dv)
    o = o * lax.rsqrt(jnp.mean(o * o, axis=-1, keepdims=True) + EPS)
    o = o.transpose(0, 2, 1, 3).reshape(B, S, MIX_WIDTH).astype(h.dtype) * onorm_gain
    return o * jax.nn.silu(gate)


def _fwd_setup_inputs(seed: int = 0) -> dict:
    key = jax.random.key(seed)
    ks = jax.random.split(key, 10)
    f32 = jnp.float32
    x = jax.random.normal(ks[0], (BATCH, SEQ, D_MODEL), f32)
    norm_gains = 1.0 + 0.02 * jax.random.normal(ks[1], (DEPTH, D_MODEL), f32)
    fox_w_in = jax.random.normal(ks[2], (N_FOX_LAYERS, D_MODEL, FOX_IN), f32) * D_MODEL ** -0.5
    fox_b_f = 1.0 + 0.1 * jax.random.normal(ks[3], (N_FOX_LAYERS, FOX_HEADS), f32)
    hgrn_w_in = jax.random.normal(ks[4], (N_HGRN_LAYERS, D_MODEL, HGRN_IN), f32) * D_MODEL ** -0.5
    hgrn_lb_logits = 0.5 * jax.random.normal(ks[5], (DEPTH, MIX_WIDTH), f32)
    hgrn_onorm = 1.0 + 0.02 * jax.random.normal(ks[6], (N_HGRN_LAYERS, MIX_WIDTH), f32)
    w_out = jax.random.normal(ks[7], (DEPTH, MIX_WIDTH, D_MODEL), f32) * MIX_WIDTH ** -0.5
    final_gain = 1.0 + 0.02 * jax.random.normal(ks[8], (D_MODEL,), f32)
    return {"x": x, "norm_gains": norm_gains, "fox_w_in": fox_w_in, "fox_b_f": fox_b_f,
            "hgrn_w_in": hgrn_w_in, "hgrn_lb_logits": hgrn_lb_logits, "hgrn_onorm": hgrn_onorm,
            "w_out": w_out, "final_gain": final_gain}


def _fwd_reference(x, norm_gains, fox_w_in, fox_b_f, hgrn_w_in, hgrn_lb_logits, hgrn_onorm, w_out, final_gain):
    lb_all = jnp.cumsum(jax.nn.softmax(hgrn_lb_logits.astype(jnp.float32), axis=0), axis=0)
    lb_all = lb_all - lb_all[0:1]
    for i in range(DEPTH):
        h = rms_norm(x, norm_gains[i])
        j = i // N_MIXERS
        if i % N_MIXERS == 0:
            y = fox_mixer(h, fox_w_in[j], fox_b_f[j])
        else:
            y = hgrn2_mixer(h, hgrn_w_in[j], lb_all[i], hgrn_onorm[j])
        x = x + y @ w_out[i]
    return rms_norm(x, final_gain)


import jax as _jax
import jax.numpy as _jnp

TWIN_FORMAT = 'train_step'
FWD_PARAMS = ['x', 'norm_gains', 'fox_w_in', 'fox_b_f', 'hgrn_w_in', 'hgrn_lb_logits', 'hgrn_onorm', 'w_out', 'final_gain']
TWIN_WEIGHTS = ['norm_gains', 'fox_w_in', 'fox_b_f', 'hgrn_w_in', 'hgrn_lb_logits', 'hgrn_onorm', 'w_out', 'final_gain']
TWIN_DIFF_INPUT = 'x'
TWIN_INPUTS = ['x', 'norm_gains', 'fox_w_in', 'fox_b_f', 'hgrn_w_in', 'hgrn_lb_logits', 'hgrn_onorm', 'w_out', 'final_gain', 'loss_target', 'm_norm_gains', 'm_fox_w_in', 'm_fox_b_f', 'm_hgrn_w_in', 'm_hgrn_lb_logits', 'm_hgrn_onorm', 'm_w_out', 'm_final_gain', 'v_norm_gains', 'v_fox_w_in', 'v_fox_b_f', 'v_hgrn_w_in', 'v_hgrn_lb_logits', 'v_hgrn_onorm', 'v_w_out', 'v_final_gain']
TWIN_OUTPUTS = ['loss', 'grad_x', 'grad_norm_gains', 'grad_fox_w_in', 'grad_fox_b_f', 'grad_hgrn_w_in', 'grad_hgrn_lb_logits', 'grad_hgrn_onorm', 'grad_w_out', 'grad_final_gain', 'delta_norm_gains', 'delta_fox_w_in', 'delta_fox_b_f', 'delta_hgrn_w_in', 'delta_hgrn_lb_logits', 'delta_hgrn_onorm', 'delta_w_out', 'delta_final_gain', 'new_m_norm_gains', 'new_m_fox_w_in', 'new_m_fox_b_f', 'new_m_hgrn_w_in', 'new_m_hgrn_lb_logits', 'new_m_hgrn_onorm', 'new_m_w_out', 'new_m_final_gain', 'new_v_norm_gains', 'new_v_fox_w_in', 'new_v_fox_b_f', 'new_v_hgrn_w_in', 'new_v_hgrn_lb_logits', 'new_v_hgrn_onorm', 'new_v_w_out', 'new_v_final_gain']
TWIN_LEAF_KINDS = {'loss': 'loss', 'grad_x': 'grad_x', 'grad_norm_gains': 'grad_w', 'grad_fox_w_in': 'grad_w', 'grad_fox_b_f': 'grad_w', 'grad_hgrn_w_in': 'grad_w', 'grad_hgrn_lb_logits': 'grad_w', 'grad_hgrn_onorm': 'grad_w', 'grad_w_out': 'grad_w', 'grad_final_gain': 'grad_w', 'delta_norm_gains': 'delta_w', 'delta_fox_w_in': 'delta_w', 'delta_fox_b_f': 'delta_w', 'delta_hgrn_w_in': 'delta_w', 'delta_hgrn_lb_logits': 'delta_w', 'delta_hgrn_onorm': 'delta_w', 'delta_w_out': 'delta_w', 'delta_final_gain': 'delta_w', 'new_m_norm_gains': 'new_m', 'new_m_fox_w_in': 'new_m', 'new_m_fox_b_f': 'new_m', 'new_m_hgrn_w_in': 'new_m', 'new_m_hgrn_lb_logits': 'new_m', 'new_m_hgrn_onorm': 'new_m', 'new_m_w_out': 'new_m', 'new_m_final_gain': 'new_m', 'new_v_norm_gains': 'new_v', 'new_v_fox_w_in': 'new_v', 'new_v_fox_b_f': 'new_v', 'new_v_hgrn_w_in': 'new_v', 'new_v_hgrn_lb_logits': 'new_v', 'new_v_hgrn_onorm': 'new_v', 'new_v_w_out': 'new_v', 'new_v_final_gain': 'new_v'}


def _forward(args):
    return _fwd_reference(*[args[k] for k in FWD_PARAMS])


def _output_shape():
    def fwd():
        inp = _fwd_setup_inputs(0)
        return _fwd_reference(*[inp[k] for k in FWD_PARAMS])
    out = _jax.eval_shape(fwd)
    return out.shape, out.dtype

N_MICROBATCH = 1
ADAM_LR = 0.001
ADAM_B1 = 0.9
ADAM_B2 = 0.999
ADAM_EPS = 1e-08
ADAM_WD = 0.01
ADAM_STEP = 10
PER_EXAMPLE_BATCH_AXIS = {'x': 0, 'loss_target': 0}
SHARED_INPUTS = []
_WEIGHT_DTYPES = {'norm_gains': _jnp.float32, 'fox_w_in': _jnp.float32, 'fox_b_f': _jnp.float32, 'hgrn_w_in': _jnp.float32, 'hgrn_lb_logits': _jnp.float32, 'hgrn_onorm': _jnp.float32, 'w_out': _jnp.float32, 'final_gain': _jnp.float32}
MOMENT_SCALE = {'norm_gains': 8.339457e-02, 'fox_w_in': 3.770049e-02, 'fox_b_f': 2.423357e-01, 'hgrn_w_in': 4.513505e-02, 'hgrn_lb_logits': 5.453835e-03, 'hgrn_onorm': 6.285301e-02, 'w_out': 5.362334e-02, 'final_gain': 3.197024e+01}


def _to_microbatches(a, axis):
    t = _jnp.moveaxis(a, axis, 0)
    t = t.reshape((N_MICROBATCH, t.shape[0] // N_MICROBATCH) + t.shape[1:])
    return _jnp.moveaxis(t, 1, axis + 1)


def setup_inputs(seed: int = 0) -> dict:
    inp = _fwd_setup_inputs(seed)
    key = _jax.random.fold_in(_jax.random.key(seed), 7919)
    shape, _ = _output_shape()
    out = dict(inp)
    out["loss_target"] = _jax.random.normal(_jax.random.fold_in(key, 0), shape, _jnp.float32)
    for i, name in enumerate(TWIN_WEIGHTS):
        w = inp[name].astype(_jnp.float32)
        if MOMENT_SCALE is None:
            s = _jnp.sqrt(_jnp.mean(_jnp.square(w)) + 1e-30)
        else:
            s = MOMENT_SCALE[name]
        km, kv = _jax.random.split(_jax.random.fold_in(key, i + 1))
        out[name] = w
        out["m_" + name] = s * _jax.random.normal(km, w.shape, _jnp.float32)
        out["v_" + name] = (s * s) * _jax.random.uniform(kv, w.shape, _jnp.float32, 0.5, 1.5)
    if N_MICROBATCH > 1:
        for name, axis in PER_EXAMPLE_BATCH_AXIS.items():
            out[name] = _to_microbatches(out[name], axis)
    return {'x': out['x'], 'norm_gains': out['norm_gains'], 'fox_w_in': out['fox_w_in'], 'fox_b_f': out['fox_b_f'], 'hgrn_w_in': out['hgrn_w_in'], 'hgrn_lb_logits': out['hgrn_lb_logits'], 'hgrn_onorm': out['hgrn_onorm'], 'w_out': out['w_out'], 'final_gain': out['final_gain'], 'loss_target': out['loss_target'], 'm_norm_gains': out['m_norm_gains'], 'm_fox_w_in': out['m_fox_w_in'], 'm_fox_b_f': out['m_fox_b_f'], 'm_hgrn_w_in': out['m_hgrn_w_in'], 'm_hgrn_lb_logits': out['m_hgrn_lb_logits'], 'm_hgrn_onorm': out['m_hgrn_onorm'], 'm_w_out': out['m_w_out'], 'm_final_gain': out['m_final_gain'], 'v_norm_gains': out['v_norm_gains'], 'v_fox_w_in': out['v_fox_w_in'], 'v_fox_b_f': out['v_fox_b_f'], 'v_hgrn_w_in': out['v_hgrn_w_in'], 'v_hgrn_lb_logits': out['v_hgrn_lb_logits'], 'v_hgrn_onorm': out['v_hgrn_onorm'], 'v_w_out': out['v_w_out'], 'v_final_gain': out['v_final_gain']}


def _loss(weights, diff, rest, loss_target):
    with _jax.named_scope("forward"):
        args = {**rest, TWIN_DIFF_INPUT: diff, **{k: w.astype(_WEIGHT_DTYPES[k]) for k, w in weights.items()}}
        y = _forward(args)
    with _jax.named_scope("loss_head"):
        err = _jnp.square(y.astype(_jnp.float32) - loss_target)
        return 0.5 * _jnp.sum(_jnp.mean(err, axis=-1)) if err.ndim else 0.5 * err


def _adamw(w, g, m, v):
    m = ADAM_B1 * m + (1.0 - ADAM_B1) * g
    v = ADAM_B2 * v + (1.0 - ADAM_B2) * _jnp.square(g)
    m_hat = m / (1.0 - ADAM_B1 ** ADAM_STEP)
    v_hat = v / (1.0 - ADAM_B2 ** ADAM_STEP)
    delta = -ADAM_LR * (m_hat / (_jnp.sqrt(v_hat) + ADAM_EPS) + ADAM_WD * w)
    return delta, m, v


def reference(x, norm_gains, fox_w_in, fox_b_f, hgrn_w_in, hgrn_lb_logits, hgrn_onorm, w_out, final_gain, loss_target, m_norm_gains, m_fox_w_in, m_fox_b_f, m_hgrn_w_in, m_hgrn_lb_logits, m_hgrn_onorm, m_w_out, m_final_gain, v_norm_gains, v_fox_w_in, v_fox_b_f, v_hgrn_w_in, v_hgrn_lb_logits, v_hgrn_onorm, v_w_out, v_final_gain):
    given = dict(x=x, norm_gains=norm_gains, fox_w_in=fox_w_in, fox_b_f=fox_b_f, hgrn_w_in=hgrn_w_in, hgrn_lb_logits=hgrn_lb_logits, hgrn_onorm=hgrn_onorm, w_out=w_out, final_gain=final_gain, loss_target=loss_target, m_norm_gains=m_norm_gains, m_fox_w_in=m_fox_w_in, m_fox_b_f=m_fox_b_f, m_hgrn_w_in=m_hgrn_w_in, m_hgrn_lb_logits=m_hgrn_lb_logits, m_hgrn_onorm=m_hgrn_onorm, m_w_out=m_w_out, m_final_gain=m_final_gain, v_norm_gains=v_norm_gains, v_fox_w_in=v_fox_w_in, v_fox_b_f=v_fox_b_f, v_hgrn_w_in=v_hgrn_w_in, v_hgrn_lb_logits=v_hgrn_lb_logits, v_hgrn_onorm=v_hgrn_onorm, v_w_out=v_w_out, v_final_gain=v_final_gain)
    weights = {n: given[n] for n in TWIN_WEIGHTS}
    shared = {n: given[n] for n in SHARED_INPUTS}
    per_example = {n: given[n] for n in ['x']}
    grad_fn = _jax.value_and_grad(_loss, argnums=(0, 1))

    def one_microbatch(ex, loss_target):
        ex = dict(ex)
        diff = ex.pop(TWIN_DIFF_INPUT)
        return grad_fn(weights, diff, {**shared, **ex}, loss_target)

    if N_MICROBATCH == 1:
        loss, (grad_w, grad_x) = one_microbatch(per_example, given["loss_target"])
    else:
        def body(carry, xs):
            loss_sum, grad_sum = carry
            l_k, (gw_k, gx_k) = one_microbatch(xs[0], xs[1])
            with _jax.named_scope("update"):
                return (loss_sum + l_k, _jax.tree.map(_jnp.add, grad_sum, gw_k)), gx_k

        init = (_jnp.zeros((), _jnp.float32), _jax.tree.map(_jnp.zeros_like, weights))
        (loss, grad_w), grad_x = _jax.lax.scan(body, init, (per_example, given["loss_target"]))
    with _jax.named_scope("update"):
        delta_w, new_m, new_v = {}, {}, {}
        for n in TWIN_WEIGHTS:
            delta_w[n], new_m[n], new_v[n] = _adamw(weights[n], grad_w[n], given["m_" + n], given["v_" + n])
    return (loss, grad_x, *[grad_w[n] for n in TWIN_WEIGHTS], *[delta_w[n] for n in TWIN_WEIGHTS],
            *[new_m[n] for n in TWIN_WEIGHTS], *[new_v[n] for n in TWIN_WEIGHTS])
```

```python
import functools

import jax
import jax.numpy as jnp
from jax import lax
from jax.experimental import pallas as pl
from jax.experimental.pallas import tpu as pltpu

F32 = jnp.float32
BF16 = jnp.bfloat16
MESH = pl.DeviceIdType.MESH
ANY = pl.BlockSpec(memory_space=pl.ANY)

EPS = 1e-6
HEAD_DIM = 128
HGRN_CHUNK = 64
HGRN_SUB = 16
EXP_CLAMP = 80.0
ATT_BLOCK = 512
HGRN_BLOCK = 512
GATE_BLOCK = 512
ROW_BLOCK = 256
MM_TM, MM_TN, MM_TK = 1024, 1024, 2048
VMEM_LIMIT_V7X = 56 * 1024 * 1024

ADAM_LR, ADAM_B1, ADAM_B2, ADAM_EPS, ADAM_WD, ADAM_STEP = 0.001, 0.9, 0.999, 1e-08, 0.01, 10

NT = (((1,), (1,)), ((), ()))
TN = (((0,), (0,)), ((), ()))
NN = (((1,), (0,)), ((), ()))


def _call(body, **kw):
    return pl.pallas_call(body, **kw)


def _cp(dims=None):
    kw = dict(vmem_limit_bytes=VMEM_LIMIT_V7X)
    if dims is not None:
        kw["dimension_semantics"] = dims
    return pltpu.CompilerParams(**kw)


def _sigmoid(x):
    return 1.0 / (1.0 + jnp.exp(-x))


def _dot(a, b, dn=NN):
    return lax.dot_general(a.astype(BF16), b.astype(BF16), dn, preferred_element_type=F32)


def _split3(x):
    hi = x.astype(BF16)
    r1 = x - hi.astype(F32)
    mid = r1.astype(BF16)
    lo = (r1 - mid.astype(F32)).astype(BF16)
    return hi, mid, lo


def _dot_exact(m01, x, right=False):
    hi, mid, lo = _split3(x)
    if right:
        dot = lambda p: lax.dot_general(p, m01, NN, preferred_element_type=F32)
    else:
        dot = lambda p: lax.dot_general(m01, p, NN, preferred_element_type=F32)
    return dot(hi) + dot(mid) + dot(lo)


def _row_block(rows, cap):
    if rows <= cap:
        return rows
    best = None
    for t in range(16, cap + 1, 16):
        if rows % t == 0:
            best = t
    assert best is not None, rows
    return best


def _mm(name, a, b, *, ta=False, tb=False, out_dtype=F32, res=None, n=None, b_off=0):
    m, k = (a.shape[1], a.shape[0]) if ta else a.shape
    n_full = b.shape[0] if tb else b.shape[1]
    n = n_full if n is None else n
    tm, tn, tk = min(MM_TM, m), min(MM_TN, n), min(MM_TK, k)
    assert m % tm == 0 and n % tn == 0 and k % tk == 0 and b_off % tn == 0
    nk = k // tk
    jo = b_off // tn
    a_spec = pl.BlockSpec((tk, tm), lambda i, j, kk: (kk, i)) if ta else pl.BlockSpec((tm, tk), lambda i, j, kk: (i, kk))
    b_spec = pl.BlockSpec((tn, tk), lambda i, j, kk: (j + jo, kk)) if tb else pl.BlockSpec((tk, tn), lambda i, j, kk: (kk, j + jo))
    o_spec = pl.BlockSpec((tm, tn), lambda i, j, kk: (i, j))
    dn = (((0 if ta else 1,), (1 if tb else 0,)), ((), ()))
    has_res = res is not None

    def body(*refs):
        if has_res:
            a_ref, b_ref, r_ref, o_ref, acc = refs
        else:
            a_ref, b_ref, o_ref, acc = refs
            r_ref = None
        kk = pl.program_id(2)
        p = lax.dot_general(a_ref[...].astype(BF16), b_ref[...].astype(BF16), dn, preferred_element_type=F32)

        def finish(total):
            if r_ref is not None:
                total = total + r_ref[...]
            o_ref[...] = total.astype(out_dtype)

        if nk == 1:
            finish(p)
        else:
            @pl.when(kk == 0)
            def _():
                acc[...] = p

            @pl.when(jnp.logical_and(kk > 0, kk < nk - 1))
            def _():
                acc[...] += p

            @pl.when(kk == nk - 1)
            def _():
                finish(acc[...] + p)

    ins = [a, b] + ([res] if has_res else [])
    in_specs = [a_spec, b_spec] + ([o_spec] if has_res else [])
    return _call(
        body, name=name, grid=(m // tm, n // tn, nk), in_specs=in_specs, out_specs=o_spec,
        out_shape=jax.ShapeDtypeStruct((m, n), out_dtype),
        scratch_shapes=[pltpu.VMEM((tm, tn) if nk > 1 else (8, 128), F32)],
        compiler_params=_cp(("parallel", "parallel", "arbitrary")),
    )(*ins)


def _rms_fwd(name, x, g):
    s, d = x.shape
    tm = min(ROW_BLOCK, s)

    def body(x_ref, g_ref, h_ref):
        xv = x_ref[...]
        r = lax.rsqrt(jnp.mean(xv * xv, axis=-1, keepdims=True) + EPS)
        h_ref[...] = (xv * r * g_ref[...]).astype(BF16)

    row = pl.BlockSpec((tm, d), lambda i: (i, 0))
    vec = pl.BlockSpec((1, d), lambda i: (0, 0))
    return _call(body, name=name, grid=(s // tm,), in_specs=[row, vec], out_specs=row,
                 out_shape=jax.ShapeDtypeStruct((s, d), BF16), compiler_params=_cp(("parallel",)))(x, g)


def _rms_bwd(name, x, g, dh, dres):
    s, d = x.shape
    tm = min(ROW_BLOCK, s)

    def body(x_ref, g_ref, dh_ref, dres_ref, dx_ref, dg_ref):
        @pl.when(pl.program_id(0) == 0)
        def _():
            dg_ref[...] = jnp.zeros_like(dg_ref)

        xv = x_ref[...]
        r = lax.rsqrt(jnp.mean(xv * xv, axis=-1, keepdims=True) + EPS)
        xn = xv * r
        dhv = dh_ref[...]
        dxn = dhv * g_ref[...]
        dx_ref[...] = dres_ref[...] + r * (dxn - xn * jnp.mean(dxn * xn, axis=-1, keepdims=True))
        dg_ref[...] += jnp.sum(dhv * xn, axis=0, keepdims=True)

    row = pl.BlockSpec((tm, d), lambda i: (i, 0))
    vec = pl.BlockSpec((1, d), lambda i: (0, 0))
    return _call(body, name=name, grid=(s // tm,), in_specs=[row, vec, row, row], out_specs=[row, vec],
                 out_shape=[jax.ShapeDtypeStruct((s, d), F32), jax.ShapeDtypeStruct((1, d), F32)],
                 compiler_params=_cp(("arbitrary",)))(x, g, dh, dres)


def _loss_head(name, x, tgt, g):
    s, d = x.shape
    tm = min(ROW_BLOCK, s)
    nb = s // tm

    def body(x_ref, t_ref, g_ref, dx_ref, dg_ref, loss_ref, lacc):
        i = pl.program_id(0)

        @pl.when(i == 0)
        def _():
            dg_ref[...] = jnp.zeros_like(dg_ref)
            lacc[...] = jnp.zeros_like(lacc)

        xv = x_ref[...]
        gv = g_ref[...]
        r = lax.rsqrt(jnp.mean(xv * xv, axis=-1, keepdims=True) + EPS)
        xn = xv * r
        err = xn * gv - t_ref[...]
        lacc[...] += jnp.sum(err * err, axis=0, keepdims=True)
        dout = err * (1.0 / d)
        dg_ref[...] += jnp.sum(dout * xn, axis=0, keepdims=True)
        dxn = dout * gv
        dx_ref[...] = r * (dxn - xn * jnp.mean(dxn * xn, axis=-1, keepdims=True))

        @pl.when(i == nb - 1)
        def _():
            total = jnp.sum(lacc[...], axis=1, keepdims=True) * (0.5 / d)
            loss_ref[...] = jnp.broadcast_to(total, loss_ref.shape)

    row = pl.BlockSpec((tm, d), lambda i: (i, 0))
    vec = pl.BlockSpec((1, d), lambda i: (0, 0))
    one = pl.BlockSpec((1, 128), lambda i: (0, 0))
    return _call(body, name=name, grid=(nb,), in_specs=[row, row, vec], out_specs=[row, vec, one],
                 out_shape=[jax.ShapeDtypeStruct((s, d), F32), jax.ShapeDtypeStruct((1, d), F32),
                            jax.ShapeDtypeStruct((1, 128), F32)],
                 scratch_shapes=[pltpu.VMEM((1, d), F32)], compiler_params=_cp(("arbitrary",)))(x, tgt, g)


def _fox_gate_fwd(name, fl, bf, w):
    s = fl.shape[0]
    tb = min(GATE_BLOCK, s)

    def body(fl_ref, bf_ref, c_ref, crep_ref, carry):
        @pl.when(pl.program_id(0) == 0)
        def _():
            carry[...] = jnp.zeros_like(carry)

        z = fl_ref[...] + bf_ref[...]
        lf = jnp.minimum(z, 0.0) - jnp.log(1.0 + jnp.exp(-jnp.abs(z)))
        rows = lax.broadcasted_iota(jnp.int32, (tb, tb), 0)
        cols = lax.broadcasted_iota(jnp.int32, (tb, tb), 1)
        tri = (rows >= cols).astype(BF16)
        cs = _dot_exact(tri, lf) + carry[...]
        c_ref[...] = cs
        carry[...] = c_ref[tb - 1:tb, :]
        sel_r = lax.broadcasted_iota(jnp.int32, (128, w), 0)
        sel_c = lax.broadcasted_iota(jnp.int32, (128, w), 1)
        sel = (sel_r == sel_c // HEAD_DIM).astype(BF16)
        crep_ref[...] = _dot_exact(sel, cs, right=True)

    blk = pl.BlockSpec((tb, 128), lambda i: (i, 0))
    return _call(body, name=name, grid=(s // tb,),
                 in_specs=[blk, pl.BlockSpec((1, 128), lambda i: (0, 0))],
                 out_specs=[blk, pl.BlockSpec((tb, w), lambda i: (i, 0))],
                 out_shape=[jax.ShapeDtypeStruct((s, 128), F32), jax.ShapeDtypeStruct((s, w), F32)],
                 scratch_shapes=[pltpu.VMEM((1, 128), F32)], compiler_params=_cp(("arbitrary",)))(fl, bf)


def _fox_gate_bwd(name, drow, dcol, fl, bf):
    s = fl.shape[0]
    tb = min(GATE_BLOCK, s)
    nb = s // tb

    def body(dr_ref, dc_ref, fl_ref, bf_ref, dfl_ref, dbf_ref, carry, tmp):
        @pl.when(pl.program_id(0) == 0)
        def _():
            carry[...] = jnp.zeros_like(carry)
            dbf_ref[...] = jnp.zeros_like(dbf_ref)

        rows = lax.broadcasted_iota(jnp.int32, (tb, tb), 0)
        cols = lax.broadcasted_iota(jnp.int32, (tb, tb), 1)
        triu = (rows <= cols).astype(BF16)
        dlf = _dot_exact(triu, dr_ref[...] + dc_ref[...]) + carry[...]
        tmp[...] = dlf
        carry[...] = tmp[0:1, :]
        z = fl_ref[...] + bf_ref[...]
        dfl = dlf * (1.0 / (1.0 + jnp.exp(z)))
        dfl_ref[...] = dfl
        dbf_ref[...] += jnp.sum(dfl, axis=0, keepdims=True)

    blk = pl.BlockSpec((tb, 128), lambda i: (nb - 1 - i, 0))
    vec = pl.BlockSpec((1, 128), lambda i: (0, 0))
    return _call(body, name=name, grid=(nb,), in_specs=[blk, blk, blk, vec], out_specs=[blk, vec],
                 out_shape=[jax.ShapeDtypeStruct((s, 128), F32), jax.ShapeDtypeStruct((1, 128), F32)],
                 scratch_shapes=[pltpu.VMEM((1, 128), F32), pltpu.VMEM((tb, 128), F32)],
                 compiler_params=_cp(("arbitrary",)))(drow, dcol, fl, bf)


def _attn_fwd(name, qkv, c_row, h_count):
    s = qkv.shape[0]
    w = h_count * HEAD_DIM
    t = min(ATT_BLOCK, s)
    scale = HEAD_DIM ** -0.5

    def body(q_ref, k_ref, v_ref, c_ref, o_ref, lse_ref, m_s, l_s, acc_s):
        qi = pl.program_id(1)
        q = q_ref[...]
        m_s[...] = jnp.full(m_s.shape, -jnp.inf, F32)
        l_s[...] = jnp.zeros_like(l_s)
        acc_s[...] = jnp.zeros_like(acc_s)

        def step(kb, masked):
            off = pl.multiple_of(kb * t, t)
            kk = k_ref[pl.ds(off, t), :]
            vv = v_ref[pl.ds(off, t), :]
            sc = lax.dot_general(q, kk, NT, preferred_element_type=F32) * scale - c_ref[0, :, pl.ds(off, t)]
            if masked:
                rows = lax.broadcasted_iota(jnp.int32, (t, t), 0)
                cols = lax.broadcasted_iota(jnp.int32, (t, t), 1)
                sc = jnp.where(rows >= cols, sc, -jnp.inf)
            m_prev = m_s[...]
            m_new = jnp.maximum(m_prev, jnp.max(sc, axis=1, keepdims=True))
            p = jnp.exp(sc - m_new)
            alpha = jnp.exp(m_prev - m_new)
            l_s[...] = alpha * l_s[...] + jnp.sum(p, axis=1, keepdims=True)
            acc_s[...] = alpha * acc_s[...] + lax.dot_general(p.astype(BF16), vv, NN, preferred_element_type=F32)
            m_s[...] = m_new

        def loop_body(kb, carry):
            step(kb, False)
            return carry

        lax.fori_loop(0, qi, loop_body, 0)
        step(qi, True)
        l = l_s[...]
        o_ref[...] = acc_s[...] / l
        lse_ref[...] = jnp.broadcast_to(m_s[...] + jnp.log(l), lse_ref.shape)

    hh = h_count
    blk = lambda off: pl.BlockSpec((t, HEAD_DIM), lambda h, i: (i, off + h))
    whole = lambda off: pl.BlockSpec((s, HEAD_DIM), lambda h, i: (0, off + h))
    return _call(
        body, name=name, grid=(hh, s // t),
        in_specs=[blk(0), whole(hh), whole(2 * hh), pl.BlockSpec((1, 1, s), lambda h, i: (h, 0, 0))],
        out_specs=[blk(0), blk(0)],
        out_shape=[jax.ShapeDtypeStruct((s, w), F32), jax.ShapeDtypeStruct((s, w), F32)],
        scratch_shapes=[pltpu.VMEM((t, 1), F32), pltpu.VMEM((t, 1), F32), pltpu.VMEM((t, HEAD_DIM), F32)],
        compiler_params=_cp(("parallel", "arbitrary")),
    )(qkv, qkv, qkv, c_row)


def _attn_dq(name, qkv, c_row, do, lse_rep, delta_rep, h_count):
    s = qkv.shape[0]
    w = h_count * HEAD_DIM
    t = min(ATT_BLOCK, s)
    scale = HEAD_DIM ** -0.5

    def body(q_ref, k_ref, v_ref, c_ref, do_ref, lse_ref, dl_ref, dq_ref, rs_ref, acc_s, rs_s):
        qi = pl.program_id(1)
        q = q_ref[...]
        dov = do_ref[...]
        lse = jnp.max(lse_ref[...], axis=1, keepdims=True)
        delta = jnp.max(dl_ref[...], axis=1, keepdims=True)
        acc_s[...] = jnp.zeros_like(acc_s)
        rs_s[...] = jnp.zeros_like(rs_s)

        def step(kb, masked):
            off = pl.multiple_of(kb * t, t)
            kk = k_ref[pl.ds(off, t), :]
            vv = v_ref[pl.ds(off, t), :]
            sc = lax.dot_general(q, kk, NT, preferred_element_type=F32) * scale - c_ref[0, :, pl.ds(off, t)]
            p = jnp.exp(sc - lse)
            if masked:
                rows = lax.broadcasted_iota(jnp.int32, (t, t), 0)
                cols = lax.broadcasted_iota(jnp.int32, (t, t), 1)
                p = jnp.where(rows >= cols, p, 0.0)
            dp = lax.dot_general(dov, vv, NT, preferred_element_type=F32)
            ds = p * (dp - delta)
            acc_s[...] += lax.dot_general(ds.astype(BF16), kk, NN, preferred_element_type=F32)
            rs_s[...] += jnp.sum(ds, axis=1, keepdims=True)

        def loop_body(kb, carry):
            step(kb, False)
            return carry

        lax.fori_loop(0, qi, loop_body, 0)
        step(qi, True)
        dq_ref[...] = (acc_s[...] * scale).astype(BF16)
        rs_ref[...] = jnp.broadcast_to(rs_s[...], rs_ref.shape)

    hh = h_count
    blk = pl.BlockSpec((t, HEAD_DIM), lambda h, i: (i, h))
    whole = lambda off: pl.BlockSpec((s, HEAD_DIM), lambda h, i: (0, off + h))
    return _call(
        body, name=name, grid=(hh, s // t),
        in_specs=[blk, whole(hh), whole(2 * hh), pl.BlockSpec((1, 1, s), lambda h, i: (h, 0, 0)), blk, blk, blk],
        out_specs=[blk, blk],
        out_shape=[jax.ShapeDtypeStruct((s, w), BF16), jax.ShapeDtypeStruct((s, w), F32)],
        scratch_shapes=[pltpu.VMEM((t, HEAD_DIM), F32), pltpu.VMEM((t, 1), F32)],
        compiler_params=_cp(("parallel", "arbitrary")),
    )(qkv, qkv, qkv, c_row, do, lse_rep, delta_rep)


def _attn_dkv(name, qkv, do, lse_row, delta_row, c_rep, h_count):
    s = qkv.shape[0]
    w = h_count * HEAD_DIM
    t = min(ATT_BLOCK, s)
    nq = s // t
    scale = HEAD_DIM ** -0.5

    def body(k_ref, v_ref, q_ref, do_ref, lse_ref, dl_ref, c_ref, dk_ref, dv_ref, dc_ref, dk_s, dv_s, dc_s):
        kj = pl.program_id(1)
        kk = k_ref[...]
        vv = v_ref[...]
        ccol = jnp.max(c_ref[...], axis=1, keepdims=True)
        dk_s[...] = jnp.zeros_like(dk_s)
        dv_s[...] = jnp.zeros_like(dv_s)
        dc_s[...] = jnp.zeros_like(dc_s)

        def step(qb, masked):
            off = pl.multiple_of(qb * t, t)
            q = q_ref[pl.ds(off, t), :]
            dov = do_ref[pl.ds(off, t), :]
            st = lax.dot_general(kk, q, NT, preferred_element_type=F32) * scale - ccol
            pt = jnp.exp(st - lse_ref[0, :, pl.ds(off, t)])
            if masked:
                rows = lax.broadcasted_iota(jnp.int32, (t, t), 0)
                cols = lax.broadcasted_iota(jnp.int32, (t, t), 1)
                pt = jnp.where(cols >= rows, pt, 0.0)
            dv_s[...] += lax.dot_general(pt.astype(BF16), dov, NN, preferred_element_type=F32)
            dpt = lax.dot_general(vv, dov, NT, preferred_element_type=F32)
            dst = pt * (dpt - dl_ref[0, :, pl.ds(off, t)])
            dk_s[...] += lax.dot_general(dst.astype(BF16), q, NN, preferred_element_type=F32)
            dc_s[...] += jnp.sum(dst, axis=1, keepdims=True)

        step(kj, True)

        def loop_body(qb, carry):
            step(qb, False)
            return carry

        lax.fori_loop(kj + 1, nq, loop_body, 0)
        dk_ref[...] = (dk_s[...] * scale).astype(BF16)
        dv_ref[...] = dv_s[...].astype(BF16)
        dc_ref[...] = jnp.broadcast_to(-dc_s[...], dc_ref.shape)

    hh = h_count
    blk = lambda off: pl.BlockSpec((t, HEAD_DIM), lambda h, j: (j, off + h))
    whole = pl.BlockSpec((s, HEAD_DIM), lambda h, j: (0, h))
    rowv = pl.BlockSpec((1, 1, s), lambda h, j: (h, 0, 0))
    return _call(
        body, name=name, grid=(hh, nq),
        in_specs=[blk(hh), blk(2 * hh), whole, whole, rowv, rowv, blk(0)],
        out_specs=[blk(0), blk(0), blk(0)],
        out_shape=[jax.ShapeDtypeStruct((s, w), BF16), jax.ShapeDtypeStruct((s, w), BF16),
                   jax.ShapeDtypeStruct((s, w), F32)],
        scratch_shapes=[pltpu.VMEM((t, HEAD_DIM), F32), pltpu.VMEM((t, HEAD_DIM), F32), pltpu.VMEM((t, 1), F32)],
        compiler_params=_cp(("parallel", "arbitrary")),
    )(qkv, qkv, qkv, do, lse_row, delta_row, c_rep)


def _fox_post(name, o, gate):
    s, w = o.shape
    tm = min(ROW_BLOCK, s)

    def body(o_ref, g_ref, y_ref):
        g = g_ref[...]
        y_ref[...] = (o_ref[...] * (g * _sigmoid(g))).astype(BF16)

    row = pl.BlockSpec((tm, w), lambda i: (i, 0))
    return _call(body, name=name, grid=(s // tm,), in_specs=[row, row], out_specs=row,
                 out_shape=jax.ShapeDtypeStruct((s, w), BF16), compiler_params=_cp(("parallel",)))(o, gate)


def _fox_pre_bwd(name, dy, o, gate):
    s, w = o.shape
    tm = min(ROW_BLOCK, s)

    def body(dy_ref, o_ref, g_ref, do_ref, dg_ref, dl_ref):
        g = g_ref[...]
        sg = _sigmoid(g)
        dyv = dy_ref[...]
        ov = o_ref[...]
        dov = dyv * (g * sg)
        do_ref[...] = dov.astype(BF16)
        dg_ref[...] = (dyv * ov * (sg * (1.0 + g * (1.0 - sg)))).astype(BF16)
        prod = dov * ov
        for h in range(w // HEAD_DIM):
            sl = slice(h * HEAD_DIM, (h + 1) * HEAD_DIM)
            dl_ref[:, sl] = jnp.broadcast_to(jnp.sum(prod[:, sl], axis=1, keepdims=True), (tm, HEAD_DIM))

    row = pl.BlockSpec((tm, w), lambda i: (i, 0))
    return _call(body, name=name, grid=(s // tm,), in_specs=[row, row, row], out_specs=[row, row, row],
                 out_shape=[jax.ShapeDtypeStruct((s, w), BF16), jax.ShapeDtypeStruct((s, w), BF16),
                            jax.ShapeDtypeStruct((s, w), F32)],
                 compiler_params=_cp(("parallel",)))(dy, o, gate)


def _hgrn_chunk_terms(q_c, k_c, b_c, b_s, base):
    nsub = HGRN_CHUNK // HGRN_SUB
    refs = [jnp.zeros((1, HEAD_DIM), F32)]
    for i in range(1, nsub):
        r0 = base + i * HGRN_SUB - 1
        refs.append(b_s[r0:r0 + 1, :])
    rfull = jnp.concatenate([jnp.broadcast_to(r, (HGRN_SUB, HEAD_DIM)) for r in refs], axis=0)
    eq = jnp.exp(b_c - rfull)
    qe = q_c * eq
    es = [jnp.exp(jnp.minimum(r - b_c, EXP_CLAMP)) for r in refs]
    kes = [(k_c * e).astype(BF16) for e in es]
    return eq, qe, es, kes


def _hgrn_block_pre(q_ref, f_ref, lbl_ref, tri, b_s):
    lb = _sigmoid(lbl_ref[1:2, :] - lbl_ref[0:1, :])
    qr = q_ref[...]
    sq = _sigmoid(qr)
    q = qr * sq
    fz = f_ref[...]
    sg = _sigmoid(fz)
    f = lb + (1.0 - lb) * sg
    g = jnp.log(f)
    k = (1.0 - lb) * (1.0 / (1.0 + jnp.exp(fz)))
    b = _dot_exact(tri, g)
    b_s[...] = b
    return lb, qr, sq, q, sg, f, k, b


def _chunk_masks(t):
    rows = lax.broadcasted_iota(jnp.int32, (t, t), 0)
    cols = lax.broadcasted_iota(jnp.int32, (t, t), 1)
    same = (rows // HGRN_CHUNK) == (cols // HGRN_CHUNK)
    return rows, cols, same


def _hgrn_fwd(name, p1, lbl, onorm, h_count):
    s = p1.shape[0]
    w = h_count * HEAD_DIM
    t = min(HGRN_BLOCK, s)
    nc = t // HGRN_CHUNK
    nsub = HGRN_CHUNK // HGRN_SUB
    cc = HGRN_CHUNK

    def body(q_ref, f_ref, i_ref, g_ref, lbl_ref, on_ref, y_ref, o_ref, st_ref, state_s, b_s):
        @pl.when(pl.program_id(1) == 0)
        def _():
            state_s[...] = jnp.zeros_like(state_s)

        rows, cols, same = _chunk_masks(t)
        tri = jnp.logical_and(same, rows >= cols).astype(BF16)
        lb, qr, sq, q, sg, f, k, b = _hgrn_block_pre(q_ref, f_ref, lbl_ref, tri, b_s)
        v = i_ref[...]
        r64 = lax.broadcasted_iota(jnp.int32, (cc, cc), 0)
        c64 = lax.broadcasted_iota(jnp.int32, (cc, cc), 1)
        for n in range(nc):
            sl = slice(n * cc, (n + 1) * cc)
            q_c, k_c, v_c, b_c = q[sl], k[sl], v[sl], b[sl]
            bl = b_s[n * cc + cc - 1:n * cc + cc, :]
            eq, qe, es, kes = _hgrn_chunk_terms(q_c, k_c, b_c, b_s, n * cc)
            qeb = qe.astype(BF16)
            a = jnp.concatenate(
                [lax.dot_general(qeb[i * HGRN_SUB:(i + 1) * HGRN_SUB], kes[i], NT, preferred_element_type=F32)
                 for i in range(nsub)], axis=0)
            a = jnp.where(r64 >= c64, a, 0.0)
            st = state_s[...]
            st_ref[0, n] = st
            inter = _dot(q_c * jnp.exp(b_c), st, NT)
            intra = _dot(a, v_c)
            o_ref[sl, :] = inter + intra
            kb = k_c * jnp.exp(bl - b_c)
            state_s[...] = st * jnp.exp(bl) + _dot(v_c, kb, TN)
        o = o_ref[...]
        rr = lax.rsqrt(jnp.mean(o * o, axis=-1, keepdims=True) + EPS)
        gate = g_ref[...]
        y_ref[...] = ((o * rr) * on_ref[...] * (gate * _sigmoid(gate))).astype(BF16)

    hh = h_count
    blk = lambda off: pl.BlockSpec((t, HEAD_DIM), lambda h, i: (i, off + h))
    return _call(
        body, name=name, grid=(hh, s // t),
        in_specs=[blk(0), blk(hh), blk(2 * hh), blk(3 * hh),
                  pl.BlockSpec((2, HEAD_DIM), lambda h, i: (0, h)), pl.BlockSpec((1, HEAD_DIM), lambda h, i: (0, h))],
        out_specs=[blk(0), blk(0), pl.BlockSpec((1, nc, HEAD_DIM, HEAD_DIM), lambda h, i: (h, i, 0, 0))],
        out_shape=[jax.ShapeDtypeStruct((s, w), BF16), jax.ShapeDtypeStruct((s, w), F32),
                   jax.ShapeDtypeStruct((hh, s // cc, HEAD_DIM, HEAD_DIM), F32)],
        scratch_shapes=[pltpu.VMEM((HEAD_DIM, HEAD_DIM), F32), pltpu.VMEM((t, HEAD_DIM), F32)],
        compiler_params=_cp(("parallel", "arbitrary")),
    )(p1, p1, p1, p1, lbl, onorm)


def _hgrn_bwd(name, p1, lbl, onorm, o, dy, states, h_count):
    s = p1.shape[0]
    w = h_count * HEAD_DIM
    t = min(HGRN_BLOCK, s)
    nb = s // t
    nc = t // HGRN_CHUNK
    nsub = HGRN_CHUNK // HGRN_SUB
    cc = HGRN_CHUNK

    def body(q_ref, f_ref, i_ref, g_ref, lbl_ref, on_ref, o_ref, dy_ref, st_ref,
             dq_ref, df_ref, di_ref, dgate_ref, dlog_ref, don_ref,
             dst_s, b_s, do_s, dqs, dks, dbs, exs):
        @pl.when(pl.program_id(1) == 0)
        def _():
            dst_s[...] = jnp.zeros_like(dst_s)
            dlog_ref[...] = jnp.zeros_like(dlog_ref)
            don_ref[...] = jnp.zeros_like(don_ref)

        rows, cols, same = _chunk_masks(t)
        tri = jnp.logical_and(same, rows >= cols).astype(BF16)
        triu = jnp.logical_and(same, rows <= cols).astype(BF16)
        lb, qr, sq, q, sg, f, k, b = _hgrn_block_pre(q_ref, f_ref, lbl_ref, tri, b_s)
        v = i_ref[...]

        ov = o_ref[...]
        rr = lax.rsqrt(jnp.mean(ov * ov, axis=-1, keepdims=True) + EPS)
        on = ov * rr
        gate = g_ref[...]
        sgt = _sigmoid(gate)
        silu = gate * sgt
        dyv = dy_ref[...]
        gain = on_ref[...]
        dgate_ref[...] = (dyv * on * gain * (sgt * (1.0 + gate * (1.0 - sgt)))).astype(BF16)
        don_ref[...] += jnp.sum(dyv * on * silu, axis=0, keepdims=True)
        d_on = dyv * gain * silu
        do_s[...] = rr * (d_on - on * jnp.mean(d_on * on, axis=-1, keepdims=True))

        r64 = lax.broadcasted_iota(jnp.int32, (cc, cc), 0)
        c64 = lax.broadcasted_iota(jnp.int32, (cc, cc), 1)
        upper = r64 <= c64
        for n in reversed(range(nc)):
            sl = slice(n * cc, (n + 1) * cc)
            q_c, k_c, v_c, b_c = q[sl], k[sl], v[sl], b[sl]
            do_c = do_s[sl, :]
            bl = b_s[n * cc + cc - 1:n * cc + cc, :]
            ebl = jnp.exp(bl)
            eq, qe, es, kes = _hgrn_chunk_terms(q_c, k_c, b_c, b_s, n * cc)
            qeb = qe.astype(BF16)
            dob = do_c.astype(BF16)
            vb = v_c.astype(BF16)
            st = st_ref[0, n]
            dstn = dst_s[...]
            qb_ = q_c * jnp.exp(b_c)
            kb_ = k_c * jnp.exp(bl - b_c)
            at = jnp.zeros((cc, cc), F32)
            for i in range(nsub):
                blk_i = (c64 // HGRN_SUB) == i
                at = at + jnp.where(blk_i, lax.dot_general(kes[i], qeb, NT, preferred_element_type=F32), 0.0)
            at = jnp.where(upper, at, 0.0)
            dv = _dot(at, dob) + _dot(kb_, dstn, NT)
            dqb = _dot(dob, st)
            dkb = _dot(vb, dstn)
            da = jnp.where(r64 >= c64, lax.dot_general(dob, vb, NT, preferred_element_type=F32), 0.0)
            dat = jnp.where(upper, lax.dot_general(vb, dob, NT, preferred_element_type=F32), 0.0)
            dab = da.astype(BF16)
            dq_raw = jnp.concatenate(
                [lax.dot_general(dab[i * HGRN_SUB:(i + 1) * HGRN_SUB], kes[i], NN, preferred_element_type=F32)
                 for i in range(nsub)], axis=0)
            db = qb_.astype(BF16).astype(F32) * dqb + qeb.astype(F32) * dq_raw
            dkbk = dkb * kb_.astype(BF16).astype(F32)
            db = db - dkbk
            dk_in = jnp.zeros((cc, HEAD_DIM), F32)
            for i in range(nsub):
                blk_i = (c64 // HGRN_SUB) == i
                dk_raw = _dot(jnp.where(blk_i, dat, 0.0), qeb)
                dk_in = dk_in + dk_raw * es[i]
                db = db - kes[i].astype(F32) * dk_raw
            dqs[sl, :] = dqb * jnp.exp(b_c) + dq_raw * eq
            dks[sl, :] = dkb * jnp.exp(bl - b_c) + dk_in
            dbs[sl, :] = db
            extra = jnp.sum(dkbk, axis=0, keepdims=True) + jnp.sum(dstn * st, axis=0, keepdims=True) * ebl
            exs[sl, :] = jnp.broadcast_to(extra, (cc, HEAD_DIM))
            di_ref[sl, :] = dv.astype(BF16)
            dst_s[...] = dstn * ebl + _dot(dob, qb_, TN)

        dg = _dot_exact(triu, dbs[...]) + exs[...]
        df = dg / f - dks[...]
        df_ref[...] = (df * (1.0 - lb) * sg * (1.0 - sg)).astype(BF16)
        dq_ref[...] = (dqs[...] * (sq * (1.0 + qr * (1.0 - sq)))).astype(BF16)
        dlb = jnp.sum(df * (1.0 - sg), axis=0, keepdims=True) * (lb * (1.0 - lb))
        dlog_ref[0:1, :] += -dlb
        dlog_ref[1:2, :] += dlb

    hh = h_count
    blk = lambda off: pl.BlockSpec((t, HEAD_DIM), lambda h, i: (nb - 1 - i, off + h))
    two = pl.BlockSpec((2, HEAD_DIM), lambda h, i: (0, h))
    one = pl.BlockSpec((1, HEAD_DIM), lambda h, i: (0, h))
    act = jax.ShapeDtypeStruct((s, w), BF16)
    tile = pltpu.VMEM((t, HEAD_DIM), F32)
    return _call(
        body, name=name, grid=(hh, nb),
        in_specs=[blk(0), blk(hh), blk(2 * hh), blk(3 * hh), two, one, blk(0), blk(0),
                  pl.BlockSpec((1, nc, HEAD_DIM, HEAD_DIM), lambda h, i: (h, nb - 1 - i, 0, 0))],
        out_specs=[blk(0), blk(0), blk(0), blk(0), two, one],
        out_shape=[act, act, act, act, jax.ShapeDtypeStruct((2, w), F32), jax.ShapeDtypeStruct((1, w), F32)],
        scratch_shapes=[pltpu.VMEM((HEAD_DIM, HEAD_DIM), F32), tile, tile, tile, tile, tile, tile],
        compiler_params=_cp(("parallel", "arbitrary")),
    )(p1, p1, p1, p1, lbl, onorm, o, dy, states)


def _adamw(name, w, g, m, v):
    r, c = w.shape
    tr = r if r <= 128 else _row_block(r, 128)
    c1 = 1.0 - ADAM_B1 ** ADAM_STEP
    c2 = 1.0 - ADAM_B2 ** ADAM_STEP

    def body(w_ref, g_ref, m_ref, v_ref, d_ref, nm_ref, nv_ref):
        gv = g_ref[...]
        nm = ADAM_B1 * m_ref[...] + (1.0 - ADAM_B1) * gv
        nv = ADAM_B2 * v_ref[...] + (1.0 - ADAM_B2) * (gv * gv)
        nm_ref[...] = nm
        nv_ref[...] = nv
        d_ref[...] = -ADAM_LR * ((nm / c1) / (jnp.sqrt(nv / c2) + ADAM_EPS) + ADAM_WD * w_ref[...])

    blk = pl.BlockSpec((tr, c), lambda i: (i, 0))
    sh = jax.ShapeDtypeStruct((r, c), F32)
    return _call(body, name=name, grid=(r // tr,), in_specs=[blk] * 4, out_specs=[blk] * 3,
                 out_shape=[sh, sh, sh], compiler_params=_cp(("parallel",)))(w, g, m, v)


def _pair_add(name, g2, recv, core):
    _, nch, r, c = g2.shape
    tr = _row_block(r, 4096)

    grid_spec = pltpu.PrefetchScalarGridSpec(
        num_scalar_prefetch=1, grid=(nch, r // tr),
        in_specs=[pl.BlockSpec((1, 1, tr, c), lambda j, i, cr: (cr[0], j, i, 0)),
                  pl.BlockSpec((1, tr, c), lambda j, i, cr: (j, i, 0))],
        out_specs=pl.BlockSpec((1, tr, c), lambda j, i, cr: (j, i, 0)))

    def body(core_ref, a_ref, b_ref, o_ref):
        o_ref[...] = (a_ref[0] + b_ref[...]).astype(BF16)

    return _call(body, name=name, grid_spec=grid_spec, out_shape=jax.ShapeDtypeStruct((nch, r, c), BF16),
                 compiler_params=_cp(("parallel", "parallel")))(core, g2, recv)


def _sum_slots(name, x, out_dtype=F32):
    n, r, c = x.shape
    tr = _row_block(r, 4096)

    def body(x_ref, o_ref):
        acc = x_ref[0].astype(F32)
        for j in range(1, n):
            acc = acc + x_ref[j].astype(F32)
        o_ref[...] = acc.astype(out_dtype)

    return _call(body, name=name, grid=(r // tr,),
                 in_specs=[pl.BlockSpec((n, tr, c), lambda i: (0, i, 0))],
                 out_specs=pl.BlockSpec((tr, c), lambda i: (i, 0)),
                 out_shape=jax.ShapeDtypeStruct((r, c), out_dtype), compiler_params=_cp(("parallel",)))(x)


def _pos():
    return lax.axis_index("x"), lax.axis_index("y"), lax.axis_index("c")


def _flip(v, f):
    return (1 - v) if f else v


CHIP_FLIPS = ((0, 1), (1, 0), (1, 1))
DEV_FLIPS = tuple((fx, fy, fc) for fx in (0, 1) for fy in (0, 1) for fc in (0, 1))[1:]


def _remote(src, dst, ssem, rsem, dev):
    return pltpu.make_async_remote_copy(src_ref=src, dst_ref=dst, send_sem=ssem, recv_sem=rsem,
                                        device_id=dev, device_id_type=MESH)


def _ag_weights(name, half):
    r, c = half.shape

    def body(h_ref, big_ref, lsem, ssem, rsem):
        x, y, cc = _pos()
        me = 2 * x + y
        sib = (x, y, 1 - cc)
        mine = big_ref.at[cc, me]
        loc = pltpu.make_async_copy(h_ref, mine, lsem)
        loc.start()
        started = []
        for k, (fx, fy) in enumerate(CHIP_FLIPS):
            cp = _remote(h_ref, mine, ssem.at[k], rsem.at[k], (_flip(x, fx), _flip(y, fy), cc))
            cp.start()
            started.append(cp)
        cp = _remote(h_ref, mine, ssem.at[3], rsem.at[3], sib)
        cp.start()
        started.append(cp)
        chips = [2 * _flip(x, fx) + _flip(y, fy) for fx, fy in CHIP_FLIPS]
        for k, (fx, fy) in enumerate(CHIP_FLIPS):
            blk = big_ref.at[cc, chips[k]]
            _remote(blk, blk, ssem.at[k], rsem.at[k], (_flip(x, fx), _flip(y, fy), cc)).wait_recv()
            fw = _remote(blk, blk, ssem.at[4 + k], rsem.at[4 + k], sib)
            fw.start()
            started.append(fw)
        theirs = big_ref.at[1 - cc, me]
        _remote(theirs, theirs, ssem.at[3], rsem.at[3], sib).wait_recv()
        for k in range(3):
            blk = big_ref.at[1 - cc, chips[k]]
            _remote(blk, blk, ssem.at[4 + k], rsem.at[4 + k], sib).wait_recv()
        for cp in started:
            cp.wait_send()
        loc.wait()

    return _call(body, name=name, in_specs=[ANY], out_specs=ANY,
                 out_shape=jax.ShapeDtypeStruct((2, 4, r, c), half.dtype),
                 scratch_shapes=[pltpu.SemaphoreType.DMA, pltpu.SemaphoreType.DMA((7,)), pltpu.SemaphoreType.DMA((7,))])(half)


def _ag_chips(name, v):
    r, c = v.shape

    def body(v_ref, out_ref, lsem, ssem, rsem):
        x, y, cc = _pos()
        me = 2 * x + y
        loc = pltpu.make_async_copy(v_ref, out_ref.at[me], lsem)
        loc.start()
        started = []
        for k, (fx, fy) in enumerate(CHIP_FLIPS):
            cp = _remote(v_ref, out_ref.at[me], ssem.at[k], rsem.at[k], (_flip(x, fx), _flip(y, fy), cc))
            cp.start()
            started.append(cp)
        for k, (fx, fy) in enumerate(CHIP_FLIPS):
            blk = out_ref.at[2 * _flip(x, fx) + _flip(y, fy)]
            _remote(blk, blk, ssem.at[k], rsem.at[k], (_flip(x, fx), _flip(y, fy), cc)).wait_recv()
        for cp in started:
            cp.wait_send()
        loc.wait()

    return _call(body, name=name, in_specs=[ANY], out_specs=ANY,
                 out_shape=jax.ShapeDtypeStruct((4, r, c), v.dtype),
                 scratch_shapes=[pltpu.SemaphoreType.DMA, pltpu.SemaphoreType.DMA((3,)), pltpu.SemaphoreType.DMA((3,))])(v)


def _ag_devices(name, v):
    r, c = v.shape

    def body(v_ref, out_ref, lsem, ssem, rsem):
        x, y, cc = _pos()
        me = 4 * x + 2 * y + cc
        loc = pltpu.make_async_copy(v_ref, out_ref.at[me], lsem)
        loc.start()
        started = []
        for k, (fx, fy, fc) in enumerate(DEV_FLIPS):
            cp = _remote(v_ref, out_ref.at[me], ssem.at[k], rsem.at[k], (_flip(x, fx), _flip(y, fy), _flip(cc, fc)))
            cp.start()
            started.append(cp)
        for k, (fx, fy, fc) in enumerate(DEV_FLIPS):
            px, py, pc = _flip(x, fx), _flip(y, fy), _flip(cc, fc)
            blk = out_ref.at[4 * px + 2 * py + pc]
            _remote(blk, blk, ssem.at[k], rsem.at[k], (px, py, pc)).wait_recv()
        for cp in started:
            cp.wait_send()
        loc.wait()

    return _call(body, name=name, in_specs=[ANY], out_specs=ANY,
                 out_shape=jax.ShapeDtypeStruct((8, r, c), v.dtype),
                 scratch_shapes=[pltpu.SemaphoreType.DMA, pltpu.SemaphoreType.DMA((7,)), pltpu.SemaphoreType.DMA((7,))])(v)


def _sibling_other_half(name, g2):
    _, nch, r, c = g2.shape

    def body(g_ref, out_ref, ssem, rsem):
        x, y, cc = _pos()
        cp = _remote(g_ref.at[1 - cc], out_ref, ssem, rsem, (x, y, 1 - cc))
        cp.start()
        cp.wait()

    return _call(body, name=name, in_specs=[ANY], out_specs=ANY,
                 out_shape=jax.ShapeDtypeStruct((nch, r, c), g2.dtype),
                 scratch_shapes=[pltpu.SemaphoreType.DMA, pltpu.SemaphoreType.DMA])(g2)


def _scatter_chips(name, p):
    nch, r, c = p.shape

    def body(p_ref, out_ref, lsem, ssem, rsem):
        x, y, cc = _pos()
        me = 2 * x + y
        loc = pltpu.make_async_copy(p_ref.at[me], out_ref.at[me], lsem)
        loc.start()
        started = []
        for k, (fx, fy) in enumerate(CHIP_FLIPS):
            px, py = _flip(x, fx), _flip(y, fy)
            cp = _remote(p_ref.at[2 * px + py], out_ref.at[me], ssem.at[k], rsem.at[k], (px, py, cc))
            cp.start()
            started.append(cp)
        for k, (fx, fy) in enumerate(CHIP_FLIPS):
            px, py = _flip(x, fx), _flip(y, fy)
            blk = out_ref.at[2 * px + py]
            _remote(blk, blk, ssem.at[k], rsem.at[k], (px, py, cc)).wait_recv()
        for cp in started:
            cp.wait_send()
        loc.wait()

    return _call(body, name=name, in_specs=[ANY], out_specs=ANY,
                 out_shape=jax.ShapeDtypeStruct((nch, r, c), p.dtype),
                 scratch_shapes=[pltpu.SemaphoreType.DMA, pltpu.SemaphoreType.DMA((3,)), pltpu.SemaphoreType.DMA((3,))])(p)


def _sibling_join(name, mine):
    r, c = mine.shape

    def body(m_ref, out_ref, lsem, ssem, rsem):
        x, y, cc = _pos()
        loc = pltpu.make_async_copy(m_ref, out_ref.at[cc], lsem)
        loc.start()
        cp = _remote(m_ref, out_ref.at[cc], ssem, rsem, (x, y, 1 - cc))
        cp.start()
        theirs = out_ref.at[1 - cc]
        _remote(theirs, theirs, ssem, rsem, (x, y, 1 - cc)).wait_recv()
        cp.wait_send()
        loc.wait()

    return _call(body, name=name, in_specs=[ANY], out_specs=ANY,
                 out_shape=jax.ShapeDtypeStruct((2, r, c), mine.dtype),
                 scratch_shapes=[pltpu.SemaphoreType.DMA, pltpu.SemaphoreType.DMA, pltpu.SemaphoreType.DMA])(mine)


def _pad_lanes(a, width=128):
    return jnp.pad(a, ((0, 0), (0, width - a.shape[1])))


def kernel(x, norm_gains, fox_w_in, fox_b_f, hgrn_w_in, hgrn_lb_logits, hgrn_onorm, w_out, final_gain, loss_target, m_norm_gains, m_fox_w_in, m_fox_b_f, m_hgrn_w_in, m_hgrn_lb_logits, m_hgrn_onorm, m_w_out, m_final_gain, v_norm_gains, v_fox_w_in, v_fox_b_f, v_hgrn_w_in, v_hgrn_lb_logits, v_hgrn_onorm, v_w_out, v_final_gain):
    s, d = x.shape[1], x.shape[2]
    wq = w_out.shape[1]
    w = 4 * wq
    hh = w // HEAD_DIM
    fox_cols = fox_w_in.shape[2]
    assert 4 * fox_cols == 4 * w + hh and hgrn_w_in.shape[2] == w
    core = lax.axis_index("c")

    x0 = x[0]
    tgt = loss_target[0]

    n_fox, n_hgrn, n_out = d * fox_cols, d * w, 2 * wq * d
    shard = jnp.concatenate([fox_w_in.reshape(-1), hgrn_w_in.reshape(-1), w_out.reshape(-1)]).astype(BF16)
    total = n_fox + n_hgrn + n_out
    assert total % 256 == 0
    rows_half = total // 256
    halves = shard.reshape(2, rows_half, 128)
    big = _ag_weights("ag_weights", lax.dynamic_index_in_dim(halves, core, 0, keepdims=False))
    flat = big.transpose(1, 0, 2, 3).reshape(4, total)
    fox_full = flat[:, :n_fox].reshape(4, d, fox_cols).transpose(1, 0, 2).reshape(d, 4 * fox_cols)
    w_main = jnp.concatenate([fox_full[:, :3 * w], fox_full[:, 3 * w + hh:]], axis=1)
    w_fl = _pad_lanes(fox_full[:, 3 * w:3 * w + hh])
    w_h = flat[:, n_fox:n_fox + n_hgrn].reshape(4, d, w).transpose(1, 0, 2).reshape(d, 4 * w)
    w_o = flat[:, n_fox + n_hgrn:].reshape(4, 2, wq, d).transpose(1, 0, 2, 3).reshape(2, w, d)
    onorm_full = _ag_chips("ag_onorm", hgrn_onorm.reshape(wq // 128, 128)).reshape(1, w)

    bf_pad = _pad_lanes(fox_b_f)
    g0, g1 = norm_gains[0:1], norm_gains[1:2]
    gf = final_gain.reshape(1, d)

    h0 = _rms_fwd("rms0_fwd", x0, g0)
    qkv = _mm("fox_qkv", h0, w_main, out_dtype=BF16, n=3 * w)
    gate0 = _mm("fox_gate", h0, w_main, n=w, b_off=3 * w)
    fl = _mm("fox_flogit", h0, w_fl)
    c_all, c_rep = _fox_gate_fwd("fox_cumsum", fl, bf_pad, w)
    c_row = c_all[:, :hh].T.reshape(hh, 1, s)
    o0, lse_rep = _attn_fwd("fox_attn_fwd", qkv, c_row, hh)
    y0 = _fox_post("fox_post", o0, gate0)
    x1 = _mm("fox_out", y0, w_o[0], res=x0)
    h1 = _rms_fwd("rms1_fwd", x1, g1)
    p1 = _mm("hgrn_in", h1, w_h)
    y1, o1, states = _hgrn_fwd("hgrn_fwd", p1, hgrn_lb_logits, onorm_full, hh)
    x2 = _mm("hgrn_out", y1, w_o[1], res=x1)
    dx2, d_gf, loss_tile = _loss_head("loss_head", x2, tgt, gf)

    dy1 = _mm("hgrn_out_dy", dx2, w_o[1], tb=True)
    d_wo1 = _mm("hgrn_out_dw", y1, dx2, ta=True)
    dq1, df1, di1, dgate1, d_lbl, d_onorm = _hgrn_bwd("hgrn_bwd", p1, hgrn_lb_logits, onorm_full, o1, dy1, states, hh)
    dp1 = jnp.concatenate([dq1, df1, di1, dgate1], axis=1)
    dh1 = _mm("hgrn_in_dh", dp1, w_h, tb=True)
    d_wh = _mm("hgrn_in_dw", h1, dp1, ta=True)
    dx1, d_g1 = _rms_bwd("rms1_bwd", x1, g1, dh1, dx2)
    dy0 = _mm("fox_out_dy", dx1, w_o[0], tb=True)
    d_wo0 = _mm("fox_out_dw", y0, dx1, ta=True)
    do0, dgate0, delta_rep = _fox_pre_bwd("fox_pre_bwd", dy0, o0, gate0)
    lse_row = lse_rep[:, ::HEAD_DIM].T.reshape(hh, 1, s)
    delta_row = delta_rep[:, ::HEAD_DIM].T.reshape(hh, 1, s)
    dq0, rowsum_rep = _attn_dq("fox_attn_dq", qkv, c_row, do0, lse_rep, delta_rep, hh)
    dk0, dv0, dc_rep = _attn_dkv("fox_attn_dkv", qkv, do0, lse_row, delta_row, c_rep, hh)
    dfl, d_bf = _fox_gate_bwd("fox_cumsum_bwd", _pad_lanes(rowsum_rep[:, ::HEAD_DIM]),
                              _pad_lanes(dc_rep[:, ::HEAD_DIM]), fl, bf_pad)
    dp0 = jnp.concatenate([dq0, dk0, dv0, dgate0], axis=1)
    dh0 = _mm("fox_in_dh", dp0, w_main, tb=True, res=_mm("fox_fl_dh", dfl, w_fl, tb=True))
    d_wmain = _mm("fox_in_dw", h0, dp0, ta=True)
    d_wfl = _mm("fox_fl_dw", h0, dfl, ta=True)
    grad_x, d_g0 = _rms_bwd("rms0_bwd", x0, g0, dh0, dx1)

    d_fox = jnp.concatenate([d_wmain[:, :3 * w], d_wfl[:, :hh], d_wmain[:, 3 * w:]], axis=1)
    per_chip = jnp.concatenate([
        d_fox.reshape(d, 4, fox_cols).transpose(1, 0, 2).reshape(4, n_fox),
        d_wh.reshape(d, 4, w).transpose(1, 0, 2).reshape(4, n_hgrn),
        jnp.stack([d_wo0, d_wo1]).reshape(2, 4, wq, d).transpose(1, 0, 2, 3).reshape(4, n_out)], axis=1)
    g2 = per_chip.reshape(4, 2, rows_half, 128).transpose(1, 0, 2, 3)
    from_sibling = _sibling_other_half("rs_sibling", g2)
    pair = _pair_add("rs_pair_add", g2, from_sibling, core.reshape(1).astype(jnp.int32))
    from_chips = _scatter_chips("rs_scatter", pair)
    reduced_half = _sum_slots("rs_sum", from_chips)
    reduced = _sibling_join("rs_join", reduced_half).reshape(total)
    g_fox = reduced[:n_fox].reshape(d, fox_cols)
    g_hgrn = reduced[n_fox:n_fox + n_hgrn].reshape(d, w)
    g_out = reduced[n_fox + n_hgrn:].reshape(2 * wq, d)

    small = jnp.concatenate([jnp.concatenate([d_g0, d_g1], axis=0).reshape(-1), d_lbl.reshape(-1), d_gf.reshape(-1),
                             d_onorm.reshape(-1), d_bf.reshape(-1)])
    n_small = small.shape[0]
    pad_to = -(-n_small // 1024) * 1024
    small = jnp.pad(small, (0, pad_to - n_small)).reshape(pad_to // 128, 128)
    small = _sum_slots("small_sum", _ag_devices("small_gather", small)).reshape(-1)
    g_norm = small[:2 * d].reshape(2, d)
    g_lbl = small[2 * d:2 * d + 2 * w].reshape(2, w)
    g_gf = small[2 * d + 2 * w:3 * d + 2 * w]
    g_onorm_full = small[3 * d + 2 * w:3 * d + 3 * w]
    g_bf = small[3 * d + 3 * w:3 * d + 3 * w + hh].reshape(1, hh)
    chip = 2 * lax.axis_index("x") + lax.axis_index("y")
    g_onorm = lax.dynamic_slice_in_dim(g_onorm_full, chip * wq, wq).reshape(1, wq)

    loss = lax.psum(loss_tile[0, 0], ("x", "y", "c"))

    def upd(name, wt, g, m, v):
        shp = wt.shape
        two = lambda a: a.reshape(-1, shp[-1])
        dl, nm, nv = _adamw(name, two(wt), two(g), two(m), two(v))
        return g.reshape(shp), dl.reshape(shp), nm.reshape(shp), nv.reshape(shp)

    res = [
        upd("adamw_norm_gains", norm_gains, g_norm, m_norm_gains, v_norm_gains),
        upd("adamw_fox_w_in", fox_w_in, g_fox, m_fox_w_in, v_fox_w_in),
        upd("adamw_fox_b_f", fox_b_f, g_bf, m_fox_b_f, v_fox_b_f),
        upd("adamw_hgrn_w_in", hgrn_w_in, g_hgrn, m_hgrn_w_in, v_hgrn_w_in),
        upd("adamw_lb_logits", hgrn_lb_logits, g_lbl, m_hgrn_lb_logits, v_hgrn_lb_logits),
        upd("adamw_onorm", hgrn_onorm, g_onorm, m_hgrn_onorm, v_hgrn_onorm),
        upd("adamw_w_out", w_out, g_out, m_w_out, v_w_out),
        upd("adamw_final_gain", final_gain.reshape(1, d), g_gf.reshape(1, d), m_final_gain.reshape(1, d),
            v_final_gain.reshape(1, d)),
    ]
    res[-1] = tuple(a.reshape(d) for a in res[-1])
    grads, deltas, new_m, new_v = zip(*res)
    return (loss, grad_x[None], *grads, *deltas, *new_m, *new_v)
```

```python
import functools

import jax
import jax.numpy as jnp
from jax import lax
from jax.experimental import pallas as pl
from jax.experimental.pallas import tpu as pltpu

F32 = jnp.float32
BF16 = jnp.bfloat16
MESH = pl.DeviceIdType.MESH
ANY = pl.BlockSpec(memory_space=pl.ANY)

EPS = 1e-6
HEAD_DIM = 128
HGRN_CHUNK = 64
HGRN_SUB = 16
EXP_CLAMP = 80.0
ATT_BLOCK = 512
HGRN_BLOCK = 512
GATE_BLOCK = 512
ROW_BLOCK = 256
MM_TM, MM_TN, MM_TK = 1024, 1024, 2048
VMEM_LIMIT_V7X = 56 * 1024 * 1024

ADAM_LR, ADAM_B1, ADAM_B2, ADAM_EPS, ADAM_WD, ADAM_STEP = 0.001, 0.9, 0.999, 1e-08, 0.01, 10

NT = (((1,), (1,)), ((), ()))
TN = (((0,), (0,)), ((), ()))
NN = (((1,), (0,)), ((), ()))


def _call(body, **kw):
    return pl.pallas_call(body, **kw)


def _cp(dims=None):
    kw = dict(vmem_limit_bytes=VMEM_LIMIT_V7X)
    if dims is not None:
        kw["dimension_semantics"] = dims
    return pltpu.CompilerParams(**kw)


def _sigmoid(x):
    return 1.0 / (1.0 + jnp.exp(-x))


def _dot(a, b, dn=NN):
    return lax.dot_general(a.astype(BF16), b.astype(BF16), dn, preferred_element_type=F32)


def _split3(x):
    hi = x.astype(BF16)
    r1 = x - hi.astype(F32)
    mid = r1.astype(BF16)
    lo = (r1 - mid.astype(F32)).astype(BF16)
    return hi, mid, lo


def _dot_exact(m01, x, right=False):
    hi, mid, lo = _split3(x)
    if right:
        dot = lambda p: lax.dot_general(p, m01, NN, preferred_element_type=F32)
    else:
        dot = lambda p: lax.dot_general(m01, p, NN, preferred_element_type=F32)
    return dot(hi) + dot(mid) + dot(lo)


def _row_block(rows, cap):
    if rows <= cap:
        return rows
    best = None
    for t in range(16, cap + 1, 16):
        if rows % t == 0:
            best = t
    assert best is not None, rows
    return best


def _mm(name, a, b, *, ta=False, tb=False, out_dtype=F32, res=None, n=None, b_off=0):
    m, k = (a.shape[1], a.shape[0]) if ta else a.shape
    n_full = b.shape[0] if tb else b.shape[1]
    n = n_full if n is None else n
    tm, tn, tk = min(MM_TM, m), min(MM_TN, n), min(MM_TK, k)
    assert m % tm == 0 and n % tn == 0 and k % tk == 0 and b_off % tn == 0
    nk = k // tk
    jo = b_off // tn
    a_spec = pl.BlockSpec((tk, tm), lambda i, j, kk: (kk, i)) if ta else pl.BlockSpec((tm, tk), lambda i, j, kk: (i, kk))
    b_spec = pl.BlockSpec((tn, tk), lambda i, j, kk: (j + jo, kk)) if tb else pl.BlockSpec((tk, tn), lambda i, j, kk: (kk, j + jo))
    o_spec = pl.BlockSpec((tm, tn), lambda i, j, kk: (i, j))
    dn = (((0 if ta else 1,), (1 if tb else 0,)), ((), ()))
    has_res = res is not None

    def body(*refs):
        if has_res:
            a_ref, b_ref, r_ref, o_ref, acc = refs
        else:
            a_ref, b_ref, o_ref, acc = refs
            r_ref = None
        kk = pl.program_id(2)
        p = lax.dot_general(a_ref[...].astype(BF16), b_ref[...].astype(BF16), dn, preferred_element_type=F32)

        def finish(total):
            if r_ref is not None:
                total = total + r_ref[...]
            o_ref[...] = total.astype(out_dtype)

        if nk == 1:
            finish(p)
        else:
            @pl.when(kk == 0)
            def _():
                acc[...] = p

            @pl.when(jnp.logical_and(kk > 0, kk < nk - 1))
            def _():
                acc[...] += p

            @pl.when(kk == nk - 1)
            def _():
                finish(acc[...] + p)

    ins = [a, b] + ([res] if has_res else [])
    in_specs = [a_spec, b_spec] + ([o_spec] if has_res else [])
    return _call(
        body, name=name, grid=(m // tm, n // tn, nk), in_specs=in_specs, out_specs=o_spec,
        out_shape=jax.ShapeDtypeStruct((m, n), out_dtype),
        scratch_shapes=[pltpu.VMEM((tm, tn) if nk > 1 else (8, 128), F32)],
        compiler_params=_cp(("parallel", "parallel", "arbitrary")),
    )(*ins)


def _rms_fwd(name, x, g):
    s, d = x.shape
    tm = min(ROW_BLOCK, s)

    def body(x_ref, g_ref, h_ref):
        xv = x_ref[...]
        r = lax.rsqrt(jnp.mean(xv * xv, axis=-1, keepdims=True) + EPS)
        h_ref[...] = (xv * r * g_ref[...]).astype(BF16)

    row = pl.BlockSpec((tm, d), lambda i: (i, 0))
    vec = pl.BlockSpec((1, d), lambda i: (0, 0))
    return _call(body, name=name, grid=(s // tm,), in_specs=[row, vec], out_specs=row,
                 out_shape=jax.ShapeDtypeStruct((s, d), BF16), compiler_params=_cp(("parallel",)))(x, g)


def _rms_bwd(name, x, g, dh, dres):
    s, d = x.shape
    tm = min(ROW_BLOCK, s)

    def body(x_ref, g_ref, dh_ref, dres_ref, dx_ref, dg_ref):
        @pl.when(pl.program_id(0) == 0)
        def _():
            dg_ref[...] = jnp.zeros_like(dg_ref)

        xv = x_ref[...]
        r = lax.rsqrt(jnp.mean(xv * xv, axis=-1, keepdims=True) + EPS)
        xn = xv * r
        dhv = dh_ref[...]
        dxn = dhv * g_ref[...]
        dx_ref[...] = dres_ref[...] + r * (dxn - xn * jnp.mean(dxn * xn, axis=-1, keepdims=True))
        dg_ref[...] += jnp.sum(dhv * xn, axis=0, keepdims=True)

    row = pl.BlockSpec((tm, d), lambda i: (i, 0))
    vec = pl.BlockSpec((1, d), lambda i: (0, 0))
    return _call(body, name=name, grid=(s // tm,), in_specs=[row, vec, row, row], out_specs=[row, vec],
                 out_shape=[jax.ShapeDtypeStruct((s, d), F32), jax.ShapeDtypeStruct((1, d), F32)],
                 compiler_params=_cp(("arbitrary",)))(x, g, dh, dres)


def _loss_head(name, x, tgt, g):
    s, d = x.shape
    tm = min(ROW_BLOCK, s)
    nb = s // tm

    def body(x_ref, t_ref, g_ref, dx_ref, dg_ref, loss_ref, lacc):
        i = pl.program_id(0)

        @pl.when(i == 0)
        def _():
            dg_ref[...] = jnp.zeros_like(dg_ref)
            lacc[...] = jnp.zeros_like(lacc)

        xv = x_ref[...]
        gv = g_ref[...]
        r = lax.rsqrt(jnp.mean(xv * xv, axis=-1, keepdims=True) + EPS)
        xn = xv * r
        err = xn * gv - t_ref[...]
        lacc[...] += jnp.sum(err * err, axis=0, keepdims=True)
        dout = err * (1.0 / d)
        dg_ref[...] += jnp.sum(dout * xn, axis=0, keepdims=True)
        dxn = dout * gv
        dx_ref[...] = r * (dxn - xn * jnp.mean(dxn * xn, axis=-1, keepdims=True))

        @pl.when(i == nb - 1)
        def _():
            total = jnp.sum(lacc[...], axis=1, keepdims=True) * (0.5 / d)
            loss_ref[...] = jnp.broadcast_to(total, loss_ref.shape)

    row = pl.BlockSpec((tm, d), lambda i: (i, 0))
    vec = pl.BlockSpec((1, d), lambda i: (0, 0))
    one = pl.BlockSpec((1, 128), lambda i: (0, 0))
    return _call(body, name=name, grid=(nb,), in_specs=[row, row, vec], out_specs=[row, vec, one],
                 out_shape=[jax.ShapeDtypeStruct((s, d), F32), jax.ShapeDtypeStruct((1, d), F32),
                            jax.ShapeDtypeStruct((1, 128), F32)],
                 scratch_shapes=[pltpu.VMEM((1, d), F32)], compiler_params=_cp(("arbitrary",)))(x, tgt, g)


def _fox_gate_fwd(name, fl, bf, w):
    s = fl.shape[0]
    tb = min(GATE_BLOCK, s)

    def body(fl_ref, bf_ref, c_ref, crep_ref, carry):
        @pl.when(pl.program_id(0) == 0)
        def _():
            carry[...] = jnp.zeros_like(carry)

        z = fl_ref[...] + bf_ref[...]
        lf = jnp.minimum(z, 0.0) - jnp.log(1.0 + jnp.exp(-jnp.abs(z)))
        rows = lax.broadcasted_iota(jnp.int32, (tb, tb), 0)
        cols = lax.broadcasted_iota(jnp.int32, (tb, tb), 1)
        tri = (rows >= cols).astype(BF16)
        cs = _dot_exact(tri, lf) + carry[...]
        c_ref[...] = cs
        carry[...] = c_ref[tb - 1:tb, :]
        sel_r = lax.broadcasted_iota(jnp.int32, (128, w), 0)
        sel_c = lax.broadcasted_iota(jnp.int32, (128, w), 1)
        sel = (sel_r == sel_c // HEAD_DIM).astype(BF16)
        crep_ref[...] = _dot_exact(sel, cs, right=True)

    blk = pl.BlockSpec((tb, 128), lambda i: (i, 0))
    return _call(body, name=name, grid=(s // tb,),
                 in_specs=[blk, pl.BlockSpec((1, 128), lambda i: (0, 0))],
                 out_specs=[blk, pl.BlockSpec((tb, w), lambda i: (i, 0))],
                 out_shape=[jax.ShapeDtypeStruct((s, 128), F32), jax.ShapeDtypeStruct((s, w), F32)],
                 scratch_shapes=[pltpu.VMEM((1, 128), F32)], compiler_params=_cp(("arbitrary",)))(fl, bf)


def _fox_gate_bwd(name, drow, dcol, fl, bf):
    s = fl.shape[0]
    tb = min(GATE_BLOCK, s)
    nb = s // tb

    def body(dr_ref, dc_ref, fl_ref, bf_ref, dfl_ref, dbf_ref, carry, tmp):
        @pl.when(pl.program_id(0) == 0)
        def _():
            carry[...] = jnp.zeros_like(carry)
            dbf_ref[...] = jnp.zeros_like(dbf_ref)

        rows = lax.broadcasted_iota(jnp.int32, (tb, tb), 0)
        cols = lax.broadcasted_iota(jnp.int32, (tb, tb), 1)
        triu = (rows <= cols).astype(BF16)
        dlf = _dot_exact(triu, dr_ref[...] + dc_ref[...]) + carry[...]
        tmp[...] = dlf
        carry[...] = tmp[0:1, :]
        z = fl_ref[...] + bf_ref[...]
        dfl = dlf * (1.0 / (1.0 + jnp.exp(z)))
        dfl_ref[...] = dfl
        dbf_ref[...] += jnp.sum(dfl, axis=0, keepdims=True)

    blk = pl.BlockSpec((tb, 128), lambda i: (nb - 1 - i, 0))
    vec = pl.BlockSpec((1, 128), lambda i: (0, 0))
    return _call(body, name=name, grid=(nb,), in_specs=[blk, blk, blk, vec], out_specs=[blk, vec],
                 out_shape=[jax.ShapeDtypeStruct((s, 128), F32), jax.ShapeDtypeStruct((1, 128), F32)],
                 scratch_shapes=[pltpu.VMEM((1, 128), F32), pltpu.VMEM((tb, 128), F32)],
                 compiler_params=_cp(("arbitrary",)))(drow, dcol, fl, bf)


def _attn_fwd(name, qkv, c_rep, h_count):
    s = qkv.shape[0]
    w = h_count * HEAD_DIM
    t = min(ATT_BLOCK, s)
    scale = HEAD_DIM ** -0.5

    def body(q_ref, k_ref, v_ref, c_ref, o_ref, lse_ref, m_s, l_s, acc_s):
        qi = pl.program_id(1)
        q = q_ref[...]
        m_s[...] = jnp.full(m_s.shape, -jnp.inf, F32)
        l_s[...] = jnp.zeros_like(l_s)
        acc_s[...] = jnp.zeros_like(acc_s)

        def step(kb, masked):
            off = pl.multiple_of(kb * t, t)
            kk = k_ref[pl.ds(off, t), :]
            vv = v_ref[pl.ds(off, t), :]
            cc = jnp.tile(c_ref[pl.ds(off, t), :], (1, t // HEAD_DIM))
            st = lax.dot_general(kk, q, NT, preferred_element_type=F32) * scale - cc
            if masked:
                rows = lax.broadcasted_iota(jnp.int32, (t, t), 0)
                cols = lax.broadcasted_iota(jnp.int32, (t, t), 1)
                st = jnp.where(cols >= rows, st, -jnp.inf)
            m_prev = m_s[...]
            m_new = jnp.maximum(m_prev, jnp.max(st, axis=0, keepdims=True))
            pt = jnp.exp(st - m_new)
            alpha = jnp.exp(m_prev - m_new)
            l_s[...] = alpha * l_s[...] + jnp.sum(pt, axis=0, keepdims=True)
            acc_s[...] = alpha * acc_s[...] + lax.dot_general(vv, pt.astype(BF16), TN, preferred_element_type=F32)
            m_s[...] = m_new

        def loop_body(kb, carry):
            step(kb, False)
            return carry

        lax.fori_loop(0, qi, loop_body, 0)
        step(qi, True)
        l = l_s[...]
        o_ref[...] = (acc_s[...] / l).T
        lse_ref[0] = m_s[...] + jnp.log(l)

    hh = h_count
    blk = lambda off: pl.BlockSpec((t, HEAD_DIM), lambda h, i: (i, off + h))
    whole = lambda off: pl.BlockSpec((s, HEAD_DIM), lambda h, i: (0, off + h))
    rowv = pl.BlockSpec((1, 1, t), lambda h, i: (h, 0, i))
    return _call(
        body, name=name, grid=(hh, s // t),
        in_specs=[blk(0), whole(hh), whole(2 * hh), whole(0)],
        out_specs=[blk(0), rowv],
        out_shape=[jax.ShapeDtypeStruct((s, w), F32), jax.ShapeDtypeStruct((hh, 1, s), F32)],
        scratch_shapes=[pltpu.VMEM((1, t), F32), pltpu.VMEM((1, t), F32), pltpu.VMEM((HEAD_DIM, t), F32)],
        compiler_params=_cp(("parallel", "arbitrary")),
    )(qkv, qkv, qkv, c_rep)


def _attn_dq(name, qkv, c_rep, do, lse_row, delta_row, h_count):
    s = qkv.shape[0]
    w = h_count * HEAD_DIM
    t = min(ATT_BLOCK, s)
    scale = HEAD_DIM ** -0.5

    def body(q_ref, k_ref, v_ref, c_ref, do_ref, lse_ref, dl_ref, dq_ref, rs_ref, acc_s, rs_s):
        qi = pl.program_id(1)
        q = q_ref[...]
        dov = do_ref[...]
        lse = lse_ref[0]
        delta = dl_ref[0]
        acc_s[...] = jnp.zeros_like(acc_s)
        rs_s[...] = jnp.zeros_like(rs_s)

        def step(kb, masked):
            off = pl.multiple_of(kb * t, t)
            kk = k_ref[pl.ds(off, t), :]
            vv = v_ref[pl.ds(off, t), :]
            cc = jnp.tile(c_ref[pl.ds(off, t), :], (1, t // HEAD_DIM))
            st = lax.dot_general(kk, q, NT, preferred_element_type=F32) * scale - cc
            pt = jnp.exp(st - lse)
            if masked:
                rows = lax.broadcasted_iota(jnp.int32, (t, t), 0)
                cols = lax.broadcasted_iota(jnp.int32, (t, t), 1)
                pt = jnp.where(cols >= rows, pt, 0.0)
            dpt = lax.dot_general(vv, dov, NT, preferred_element_type=F32)
            dst = pt * (dpt - delta)
            acc_s[...] += lax.dot_general(kk, dst.astype(BF16), TN, preferred_element_type=F32)
            rs_s[...] += jnp.sum(dst, axis=0, keepdims=True)

        def loop_body(kb, carry):
            step(kb, False)
            return carry

        lax.fori_loop(0, qi, loop_body, 0)
        step(qi, True)
        dq_ref[...] = (acc_s[...] * scale).T.astype(BF16)
        rs_ref[0] = rs_s[...]

    hh = h_count
    blk = pl.BlockSpec((t, HEAD_DIM), lambda h, i: (i, h))
    whole = lambda off: pl.BlockSpec((s, HEAD_DIM), lambda h, i: (0, off + h))
    rowv = pl.BlockSpec((1, 1, t), lambda h, i: (h, 0, i))
    return _call(
        body, name=name, grid=(hh, s // t),
        in_specs=[blk, whole(hh), whole(2 * hh), whole(0), blk, rowv, rowv],
        out_specs=[blk, rowv],
        out_shape=[jax.ShapeDtypeStruct((s, w), BF16), jax.ShapeDtypeStruct((hh, 1, s), F32)],
        scratch_shapes=[pltpu.VMEM((HEAD_DIM, t), F32), pltpu.VMEM((1, t), F32)],
        compiler_params=_cp(("parallel", "arbitrary")),
    )(qkv, qkv, qkv, c_rep, do, lse_row, delta_row)


def _attn_dkv(name, qkv, do, lse_row, delta_row, c_rep, h_count):
    s = qkv.shape[0]
    w = h_count * HEAD_DIM
    t = min(ATT_BLOCK, s)
    nq = s // t
    scale = HEAD_DIM ** -0.5

    def body(k_ref, v_ref, q_ref, do_ref, lse_ref, dl_ref, c_ref, dk_ref, dv_ref, dc_ref, dk_s, dv_s, dc_s):
        kj = pl.program_id(1)
        kk = k_ref[...]
        vv = v_ref[...]
        ccol = jnp.tile(c_ref[...], (1, t // HEAD_DIM))
        dk_s[...] = jnp.zeros_like(dk_s)
        dv_s[...] = jnp.zeros_like(dv_s)
        dc_s[...] = jnp.zeros_like(dc_s)

        def step(qb, masked):
            off = pl.multiple_of(qb * t, t)
            q = q_ref[pl.ds(off, t), :]
            dov = do_ref[pl.ds(off, t), :]
            st = lax.dot_general(kk, q, NT, preferred_element_type=F32) * scale - ccol
            pt = jnp.exp(st - lse_ref[0, :, pl.ds(off, t)])
            if masked:
                rows = lax.broadcasted_iota(jnp.int32, (t, t), 0)
                cols = lax.broadcasted_iota(jnp.int32, (t, t), 1)
                pt = jnp.where(cols >= rows, pt, 0.0)
            dv_s[...] += lax.dot_general(pt.astype(BF16), dov, NN, preferred_element_type=F32)
            dpt = lax.dot_general(vv, dov, NT, preferred_element_type=F32)
            dst = pt * (dpt - dl_ref[0, :, pl.ds(off, t)])
            dk_s[...] += lax.dot_general(dst.astype(BF16), q, NN, preferred_element_type=F32)
            dc_s[...] += jnp.sum(dst, axis=1, keepdims=True)

        step(kj, True)

        def loop_body(qb, carry):
            step(qb, False)
            return carry

        lax.fori_loop(kj + 1, nq, loop_body, 0)
        dk_ref[...] = (dk_s[...] * scale).astype(BF16)
        dv_ref[...] = dv_s[...].astype(BF16)
        dc_ref[...] = jnp.broadcast_to(-dc_s[...], dc_ref.shape)

    hh = h_count
    blk = lambda off: pl.BlockSpec((t, HEAD_DIM), lambda h, j: (j, off + h))
    whole = pl.BlockSpec((s, HEAD_DIM), lambda h, j: (0, h))
    rowv = pl.BlockSpec((1, 1, s), lambda h, j: (h, 0, 0))
    return _call(
        body, name=name, grid=(hh, nq),
        in_specs=[blk(hh), blk(2 * hh), whole, whole, rowv, rowv, blk(0)],
        out_specs=[blk(0), blk(0), blk(0)],
        out_shape=[jax.ShapeDtypeStruct((s, w), BF16), jax.ShapeDtypeStruct((s, w), BF16),
                   jax.ShapeDtypeStruct((s, w), F32)],
        scratch_shapes=[pltpu.VMEM((t, HEAD_DIM), F32), pltpu.VMEM((t, HEAD_DIM), F32), pltpu.VMEM((t, 1), F32)],
        compiler_params=_cp(("parallel", "arbitrary")),
    )(qkv, qkv, qkv, do, lse_row, delta_row, c_rep)


def _fox_post(name, o, gate):
    s, w = o.shape
    tm = min(ROW_BLOCK, s)

    def body(o_ref, g_ref, y_ref):
        g = g_ref[...]
        y_ref[...] = (o_ref[...] * (g * _sigmoid(g))).astype(BF16)

    row = pl.BlockSpec((tm, w), lambda i: (i, 0))
    return _call(body, name=name, grid=(s // tm,), in_specs=[row, row], out_specs=row,
                 out_shape=jax.ShapeDtypeStruct((s, w), BF16), compiler_params=_cp(("parallel",)))(o, gate)


def _fox_pre_bwd(name, dy, o, gate):
    s, w = o.shape
    tm = min(ROW_BLOCK, s)

    def body(dy_ref, o_ref, g_ref, do_ref, dg_ref, dl_ref):
        g = g_ref[...]
        sg = _sigmoid(g)
        dyv = dy_ref[...]
        ov = o_ref[...]
        dov = dyv * (g * sg)
        do_ref[...] = dov.astype(BF16)
        dg_ref[...] = (dyv * ov * (sg * (1.0 + g * (1.0 - sg)))).astype(BF16)
        prod = dov * ov
        for h in range(w // HEAD_DIM):
            sl = slice(h * HEAD_DIM, (h + 1) * HEAD_DIM)
            dl_ref[:, sl] = jnp.broadcast_to(jnp.sum(prod[:, sl], axis=1, keepdims=True), (tm, HEAD_DIM))

    row = pl.BlockSpec((tm, w), lambda i: (i, 0))
    return _call(body, name=name, grid=(s // tm,), in_specs=[row, row, row], out_specs=[row, row, row],
                 out_shape=[jax.ShapeDtypeStruct((s, w), BF16), jax.ShapeDtypeStruct((s, w), BF16),
                            jax.ShapeDtypeStruct((s, w), F32)],
                 compiler_params=_cp(("parallel",)))(dy, o, gate)


def _hgrn_chunk_terms(q_c, k_c, b_c, b_s, base):
    nsub = HGRN_CHUNK // HGRN_SUB
    refs = [jnp.zeros((1, HEAD_DIM), F32)]
    for i in range(1, nsub):
        r0 = base + i * HGRN_SUB - 1
        refs.append(b_s[r0:r0 + 1, :])
    rfull = jnp.concatenate([jnp.broadcast_to(r, (HGRN_SUB, HEAD_DIM)) for r in refs], axis=0)
    eq = jnp.exp(b_c - rfull)
    qe = q_c * eq
    es = [jnp.exp(jnp.minimum(r - b_c, EXP_CLAMP)) for r in refs]
    kes = [(k_c * e).astype(BF16) for e in es]
    return eq, qe, es, kes


def _hgrn_block_pre(q_ref, f_ref, lbl_ref, tri, b_s):
    lb = _sigmoid(lbl_ref[1:2, :] - lbl_ref[0:1, :])
    qr = q_ref[...]
    sq = _sigmoid(qr)
    q = qr * sq
    fz = f_ref[...]
    sg = _sigmoid(fz)
    f = lb + (1.0 - lb) * sg
    g = jnp.log(f)
    k = (1.0 - lb) * (1.0 / (1.0 + jnp.exp(fz)))
    b = _dot_exact(tri, g)
    b_s[...] = b
    return lb, qr, sq, q, sg, f, k, b


def _chunk_masks(t):
    rows = lax.broadcasted_iota(jnp.int32, (t, t), 0)
    cols = lax.broadcasted_iota(jnp.int32, (t, t), 1)
    same = (rows // HGRN_CHUNK) == (cols // HGRN_CHUNK)
    return rows, cols, same


def _hgrn_fwd(name, p1, lbl, onorm, h_count):
    s = p1.shape[0]
    w = h_count * HEAD_DIM
    t = min(HGRN_BLOCK, s)
    nc = t // HGRN_CHUNK
    nsub = HGRN_CHUNK // HGRN_SUB
    cc = HGRN_CHUNK

    def body(q_ref, f_ref, i_ref, g_ref, lbl_ref, on_ref, y_ref, o_ref, st_ref, state_s, b_s):
        @pl.when(pl.program_id(1) == 0)
        def _():
            state_s[...] = jnp.zeros_like(state_s)

        rows, cols, same = _chunk_masks(t)
        tri = jnp.logical_and(same, rows >= cols).astype(BF16)
        lb, qr, sq, q, sg, f, k, b = _hgrn_block_pre(q_ref, f_ref, lbl_ref, tri, b_s)
        v = i_ref[...]
        r64 = lax.broadcasted_iota(jnp.int32, (cc, cc), 0)
        c64 = lax.broadcasted_iota(jnp.int32, (cc, cc), 1)
        for n in range(nc):
            sl = slice(n * cc, (n + 1) * cc)
            q_c, k_c, v_c, b_c = q[sl], k[sl], v[sl], b[sl]
            bl = b_s[n * cc + cc - 1:n * cc + cc, :]
            eq, qe, es, kes = _hgrn_chunk_terms(q_c, k_c, b_c, b_s, n * cc)
            qeb = qe.astype(BF16)
            a = jnp.concatenate(
                [lax.dot_general(qeb[i * HGRN_SUB:(i + 1) * HGRN_SUB], kes[i], NT, preferred_element_type=F32)
                 for i in range(nsub)], axis=0)
            a = jnp.where(r64 >= c64, a, 0.0)
            st = state_s[...]
            st_ref[0, n] = st
            inter = _dot(q_c * jnp.exp(b_c), st, NT)
            intra = _dot(a, v_c)
            o_ref[sl, :] = inter + intra
            kb = k_c * jnp.exp(bl - b_c)
            state_s[...] = st * jnp.exp(bl) + _dot(v_c, kb, TN)
        o = o_ref[...]
        rr = lax.rsqrt(jnp.mean(o * o, axis=-1, keepdims=True) + EPS)
        gate = g_ref[...]
        y_ref[...] = ((o * rr) * on_ref[...] * (gate * _sigmoid(gate))).astype(BF16)

    hh = h_count
    blk = lambda off: pl.BlockSpec((t, HEAD_DIM), lambda h, i: (i, off + h))
    return _call(
        body, name=name, grid=(hh, s // t),
        in_specs=[blk(0), blk(hh), blk(2 * hh), blk(3 * hh),
                  pl.BlockSpec((2, HEAD_DIM), lambda h, i: (0, h)), pl.BlockSpec((1, HEAD_DIM), lambda h, i: (0, h))],
        out_specs=[blk(0), blk(0), pl.BlockSpec((1, nc, HEAD_DIM, HEAD_DIM), lambda h, i: (h, i, 0, 0))],
        out_shape=[jax.ShapeDtypeStruct((s, w), BF16), jax.ShapeDtypeStruct((s, w), F32),
                   jax.ShapeDtypeStruct((hh, s // cc, HEAD_DIM, HEAD_DIM), F32)],
        scratch_shapes=[pltpu.VMEM((HEAD_DIM, HEAD_DIM), F32), pltpu.VMEM((t, HEAD_DIM), F32)],
        compiler_params=_cp(("parallel", "arbitrary")),
    )(p1, p1, p1, p1, lbl, onorm)


def _hgrn_bwd(name, p1, lbl, onorm, o, dy, states, h_count):
    s = p1.shape[0]
    w = h_count * HEAD_DIM
    t = min(HGRN_BLOCK, s)
    nb = s // t
    nc = t // HGRN_CHUNK
    nsub = HGRN_CHUNK // HGRN_SUB
    cc = HGRN_CHUNK

    def body(q_ref, f_ref, i_ref, g_ref, lbl_ref, on_ref, o_ref, dy_ref, st_ref,
             dq_ref, df_ref, di_ref, dgate_ref, dlog_ref, don_ref,
             dst_s, b_s, do_s, dqs, dks, dbs, exs):
        @pl.when(pl.program_id(1) == 0)
        def _():
            dst_s[...] = jnp.zeros_like(dst_s)
            dlog_ref[...] = jnp.zeros_like(dlog_ref)
            don_ref[...] = jnp.zeros_like(don_ref)

        rows, cols, same = _chunk_masks(t)
        tri = jnp.logical_and(same, rows >= cols).astype(BF16)
        triu = jnp.logical_and(same, rows <= cols).astype(BF16)
        lb, qr, sq, q, sg, f, k, b = _hgrn_block_pre(q_ref, f_ref, lbl_ref, tri, b_s)
        v = i_ref[...]

        ov = o_ref[...]
        rr = lax.rsqrt(jnp.mean(ov * ov, axis=-1, keepdims=True) + EPS)
        on = ov * rr
        gate = g_ref[...]
        sgt = _sigmoid(gate)
        silu = gate * sgt
        dyv = dy_ref[...]
        gain = on_ref[...]
        dgate_ref[...] = (dyv * on * gain * (sgt * (1.0 + gate * (1.0 - sgt)))).astype(BF16)
        don_ref[...] += jnp.sum(dyv * on * silu, axis=0, keepdims=True)
        d_on = dyv * gain * silu
        do_s[...] = rr * (d_on - on * jnp.mean(d_on * on, axis=-1, keepdims=True))

        r64 = lax.broadcasted_iota(jnp.int32, (cc, cc), 0)
        c64 = lax.broadcasted_iota(jnp.int32, (cc, cc), 1)
        upper = r64 <= c64
        for n in reversed(range(nc)):
            sl = slice(n * cc, (n + 1) * cc)
            q_c, k_c, v_c, b_c = q[sl], k[sl], v[sl], b[sl]
            do_c = do_s[sl, :]
            bl = b_s[n * cc + cc - 1:n * cc + cc, :]
            ebl = jnp.exp(bl)
            eq, qe, es, kes = _hgrn_chunk_terms(q_c, k_c, b_c, b_s, n * cc)
            qeb = qe.astype(BF16)
            dob = do_c.astype(BF16)
            vb = v_c.astype(BF16)
            st = st_ref[0, n]
            dstn = dst_s[...]
            qb_ = q_c * jnp.exp(b_c)
            kb_ = k_c * jnp.exp(bl - b_c)
            at = jnp.zeros((cc, cc), F32)
            for i in range(nsub):
                blk_i = (c64 // HGRN_SUB) == i
                at = at + jnp.where(blk_i, lax.dot_general(kes[i], qeb, NT, preferred_element_type=F32), 0.0)
            at = jnp.where(upper, at, 0.0)
            dv = _dot(at, dob) + _dot(kb_, dstn, NT)
            dqb = _dot(dob, st)
            dkb = _dot(vb, dstn)
            da = jnp.where(r64 >= c64, lax.dot_general(dob, vb, NT, preferred_element_type=F32), 0.0)
            dat = jnp.where(upper, lax.dot_general(vb, dob, NT, preferred_element_type=F32), 0.0)
            dab = da.astype(BF16)
            dq_raw = jnp.concatenate(
                [lax.dot_general(dab[i * HGRN_SUB:(i + 1) * HGRN_SUB], kes[i], NN, preferred_element_type=F32)
                 for i in range(nsub)], axis=0)
            db = qb_.astype(BF16).astype(F32) * dqb + qeb.astype(F32) * dq_raw
            dkbk = dkb * kb_.astype(BF16).astype(F32)
            db = db - dkbk
            dk_in = jnp.zeros((cc, HEAD_DIM), F32)
            for i in range(nsub):
                blk_i = (c64 // HGRN_SUB) == i
                dk_raw = _dot(jnp.where(blk_i, dat, 0.0), qeb)
                dk_in = dk_in + dk_raw * es[i]
                db = db - kes[i].astype(F32) * dk_raw
            dqs[sl, :] = dqb * jnp.exp(b_c) + dq_raw * eq
            dks[sl, :] = dkb * jnp.exp(bl - b_c) + dk_in
            dbs[sl, :] = db
            extra = jnp.sum(dkbk, axis=0, keepdims=True) + jnp.sum(dstn * st, axis=0, keepdims=True) * ebl
            exs[sl, :] = jnp.broadcast_to(extra, (cc, HEAD_DIM))
            di_ref[sl, :] = dv.astype(BF16)
            dst_s[...] = dstn * ebl + _dot(dob, qb_, TN)

        dg = _dot_exact(triu, dbs[...]) + exs[...]
        df = dg / f - dks[...]
        df_ref[...] = (df * (1.0 - lb) * sg * (1.0 - sg)).astype(BF16)
        dq_ref[...] = (dqs[...] * (sq * (1.0 + qr * (1.0 - sq)))).astype(BF16)
        dlb = jnp.sum(df * (1.0 - sg), axis=0, keepdims=True) * (lb * (1.0 - lb))
        dlog_ref[0:1, :] += -dlb
        dlog_ref[1:2, :] += dlb

    hh = h_count
    blk = lambda off: pl.BlockSpec((t, HEAD_DIM), lambda h, i: (nb - 1 - i, off + h))
    two = pl.BlockSpec((2, HEAD_DIM), lambda h, i: (0, h))
    one = pl.BlockSpec((1, HEAD_DIM), lambda h, i: (0, h))
    act = jax.ShapeDtypeStruct((s, w), BF16)
    tile = pltpu.VMEM((t, HEAD_DIM), F32)
    return _call(
        body, name=name, grid=(hh, nb),
        in_specs=[blk(0), blk(hh), blk(2 * hh), blk(3 * hh), two, one, blk(0), blk(0),
                  pl.BlockSpec((1, nc, HEAD_DIM, HEAD_DIM), lambda h, i: (h, nb - 1 - i, 0, 0))],
        out_specs=[blk(0), blk(0), blk(0), blk(0), two, one],
        out_shape=[act, act, act, act, jax.ShapeDtypeStruct((2, w), F32), jax.ShapeDtypeStruct((1, w), F32)],
        scratch_shapes=[pltpu.VMEM((HEAD_DIM, HEAD_DIM), F32), tile, tile, tile, tile, tile, tile],
        compiler_params=_cp(("parallel", "arbitrary")),
    )(p1, p1, p1, p1, lbl, onorm, o, dy, states)


def _adamw(name, w, g, m, v):
    r, c = w.shape
    tr = r if r <= 128 else _row_block(r, 128)
    c1 = 1.0 - ADAM_B1 ** ADAM_STEP
    c2 = 1.0 - ADAM_B2 ** ADAM_STEP

    def body(w_ref, g_ref, m_ref, v_ref, d_ref, nm_ref, nv_ref):
        gv = g_ref[...]
        nm = ADAM_B1 * m_ref[...] + (1.0 - ADAM_B1) * gv
        nv = ADAM_B2 * v_ref[...] + (1.0 - ADAM_B2) * (gv * gv)
        nm_ref[...] = nm
        nv_ref[...] = nv
        d_ref[...] = -ADAM_LR * ((nm / c1) / (jnp.sqrt(nv / c2) + ADAM_EPS) + ADAM_WD * w_ref[...])

    blk = pl.BlockSpec((tr, c), lambda i: (i, 0))
    sh = jax.ShapeDtypeStruct((r, c), F32)
    return _call(body, name=name, grid=(r // tr,), in_specs=[blk] * 4, out_specs=[blk] * 3,
                 out_shape=[sh, sh, sh], compiler_params=_cp(("parallel",)))(w, g, m, v)


SLAB_BLOCK_BYTES = 2 * 1024 * 1024


def _slab_rows(r, c):
    return _row_block(r, max(16, SLAB_BLOCK_BYTES // (4 * c) // 16 * 16))


def _pair_add(name, g2, recv, core):
    _, nch, r, c = g2.shape
    tr = _slab_rows(r, c)

    grid_spec = pltpu.PrefetchScalarGridSpec(
        num_scalar_prefetch=1, grid=(nch, r // tr),
        in_specs=[pl.BlockSpec((1, 1, tr, c), lambda j, i, cr: (cr[0], j, i, 0)),
                  pl.BlockSpec((1, tr, c), lambda j, i, cr: (j, i, 0))],
        out_specs=pl.BlockSpec((1, tr, c), lambda j, i, cr: (j, i, 0)))

    def body(core_ref, a_ref, b_ref, o_ref):
        o_ref[...] = (a_ref[0] + b_ref[...]).astype(BF16)

    return _call(body, name=name, grid_spec=grid_spec, out_shape=jax.ShapeDtypeStruct((nch, r, c), BF16),
                 compiler_params=_cp(("parallel", "parallel")))(core, g2, recv)


def _sum_slots(name, x, out_dtype=F32):
    n, r, c = x.shape
    tr = _slab_rows(r, c)

    def body(x_ref, o_ref):
        acc = x_ref[0].astype(F32)
        for j in range(1, n):
            acc = acc + x_ref[j].astype(F32)
        o_ref[...] = acc.astype(out_dtype)

    return _call(body, name=name, grid=(r // tr,),
                 in_specs=[pl.BlockSpec((n, tr, c), lambda i: (0, i, 0))],
                 out_specs=pl.BlockSpec((tr, c), lambda i: (i, 0)),
                 out_shape=jax.ShapeDtypeStruct((r, c), out_dtype), compiler_params=_cp(("parallel",)))(x)


def _pos():
    return lax.axis_index("x"), lax.axis_index("y"), lax.axis_index("c")


def _flip(v, f):
    return (1 - v) if f else v


CHIP_FLIPS = ((0, 1), (1, 0), (1, 1))
DEV_FLIPS = tuple((fx, fy, fc) for fx in (0, 1) for fy in (0, 1) for fc in (0, 1))[1:]


def _remote(src, dst, ssem, rsem, dev):
    return pltpu.make_async_remote_copy(src_ref=src, dst_ref=dst, send_sem=ssem, recv_sem=rsem,
                                        device_id=dev, device_id_type=MESH)


def _ag_weights(name, halves):
    n = len(halves)

    def body(*refs):
        h_refs, big_refs, (lsem, ssem, rsem) = refs[:n], refs[n:2 * n], refs[2 * n:]
        x, y, cc = _pos()
        me = 2 * x + y
        sib = (x, y, 1 - cc)
        peers = [(_flip(x, fx), _flip(y, fy), cc) for fx, fy in CHIP_FLIPS]
        chips = [2 * px + py for px, py, _ in peers]
        pending = []
        for a in range(n):
            mine = big_refs[a].at[cc, me]
            loc = pltpu.make_async_copy(h_refs[a], mine, lsem.at[a])
            loc.start()
            pending.append(loc.wait)
            for k in range(3):
                cp = _remote(h_refs[a], mine, ssem.at[7 * a + k], rsem.at[7 * a + k], peers[k])
                cp.start()
                pending.append(cp.wait_send)
            cp = _remote(h_refs[a], mine, ssem.at[7 * a + 3], rsem.at[7 * a + 3], sib)
            cp.start()
            pending.append(cp.wait_send)
        for k in range(3):
            for a in range(n):
                blk = big_refs[a].at[cc, chips[k]]
                _remote(blk, blk, ssem.at[7 * a + k], rsem.at[7 * a + k], peers[k]).wait_recv()
                fw = _remote(blk, blk, ssem.at[7 * a + 4 + k], rsem.at[7 * a + 4 + k], sib)
                fw.start()
                pending.append(fw.wait_send)
        for a in range(n):
            theirs = big_refs[a].at[1 - cc, me]
            _remote(theirs, theirs, ssem.at[7 * a + 3], rsem.at[7 * a + 3], sib).wait_recv()
            for k in range(3):
                blk = big_refs[a].at[1 - cc, chips[k]]
                _remote(blk, blk, ssem.at[7 * a + 4 + k], rsem.at[7 * a + 4 + k], sib).wait_recv()
        for wait in pending:
            wait()

    return _call(body, name=name, in_specs=[ANY] * n, out_specs=[ANY] * n,
                 out_shape=[jax.ShapeDtypeStruct((2, 4) + h.shape, h.dtype) for h in halves],
                 scratch_shapes=[pltpu.SemaphoreType.DMA((n,)), pltpu.SemaphoreType.DMA((7 * n,)),
                                 pltpu.SemaphoreType.DMA((7 * n,))])(*halves)


def _ag_chips(name, v):
    r, c = v.shape

    def body(v_ref, out_ref, lsem, ssem, rsem):
        x, y, cc = _pos()
        me = 2 * x + y
        loc = pltpu.make_async_copy(v_ref, out_ref.at[me], lsem)
        loc.start()
        started = []
        for k, (fx, fy) in enumerate(CHIP_FLIPS):
            cp = _remote(v_ref, out_ref.at[me], ssem.at[k], rsem.at[k], (_flip(x, fx), _flip(y, fy), cc))
            cp.start()
            started.append(cp)
        for k, (fx, fy) in enumerate(CHIP_FLIPS):
            blk = out_ref.at[2 * _flip(x, fx) + _flip(y, fy)]
            _remote(blk, blk, ssem.at[k], rsem.at[k], (_flip(x, fx), _flip(y, fy), cc)).wait_recv()
        for cp in started:
            cp.wait_send()
        loc.wait()

    return _call(body, name=name, in_specs=[ANY], out_specs=ANY,
                 out_shape=jax.ShapeDtypeStruct((4, r, c), v.dtype),
                 scratch_shapes=[pltpu.SemaphoreType.DMA, pltpu.SemaphoreType.DMA((3,)), pltpu.SemaphoreType.DMA((3,))])(v)


def _ag_devices(name, v):
    r, c = v.shape

    def body(v_ref, out_ref, lsem, ssem, rsem):
        x, y, cc = _pos()
        me = 4 * x + 2 * y + cc
        loc = pltpu.make_async_copy(v_ref, out_ref.at[me], lsem)
        loc.start()
        started = []
        for k, (fx, fy, fc) in enumerate(DEV_FLIPS):
            cp = _remote(v_ref, out_ref.at[me], ssem.at[k], rsem.at[k], (_flip(x, fx), _flip(y, fy), _flip(cc, fc)))
            cp.start()
            started.append(cp)
        for k, (fx, fy, fc) in enumerate(DEV_FLIPS):
            px, py, pc = _flip(x, fx), _flip(y, fy), _flip(cc, fc)
            blk = out_ref.at[4 * px + 2 * py + pc]
            _remote(blk, blk, ssem.at[k], rsem.at[k], (px, py, pc)).wait_recv()
        for cp in started:
            cp.wait_send()
        loc.wait()

    return _call(body, name=name, in_specs=[ANY], out_specs=ANY,
                 out_shape=jax.ShapeDtypeStruct((8, r, c), v.dtype),
                 scratch_shapes=[pltpu.SemaphoreType.DMA, pltpu.SemaphoreType.DMA((7,)), pltpu.SemaphoreType.DMA((7,))])(v)


def _sibling_other_half(name, g2s):
    n = len(g2s)

    def body(*refs):
        g_refs, out_refs, (ssem, rsem) = refs[:n], refs[n:2 * n], refs[2 * n:]
        x, y, cc = _pos()
        cps = [_remote(g_refs[a].at[1 - cc], out_refs[a], ssem.at[a], rsem.at[a], (x, y, 1 - cc)) for a in range(n)]
        for cp in cps:
            cp.start()
        for cp in cps:
            cp.wait()

    return _call(body, name=name, in_specs=[ANY] * n, out_specs=[ANY] * n,
                 out_shape=[jax.ShapeDtypeStruct(g.shape[1:], g.dtype) for g in g2s],
                 scratch_shapes=[pltpu.SemaphoreType.DMA((n,)), pltpu.SemaphoreType.DMA((n,))])(*g2s)


def _scatter_chips(name, ps):
    n = len(ps)

    def body(*refs):
        p_refs, out_refs, (lsem, ssem, rsem) = refs[:n], refs[n:2 * n], refs[2 * n:]
        x, y, cc = _pos()
        me = 2 * x + y
        peers = [(_flip(x, fx), _flip(y, fy), cc) for fx, fy in CHIP_FLIPS]
        pending = []
        for a in range(n):
            loc = pltpu.make_async_copy(p_refs[a].at[me], out_refs[a].at[me], lsem.at[a])
            loc.start()
            pending.append(loc.wait)
            for k, (px, py, _) in enumerate(peers):
                cp = _remote(p_refs[a].at[2 * px + py], out_refs[a].at[me], ssem.at[3 * a + k], rsem.at[3 * a + k], peers[k])
                cp.start()
                pending.append(cp.wait_send)
        for a in range(n):
            for k, (px, py, _) in enumerate(peers):
                blk = out_refs[a].at[2 * px + py]
                _remote(blk, blk, ssem.at[3 * a + k], rsem.at[3 * a + k], peers[k]).wait_recv()
        for wait in pending:
            wait()

    return _call(body, name=name, in_specs=[ANY] * n, out_specs=[ANY] * n,
                 out_shape=[jax.ShapeDtypeStruct(p.shape, p.dtype) for p in ps],
                 scratch_shapes=[pltpu.SemaphoreType.DMA((n,)), pltpu.SemaphoreType.DMA((3 * n,)),
                                 pltpu.SemaphoreType.DMA((3 * n,))])(*ps)


def _sibling_join(name, mines):
    n = len(mines)

    def body(*refs):
        m_refs, out_refs, (lsem, ssem, rsem) = refs[:n], refs[n:2 * n], refs[2 * n:]
        x, y, cc = _pos()
        sib = (x, y, 1 - cc)
        pending = []
        for a in range(n):
            loc = pltpu.make_async_copy(m_refs[a], out_refs[a].at[cc], lsem.at[a])
            loc.start()
            cp = _remote(m_refs[a], out_refs[a].at[cc], ssem.at[a], rsem.at[a], sib)
            cp.start()
            pending += [loc.wait, cp.wait_send]
        for a in range(n):
            theirs = out_refs[a].at[1 - cc]
            _remote(theirs, theirs, ssem.at[a], rsem.at[a], sib).wait_recv()
        for wait in pending:
            wait()

    return _call(body, name=name, in_specs=[ANY] * n, out_specs=[ANY] * n,
                 out_shape=[jax.ShapeDtypeStruct((2,) + m.shape, m.dtype) for m in mines],
                 scratch_shapes=[pltpu.SemaphoreType.DMA((n,)), pltpu.SemaphoreType.DMA((n,)),
                                 pltpu.SemaphoreType.DMA((n,))])(*mines)


def _pad_lanes(a, width=128):
    return jnp.pad(a, ((0, 0), (0, width - a.shape[1])))


def kernel(x, norm_gains, fox_w_in, fox_b_f, hgrn_w_in, hgrn_lb_logits, hgrn_onorm, w_out, final_gain, loss_target, m_norm_gains, m_fox_w_in, m_fox_b_f, m_hgrn_w_in, m_hgrn_lb_logits, m_hgrn_onorm, m_w_out, m_final_gain, v_norm_gains, v_fox_w_in, v_fox_b_f, v_hgrn_w_in, v_hgrn_lb_logits, v_hgrn_onorm, v_w_out, v_final_gain):
    s, d = x.shape[1], x.shape[2]
    wq = w_out.shape[1]
    w = 4 * wq
    hh = w // HEAD_DIM
    fox_cols = fox_w_in.shape[2]
    assert 4 * fox_cols == 4 * w + hh and hgrn_w_in.shape[2] == w
    core = lax.axis_index("c")

    x0 = x[0]
    tgt = loss_target[0]

    def my_half(a):
        return lax.dynamic_index_in_dim(a, core, 0, keepdims=False).astype(BF16)

    big_fox, big_hgrn, big_out = _ag_weights("ag_weights", [
        my_half(fox_w_in.reshape(2, d // 2, fox_cols)), my_half(hgrn_w_in.reshape(2, d // 2, w)), my_half(w_out)])
    fox_chip = [big_fox[:, j].reshape(d, fox_cols) for j in range(4)]
    def fox_columns(a, b):
        out = []
        for j in range(4):
            lo, hi = max(a, j * fox_cols), min(b, (j + 1) * fox_cols)
            if lo < hi:
                out.append(fox_chip[j][:, lo - j * fox_cols:hi - j * fox_cols])
        return out

    w_main = jnp.concatenate(fox_columns(0, 3 * w) + fox_columns(3 * w + hh, 4 * w + hh), axis=1)
    w_fl = _pad_lanes(jnp.concatenate(fox_columns(3 * w, 3 * w + hh), axis=1))
    w_h = big_hgrn.transpose(0, 2, 1, 3).reshape(d, 4 * w)
    w_o = big_out.reshape(2, w, d)
    onorm_full = _ag_chips("ag_onorm", hgrn_onorm.reshape(wq // 128, 128)).reshape(1, w)

    bf_pad = _pad_lanes(fox_b_f)
    g0, g1 = norm_gains[0:1], norm_gains[1:2]
    gf = final_gain.reshape(1, d)

    h0 = _rms_fwd("rms0_fwd", x0, g0)
    qkv = _mm("fox_qkv", h0, w_main, out_dtype=BF16, n=3 * w)
    gate0 = _mm("fox_gate", h0, w_main, n=w, b_off=3 * w)
    fl = _mm("fox_flogit", h0, w_fl)
    _, c_rep = _fox_gate_fwd("fox_cumsum", fl, bf_pad, w)
    o0, lse_row = _attn_fwd("fox_attn_fwd", qkv, c_rep, hh)
    y0 = _fox_post("fox_post", o0, gate0)
    x1 = _mm("fox_out", y0, w_o[0], res=x0)
    h1 = _rms_fwd("rms1_fwd", x1, g1)
    p1 = _mm("hgrn_in", h1, w_h)
    y1, o1, states = _hgrn_fwd("hgrn_fwd", p1, hgrn_lb_logits, onorm_full, hh)
    x2 = _mm("hgrn_out", y1, w_o[1], res=x1)
    dx2, d_gf, loss_tile = _loss_head("loss_head", x2, tgt, gf)

    dy1 = _mm("hgrn_out_dy", dx2, w_o[1], tb=True)
    d_wo1 = _mm("hgrn_out_dw", y1, dx2, ta=True)
    dq1, df1, di1, dgate1, d_lbl, d_onorm = _hgrn_bwd("hgrn_bwd", p1, hgrn_lb_logits, onorm_full, o1, dy1, states, hh)
    dp1 = jnp.concatenate([dq1, df1, di1, dgate1], axis=1)
    dh1 = _mm("hgrn_in_dh", dp1, w_h, tb=True)
    d_wh = _mm("hgrn_in_dw", h1, dp1, ta=True)
    dx1, d_g1 = _rms_bwd("rms1_bwd", x1, g1, dh1, dx2)
    dy0 = _mm("fox_out_dy", dx1, w_o[0], tb=True)
    d_wo0 = _mm("fox_out_dw", y0, dx1, ta=True)
    do0, dgate0, delta_rep = _fox_pre_bwd("fox_pre_bwd", dy0, o0, gate0)
    delta_row = delta_rep[:, ::HEAD_DIM].T.reshape(hh, 1, s)
    dq0, rowsum_row = _attn_dq("fox_attn_dq", qkv, c_rep, do0, lse_row, delta_row, hh)
    dk0, dv0, dc_rep = _attn_dkv("fox_attn_dkv", qkv, do0, lse_row, delta_row, c_rep, hh)
    dfl, d_bf = _fox_gate_bwd("fox_cumsum_bwd", _pad_lanes(rowsum_row.reshape(hh, s).T),
                              _pad_lanes(dc_rep[:, ::HEAD_DIM]), fl, bf_pad)
    dp0 = jnp.concatenate([dq0, dk0, dv0, dgate0], axis=1)
    dh0 = _mm("fox_in_dh", dp0, w_main, tb=True, res=_mm("fox_fl_dh", dfl, w_fl, tb=True))
    d_wmain = _mm("fox_in_dw", h0, dp0, ta=True)
    d_wfl = _mm("fox_fl_dw", h0, dfl, ta=True)
    grad_x, d_g0 = _rms_bwd("rms0_bwd", x0, g0, dh0, dx1)

    def grad_columns(a, b):
        out = []
        for lo, hi, src, shift in ((0, 3 * w, d_wmain, 0), (3 * w, 3 * w + hh, d_wfl, 3 * w),
                                   (3 * w + hh, 4 * w + hh, d_wmain, hh)):
            l2, h2 = max(a, lo), min(b, hi)
            if l2 < h2:
                out.append(src[:, l2 - shift:h2 - shift])
        return out

    g2_fox = jnp.stack([jnp.concatenate(grad_columns(j * fox_cols, (j + 1) * fox_cols), axis=1).reshape(2, d // 2, fox_cols)
                        for j in range(4)], axis=1)
    g2_hgrn = d_wh.reshape(2, d // 2, 4, w).transpose(0, 2, 1, 3)
    g2_out = jnp.stack([d_wo0, d_wo1]).reshape(2, 4, wq, d)
    g2s = [g2_fox, g2_hgrn, g2_out]
    core_arr = core.reshape(1).astype(jnp.int32)
    from_sibling = _sibling_other_half("rs_sibling", g2s)
    pairs = [_pair_add("rs_pair_add_" + nm, g2, rcv, core_arr)
             for nm, g2, rcv in zip(("fox", "hgrn", "out"), g2s, from_sibling)]
    from_chips = _scatter_chips("rs_scatter", pairs)
    halves_sum = [_sum_slots("rs_sum_" + nm, fc) for nm, fc in zip(("fox", "hgrn", "out"), from_chips)]
    r_fox, r_hgrn, r_out = _sibling_join("rs_join", halves_sum)
    g_fox = r_fox.reshape(d, fox_cols)
    g_hgrn = r_hgrn.reshape(d, w)
    g_out = r_out.reshape(2 * wq, d)

    small = jnp.concatenate([jnp.concatenate([d_g0, d_g1], axis=0).reshape(-1), d_lbl.reshape(-1), d_gf.reshape(-1),
                             d_onorm.reshape(-1), d_bf.reshape(-1)])
    n_small = small.shape[0]
    pad_to = -(-n_small // 1024) * 1024
    small = jnp.pad(small, (0, pad_to - n_small)).reshape(pad_to // 128, 128)
    small = _sum_slots("small_sum", _ag_devices("small_gather", small)).reshape(-1)
    g_norm = small[:2 * d].reshape(2, d)
    g_lbl = small[2 * d:2 * d + 2 * w].reshape(2, w)
    g_gf = small[2 * d + 2 * w:3 * d + 2 * w]
    g_onorm_full = small[3 * d + 2 * w:3 * d + 3 * w]
    g_bf = small[3 * d + 3 * w:3 * d + 3 * w + hh].reshape(1, hh)
    chip = 2 * lax.axis_index("x") + lax.axis_index("y")
    g_onorm = lax.dynamic_slice_in_dim(g_onorm_full, chip * wq, wq).reshape(1, wq)

    loss = lax.psum(loss_tile[0, 0], ("x", "y", "c"))

    def upd(name, wt, g, m, v):
        shp = wt.shape
        two = lambda a: a.reshape(-1, shp[-1])
        dl, nm, nv = _adamw(name, two(wt), two(g), two(m), two(v))
        return g.reshape(shp), dl.reshape(shp), nm.reshape(shp), nv.reshape(shp)

    res = [
        upd("adamw_norm_gains", norm_gains, g_norm, m_norm_gains, v_norm_gains),
        upd("adamw_fox_w_in", fox_w_in, g_fox, m_fox_w_in, v_fox_w_in),
        upd("adamw_fox_b_f", fox_b_f, g_bf, m_fox_b_f, v_fox_b_f),
        upd("adamw_hgrn_w_in", hgrn_w_in, g_hgrn, m_hgrn_w_in, v_hgrn_w_in),
        upd("adamw_lb_logits", hgrn_lb_logits, g_lbl, m_hgrn_lb_logits, v_hgrn_lb_logits),
        upd("adamw_onorm", hgrn_onorm, g_onorm, m_hgrn_onorm, v_hgrn_onorm),
        upd("adamw_w_out", w_out, g_out, m_w_out, v_w_out),
        upd("adamw_final_gain", final_gain.reshape(1, d), g_gf.reshape(1, d), m_final_gain.reshape(1, d),
            v_final_gain.reshape(1, d)),
    ]
    res[-1] = tuple(a.reshape(d) for a in res[-1])
    grads, deltas, new_m, new_v = zip(*res)
    return (loss, grad_x[None], *grads, *deltas, *new_m, *new_v)
```

```python
import functools

import jax
import jax.numpy as jnp
from jax import lax
from jax.experimental import pallas as pl
from jax.experimental.pallas import tpu as pltpu

F32 = jnp.float32
BF16 = jnp.bfloat16
MESH = pl.DeviceIdType.MESH
ANY = pl.BlockSpec(memory_space=pl.ANY)

EPS = 1e-6
HEAD_DIM = 128
HGRN_CHUNK = 64
HGRN_SUB = 16
EXP_CLAMP = 80.0
ATT_BLOCK = 512
HGRN_BLOCK = 512
GATE_BLOCK = 512
ROW_BLOCK = 256
MM_TM, MM_TN, MM_TK = 1024, 1024, 2048
VMEM_LIMIT_V7X = 56 * 1024 * 1024

ADAM_LR, ADAM_B1, ADAM_B2, ADAM_EPS, ADAM_WD, ADAM_STEP = 0.001, 0.9, 0.999, 1e-08, 0.01, 10

NT = (((1,), (1,)), ((), ()))
TN = (((0,), (0,)), ((), ()))
NN = (((1,), (0,)), ((), ()))


def _call(body, **kw):
    return pl.pallas_call(body, **kw)


def _cp(dims=None):
    kw = dict(vmem_limit_bytes=VMEM_LIMIT_V7X)
    if dims is not None:
        kw["dimension_semantics"] = dims
    return pltpu.CompilerParams(**kw)


def _sigmoid(x):
    return 1.0 / (1.0 + jnp.exp(-x))


def _dot(a, b, dn=NN):
    return lax.dot_general(a.astype(BF16), b.astype(BF16), dn, preferred_element_type=F32)


def _split3(x):
    hi = x.astype(BF16)
    r1 = x - hi.astype(F32)
    mid = r1.astype(BF16)
    lo = (r1 - mid.astype(F32)).astype(BF16)
    return hi, mid, lo


def _dot_exact(m01, x, right=False):
    hi, mid, lo = _split3(x)
    if right:
        dot = lambda p: lax.dot_general(p, m01, NN, preferred_element_type=F32)
    else:
        dot = lambda p: lax.dot_general(m01, p, NN, preferred_element_type=F32)
    return dot(hi) + dot(mid) + dot(lo)


def _row_block(rows, cap):
    if rows <= cap:
        return rows
    best = None
    for t in range(16, cap + 1, 16):
        if rows % t == 0:
            best = t
    assert best is not None, rows
    return best


def _mm(name, a, b, *, ta=False, tb=False, out_dtype=F32, res=None, n=None, b_off=0):
    m, k = (a.shape[1], a.shape[0]) if ta else a.shape
    n_full = b.shape[0] if tb else b.shape[1]
    n = n_full if n is None else n
    tm, tn, tk = min(MM_TM, m), min(MM_TN, n), min(MM_TK, k)
    assert m % tm == 0 and n % tn == 0 and k % tk == 0 and b_off % tn == 0
    nk = k // tk
    jo = b_off // tn
    a_spec = pl.BlockSpec((tk, tm), lambda i, j, kk: (kk, i)) if ta else pl.BlockSpec((tm, tk), lambda i, j, kk: (i, kk))
    b_spec = pl.BlockSpec((tn, tk), lambda i, j, kk: (j + jo, kk)) if tb else pl.BlockSpec((tk, tn), lambda i, j, kk: (kk, j + jo))
    o_spec = pl.BlockSpec((tm, tn), lambda i, j, kk: (i, j))
    dn = (((0 if ta else 1,), (1 if tb else 0,)), ((), ()))
    has_res = res is not None

    def body(*refs):
        if has_res:
            a_ref, b_ref, r_ref, o_ref, acc = refs
        else:
            a_ref, b_ref, o_ref, acc = refs
            r_ref = None
        kk = pl.program_id(2)
        p = lax.dot_general(a_ref[...].astype(BF16), b_ref[...].astype(BF16), dn, preferred_element_type=F32)

        def finish(total):
            if r_ref is not None:
                total = total + r_ref[...]
            o_ref[...] = total.astype(out_dtype)

        if nk == 1:
            finish(p)
        else:
            @pl.when(kk == 0)
            def _():
                acc[...] = p

            @pl.when(jnp.logical_and(kk > 0, kk < nk - 1))
            def _():
                acc[...] += p

            @pl.when(kk == nk - 1)
            def _():
                finish(acc[...] + p)

    ins = [a, b] + ([res] if has_res else [])
    in_specs = [a_spec, b_spec] + ([o_spec] if has_res else [])
    return _call(
        body, name=name, grid=(m // tm, n // tn, nk), in_specs=in_specs, out_specs=o_spec,
        out_shape=jax.ShapeDtypeStruct((m, n), out_dtype),
        scratch_shapes=[pltpu.VMEM((tm, tn) if nk > 1 else (8, 128), F32)],
        compiler_params=_cp(("parallel", "parallel", "arbitrary")),
    )(*ins)


def _rms_fwd(name, x, g):
    s, d = x.shape
    tm = min(ROW_BLOCK, s)

    def body(x_ref, g_ref, h_ref):
        xv = x_ref[...]
        r = lax.rsqrt(jnp.mean(xv * xv, axis=-1, keepdims=True) + EPS)
        h_ref[...] = (xv * r * g_ref[...]).astype(BF16)

    row = pl.BlockSpec((tm, d), lambda i: (i, 0))
    vec = pl.BlockSpec((1, d), lambda i: (0, 0))
    return _call(body, name=name, grid=(s // tm,), in_specs=[row, vec], out_specs=row,
                 out_shape=jax.ShapeDtypeStruct((s, d), BF16), compiler_params=_cp(("parallel",)))(x, g)


def _rms_bwd(name, x, g, dh, dres):
    s, d = x.shape
    tm = min(ROW_BLOCK, s)

    def body(x_ref, g_ref, dh_ref, dres_ref, dx_ref, dg_ref):
        @pl.when(pl.program_id(0) == 0)
        def _():
            dg_ref[...] = jnp.zeros_like(dg_ref)

        xv = x_ref[...]
        r = lax.rsqrt(jnp.mean(xv * xv, axis=-1, keepdims=True) + EPS)
        xn = xv * r
        dhv = dh_ref[...]
        dxn = dhv * g_ref[...]
        dx_ref[...] = dres_ref[...] + r * (dxn - xn * jnp.mean(dxn * xn, axis=-1, keepdims=True))
        dg_ref[...] += jnp.sum(dhv * xn, axis=0, keepdims=True)

    row = pl.BlockSpec((tm, d), lambda i: (i, 0))
    vec = pl.BlockSpec((1, d), lambda i: (0, 0))
    return _call(body, name=name, grid=(s // tm,), in_specs=[row, vec, row, row], out_specs=[row, vec],
                 out_shape=[jax.ShapeDtypeStruct((s, d), F32), jax.ShapeDtypeStruct((1, d), F32)],
                 compiler_params=_cp(("arbitrary",)))(x, g, dh, dres)


def _loss_head(name, x, tgt, g):
    s, d = x.shape
    tm = min(ROW_BLOCK, s)
    nb = s // tm

    def body(x_ref, t_ref, g_ref, dx_ref, dg_ref, loss_ref, lacc):
        i = pl.program_id(0)

        @pl.when(i == 0)
        def _():
            dg_ref[...] = jnp.zeros_like(dg_ref)
            lacc[...] = jnp.zeros_like(lacc)

        xv = x_ref[...]
        gv = g_ref[...]
        r = lax.rsqrt(jnp.mean(xv * xv, axis=-1, keepdims=True) + EPS)
        xn = xv * r
        err = xn * gv - t_ref[...]
        lacc[...] += jnp.sum(err * err, axis=0, keepdims=True)
        dout = err * (1.0 / d)
        dg_ref[...] += jnp.sum(dout * xn, axis=0, keepdims=True)
        dxn = dout * gv
        dx_ref[...] = r * (dxn - xn * jnp.mean(dxn * xn, axis=-1, keepdims=True))

        @pl.when(i == nb - 1)
        def _():
            total = jnp.sum(lacc[...], axis=1, keepdims=True) * (0.5 / d)
            loss_ref[...] = jnp.broadcast_to(total, loss_ref.shape)

    row = pl.BlockSpec((tm, d), lambda i: (i, 0))
    vec = pl.BlockSpec((1, d), lambda i: (0, 0))
    one = pl.BlockSpec((1, 128), lambda i: (0, 0))
    return _call(body, name=name, grid=(nb,), in_specs=[row, row, vec], out_specs=[row, vec, one],
                 out_shape=[jax.ShapeDtypeStruct((s, d), F32), jax.ShapeDtypeStruct((1, d), F32),
                            jax.ShapeDtypeStruct((1, 128), F32)],
                 scratch_shapes=[pltpu.VMEM((1, d), F32)], compiler_params=_cp(("arbitrary",)))(x, tgt, g)


def _fox_gate_fwd(name, fl, bf, w):
    s = fl.shape[0]
    tb = min(GATE_BLOCK, s)

    def body(fl_ref, bf_ref, c_ref, crep_ref, carry):
        @pl.when(pl.program_id(0) == 0)
        def _():
            carry[...] = jnp.zeros_like(carry)

        z = fl_ref[...] + bf_ref[...]
        lf = jnp.minimum(z, 0.0) - jnp.log(1.0 + jnp.exp(-jnp.abs(z)))
        rows = lax.broadcasted_iota(jnp.int32, (tb, tb), 0)
        cols = lax.broadcasted_iota(jnp.int32, (tb, tb), 1)
        tri = (rows >= cols).astype(BF16)
        cs = _dot_exact(tri, lf) + carry[...]
        c_ref[...] = cs
        carry[...] = c_ref[tb - 1:tb, :]
        sel_r = lax.broadcasted_iota(jnp.int32, (128, w), 0)
        sel_c = lax.broadcasted_iota(jnp.int32, (128, w), 1)
        sel = (sel_r == sel_c // HEAD_DIM).astype(BF16)
        crep_ref[...] = _dot_exact(sel, cs, right=True)

    blk = pl.BlockSpec((tb, 128), lambda i: (i, 0))
    return _call(body, name=name, grid=(s // tb,),
                 in_specs=[blk, pl.BlockSpec((1, 128), lambda i: (0, 0))],
                 out_specs=[blk, pl.BlockSpec((tb, w), lambda i: (i, 0))],
                 out_shape=[jax.ShapeDtypeStruct((s, 128), F32), jax.ShapeDtypeStruct((s, w), F32)],
                 scratch_shapes=[pltpu.VMEM((1, 128), F32)], compiler_params=_cp(("arbitrary",)))(fl, bf)


def _fox_gate_bwd(name, drow, dcol, fl, bf):
    s = fl.shape[0]
    tb = min(GATE_BLOCK, s)
    nb = s // tb

    def body(dr_ref, dc_ref, fl_ref, bf_ref, dfl_ref, dbf_ref, carry, tmp):
        @pl.when(pl.program_id(0) == 0)
        def _():
            carry[...] = jnp.zeros_like(carry)
            dbf_ref[...] = jnp.zeros_like(dbf_ref)

        rows = lax.broadcasted_iota(jnp.int32, (tb, tb), 0)
        cols = lax.broadcasted_iota(jnp.int32, (tb, tb), 1)
        triu = (rows <= cols).astype(BF16)
        dlf = _dot_exact(triu, dr_ref[...] + dc_ref[...]) + carry[...]
        tmp[...] = dlf
        carry[...] = tmp[0:1, :]
        z = fl_ref[...] + bf_ref[...]
        dfl = dlf * (1.0 / (1.0 + jnp.exp(z)))
        dfl_ref[...] = dfl
        dbf_ref[...] += jnp.sum(dfl, axis=0, keepdims=True)

    blk = pl.BlockSpec((tb, 128), lambda i: (nb - 1 - i, 0))
    vec = pl.BlockSpec((1, 128), lambda i: (0, 0))
    return _call(body, name=name, grid=(nb,), in_specs=[blk, blk, blk, vec], out_specs=[blk, vec],
                 out_shape=[jax.ShapeDtypeStruct((s, 128), F32), jax.ShapeDtypeStruct((1, 128), F32)],
                 scratch_shapes=[pltpu.VMEM((1, 128), F32), pltpu.VMEM((tb, 128), F32)],
                 compiler_params=_cp(("arbitrary",)))(drow, dcol, fl, bf)


def _attn_fwd(name, qkv, c_rep, h_count):
    s = qkv.shape[0]
    w = h_count * HEAD_DIM
    t = min(ATT_BLOCK, s)
    scale = HEAD_DIM ** -0.5

    def body(q_ref, k_ref, v_ref, c_ref, o_ref, lse_ref, m_s, l_s, acc_s, s_buf):
        qi = pl.program_id(1)
        q = q_ref[...]
        m_s[...] = jnp.full(m_s.shape, -jnp.inf, F32)
        l_s[...] = jnp.zeros_like(l_s)
        acc_s[...] = jnp.zeros_like(acc_s)

        def scores(kb):
            off = pl.multiple_of(kb * t, t)
            return lax.dot_general(k_ref[pl.ds(off, t), :], q, NT, preferred_element_type=F32)

        def consume(kb, slot, masked):
            off = pl.multiple_of(kb * t, t)
            vv = v_ref[pl.ds(off, t), :]
            cc = jnp.tile(c_ref[pl.ds(off, t), :], (1, t // HEAD_DIM))
            st = s_buf[slot] * scale - cc
            if masked:
                rows = lax.broadcasted_iota(jnp.int32, (t, t), 0)
                cols = lax.broadcasted_iota(jnp.int32, (t, t), 1)
                st = jnp.where(cols >= rows, st, -jnp.inf)
            m_prev = m_s[...]
            m_new = jnp.maximum(m_prev, jnp.max(st, axis=0, keepdims=True))
            pt = jnp.exp(st - m_new)
            alpha = jnp.exp(m_prev - m_new)
            l_s[...] = alpha * l_s[...] + jnp.sum(pt, axis=0, keepdims=True)
            acc_s[...] = alpha * acc_s[...] + lax.dot_general(vv, pt.astype(BF16), TN, preferred_element_type=F32)
            m_s[...] = m_new

        s_buf[0] = scores(0)

        def loop_body(kb, carry):
            slot = lax.rem(kb, 2)
            s_buf[1 - slot] = scores(kb + 1)
            consume(kb, slot, False)
            return carry

        lax.fori_loop(0, qi, loop_body, 0)
        consume(qi, lax.rem(qi, 2), True)
        l = l_s[...]
        o_ref[...] = (acc_s[...] / l).T
        lse_ref[0] = m_s[...] + jnp.log(l)

    hh = h_count
    blk = lambda off: pl.BlockSpec((t, HEAD_DIM), lambda h, i: (i, off + h))
    whole = lambda off: pl.BlockSpec((s, HEAD_DIM), lambda h, i: (0, off + h))
    rowv = pl.BlockSpec((1, 1, t), lambda h, i: (h, 0, i))
    return _call(
        body, name=name, grid=(hh, s // t),
        in_specs=[blk(0), whole(hh), whole(2 * hh), whole(0)],
        out_specs=[blk(0), rowv],
        out_shape=[jax.ShapeDtypeStruct((s, w), F32), jax.ShapeDtypeStruct((hh, 1, s), F32)],
        scratch_shapes=[pltpu.VMEM((1, t), F32), pltpu.VMEM((1, t), F32), pltpu.VMEM((HEAD_DIM, t), F32),
                        pltpu.VMEM((2, t, t), F32)],
        compiler_params=_cp(("parallel", "arbitrary")),
    )(qkv, qkv, qkv, c_rep)


def _attn_bwd(name, qkv, do, lse_row, delta_row, c_rep, h_count):
    s = qkv.shape[0]
    w = h_count * HEAD_DIM
    t = min(ATT_BLOCK, s)
    nq = s // t
    scale = HEAD_DIM ** -0.5

    def body(k_ref, v_ref, q_ref, do_ref, lse_ref, dl_ref, c_ref, dk_ref, dv_ref, dc_ref, dq_ref, rs_ref,
             dk_s, dv_s, dc_s, dq_s, rs_s):
        kj = pl.program_id(1)
        kk = k_ref[...]
        vv = v_ref[...]
        ccol = jnp.tile(c_ref[...], (1, t // HEAD_DIM))
        dk_s[...] = jnp.zeros_like(dk_s)
        dv_s[...] = jnp.zeros_like(dv_s)
        dc_s[...] = jnp.zeros_like(dc_s)

        @pl.when(kj == 0)
        def _():
            dq_s[...] = jnp.zeros_like(dq_s)
            rs_s[...] = jnp.zeros_like(rs_s)

        def step(qb, masked):
            off = pl.multiple_of(qb * t, t)
            q = q_ref[pl.ds(off, t), :]
            dov = do_ref[pl.ds(off, t), :]
            st = lax.dot_general(kk, q, NT, preferred_element_type=F32) * scale - ccol
            pt = jnp.exp(st - lse_ref[0, :, pl.ds(off, t)])
            if masked:
                rows = lax.broadcasted_iota(jnp.int32, (t, t), 0)
                cols = lax.broadcasted_iota(jnp.int32, (t, t), 1)
                pt = jnp.where(cols >= rows, pt, 0.0)
            dv_s[...] += lax.dot_general(pt.astype(BF16), dov, NN, preferred_element_type=F32)
            dpt = lax.dot_general(vv, dov, NT, preferred_element_type=F32)
            dst = pt * (dpt - dl_ref[0, :, pl.ds(off, t)])
            dstb = dst.astype(BF16)
            dk_s[...] += lax.dot_general(dstb, q, NN, preferred_element_type=F32)
            dc_s[...] += jnp.sum(dst, axis=1, keepdims=True)
            dq_s[:, pl.ds(off, t)] += lax.dot_general(kk, dstb, TN, preferred_element_type=F32)
            rs_s[:, pl.ds(off, t)] += jnp.sum(dst, axis=0, keepdims=True)

        step(kj, True)

        def loop_body(qb, carry):
            step(qb, False)
            return carry

        lax.fori_loop(kj + 1, nq, loop_body, 0)
        dk_ref[...] = (dk_s[...] * scale).astype(BF16)
        dv_ref[...] = dv_s[...].astype(BF16)
        dc_ref[...] = jnp.broadcast_to(-dc_s[...], dc_ref.shape)

        @pl.when(kj == nq - 1)
        def _():
            for b in range(nq):
                dq_ref[b * t:(b + 1) * t, :] = (dq_s[:, b * t:(b + 1) * t] * scale).T.astype(BF16)
            rs_ref[0] = rs_s[...]

    hh = h_count
    blk = lambda off: pl.BlockSpec((t, HEAD_DIM), lambda h, j: (j, off + h))
    whole = pl.BlockSpec((s, HEAD_DIM), lambda h, j: (0, h))
    rowv = pl.BlockSpec((1, 1, s), lambda h, j: (h, 0, 0))
    return _call(
        body, name=name, grid=(hh, nq),
        in_specs=[blk(hh), blk(2 * hh), whole, whole, rowv, rowv, blk(0)],
        out_specs=[blk(0), blk(0), blk(0), whole, rowv],
        out_shape=[jax.ShapeDtypeStruct((s, w), BF16), jax.ShapeDtypeStruct((s, w), BF16),
                   jax.ShapeDtypeStruct((s, w), F32), jax.ShapeDtypeStruct((s, w), BF16),
                   jax.ShapeDtypeStruct((hh, 1, s), F32)],
        scratch_shapes=[pltpu.VMEM((t, HEAD_DIM), F32), pltpu.VMEM((t, HEAD_DIM), F32), pltpu.VMEM((t, 1), F32),
                        pltpu.VMEM((HEAD_DIM, s), F32), pltpu.VMEM((1, s), F32)],
        compiler_params=_cp(("parallel", "arbitrary")),
    )(qkv, qkv, qkv, do, lse_row, delta_row, c_rep)


def _fox_post(name, o, gate):
    s, w = o.shape
    tm = min(ROW_BLOCK, s)

    def body(o_ref, g_ref, y_ref):
        g = g_ref[...]
        y_ref[...] = (o_ref[...] * (g * _sigmoid(g))).astype(BF16)

    row = pl.BlockSpec((tm, w), lambda i: (i, 0))
    return _call(body, name=name, grid=(s // tm,), in_specs=[row, row], out_specs=row,
                 out_shape=jax.ShapeDtypeStruct((s, w), BF16), compiler_params=_cp(("parallel",)))(o, gate)


def _fox_pre_bwd(name, dy, o, gate):
    s, w = o.shape
    tm = min(ROW_BLOCK, s)

    def body(dy_ref, o_ref, g_ref, do_ref, dg_ref, dl_ref):
        g = g_ref[...]
        sg = _sigmoid(g)
        dyv = dy_ref[...]
        ov = o_ref[...]
        dov = dyv * (g * sg)
        do_ref[...] = dov.astype(BF16)
        dg_ref[...] = (dyv * ov * (sg * (1.0 + g * (1.0 - sg)))).astype(BF16)
        prod = dov * ov
        for h in range(w // HEAD_DIM):
            sl = slice(h * HEAD_DIM, (h + 1) * HEAD_DIM)
            dl_ref[:, sl] = jnp.broadcast_to(jnp.sum(prod[:, sl], axis=1, keepdims=True), (tm, HEAD_DIM))

    row = pl.BlockSpec((tm, w), lambda i: (i, 0))
    return _call(body, name=name, grid=(s // tm,), in_specs=[row, row, row], out_specs=[row, row, row],
                 out_shape=[jax.ShapeDtypeStruct((s, w), BF16), jax.ShapeDtypeStruct((s, w), BF16),
                            jax.ShapeDtypeStruct((s, w), F32)],
                 compiler_params=_cp(("parallel",)))(dy, o, gate)


def _hgrn_chunk_terms(q_c, k_c, b_c, b_s, base):
    nsub = HGRN_CHUNK // HGRN_SUB
    refs = [jnp.zeros((1, HEAD_DIM), F32)]
    for i in range(1, nsub):
        r0 = base + i * HGRN_SUB - 1
        refs.append(b_s[r0:r0 + 1, :])
    rfull = jnp.concatenate([jnp.broadcast_to(r, (HGRN_SUB, HEAD_DIM)) for r in refs], axis=0)
    eq = jnp.exp(b_c - rfull)
    qe = q_c * eq
    es = [jnp.exp(jnp.minimum(r - b_c, EXP_CLAMP)) for r in refs]
    kes = [(k_c * e).astype(BF16) for e in es]
    return eq, qe, es, kes


def _hgrn_block_pre(q_ref, f_ref, lbl_ref, tri, b_s):
    lb = _sigmoid(lbl_ref[1:2, :] - lbl_ref[0:1, :])
    qr = q_ref[...]
    sq = _sigmoid(qr)
    q = qr * sq
    fz = f_ref[...]
    sg = _sigmoid(fz)
    f = lb + (1.0 - lb) * sg
    g = jnp.log(f)
    k = (1.0 - lb) * (1.0 / (1.0 + jnp.exp(fz)))
    b = _dot_exact(tri, g)
    b_s[...] = b
    return lb, qr, sq, q, sg, f, k, b


def _chunk_masks(t):
    rows = lax.broadcasted_iota(jnp.int32, (t, t), 0)
    cols = lax.broadcasted_iota(jnp.int32, (t, t), 1)
    same = (rows // HGRN_CHUNK) == (cols // HGRN_CHUNK)
    return rows, cols, same


def _hgrn_fwd(name, p1, lbl, onorm, h_count):
    s = p1.shape[0]
    w = h_count * HEAD_DIM
    t = min(HGRN_BLOCK, s)
    nc = t // HGRN_CHUNK
    nsub = HGRN_CHUNK // HGRN_SUB
    cc = HGRN_CHUNK

    def body(q_ref, f_ref, i_ref, g_ref, lbl_ref, on_ref, y_ref, o_ref, st_ref, state_s, b_s):
        @pl.when(pl.program_id(1) == 0)
        def _():
            state_s[...] = jnp.zeros_like(state_s)

        rows, cols, same = _chunk_masks(t)
        tri = jnp.logical_and(same, rows >= cols).astype(BF16)
        lb, qr, sq, q, sg, f, k, b = _hgrn_block_pre(q_ref, f_ref, lbl_ref, tri, b_s)
        v = i_ref[...]
        r64 = lax.broadcasted_iota(jnp.int32, (cc, cc), 0)
        c64 = lax.broadcasted_iota(jnp.int32, (cc, cc), 1)
        for n in range(nc):
            sl = slice(n * cc, (n + 1) * cc)
            q_c, k_c, v_c, b_c = q[sl], k[sl], v[sl], b[sl]
            bl = b_s[n * cc + cc - 1:n * cc + cc, :]
            eq, qe, es, kes = _hgrn_chunk_terms(q_c, k_c, b_c, b_s, n * cc)
            qeb = qe.astype(BF16)
            a = jnp.concatenate(
                [lax.dot_general(qeb[i * HGRN_SUB:(i + 1) * HGRN_SUB], kes[i], NT, preferred_element_type=F32)
                 for i in range(nsub)], axis=0)
            a = jnp.where(r64 >= c64, a, 0.0)
            st = state_s[...]
            st_ref[0, n] = st
            inter = _dot(q_c * jnp.exp(b_c), st, NT)
            intra = _dot(a, v_c)
            o_ref[sl, :] = inter + intra
            kb = k_c * jnp.exp(bl - b_c)
            state_s[...] = st * jnp.exp(bl) + _dot(v_c, kb, TN)
        o = o_ref[...]
        rr = lax.rsqrt(jnp.mean(o * o, axis=-1, keepdims=True) + EPS)
        gate = g_ref[...]
        y_ref[...] = ((o * rr) * on_ref[...] * (gate * _sigmoid(gate))).astype(BF16)

    hh = h_count
    blk = lambda off: pl.BlockSpec((t, HEAD_DIM), lambda h, i: (i, off + h))
    return _call(
        body, name=name, grid=(hh, s // t),
        in_specs=[blk(0), blk(hh), blk(2 * hh), blk(3 * hh),
                  pl.BlockSpec((2, HEAD_DIM), lambda h, i: (0, h)), pl.BlockSpec((1, HEAD_DIM), lambda h, i: (0, h))],
        out_specs=[blk(0), blk(0), pl.BlockSpec((1, nc, HEAD_DIM, HEAD_DIM), lambda h, i: (h, i, 0, 0))],
        out_shape=[jax.ShapeDtypeStruct((s, w), BF16), jax.ShapeDtypeStruct((s, w), F32),
                   jax.ShapeDtypeStruct((hh, s // cc, HEAD_DIM, HEAD_DIM), F32)],
        scratch_shapes=[pltpu.VMEM((HEAD_DIM, HEAD_DIM), F32), pltpu.VMEM((t, HEAD_DIM), F32)],
        compiler_params=_cp(("parallel", "arbitrary")),
    )(p1, p1, p1, p1, lbl, onorm)


def _hgrn_bwd(name, p1, lbl, onorm, o, dy, states, h_count):
    s = p1.shape[0]
    w = h_count * HEAD_DIM
    t = min(HGRN_BLOCK, s)
    nb = s // t
    nc = t // HGRN_CHUNK
    nsub = HGRN_CHUNK // HGRN_SUB
    cc = HGRN_CHUNK

    def body(q_ref, f_ref, i_ref, g_ref, lbl_ref, on_ref, o_ref, dy_ref, st_ref,
             dq_ref, df_ref, di_ref, dgate_ref, dlog_ref, don_ref,
             dst_s, b_s, do_s, dqs, dks, dbs, exs):
        @pl.when(pl.program_id(1) == 0)
        def _():
            dst_s[...] = jnp.zeros_like(dst_s)
            dlog_ref[...] = jnp.zeros_like(dlog_ref)
            don_ref[...] = jnp.zeros_like(don_ref)

        rows, cols, same = _chunk_masks(t)
        tri = jnp.logical_and(same, rows >= cols).astype(BF16)
        triu = jnp.logical_and(same, rows <= cols).astype(BF16)
        lb, qr, sq, q, sg, f, k, b = _hgrn_block_pre(q_ref, f_ref, lbl_ref, tri, b_s)
        v = i_ref[...]

        ov = o_ref[...]
        rr = lax.rsqrt(jnp.mean(ov * ov, axis=-1, keepdims=True) + EPS)
        on = ov * rr
        gate = g_ref[...]
        sgt = _sigmoid(gate)
        silu = gate * sgt
        dyv = dy_ref[...]
        gain = on_ref[...]
        dgate_ref[...] = (dyv * on * gain * (sgt * (1.0 + gate * (1.0 - sgt)))).astype(BF16)
        don_ref[...] += jnp.sum(dyv * on * silu, axis=0, keepdims=True)
        d_on = dyv * gain * silu
        do_s[...] = rr * (d_on - on * jnp.mean(d_on * on, axis=-1, keepdims=True))

        r64 = lax.broadcasted_iota(jnp.int32, (cc, cc), 0)
        c64 = lax.broadcasted_iota(jnp.int32, (cc, cc), 1)
        upper = r64 <= c64
        for n in reversed(range(nc)):
            sl = slice(n * cc, (n + 1) * cc)
            q_c, k_c, v_c, b_c = q[sl], k[sl], v[sl], b[sl]
            do_c = do_s[sl, :]
            bl = b_s[n * cc + cc - 1:n * cc + cc, :]
            ebl = jnp.exp(bl)
            eq, qe, es, kes = _hgrn_chunk_terms(q_c, k_c, b_c, b_s, n * cc)
            qeb = qe.astype(BF16)
            dob = do_c.astype(BF16)
            vb = v_c.astype(BF16)
            st = st_ref[0, n]
            dstn = dst_s[...]
            qb_ = q_c * jnp.exp(b_c)
            kb_ = k_c * jnp.exp(bl - b_c)
            at = jnp.zeros((cc, cc), F32)
            for i in range(nsub):
                blk_i = (c64 // HGRN_SUB) == i
                at = at + jnp.where(blk_i, lax.dot_general(kes[i], qeb, NT, preferred_element_type=F32), 0.0)
            at = jnp.where(upper, at, 0.0)
            dv = _dot(at, dob) + _dot(kb_, dstn, NT)
            dqb = _dot(dob, st)
            dkb = _dot(vb, dstn)
            da = jnp.where(r64 >= c64, lax.dot_general(dob, vb, NT, preferred_element_type=F32), 0.0)
            dat = jnp.where(upper, lax.dot_general(vb, dob, NT, preferred_element_type=F32), 0.0)
            dab = da.astype(BF16)
            dq_raw = jnp.concatenate(
                [lax.dot_general(dab[i * HGRN_SUB:(i + 1) * HGRN_SUB], kes[i], NN, preferred_element_type=F32)
                 for i in range(nsub)], axis=0)
            db = qb_.astype(BF16).astype(F32) * dqb + qeb.astype(F32) * dq_raw
            dkbk = dkb * kb_.astype(BF16).astype(F32)
            db = db - dkbk
            dk_in = jnp.zeros((cc, HEAD_DIM), F32)
            for i in range(nsub):
                blk_i = (c64 // HGRN_SUB) == i
                dk_raw = _dot(jnp.where(blk_i, dat, 0.0), qeb)
                dk_in = dk_in + dk_raw * es[i]
                db = db - kes[i].astype(F32) * dk_raw
            dqs[sl, :] = dqb * jnp.exp(b_c) + dq_raw * eq
            dks[sl, :] = dkb * jnp.exp(bl - b_c) + dk_in
            dbs[sl, :] = db
            extra = jnp.sum(dkbk, axis=0, keepdims=True) + jnp.sum(dstn * st, axis=0, keepdims=True) * ebl
            exs[sl, :] = jnp.broadcast_to(extra, (cc, HEAD_DIM))
            di_ref[sl, :] = dv.astype(BF16)
            dst_s[...] = dstn * ebl + _dot(dob, qb_, TN)

        dg = _dot_exact(triu, dbs[...]) + exs[...]
        df = dg / f - dks[...]
        df_ref[...] = (df * (1.0 - lb) * sg * (1.0 - sg)).astype(BF16)
        dq_ref[...] = (dqs[...] * (sq * (1.0 + qr * (1.0 - sq)))).astype(BF16)
        dlb = jnp.sum(df * (1.0 - sg), axis=0, keepdims=True) * (lb * (1.0 - lb))
        dlog_ref[0:1, :] += -dlb
        dlog_ref[1:2, :] += dlb

    hh = h_count
    blk = lambda off: pl.BlockSpec((t, HEAD_DIM), lambda h, i: (nb - 1 - i, off + h))
    two = pl.BlockSpec((2, HEAD_DIM), lambda h, i: (0, h))
    one = pl.BlockSpec((1, HEAD_DIM), lambda h, i: (0, h))
    act = jax.ShapeDtypeStruct((s, w), BF16)
    tile = pltpu.VMEM((t, HEAD_DIM), F32)
    return _call(
        body, name=name, grid=(hh, nb),
        in_specs=[blk(0), blk(hh), blk(2 * hh), blk(3 * hh), two, one, blk(0), blk(0),
                  pl.BlockSpec((1, nc, HEAD_DIM, HEAD_DIM), lambda h, i: (h, nb - 1 - i, 0, 0))],
        out_specs=[blk(0), blk(0), blk(0), blk(0), two, one],
        out_shape=[act, act, act, act, jax.ShapeDtypeStruct((2, w), F32), jax.ShapeDtypeStruct((1, w), F32)],
        scratch_shapes=[pltpu.VMEM((HEAD_DIM, HEAD_DIM), F32), tile, tile, tile, tile, tile, tile],
        compiler_params=_cp(("parallel", "arbitrary")),
    )(p1, p1, p1, p1, lbl, onorm, o, dy, states)


def _adamw(name, w, g, m, v):
    r, c = w.shape
    tr = r if r <= 128 else _row_block(r, 128)
    c1 = 1.0 - ADAM_B1 ** ADAM_STEP
    c2 = 1.0 - ADAM_B2 ** ADAM_STEP

    def body(w_ref, g_ref, m_ref, v_ref, d_ref, nm_ref, nv_ref):
        gv = g_ref[...]
        nm = ADAM_B1 * m_ref[...] + (1.0 - ADAM_B1) * gv
        nv = ADAM_B2 * v_ref[...] + (1.0 - ADAM_B2) * (gv * gv)
        nm_ref[...] = nm
        nv_ref[...] = nv
        d_ref[...] = -ADAM_LR * ((nm / c1) / (jnp.sqrt(nv / c2) + ADAM_EPS) + ADAM_WD * w_ref[...])

    blk = pl.BlockSpec((tr, c), lambda i: (i, 0))
    sh = jax.ShapeDtypeStruct((r, c), F32)
    return _call(body, name=name, grid=(r // tr,), in_specs=[blk] * 4, out_specs=[blk] * 3,
                 out_shape=[sh, sh, sh], compiler_params=_cp(("parallel",)))(w, g, m, v)


SLAB_BLOCK_BYTES = 2 * 1024 * 1024


def _slab_rows(r, c):
    return _row_block(r, max(16, SLAB_BLOCK_BYTES // (4 * c) // 16 * 16))


def _pair_add(name, g2, recv, core):
    _, nch, r, c = g2.shape
    tr = _slab_rows(r, c)

    grid_spec = pltpu.PrefetchScalarGridSpec(
        num_scalar_prefetch=1, grid=(nch, r // tr),
        in_specs=[pl.BlockSpec((1, 1, tr, c), lambda j, i, cr: (cr[0], j, i, 0)),
                  pl.BlockSpec((1, tr, c), lambda j, i, cr: (j, i, 0))],
        out_specs=pl.BlockSpec((1, tr, c), lambda j, i, cr: (j, i, 0)))

    def body(core_ref, a_ref, b_ref, o_ref):
        o_ref[...] = (a_ref[0] + b_ref[...]).astype(BF16)

    return _call(body, name=name, grid_spec=grid_spec, out_shape=jax.ShapeDtypeStruct((nch, r, c), BF16),
                 compiler_params=_cp(("parallel", "parallel")))(core, g2, recv)


def _sum_slots(name, x, core=None):
    n, r, c = x.shape
    tr = _slab_rows(r, c)

    def total(x_ref):
        acc = x_ref[0].astype(F32)
        for j in range(1, n):
            acc = acc + x_ref[j].astype(F32)
        return acc

    if core is None:
        def body(x_ref, o_ref):
            o_ref[...] = total(x_ref)

        return _call(body, name=name, grid=(r // tr,),
                     in_specs=[pl.BlockSpec((n, tr, c), lambda i: (0, i, 0))],
                     out_specs=pl.BlockSpec((tr, c), lambda i: (i, 0)),
                     out_shape=jax.ShapeDtypeStruct((r, c), F32), compiler_params=_cp(("parallel",)))(x)

    def body_half(core_ref, x_ref, o_ref):
        o_ref[0] = total(x_ref)

    grid_spec = pltpu.PrefetchScalarGridSpec(
        num_scalar_prefetch=1, grid=(r // tr,),
        in_specs=[pl.BlockSpec((n, tr, c), lambda i, cr: (0, i, 0))],
        out_specs=pl.BlockSpec((1, tr, c), lambda i, cr: (cr[0], i, 0)))
    return _call(body_half, name=name, grid_spec=grid_spec, out_shape=jax.ShapeDtypeStruct((2, r, c), F32),
                 compiler_params=_cp(("parallel",)))(core, x)


def _pos():
    return lax.axis_index("x"), lax.axis_index("y"), lax.axis_index("c")


def _flip(v, f):
    return (1 - v) if f else v


CHIP_FLIPS = ((0, 1), (1, 0), (1, 1))
DEV_FLIPS = tuple((fx, fy, fc) for fx in (0, 1) for fy in (0, 1) for fc in (0, 1))[1:]


def _remote(src, dst, ssem, rsem, dev):
    return pltpu.make_async_remote_copy(src_ref=src, dst_ref=dst, send_sem=ssem, recv_sem=rsem,
                                        device_id=dev, device_id_type=MESH)


def _ag_weights(name, halves):
    n = len(halves)

    def body(*refs):
        h_refs, big_refs, (lsem, ssem, rsem) = refs[:n], refs[n:2 * n], refs[2 * n:]
        x, y, cc = _pos()
        me = 2 * x + y
        sib = (x, y, 1 - cc)
        peers = [(_flip(x, fx), _flip(y, fy), cc) for fx, fy in CHIP_FLIPS]
        chips = [2 * px + py for px, py, _ in peers]
        pending = []
        for a in range(n):
            mine = big_refs[a].at[cc, me]
            loc = pltpu.make_async_copy(h_refs[a], mine, lsem.at[a])
            loc.start()
            pending.append(loc.wait)
            for k in range(3):
                cp = _remote(h_refs[a], mine, ssem.at[7 * a + k], rsem.at[7 * a + k], peers[k])
                cp.start()
                pending.append(cp.wait_send)
            cp = _remote(h_refs[a], mine, ssem.at[7 * a + 3], rsem.at[7 * a + 3], sib)
            cp.start()
            pending.append(cp.wait_send)
        for k in range(3):
            for a in range(n):
                blk = big_refs[a].at[cc, chips[k]]
                _remote(blk, blk, ssem.at[7 * a + k], rsem.at[7 * a + k], peers[k]).wait_recv()
                fw = _remote(blk, blk, ssem.at[7 * a + 4 + k], rsem.at[7 * a + 4 + k], sib)
                fw.start()
                pending.append(fw.wait_send)
        for a in range(n):
            theirs = big_refs[a].at[1 - cc, me]
            _remote(theirs, theirs, ssem.at[7 * a + 3], rsem.at[7 * a + 3], sib).wait_recv()
            for k in range(3):
                blk = big_refs[a].at[1 - cc, chips[k]]
                _remote(blk, blk, ssem.at[7 * a + 4 + k], rsem.at[7 * a + 4 + k], sib).wait_recv()
        for wait in pending:
            wait()

    return _call(body, name=name, in_specs=[ANY] * n, out_specs=[ANY] * n,
                 out_shape=[jax.ShapeDtypeStruct((2, 4) + h.shape, h.dtype) for h in halves],
                 scratch_shapes=[pltpu.SemaphoreType.DMA((n,)), pltpu.SemaphoreType.DMA((7 * n,)),
                                 pltpu.SemaphoreType.DMA((7 * n,))])(*halves)


def _ag_chips(name, v):
    r, c = v.shape

    def body(v_ref, out_ref, lsem, ssem, rsem):
        x, y, cc = _pos()
        me = 2 * x + y
        loc = pltpu.make_async_copy(v_ref, out_ref.at[me], lsem)
        loc.start()
        started = []
        for k, (fx, fy) in enumerate(CHIP_FLIPS):
            cp = _remote(v_ref, out_ref.at[me], ssem.at[k], rsem.at[k], (_flip(x, fx), _flip(y, fy), cc))
            cp.start()
            started.append(cp)
        for k, (fx, fy) in enumerate(CHIP_FLIPS):
            blk = out_ref.at[2 * _flip(x, fx) + _flip(y, fy)]
            _remote(blk, blk, ssem.at[k], rsem.at[k], (_flip(x, fx), _flip(y, fy), cc)).wait_recv()
        for cp in started:
            cp.wait_send()
        loc.wait()

    return _call(body, name=name, in_specs=[ANY], out_specs=ANY,
                 out_shape=jax.ShapeDtypeStruct((4, r, c), v.dtype),
                 scratch_shapes=[pltpu.SemaphoreType.DMA, pltpu.SemaphoreType.DMA((3,)), pltpu.SemaphoreType.DMA((3,))])(v)


def _ag_devices(name, v):
    r, c = v.shape

    def body(v_ref, out_ref, lsem, ssem, rsem):
        x, y, cc = _pos()
        me = 4 * x + 2 * y + cc
        loc = pltpu.make_async_copy(v_ref, out_ref.at[me], lsem)
        loc.start()
        started = []
        for k, (fx, fy, fc) in enumerate(DEV_FLIPS):
            cp = _remote(v_ref, out_ref.at[me], ssem.at[k], rsem.at[k], (_flip(x, fx), _flip(y, fy), _flip(cc, fc)))
            cp.start()
            started.append(cp)
        for k, (fx, fy, fc) in enumerate(DEV_FLIPS):
            px, py, pc = _flip(x, fx), _flip(y, fy), _flip(cc, fc)
            blk = out_ref.at[4 * px + 2 * py + pc]
            _remote(blk, blk, ssem.at[k], rsem.at[k], (px, py, pc)).wait_recv()
        for cp in started:
            cp.wait_send()
        loc.wait()

    return _call(body, name=name, in_specs=[ANY], out_specs=ANY,
                 out_shape=jax.ShapeDtypeStruct((8, r, c), v.dtype),
                 scratch_shapes=[pltpu.SemaphoreType.DMA, pltpu.SemaphoreType.DMA((7,)), pltpu.SemaphoreType.DMA((7,))])(v)


def _sibling_other_half(name, g2s):
    n = len(g2s)

    def body(*refs):
        g_refs, out_refs, (ssem, rsem) = refs[:n], refs[n:2 * n], refs[2 * n:]
        x, y, cc = _pos()
        cps = [_remote(g_refs[a].at[1 - cc], out_refs[a], ssem.at[a], rsem.at[a], (x, y, 1 - cc)) for a in range(n)]
        for cp in cps:
            cp.start()
        for cp in cps:
            cp.wait()

    return _call(body, name=name, in_specs=[ANY] * n, out_specs=[ANY] * n,
                 out_shape=[jax.ShapeDtypeStruct(g.shape[1:], g.dtype) for g in g2s],
                 scratch_shapes=[pltpu.SemaphoreType.DMA((n,)), pltpu.SemaphoreType.DMA((n,))])(*g2s)


def _scatter_chips(name, ps):
    n = len(ps)

    def body(*refs):
        p_refs, out_refs, (lsem, ssem, rsem) = refs[:n], refs[n:2 * n], refs[2 * n:]
        x, y, cc = _pos()
        me = 2 * x + y
        peers = [(_flip(x, fx), _flip(y, fy), cc) for fx, fy in CHIP_FLIPS]
        pending = []
        for a in range(n):
            loc = pltpu.make_async_copy(p_refs[a].at[me], out_refs[a].at[me], lsem.at[a])
            loc.start()
            pending.append(loc.wait)
            for k, (px, py, _) in enumerate(peers):
                cp = _remote(p_refs[a].at[2 * px + py], out_refs[a].at[me], ssem.at[3 * a + k], rsem.at[3 * a + k], peers[k])
                cp.start()
                pending.append(cp.wait_send)
        for a in range(n):
            for k, (px, py, _) in enumerate(peers):
                blk = out_refs[a].at[2 * px + py]
                _remote(blk, blk, ssem.at[3 * a + k], rsem.at[3 * a + k], peers[k]).wait_recv()
        for wait in pending:
            wait()

    return _call(body, name=name, in_specs=[ANY] * n, out_specs=[ANY] * n,
                 out_shape=[jax.ShapeDtypeStruct(p.shape, p.dtype) for p in ps],
                 scratch_shapes=[pltpu.SemaphoreType.DMA((n,)), pltpu.SemaphoreType.DMA((3 * n,)),
                                 pltpu.SemaphoreType.DMA((3 * n,))])(*ps)


def _sibling_join(name, bufs):
    n = len(bufs)

    def body(*refs):
        out_refs, (ssem, rsem) = refs[n:2 * n], refs[2 * n:]
        x, y, cc = _pos()
        sib = (x, y, 1 - cc)
        cps = [_remote(out_refs[a].at[cc], out_refs[a].at[cc], ssem.at[a], rsem.at[a], sib) for a in range(n)]
        for cp in cps:
            cp.start()
        for a in range(n):
            theirs = out_refs[a].at[1 - cc]
            _remote(theirs, theirs, ssem.at[a], rsem.at[a], sib).wait_recv()
        for cp in cps:
            cp.wait_send()

    return _call(body, name=name, in_specs=[ANY] * n, out_specs=[ANY] * n,
                 out_shape=[jax.ShapeDtypeStruct(b.shape, b.dtype) for b in bufs],
                 input_output_aliases={a: a for a in range(n)},
                 scratch_shapes=[pltpu.SemaphoreType.DMA((n,)), pltpu.SemaphoreType.DMA((n,))])(*bufs)


def _pad_lanes(a, width=128):
    return jnp.pad(a, ((0, 0), (0, width - a.shape[1])))


def kernel(x, norm_gains, fox_w_in, fox_b_f, hgrn_w_in, hgrn_lb_logits, hgrn_onorm, w_out, final_gain, loss_target, m_norm_gains, m_fox_w_in, m_fox_b_f, m_hgrn_w_in, m_hgrn_lb_logits, m_hgrn_onorm, m_w_out, m_final_gain, v_norm_gains, v_fox_w_in, v_fox_b_f, v_hgrn_w_in, v_hgrn_lb_logits, v_hgrn_onorm, v_w_out, v_final_gain):
    s, d = x.shape[1], x.shape[2]
    wq = w_out.shape[1]
    w = 4 * wq
    hh = w // HEAD_DIM
    fox_cols = fox_w_in.shape[2]
    assert 4 * fox_cols == 4 * w + hh and hgrn_w_in.shape[2] == w
    core = lax.axis_index("c")

    x0 = x[0]
    tgt = loss_target[0]

    def my_half(a):
        return lax.dynamic_index_in_dim(a, core, 0, keepdims=False).astype(BF16)

    big_fox, big_hgrn, big_out = _ag_weights("ag_weights", [
        my_half(fox_w_in.reshape(2, d // 2, fox_cols)), my_half(hgrn_w_in.reshape(2, d // 2, w)), my_half(w_out)])
    fox_chip = [big_fox[:, j].reshape(d, fox_cols) for j in range(4)]
    def fox_columns(a, b):
        out = []
        for j in range(4):
            lo, hi = max(a, j * fox_cols), min(b, (j + 1) * fox_cols)
            if lo < hi:
                out.append(fox_chip[j][:, lo - j * fox_cols:hi - j * fox_cols])
        return out

    w_main = jnp.concatenate(fox_columns(0, 3 * w) + fox_columns(3 * w + hh, 4 * w + hh), axis=1)
    w_fl = _pad_lanes(jnp.concatenate(fox_columns(3 * w, 3 * w + hh), axis=1))
    w_h = big_hgrn.transpose(0, 2, 1, 3).reshape(d, 4 * w)
    w_o = big_out.reshape(2, w, d)
    onorm_full = _ag_chips("ag_onorm", hgrn_onorm.reshape(wq // 128, 128)).reshape(1, w)

    bf_pad = _pad_lanes(fox_b_f)
    g0, g1 = norm_gains[0:1], norm_gains[1:2]
    gf = final_gain.reshape(1, d)

    h0 = _rms_fwd("rms0_fwd", x0, g0)
    qkv = _mm("fox_qkv", h0, w_main, out_dtype=BF16, n=3 * w)
    gate0 = _mm("fox_gate", h0, w_main, n=w, b_off=3 * w)
    fl = _mm("fox_flogit", h0, w_fl)
    _, c_rep = _fox_gate_fwd("fox_cumsum", fl, bf_pad, w)
    o0, lse_row = _attn_fwd("fox_attn_fwd", qkv, c_rep, hh)
    y0 = _fox_post("fox_post", o0, gate0)
    x1 = _mm("fox_out", y0, w_o[0], res=x0)
    h1 = _rms_fwd("rms1_fwd", x1, g1)
    p1 = _mm("hgrn_in", h1, w_h)
    y1, o1, states = _hgrn_fwd("hgrn_fwd", p1, hgrn_lb_logits, onorm_full, hh)
    x2 = _mm("hgrn_out", y1, w_o[1], res=x1)
    dx2, d_gf, loss_tile = _loss_head("loss_head", x2, tgt, gf)

    dy1 = _mm("hgrn_out_dy", dx2, w_o[1], tb=True)
    d_wo1 = _mm("hgrn_out_dw", y1, dx2, ta=True)
    dq1, df1, di1, dgate1, d_lbl, d_onorm = _hgrn_bwd("hgrn_bwd", p1, hgrn_lb_logits, onorm_full, o1, dy1, states, hh)
    dp1 = jnp.concatenate([dq1, df1, di1, dgate1], axis=1)
    dh1 = _mm("hgrn_in_dh", dp1, w_h, tb=True)
    d_wh = _mm("hgrn_in_dw", h1, dp1, ta=True)
    dx1, d_g1 = _rms_bwd("rms1_bwd", x1, g1, dh1, dx2)
    dy0 = _mm("fox_out_dy", dx1, w_o[0], tb=True)
    d_wo0 = _mm("fox_out_dw", y0, dx1, ta=True)
    do0, dgate0, delta_rep = _fox_pre_bwd("fox_pre_bwd", dy0, o0, gate0)
    delta_row = delta_rep[:, ::HEAD_DIM].T.reshape(hh, 1, s)
    dk0, dv0, dc_rep, dq0, rowsum_row = _attn_bwd("fox_attn_bwd", qkv, do0, lse_row, delta_row, c_rep, hh)
    dfl, d_bf = _fox_gate_bwd("fox_cumsum_bwd", _pad_lanes(rowsum_row.reshape(hh, s).T),
                              _pad_lanes(dc_rep[:, ::HEAD_DIM]), fl, bf_pad)
    dp0 = jnp.concatenate([dq0, dk0, dv0, dgate0], axis=1)
    dh0 = _mm("fox_in_dh", dp0, w_main, tb=True, res=_mm("fox_fl_dh", dfl, w_fl, tb=True))
    d_wmain = _mm("fox_in_dw", h0, dp0, ta=True)
    d_wfl = _mm("fox_fl_dw", h0, dfl, ta=True)
    grad_x, d_g0 = _rms_bwd("rms0_bwd", x0, g0, dh0, dx1)

    def grad_columns(a, b):
        out = []
        for lo, hi, src, shift in ((0, 3 * w, d_wmain, 0), (3 * w, 3 * w + hh, d_wfl, 3 * w),
                                   (3 * w + hh, 4 * w + hh, d_wmain, hh)):
            l2, h2 = max(a, lo), min(b, hi)
            if l2 < h2:
                out.append(src[:, l2 - shift:h2 - shift])
        return out

    g2_fox = jnp.stack([jnp.concatenate(grad_columns(j * fox_cols, (j + 1) * fox_cols), axis=1).reshape(2, d // 2, fox_cols)
                        for j in range(4)], axis=1)
    g2_hgrn = d_wh.reshape(2, d // 2, 4, w).transpose(0, 2, 1, 3)
    g2_out = jnp.stack([d_wo0, d_wo1]).reshape(2, 4, wq, d)
    g2s = [g2_fox, g2_hgrn, g2_out]
    core_arr = core.reshape(1).astype(jnp.int32)
    from_sibling = _sibling_other_half("rs_sibling", g2s)
    pairs = [_pair_add("rs_pair_add_" + nm, g2, rcv, core_arr)
             for nm, g2, rcv in zip(("fox", "hgrn", "out"), g2s, from_sibling)]
    from_chips = _scatter_chips("rs_scatter", pairs)
    halves_sum = [_sum_slots("rs_sum_" + nm, fc, core_arr) for nm, fc in zip(("fox", "hgrn", "out"), from_chips)]
    r_fox, r_hgrn, r_out = _sibling_join("rs_join", halves_sum)
    g_fox = r_fox.reshape(d, fox_cols)
    g_hgrn = r_hgrn.reshape(d, w)
    g_out = r_out.reshape(2 * wq, d)

    small = jnp.concatenate([jnp.concatenate([d_g0, d_g1], axis=0).reshape(-1), d_lbl.reshape(-1), d_gf.reshape(-1),
                             d_onorm.reshape(-1), d_bf.reshape(-1)])
    n_small = small.shape[0]
    pad_to = -(-n_small // 1024) * 1024
    small = jnp.pad(small, (0, pad_to - n_small)).reshape(pad_to // 128, 128)
    small = _sum_slots("small_sum", _ag_devices("small_gather", small)).reshape(-1)
    g_norm = small[:2 * d].reshape(2, d)
    g_lbl = small[2 * d:2 * d + 2 * w].reshape(2, w)
    g_gf = small[2 * d + 2 * w:3 * d + 2 * w]
    g_onorm_full = small[3 * d + 2 * w:3 * d + 3 * w]
    g_bf = small[3 * d + 3 * w:3 * d + 3 * w + hh].reshape(1, hh)
    chip = 2 * lax.axis_index("x") + lax.axis_index("y")
    g_onorm = lax.dynamic_slice_in_dim(g_onorm_full, chip * wq, wq).reshape(1, wq)

    loss = lax.psum(loss_tile[0, 0], ("x", "y", "c"))

    def upd(name, wt, g, m, v):
        shp = wt.shape
        two = lambda a: a.reshape(-1, shp[-1])
        dl, nm, nv = _adamw(name, two(wt), two(g), two(m), two(v))
        return g.reshape(shp), dl.reshape(shp), nm.reshape(shp), nv.reshape(shp)

    res = [
        upd("adamw_norm_gains", norm_gains, g_norm, m_norm_gains, v_norm_gains),
        upd("adamw_fox_w_in", fox_w_in, g_fox, m_fox_w_in, v_fox_w_in),
        upd("adamw_fox_b_f", fox_b_f, g_bf, m_fox_b_f, v_fox_b_f),
        upd("adamw_hgrn_w_in", hgrn_w_in, g_hgrn, m_hgrn_w_in, v_hgrn_w_in),
        upd("adamw_lb_logits", hgrn_lb_logits, g_lbl, m_hgrn_lb_logits, v_hgrn_lb_logits),
        upd("adamw_onorm", hgrn_onorm, g_onorm, m_hgrn_onorm, v_hgrn_onorm),
        upd("adamw_w_out", w_out, g_out, m_w_out, v_w_out),
        upd("adamw_final_gain", final_gain.reshape(1, d), g_gf.reshape(1, d), m_final_gain.reshape(1, d),
            v_final_gain.reshape(1, d)),
    ]
    res[-1] = tuple(a.reshape(d) for a in res[-1])
    grads, deltas, new_m, new_v = zip(*res)
    return (loss, grad_x[None], *grads, *deltas, *new_m, *new_v)
```

```python
import functools

import jax
import jax.numpy as jnp
from jax import lax
from jax.experimental import pallas as pl
from jax.experimental.pallas import tpu as pltpu

F32 = jnp.float32
BF16 = jnp.bfloat16
MESH = pl.DeviceIdType.MESH
ANY = pl.BlockSpec(memory_space=pl.ANY)

EPS = 1e-6
HEAD_DIM = 128
HGRN_CHUNK = 64
HGRN_SUB = 32
EXP_CLAMP = 80.0
ATT_BLOCK = 512
HGRN_BLOCK = 512
GATE_BLOCK = 512
ROW_BLOCK = 256
MM_TM, MM_TN, MM_TK = 1024, 1024, 2048
VMEM_LIMIT_V7X = 56 * 1024 * 1024

ADAM_LR, ADAM_B1, ADAM_B2, ADAM_EPS, ADAM_WD, ADAM_STEP = 0.001, 0.9, 0.999, 1e-08, 0.01, 10

NT = (((1,), (1,)), ((), ()))
TN = (((0,), (0,)), ((), ()))
NN = (((1,), (0,)), ((), ()))


def _call(body, **kw):
    return pl.pallas_call(body, **kw)


def _cp(dims=None):
    kw = dict(vmem_limit_bytes=VMEM_LIMIT_V7X)
    if dims is not None:
        kw["dimension_semantics"] = dims
    return pltpu.CompilerParams(**kw)


def _sigmoid(x):
    return 1.0 / (1.0 + jnp.exp(-x))


def _dot(a, b, dn=NN):
    return lax.dot_general(a.astype(BF16), b.astype(BF16), dn, preferred_element_type=F32)


def _split3(x):
    hi = x.astype(BF16)
    r1 = x - hi.astype(F32)
    mid = r1.astype(BF16)
    lo = (r1 - mid.astype(F32)).astype(BF16)
    return hi, mid, lo


def _dot_exact(m01, x, right=False):
    hi, mid, lo = _split3(x)
    if right:
        dot = lambda p: lax.dot_general(p, m01, NN, preferred_element_type=F32)
    else:
        dot = lambda p: lax.dot_general(m01, p, NN, preferred_element_type=F32)
    return dot(hi) + dot(mid) + dot(lo)


def _row_block(rows, cap):
    if rows <= cap:
        return rows
    best = None
    for t in range(16, cap + 1, 16):
        if rows % t == 0:
            best = t
    assert best is not None, rows
    return best


def _mm(name, a, b, *, ta=False, tb=False, out_dtype=F32, res=None, n=None, b_off=0):
    m, k = (a.shape[1], a.shape[0]) if ta else a.shape
    n_full = b.shape[0] if tb else b.shape[1]
    n = n_full if n is None else n
    tm, tn, tk = min(MM_TM, m), min(MM_TN, n), min(MM_TK, k)
    assert m % tm == 0 and n % tn == 0 and k % tk == 0 and b_off % tn == 0
    nk = k // tk
    jo = b_off // tn
    a_spec = pl.BlockSpec((tk, tm), lambda i, j, kk: (kk, i)) if ta else pl.BlockSpec((tm, tk), lambda i, j, kk: (i, kk))
    b_spec = pl.BlockSpec((tn, tk), lambda i, j, kk: (j + jo, kk)) if tb else pl.BlockSpec((tk, tn), lambda i, j, kk: (kk, j + jo))
    o_spec = pl.BlockSpec((tm, tn), lambda i, j, kk: (i, j))
    dn = (((0 if ta else 1,), (1 if tb else 0,)), ((), ()))
    has_res = res is not None

    def body(*refs):
        if has_res:
            a_ref, b_ref, r_ref, o_ref, acc = refs
        else:
            a_ref, b_ref, o_ref, acc = refs
            r_ref = None
        kk = pl.program_id(2)
        p = lax.dot_general(a_ref[...].astype(BF16), b_ref[...].astype(BF16), dn, preferred_element_type=F32)

        def finish(total):
            if r_ref is not None:
                total = total + r_ref[...]
            o_ref[...] = total.astype(out_dtype)

        if nk == 1:
            finish(p)
        else:
            @pl.when(kk == 0)
            def _():
                acc[...] = p

            @pl.when(jnp.logical_and(kk > 0, kk < nk - 1))
            def _():
                acc[...] += p

            @pl.when(kk == nk - 1)
            def _():
                finish(acc[...] + p)

    ins = [a, b] + ([res] if has_res else [])
    in_specs = [a_spec, b_spec] + ([o_spec] if has_res else [])
    return _call(
        body, name=name, grid=(m // tm, n // tn, nk), in_specs=in_specs, out_specs=o_spec,
        out_shape=jax.ShapeDtypeStruct((m, n), out_dtype),
        scratch_shapes=[pltpu.VMEM((tm, tn) if nk > 1 else (8, 128), F32)],
        compiler_params=_cp(("parallel", "parallel", "arbitrary")),
    )(*ins)


def _rms_fwd(name, x, g):
    s, d = x.shape
    tm = min(ROW_BLOCK, s)

    def body(x_ref, g_ref, h_ref):
        xv = x_ref[...]
        r = lax.rsqrt(jnp.mean(xv * xv, axis=-1, keepdims=True) + EPS)
        h_ref[...] = (xv * r * g_ref[...]).astype(BF16)

    row = pl.BlockSpec((tm, d), lambda i: (i, 0))
    vec = pl.BlockSpec((1, d), lambda i: (0, 0))
    return _call(body, name=name, grid=(s // tm,), in_specs=[row, vec], out_specs=row,
                 out_shape=jax.ShapeDtypeStruct((s, d), BF16), compiler_params=_cp(("parallel",)))(x, g)


def _rms_bwd(name, x, g, dh, dres):
    s, d = x.shape
    tm = min(ROW_BLOCK, s)

    def body(x_ref, g_ref, dh_ref, dres_ref, dx_ref, dg_ref):
        @pl.when(pl.program_id(0) == 0)
        def _():
            dg_ref[...] = jnp.zeros_like(dg_ref)

        xv = x_ref[...]
        r = lax.rsqrt(jnp.mean(xv * xv, axis=-1, keepdims=True) + EPS)
        xn = xv * r
        dhv = dh_ref[...]
        dxn = dhv * g_ref[...]
        dx_ref[...] = dres_ref[...] + r * (dxn - xn * jnp.mean(dxn * xn, axis=-1, keepdims=True))
        dg_ref[...] += jnp.sum(dhv * xn, axis=0, keepdims=True)

    row = pl.BlockSpec((tm, d), lambda i: (i, 0))
    vec = pl.BlockSpec((1, d), lambda i: (0, 0))
    return _call(body, name=name, grid=(s // tm,), in_specs=[row, vec, row, row], out_specs=[row, vec],
                 out_shape=[jax.ShapeDtypeStruct((s, d), F32), jax.ShapeDtypeStruct((1, d), F32)],
                 compiler_params=_cp(("arbitrary",)))(x, g, dh, dres)


def _loss_head(name, x, tgt, g):
    s, d = x.shape
    tm = min(ROW_BLOCK, s)
    nb = s // tm

    def body(x_ref, t_ref, g_ref, dx_ref, dg_ref, loss_ref, lacc):
        i = pl.program_id(0)

        @pl.when(i == 0)
        def _():
            dg_ref[...] = jnp.zeros_like(dg_ref)
            lacc[...] = jnp.zeros_like(lacc)

        xv = x_ref[...]
        gv = g_ref[...]
        r = lax.rsqrt(jnp.mean(xv * xv, axis=-1, keepdims=True) + EPS)
        xn = xv * r
        err = xn * gv - t_ref[...]
        lacc[...] += jnp.sum(err * err, axis=0, keepdims=True)
        dout = err * (1.0 / d)
        dg_ref[...] += jnp.sum(dout * xn, axis=0, keepdims=True)
        dxn = dout * gv
        dx_ref[...] = r * (dxn - xn * jnp.mean(dxn * xn, axis=-1, keepdims=True))

        @pl.when(i == nb - 1)
        def _():
            total = jnp.sum(lacc[...], axis=1, keepdims=True) * (0.5 / d)
            loss_ref[...] = jnp.broadcast_to(total, loss_ref.shape)

    row = pl.BlockSpec((tm, d), lambda i: (i, 0))
    vec = pl.BlockSpec((1, d), lambda i: (0, 0))
    one = pl.BlockSpec((1, 128), lambda i: (0, 0))
    return _call(body, name=name, grid=(nb,), in_specs=[row, row, vec], out_specs=[row, vec, one],
                 out_shape=[jax.ShapeDtypeStruct((s, d), F32), jax.ShapeDtypeStruct((1, d), F32),
                            jax.ShapeDtypeStruct((1, 128), F32)],
                 scratch_shapes=[pltpu.VMEM((1, d), F32)], compiler_params=_cp(("arbitrary",)))(x, tgt, g)


def _fox_gate_fwd(name, fl, bf, w):
    s = fl.shape[0]
    tb = min(GATE_BLOCK, s)

    def body(fl_ref, bf_ref, c_ref, crep_ref, carry):
        @pl.when(pl.program_id(0) == 0)
        def _():
            carry[...] = jnp.zeros_like(carry)

        z = fl_ref[...] + bf_ref[...]
        lf = jnp.minimum(z, 0.0) - jnp.log(1.0 + jnp.exp(-jnp.abs(z)))
        rows = lax.broadcasted_iota(jnp.int32, (tb, tb), 0)
        cols = lax.broadcasted_iota(jnp.int32, (tb, tb), 1)
        tri = (rows >= cols).astype(BF16)
        cs = _dot_exact(tri, lf) + carry[...]
        c_ref[...] = cs
        carry[...] = c_ref[tb - 1:tb, :]
        sel_r = lax.broadcasted_iota(jnp.int32, (128, w), 0)
        sel_c = lax.broadcasted_iota(jnp.int32, (128, w), 1)
        sel = (sel_r == sel_c // HEAD_DIM).astype(BF16)
        crep_ref[...] = _dot_exact(sel, cs, right=True)

    blk = pl.BlockSpec((tb, 128), lambda i: (i, 0))
    return _call(body, name=name, grid=(s // tb,),
                 in_specs=[blk, pl.BlockSpec((1, 128), lambda i: (0, 0))],
                 out_specs=[blk, pl.BlockSpec((tb, w), lambda i: (i, 0))],
                 out_shape=[jax.ShapeDtypeStruct((s, 128), F32), jax.ShapeDtypeStruct((s, w), F32)],
                 scratch_shapes=[pltpu.VMEM((1, 128), F32)], compiler_params=_cp(("arbitrary",)))(fl, bf)


def _fox_gate_bwd(name, drow, dcol, fl, bf):
    s = fl.shape[0]
    tb = min(GATE_BLOCK, s)
    nb = s // tb

    def body(dr_ref, dc_ref, fl_ref, bf_ref, dfl_ref, dbf_ref, carry, tmp):
        @pl.when(pl.program_id(0) == 0)
        def _():
            carry[...] = jnp.zeros_like(carry)
            dbf_ref[...] = jnp.zeros_like(dbf_ref)

        rows = lax.broadcasted_iota(jnp.int32, (tb, tb), 0)
        cols = lax.broadcasted_iota(jnp.int32, (tb, tb), 1)
        triu = (rows <= cols).astype(BF16)
        dlf = _dot_exact(triu, dr_ref[...] + dc_ref[...]) + carry[...]
        tmp[...] = dlf
        carry[...] = tmp[0:1, :]
        z = fl_ref[...] + bf_ref[...]
        dfl = dlf * (1.0 / (1.0 + jnp.exp(z)))
        dfl_ref[...] = dfl
        dbf_ref[...] += jnp.sum(dfl, axis=0, keepdims=True)

    blk = pl.BlockSpec((tb, 128), lambda i: (nb - 1 - i, 0))
    vec = pl.BlockSpec((1, 128), lambda i: (0, 0))
    return _call(body, name=name, grid=(nb,), in_specs=[blk, blk, blk, vec], out_specs=[blk, vec],
                 out_shape=[jax.ShapeDtypeStruct((s, 128), F32), jax.ShapeDtypeStruct((1, 128), F32)],
                 scratch_shapes=[pltpu.VMEM((1, 128), F32), pltpu.VMEM((tb, 128), F32)],
                 compiler_params=_cp(("arbitrary",)))(drow, dcol, fl, bf)


def _attn_fwd(name, qkv, c_rep, h_count):
    s = qkv.shape[0]
    w = h_count * HEAD_DIM
    t = min(ATT_BLOCK, s)
    scale = HEAD_DIM ** -0.5
    hp = 2 if h_count % 2 == 0 else 1
    wb = hp * HEAD_DIM

    def body(q_ref, k_ref, v_ref, c_ref, o_ref, lse_ref, m_s, l_s, acc_s):
        qi = pl.program_id(1)
        m_s[...] = jnp.full(m_s.shape, -jnp.inf, F32)
        l_s[...] = jnp.zeros_like(l_s)
        acc_s[...] = jnp.zeros_like(acc_s)

        def step(kb, masked):
            off = pl.multiple_of(kb * t, t)
            for a in range(hp):
                cols_a = slice(a * HEAD_DIM, (a + 1) * HEAD_DIM)
                kk = k_ref[pl.ds(off, t), cols_a]
                vv = v_ref[pl.ds(off, t), cols_a]
                cc = jnp.tile(c_ref[pl.ds(off, t), cols_a], (1, t // HEAD_DIM))
                st = lax.dot_general(kk, q_ref[:, cols_a], NT, preferred_element_type=F32) * scale - cc
                if masked:
                    rows = lax.broadcasted_iota(jnp.int32, (t, t), 0)
                    cols = lax.broadcasted_iota(jnp.int32, (t, t), 1)
                    st = jnp.where(cols >= rows, st, -jnp.inf)
                m_prev = m_s[a]
                m_new = jnp.maximum(m_prev, jnp.max(st, axis=0, keepdims=True))
                pt = jnp.exp(st - m_new)
                alpha = jnp.exp(m_prev - m_new)
                l_s[a] = alpha * l_s[a] + jnp.sum(pt, axis=0, keepdims=True)
                acc_s[a] = alpha * acc_s[a] + lax.dot_general(vv, pt.astype(BF16), TN, preferred_element_type=F32)
                m_s[a] = m_new

        def loop_body(kb, carry):
            step(kb, False)
            return carry

        lax.fori_loop(0, qi, loop_body, 0)
        step(qi, True)
        for a in range(hp):
            l = l_s[a]
            o_ref[:, a * HEAD_DIM:(a + 1) * HEAD_DIM] = (acc_s[a] / l).T
            lse_ref[a] = m_s[a] + jnp.log(l)

    hh = h_count
    blk = lambda off: pl.BlockSpec((t, wb), lambda h, i: (i, off + h))
    whole = lambda off: pl.BlockSpec((s, wb), lambda h, i: (0, off + h))
    rowv = pl.BlockSpec((hp, 1, t), lambda h, i: (h, 0, i))
    return _call(
        body, name=name, grid=(hh // hp, s // t),
        in_specs=[blk(0), whole(hh // hp), whole(2 * hh // hp), whole(0)],
        out_specs=[blk(0), rowv],
        out_shape=[jax.ShapeDtypeStruct((s, w), F32), jax.ShapeDtypeStruct((hh, 1, s), F32)],
        scratch_shapes=[pltpu.VMEM((hp, 1, t), F32), pltpu.VMEM((hp, 1, t), F32), pltpu.VMEM((hp, HEAD_DIM, t), F32)],
        compiler_params=_cp(("parallel", "arbitrary")),
    )(qkv, qkv, qkv, c_rep)


def _attn_bwd(name, qkv, do, lse_row, delta_row, c_rep, h_count):
    s = qkv.shape[0]
    w = h_count * HEAD_DIM
    t = min(ATT_BLOCK, s)
    nq = s // t
    scale = HEAD_DIM ** -0.5

    def body(k_ref, v_ref, q_ref, do_ref, lse_ref, dl_ref, c_ref, dk_ref, dv_ref, dc_ref, dq_ref, rs_ref,
             dk_s, dv_s, dc_s, dq_s, rs_s):
        kj = pl.program_id(1)
        kk = k_ref[...]
        vv = v_ref[...]
        ccol = jnp.tile(c_ref[...], (1, t // HEAD_DIM))
        dk_s[...] = jnp.zeros_like(dk_s)
        dv_s[...] = jnp.zeros_like(dv_s)
        dc_s[...] = jnp.zeros_like(dc_s)

        @pl.when(kj == 0)
        def _():
            dq_s[...] = jnp.zeros_like(dq_s)
            rs_s[...] = jnp.zeros_like(rs_s)

        def step(qb, masked):
            off = pl.multiple_of(qb * t, t)
            q = q_ref[pl.ds(off, t), :]
            dov = do_ref[pl.ds(off, t), :]
            st = lax.dot_general(kk, q, NT, preferred_element_type=F32) * scale - ccol
            pt = jnp.exp(st - lse_ref[0, :, pl.ds(off, t)])
            if masked:
                rows = lax.broadcasted_iota(jnp.int32, (t, t), 0)
                cols = lax.broadcasted_iota(jnp.int32, (t, t), 1)
                pt = jnp.where(cols >= rows, pt, 0.0)
            dv_s[...] += lax.dot_general(pt.astype(BF16), dov, NN, preferred_element_type=F32)
            dpt = lax.dot_general(vv, dov, NT, preferred_element_type=F32)
            dst = pt * (dpt - dl_ref[0, :, pl.ds(off, t)])
            dstb = dst.astype(BF16)
            dk_s[...] += lax.dot_general(dstb, q, NN, preferred_element_type=F32)
            dc_s[...] += jnp.sum(dst, axis=1, keepdims=True)
            dq_s[:, pl.ds(off, t)] += lax.dot_general(kk, dstb, TN, preferred_element_type=F32)
            rs_s[:, pl.ds(off, t)] += jnp.sum(dst, axis=0, keepdims=True)

        step(kj, True)

        def loop_body(qb, carry):
            step(qb, False)
            return carry

        lax.fori_loop(kj + 1, nq, loop_body, 0)
        dk_ref[...] = (dk_s[...] * scale).astype(BF16)
        dv_ref[...] = dv_s[...].astype(BF16)
        dc_ref[...] = jnp.broadcast_to(-dc_s[...], dc_ref.shape)

        @pl.when(kj == nq - 1)
        def _():
            for b in range(nq):
                dq_ref[b * t:(b + 1) * t, :] = (dq_s[:, b * t:(b + 1) * t] * scale).T.astype(BF16)
            rs_ref[0] = rs_s[...]

    hh = h_count
    blk = lambda off: pl.BlockSpec((t, HEAD_DIM), lambda h, j: (j, off + h))
    whole = pl.BlockSpec((s, HEAD_DIM), lambda h, j: (0, h))
    rowv = pl.BlockSpec((1, 1, s), lambda h, j: (h, 0, 0))
    return _call(
        body, name=name, grid=(hh, nq),
        in_specs=[blk(hh), blk(2 * hh), whole, whole, rowv, rowv, blk(0)],
        out_specs=[blk(0), blk(0), blk(0), whole, rowv],
        out_shape=[jax.ShapeDtypeStruct((s, w), BF16), jax.ShapeDtypeStruct((s, w), BF16),
                   jax.ShapeDtypeStruct((s, w), F32), jax.ShapeDtypeStruct((s, w), BF16),
                   jax.ShapeDtypeStruct((hh, 1, s), F32)],
        scratch_shapes=[pltpu.VMEM((t, HEAD_DIM), F32), pltpu.VMEM((t, HEAD_DIM), F32), pltpu.VMEM((t, 1), F32),
                        pltpu.VMEM((HEAD_DIM, s), F32), pltpu.VMEM((1, s), F32)],
        compiler_params=_cp(("parallel", "arbitrary")),
    )(qkv, qkv, qkv, do, lse_row, delta_row, c_rep)


def _fox_post(name, o, gate):
    s, w = o.shape
    tm = min(ROW_BLOCK, s)

    def body(o_ref, g_ref, y_ref):
        g = g_ref[...]
        y_ref[...] = (o_ref[...] * (g * _sigmoid(g))).astype(BF16)

    row = pl.BlockSpec((tm, w), lambda i: (i, 0))
    return _call(body, name=name, grid=(s // tm,), in_specs=[row, row], out_specs=row,
                 out_shape=jax.ShapeDtypeStruct((s, w), BF16), compiler_params=_cp(("parallel",)))(o, gate)


def _fox_pre_bwd(name, dy, o, gate):
    s, w = o.shape
    tm = min(ROW_BLOCK, s)

    def body(dy_ref, o_ref, g_ref, do_ref, dg_ref, dl_ref):
        g = g_ref[...]
        sg = _sigmoid(g)
        dyv = dy_ref[...]
        ov = o_ref[...]
        dov = dyv * (g * sg)
        do_ref[...] = dov.astype(BF16)
        dg_ref[...] = (dyv * ov * (sg * (1.0 + g * (1.0 - sg)))).astype(BF16)
        prod = dov * ov
        for h in range(w // HEAD_DIM):
            sl = slice(h * HEAD_DIM, (h + 1) * HEAD_DIM)
            dl_ref[:, sl] = jnp.broadcast_to(jnp.sum(prod[:, sl], axis=1, keepdims=True), (tm, HEAD_DIM))

    row = pl.BlockSpec((tm, w), lambda i: (i, 0))
    return _call(body, name=name, grid=(s // tm,), in_specs=[row, row, row], out_specs=[row, row, row],
                 out_shape=[jax.ShapeDtypeStruct((s, w), BF16), jax.ShapeDtypeStruct((s, w), BF16),
                            jax.ShapeDtypeStruct((s, w), F32)],
                 compiler_params=_cp(("parallel",)))(dy, o, gate)


def _hgrn_chunk_terms(q_c, k_c, b_c, b_s, base):
    nsub = HGRN_CHUNK // HGRN_SUB
    refs = [jnp.zeros((1, HEAD_DIM), F32)]
    for i in range(1, nsub):
        r0 = base + i * HGRN_SUB - 1
        refs.append(b_s[r0:r0 + 1, :])
    rfull = jnp.concatenate([jnp.broadcast_to(r, (HGRN_SUB, HEAD_DIM)) for r in refs], axis=0)
    eq = jnp.exp(b_c - rfull)
    qe = q_c * eq
    es = [jnp.exp(jnp.minimum(r - b_c, EXP_CLAMP)) for r in refs]
    kes = [(k_c * e).astype(BF16) for e in es]
    return eq, qe, es, kes


def _hgrn_block_pre(q_ref, f_ref, lbl_ref, tri, b_s):
    lb = _sigmoid(lbl_ref[1:2, :] - lbl_ref[0:1, :])
    qr = q_ref[...]
    sq = _sigmoid(qr)
    q = qr * sq
    fz = f_ref[...]
    sg = _sigmoid(fz)
    f = lb + (1.0 - lb) * sg
    g = jnp.log(f)
    k = (1.0 - lb) * (1.0 / (1.0 + jnp.exp(fz)))
    b = _dot_exact(tri, g)
    b_s[...] = b
    return lb, qr, sq, q, sg, f, k, b


def _chunk_masks(t):
    rows = lax.broadcasted_iota(jnp.int32, (t, t), 0)
    cols = lax.broadcasted_iota(jnp.int32, (t, t), 1)
    same = (rows // HGRN_CHUNK) == (cols // HGRN_CHUNK)
    return rows, cols, same


def _hgrn_fwd(name, p1, lbl, onorm, h_count):
    s = p1.shape[0]
    w = h_count * HEAD_DIM
    t = min(HGRN_BLOCK, s)
    nc = t // HGRN_CHUNK
    nsub = HGRN_CHUNK // HGRN_SUB
    cc = HGRN_CHUNK

    def body(q_ref, f_ref, i_ref, g_ref, lbl_ref, on_ref, y_ref, o_ref, st_ref, state_s, b_s):
        @pl.when(pl.program_id(1) == 0)
        def _():
            state_s[...] = jnp.zeros_like(state_s)

        rows, cols, same = _chunk_masks(t)
        tri = jnp.logical_and(same, rows >= cols).astype(BF16)
        lb, qr, sq, q, sg, f, k, b = _hgrn_block_pre(q_ref, f_ref, lbl_ref, tri, b_s)
        v = i_ref[...]
        r64 = lax.broadcasted_iota(jnp.int32, (cc, cc), 0)
        c64 = lax.broadcasted_iota(jnp.int32, (cc, cc), 1)
        for n in range(nc):
            sl = slice(n * cc, (n + 1) * cc)
            q_c, k_c, v_c, b_c = q[sl], k[sl], v[sl], b[sl]
            bl = b_s[n * cc + cc - 1:n * cc + cc, :]
            eq, qe, es, kes = _hgrn_chunk_terms(q_c, k_c, b_c, b_s, n * cc)
            qeb = qe.astype(BF16)
            a = jnp.concatenate(
                [lax.dot_general(qeb[i * HGRN_SUB:(i + 1) * HGRN_SUB], kes[i], NT, preferred_element_type=F32)
                 for i in range(nsub)], axis=0)
            a = jnp.where(r64 >= c64, a, 0.0)
            st = state_s[...]
            st_ref[0, n] = st
            inter = _dot(q_c * jnp.exp(b_c), st, NT)
            intra = _dot(a, v_c)
            o_ref[sl, :] = inter + intra
            kb = k_c * jnp.exp(bl - b_c)
            state_s[...] = st * jnp.exp(bl) + _dot(v_c, kb, TN)
        o = o_ref[...]
        rr = lax.rsqrt(jnp.mean(o * o, axis=-1, keepdims=True) + EPS)
        gate = g_ref[...]
        y_ref[...] = ((o * rr) * on_ref[...] * (gate * _sigmoid(gate))).astype(BF16)

    hh = h_count
    blk = lambda off: pl.BlockSpec((t, HEAD_DIM), lambda h, i: (i, off + h))
    return _call(
        body, name=name, grid=(hh, s // t),
        in_specs=[blk(0), blk(hh), blk(2 * hh), blk(3 * hh),
                  pl.BlockSpec((2, HEAD_DIM), lambda h, i: (0, h)), pl.BlockSpec((1, HEAD_DIM), lambda h, i: (0, h))],
        out_specs=[blk(0), blk(0), pl.BlockSpec((1, nc, HEAD_DIM, HEAD_DIM), lambda h, i: (h, i, 0, 0))],
        out_shape=[jax.ShapeDtypeStruct((s, w), BF16), jax.ShapeDtypeStruct((s, w), F32),
                   jax.ShapeDtypeStruct((hh, s // cc, HEAD_DIM, HEAD_DIM), F32)],
        scratch_shapes=[pltpu.VMEM((HEAD_DIM, HEAD_DIM), F32), pltpu.VMEM((t, HEAD_DIM), F32)],
        compiler_params=_cp(("parallel", "arbitrary")),
    )(p1, p1, p1, p1, lbl, onorm)


def _hgrn_bwd(name, p1, lbl, onorm, o, dy, states, h_count):
    s = p1.shape[0]
    w = h_count * HEAD_DIM
    t = min(HGRN_BLOCK, s)
    nb = s // t
    nc = t // HGRN_CHUNK
    nsub = HGRN_CHUNK // HGRN_SUB
    cc = HGRN_CHUNK

    def body(q_ref, f_ref, i_ref, g_ref, lbl_ref, on_ref, o_ref, dy_ref, st_ref,
             dq_ref, df_ref, di_ref, dgate_ref, dlog_ref, don_ref,
             dst_s, b_s, do_s, dqs, dks, dbs, exs):
        @pl.when(pl.program_id(1) == 0)
        def _():
            dst_s[...] = jnp.zeros_like(dst_s)
            dlog_ref[...] = jnp.zeros_like(dlog_ref)
            don_ref[...] = jnp.zeros_like(don_ref)

        rows, cols, same = _chunk_masks(t)
        tri = jnp.logical_and(same, rows >= cols).astype(BF16)
        triu = jnp.logical_and(same, rows <= cols).astype(BF16)
        lb, qr, sq, q, sg, f, k, b = _hgrn_block_pre(q_ref, f_ref, lbl_ref, tri, b_s)
        v = i_ref[...]

        ov = o_ref[...]
        rr = lax.rsqrt(jnp.mean(ov * ov, axis=-1, keepdims=True) + EPS)
        on = ov * rr
        gate = g_ref[...]
        sgt = _sigmoid(gate)
        silu = gate * sgt
        dyv = dy_ref[...]
        gain = on_ref[...]
        dgate_ref[...] = (dyv * on * gain * (sgt * (1.0 + gate * (1.0 - sgt)))).astype(BF16)
        don_ref[...] += jnp.sum(dyv * on * silu, axis=0, keepdims=True)
        d_on = dyv * gain * silu
        do_s[...] = rr * (d_on - on * jnp.mean(d_on * on, axis=-1, keepdims=True))

        r64 = lax.broadcasted_iota(jnp.int32, (cc, cc), 0)
        c64 = lax.broadcasted_iota(jnp.int32, (cc, cc), 1)
        upper = r64 <= c64
        for n in reversed(range(nc)):
            sl = slice(n * cc, (n + 1) * cc)
            q_c, k_c, v_c, b_c = q[sl], k[sl], v[sl], b[sl]
            do_c = do_s[sl, :]
            bl = b_s[n * cc + cc - 1:n * cc + cc, :]
            ebl = jnp.exp(bl)
            eq, qe, es, kes = _hgrn_chunk_terms(q_c, k_c, b_c, b_s, n * cc)
            qeb = qe.astype(BF16)
            dob = do_c.astype(BF16)
            vb = v_c.astype(BF16)
            st = st_ref[0, n]
            dstn = dst_s[...]
            qb_ = q_c * jnp.exp(b_c)
            kb_ = k_c * jnp.exp(bl - b_c)
            at = jnp.zeros((cc, cc), F32)
            for i in range(nsub):
                blk_i = (c64 // HGRN_SUB) == i
                at = at + jnp.where(blk_i, lax.dot_general(kes[i], qeb, NT, preferred_element_type=F32), 0.0)
            at = jnp.where(upper, at, 0.0)
            dv = _dot(at, dob) + _dot(kb_, dstn, NT)
            dqb = _dot(dob, st)
            dkb = _dot(vb, dstn)
            da = jnp.where(r64 >= c64, lax.dot_general(dob, vb, NT, preferred_element_type=F32), 0.0)
            dat = jnp.where(upper, lax.dot_general(vb, dob, NT, preferred_element_type=F32), 0.0)
            dab = da.astype(BF16)
            dq_raw = jnp.concatenate(
                [lax.dot_general(dab[i * HGRN_SUB:(i + 1) * HGRN_SUB], kes[i], NN, preferred_element_type=F32)
                 for i in range(nsub)], axis=0)
            db = qb_.astype(BF16).astype(F32) * dqb + qeb.astype(F32) * dq_raw
            dkbk = dkb * kb_.astype(BF16).astype(F32)
            db = db - dkbk
            dk_in = jnp.zeros((cc, HEAD_DIM), F32)
            for i in range(nsub):
                blk_i = (c64 // HGRN_SUB) == i
                dk_raw = _dot(jnp.where(blk_i, dat, 0.0), qeb)
                dk_in = dk_in + dk_raw * es[i]
                db = db - kes[i].astype(F32) * dk_raw
            dqs[sl, :] = dqb * jnp.exp(b_c) + dq_raw * eq
            dks[sl, :] = dkb * jnp.exp(bl - b_c) + dk_in
            dbs[sl, :] = db
            extra = jnp.sum(dkbk, axis=0, keepdims=True) + jnp.sum(dstn * st, axis=0, keepdims=True) * ebl
            exs[sl, :] = jnp.broadcast_to(extra, (cc, HEAD_DIM))
            di_ref[sl, :] = dv.astype(BF16)
            dst_s[...] = dstn * ebl + _dot(dob, qb_, TN)

        dg = _dot_exact(triu, dbs[...]) + exs[...]
        df = dg / f - dks[...]
        df_ref[...] = (df * (1.0 - lb) * sg * (1.0 - sg)).astype(BF16)
        dq_ref[...] = (dqs[...] * (sq * (1.0 + qr * (1.0 - sq)))).astype(BF16)
        dlb = jnp.sum(df * (1.0 - sg), axis=0, keepdims=True) * (lb * (1.0 - lb))
        dlog_ref[0:1, :] += -dlb
        dlog_ref[1:2, :] += dlb

    hh = h_count
    blk = lambda off: pl.BlockSpec((t, HEAD_DIM), lambda h, i: (nb - 1 - i, off + h))
    two = pl.BlockSpec((2, HEAD_DIM), lambda h, i: (0, h))
    one = pl.BlockSpec((1, HEAD_DIM), lambda h, i: (0, h))
    act = jax.ShapeDtypeStruct((s, w), BF16)
    tile = pltpu.VMEM((t, HEAD_DIM), F32)
    return _call(
        body, name=name, grid=(hh, nb),
        in_specs=[blk(0), blk(hh), blk(2 * hh), blk(3 * hh), two, one, blk(0), blk(0),
                  pl.BlockSpec((1, nc, HEAD_DIM, HEAD_DIM), lambda h, i: (h, nb - 1 - i, 0, 0))],
        out_specs=[blk(0), blk(0), blk(0), blk(0), two, one],
        out_shape=[act, act, act, act, jax.ShapeDtypeStruct((2, w), F32), jax.ShapeDtypeStruct((1, w), F32)],
        scratch_shapes=[pltpu.VMEM((HEAD_DIM, HEAD_DIM), F32), tile, tile, tile, tile, tile, tile],
        compiler_params=_cp(("parallel", "arbitrary")),
    )(p1, p1, p1, p1, lbl, onorm, o, dy, states)


def _adamw(name, w, g, m, v):
    r, c = w.shape
    tr = r if r <= 128 else _row_block(r, 128)
    c1 = 1.0 - ADAM_B1 ** ADAM_STEP
    c2 = 1.0 - ADAM_B2 ** ADAM_STEP

    def body(w_ref, g_ref, m_ref, v_ref, d_ref, nm_ref, nv_ref):
        gv = g_ref[...]
        nm = ADAM_B1 * m_ref[...] + (1.0 - ADAM_B1) * gv
        nv = ADAM_B2 * v_ref[...] + (1.0 - ADAM_B2) * (gv * gv)
        nm_ref[...] = nm
        nv_ref[...] = nv
        d_ref[...] = -ADAM_LR * ((nm / c1) / (jnp.sqrt(nv / c2) + ADAM_EPS) + ADAM_WD * w_ref[...])

    blk = pl.BlockSpec((tr, c), lambda i: (i, 0))
    sh = jax.ShapeDtypeStruct((r, c), F32)
    return _call(body, name=name, grid=(r // tr,), in_specs=[blk] * 4, out_specs=[blk] * 3,
                 out_shape=[sh, sh, sh], compiler_params=_cp(("parallel",)))(w, g, m, v)


SLAB_BLOCK_BYTES = 2 * 1024 * 1024


def _slab_rows(r, c):
    return _row_block(r, max(16, SLAB_BLOCK_BYTES // (4 * c) // 16 * 16))


def _pair_add(name, g2, recv, core):
    _, nch, r, c = g2.shape
    tr = _slab_rows(r, c)

    grid_spec = pltpu.PrefetchScalarGridSpec(
        num_scalar_prefetch=1, grid=(nch, r // tr),
        in_specs=[pl.BlockSpec((1, 1, tr, c), lambda j, i, cr: (cr[0], j, i, 0)),
                  pl.BlockSpec((1, tr, c), lambda j, i, cr: (j, i, 0))],
        out_specs=pl.BlockSpec((1, tr, c), lambda j, i, cr: (j, i, 0)))

    def body(core_ref, a_ref, b_ref, o_ref):
        o_ref[...] = (a_ref[0] + b_ref[...]).astype(BF16)

    return _call(body, name=name, grid_spec=grid_spec, out_shape=jax.ShapeDtypeStruct((nch, r, c), BF16),
                 compiler_params=_cp(("parallel", "parallel")))(core, g2, recv)


def _sum_slots(name, x, core=None):
    n, r, c = x.shape
    tr = _slab_rows(r, c)

    def total(x_ref):
        acc = x_ref[0].astype(F32)
        for j in range(1, n):
            acc = acc + x_ref[j].astype(F32)
        return acc

    if core is None:
        def body(x_ref, o_ref):
            o_ref[...] = total(x_ref)

        return _call(body, name=name, grid=(r // tr,),
                     in_specs=[pl.BlockSpec((n, tr, c), lambda i: (0, i, 0))],
                     out_specs=pl.BlockSpec((tr, c), lambda i: (i, 0)),
                     out_shape=jax.ShapeDtypeStruct((r, c), F32), compiler_params=_cp(("parallel",)))(x)

    def body_half(core_ref, x_ref, o_ref):
        o_ref[0] = total(x_ref)

    grid_spec = pltpu.PrefetchScalarGridSpec(
        num_scalar_prefetch=1, grid=(r // tr,),
        in_specs=[pl.BlockSpec((n, tr, c), lambda i, cr: (0, i, 0))],
        out_specs=pl.BlockSpec((1, tr, c), lambda i, cr: (cr[0], i, 0)))
    return _call(body_half, name=name, grid_spec=grid_spec, out_shape=jax.ShapeDtypeStruct((2, r, c), F32),
                 compiler_params=_cp(("parallel",)))(core, x)


def _pos():
    return lax.axis_index("x"), lax.axis_index("y"), lax.axis_index("c")


def _flip(v, f):
    return (1 - v) if f else v


CHIP_FLIPS = ((0, 1), (1, 0), (1, 1))
DEV_FLIPS = tuple((fx, fy, fc) for fx in (0, 1) for fy in (0, 1) for fc in (0, 1))[1:]


def _remote(src, dst, ssem, rsem, dev):
    return pltpu.make_async_remote_copy(src_ref=src, dst_ref=dst, send_sem=ssem, recv_sem=rsem,
                                        device_id=dev, device_id_type=MESH)


def _ag_weights(name, halves):
    n = len(halves)

    def body(*refs):
        h_refs, big_refs, (lsem, ssem, rsem) = refs[:n], refs[n:2 * n], refs[2 * n:]
        x, y, cc = _pos()
        me = 2 * x + y
        sib = (x, y, 1 - cc)
        peers = [(_flip(x, fx), _flip(y, fy), cc) for fx, fy in CHIP_FLIPS]
        chips = [2 * px + py for px, py, _ in peers]
        pending = []
        for a in range(n):
            mine = big_refs[a].at[cc, me]
            loc = pltpu.make_async_copy(h_refs[a], mine, lsem.at[a])
            loc.start()
            pending.append(loc.wait)
            for k in range(3):
                cp = _remote(h_refs[a], mine, ssem.at[7 * a + k], rsem.at[7 * a + k], peers[k])
                cp.start()
                pending.append(cp.wait_send)
            cp = _remote(h_refs[a], mine, ssem.at[7 * a + 3], rsem.at[7 * a + 3], sib)
            cp.start()
            pending.append(cp.wait_send)
        for k in range(3):
            for a in range(n):
                blk = big_refs[a].at[cc, chips[k]]
                _remote(blk, blk, ssem.at[7 * a + k], rsem.at[7 * a + k], peers[k]).wait_recv()
                fw = _remote(blk, blk, ssem.at[7 * a + 4 + k], rsem.at[7 * a + 4 + k], sib)
                fw.start()
                pending.append(fw.wait_send)
        for a in range(n):
            theirs = big_refs[a].at[1 - cc, me]
            _remote(theirs, theirs, ssem.at[7 * a + 3], rsem.at[7 * a + 3], sib).wait_recv()
            for k in range(3):
                blk = big_refs[a].at[1 - cc, chips[k]]
                _remote(blk, blk, ssem.at[7 * a + 4 + k], rsem.at[7 * a + 4 + k], sib).wait_recv()
        for wait in pending:
            wait()

    return _call(body, name=name, in_specs=[ANY] * n, out_specs=[ANY] * n,
                 out_shape=[jax.ShapeDtypeStruct((2, 4) + h.shape, h.dtype) for h in halves],
                 scratch_shapes=[pltpu.SemaphoreType.DMA((n,)), pltpu.SemaphoreType.DMA((7 * n,)),
                                 pltpu.SemaphoreType.DMA((7 * n,))])(*halves)


def _ag_chips(name, v):
    r, c = v.shape

    def body(v_ref, out_ref, lsem, ssem, rsem):
        x, y, cc = _pos()
        me = 2 * x + y
        loc = pltpu.make_async_copy(v_ref, out_ref.at[me], lsem)
        loc.start()
        started = []
        for k, (fx, fy) in enumerate(CHIP_FLIPS):
            cp = _remote(v_ref, out_ref.at[me], ssem.at[k], rsem.at[k], (_flip(x, fx), _flip(y, fy), cc))
            cp.start()
            started.append(cp)
        for k, (fx, fy) in enumerate(CHIP_FLIPS):
            blk = out_ref.at[2 * _flip(x, fx) + _flip(y, fy)]
            _remote(blk, blk, ssem.at[k], rsem.at[k], (_flip(x, fx), _flip(y, fy), cc)).wait_recv()
        for cp in started:
            cp.wait_send()
        loc.wait()

    return _call(body, name=name, in_specs=[ANY], out_specs=ANY,
                 out_shape=jax.ShapeDtypeStruct((4, r, c), v.dtype),
                 scratch_shapes=[pltpu.SemaphoreType.DMA, pltpu.SemaphoreType.DMA((3,)), pltpu.SemaphoreType.DMA((3,))])(v)


def _ag_devices(name, v):
    r, c = v.shape

    def body(v_ref, out_ref, lsem, ssem, rsem):
        x, y, cc = _pos()
        me = 4 * x + 2 * y + cc
        loc = pltpu.make_async_copy(v_ref, out_ref.at[me], lsem)
        loc.start()
        started = []
        for k, (fx, fy, fc) in enumerate(DEV_FLIPS):
            cp = _remote(v_ref, out_ref.at[me], ssem.at[k], rsem.at[k], (_flip(x, fx), _flip(y, fy), _flip(cc, fc)))
            cp.start()
            started.append(cp)
        for k, (fx, fy, fc) in enumerate(DEV_FLIPS):
            px, py, pc = _flip(x, fx), _flip(y, fy), _flip(cc, fc)
            blk = out_ref.at[4 * px + 2 * py + pc]
            _remote(blk, blk, ssem.at[k], rsem.at[k], (px, py, pc)).wait_recv()
        for cp in started:
            cp.wait_send()
        loc.wait()

    return _call(body, name=name, in_specs=[ANY], out_specs=ANY,
                 out_shape=jax.ShapeDtypeStruct((8, r, c), v.dtype),
                 scratch_shapes=[pltpu.SemaphoreType.DMA, pltpu.SemaphoreType.DMA((7,)), pltpu.SemaphoreType.DMA((7,))])(v)


def _sibling_other_half(name, g2s):
    n = len(g2s)

    def body(*refs):
        g_refs, out_refs, (ssem, rsem) = refs[:n], refs[n:2 * n], refs[2 * n:]
        x, y, cc = _pos()
        cps = [_remote(g_refs[a].at[1 - cc], out_refs[a], ssem.at[a], rsem.at[a], (x, y, 1 - cc)) for a in range(n)]
        for cp in cps:
            cp.start()
        for cp in cps:
            cp.wait()

    return _call(body, name=name, in_specs=[ANY] * n, out_specs=[ANY] * n,
                 out_shape=[jax.ShapeDtypeStruct(g.shape[1:], g.dtype) for g in g2s],
                 scratch_shapes=[pltpu.SemaphoreType.DMA((n,)), pltpu.SemaphoreType.DMA((n,))])(*g2s)


def _scatter_chips(name, ps):
    n = len(ps)

    def body(*refs):
        p_refs, out_refs, (lsem, ssem, rsem) = refs[:n], refs[n:2 * n], refs[2 * n:]
        x, y, cc = _pos()
        me = 2 * x + y
        peers = [(_flip(x, fx), _flip(y, fy), cc) for fx, fy in CHIP_FLIPS]
        pending = []
        for a in range(n):
            loc = pltpu.make_async_copy(p_refs[a].at[me], out_refs[a].at[me], lsem.at[a])
            loc.start()
            pending.append(loc.wait)
            for k, (px, py, _) in enumerate(peers):
                cp = _remote(p_refs[a].at[2 * px + py], out_refs[a].at[me], ssem.at[3 * a + k], rsem.at[3 * a + k], peers[k])
                cp.start()
                pending.append(cp.wait_send)
        for a in range(n):
            for k, (px, py, _) in enumerate(peers):
                blk = out_refs[a].at[2 * px + py]
                _remote(blk, blk, ssem.at[3 * a + k], rsem.at[3 * a + k], peers[k]).wait_recv()
        for wait in pending:
            wait()

    return _call(body, name=name, in_specs=[ANY] * n, out_specs=[ANY] * n,
                 out_shape=[jax.ShapeDtypeStruct(p.shape, p.dtype) for p in ps],
                 scratch_shapes=[pltpu.SemaphoreType.DMA((n,)), pltpu.SemaphoreType.DMA((3 * n,)),
                                 pltpu.SemaphoreType.DMA((3 * n,))])(*ps)


def _sibling_join(name, bufs):
    n = len(bufs)

    def body(*refs):
        out_refs, (ssem, rsem) = refs[n:2 * n], refs[2 * n:]
        x, y, cc = _pos()
        sib = (x, y, 1 - cc)
        cps = [_remote(out_refs[a].at[cc], out_refs[a].at[cc], ssem.at[a], rsem.at[a], sib) for a in range(n)]
        for cp in cps:
            cp.start()
        for a in range(n):
            theirs = out_refs[a].at[1 - cc]
            _remote(theirs, theirs, ssem.at[a], rsem.at[a], sib).wait_recv()
        for cp in cps:
            cp.wait_send()

    return _call(body, name=name, in_specs=[ANY] * n, out_specs=[ANY] * n,
                 out_shape=[jax.ShapeDtypeStruct(b.shape, b.dtype) for b in bufs],
                 input_output_aliases={a: a for a in range(n)},
                 scratch_shapes=[pltpu.SemaphoreType.DMA((n,)), pltpu.SemaphoreType.DMA((n,))])(*bufs)


def _pad_lanes(a, width=128):
    return jnp.pad(a, ((0, 0), (0, width - a.shape[1])))


def kernel(x, norm_gains, fox_w_in, fox_b_f, hgrn_w_in, hgrn_lb_logits, hgrn_onorm, w_out, final_gain, loss_target, m_norm_gains, m_fox_w_in, m_fox_b_f, m_hgrn_w_in, m_hgrn_lb_logits, m_hgrn_onorm, m_w_out, m_final_gain, v_norm_gains, v_fox_w_in, v_fox_b_f, v_hgrn_w_in, v_hgrn_lb_logits, v_hgrn_onorm, v_w_out, v_final_gain):
    s, d = x.shape[1], x.shape[2]
    wq = w_out.shape[1]
    w = 4 * wq
    hh = w // HEAD_DIM
    fox_cols = fox_w_in.shape[2]
    assert 4 * fox_cols == 4 * w + hh and hgrn_w_in.shape[2] == w
    core = lax.axis_index("c")

    x0 = x[0]
    tgt = loss_target[0]

    def my_half(a):
        return lax.dynamic_index_in_dim(a, core, 0, keepdims=False).astype(BF16)

    big_fox, big_hgrn, big_out = _ag_weights("ag_weights", [
        my_half(fox_w_in.reshape(2, d // 2, fox_cols)), my_half(hgrn_w_in.reshape(2, d // 2, w)), my_half(w_out)])
    fox_chip = [big_fox[:, j].reshape(d, fox_cols) for j in range(4)]
    def fox_columns(a, b):
        out = []
        for j in range(4):
            lo, hi = max(a, j * fox_cols), min(b, (j + 1) * fox_cols)
            if lo < hi:
                out.append(fox_chip[j][:, lo - j * fox_cols:hi - j * fox_cols])
        return out

    w_main = jnp.concatenate(fox_columns(0, 3 * w) + fox_columns(3 * w + hh, 4 * w + hh), axis=1)
    w_fl = _pad_lanes(jnp.concatenate(fox_columns(3 * w, 3 * w + hh), axis=1))
    w_h = big_hgrn.transpose(0, 2, 1, 3).reshape(d, 4 * w)
    w_o = big_out.reshape(2, w, d)
    onorm_full = _ag_chips("ag_onorm", hgrn_onorm.reshape(wq // 128, 128)).reshape(1, w)

    bf_pad = _pad_lanes(fox_b_f)
    g0, g1 = norm_gains[0:1], norm_gains[1:2]
    gf = final_gain.reshape(1, d)

    h0 = _rms_fwd("rms0_fwd", x0, g0)
    qkv = _mm("fox_qkv", h0, w_main, out_dtype=BF16, n=3 * w)
    gate0 = _mm("fox_gate", h0, w_main, n=w, b_off=3 * w)
    fl = _mm("fox_flogit", h0, w_fl)
    _, c_rep = _fox_gate_fwd("fox_cumsum", fl, bf_pad, w)
    o0, lse_row = _attn_fwd("fox_attn_fwd", qkv, c_rep, hh)
    y0 = _fox_post("fox_post", o0, gate0)
    x1 = _mm("fox_out", y0, w_o[0], res=x0)
    h1 = _rms_fwd("rms1_fwd", x1, g1)
    p1 = _mm("hgrn_in", h1, w_h)
    y1, o1, states = _hgrn_fwd("hgrn_fwd", p1, hgrn_lb_logits, onorm_full, hh)
    x2 = _mm("hgrn_out", y1, w_o[1], res=x1)
    dx2, d_gf, loss_tile = _loss_head("loss_head", x2, tgt, gf)

    dy1 = _mm("hgrn_out_dy", dx2, w_o[1], tb=True)
    d_wo1 = _mm("hgrn_out_dw", y1, dx2, ta=True)
    dq1, df1, di1, dgate1, d_lbl, d_onorm = _hgrn_bwd("hgrn_bwd", p1, hgrn_lb_logits, onorm_full, o1, dy1, states, hh)
    dp1 = jnp.concatenate([dq1, df1, di1, dgate1], axis=1)
    dh1 = _mm("hgrn_in_dh", dp1, w_h, tb=True)
    d_wh = _mm("hgrn_in_dw", h1, dp1, ta=True)
    dx1, d_g1 = _rms_bwd("rms1_bwd", x1, g1, dh1, dx2)
    dy0 = _mm("fox_out_dy", dx1, w_o[0], tb=True)
    d_wo0 = _mm("fox_out_dw", y0, dx1, ta=True)
    do0, dgate0, delta_rep = _fox_pre_bwd("fox_pre_bwd", dy0, o0, gate0)
    delta_row = delta_rep[:, ::HEAD_DIM].T.reshape(hh, 1, s)
    dk0, dv0, dc_rep, dq0, rowsum_row = _attn_bwd("fox_attn_bwd", qkv, do0, lse_row, delta_row, c_rep, hh)
    dfl, d_bf = _fox_gate_bwd("fox_cumsum_bwd", _pad_lanes(rowsum_row.reshape(hh, s).T),
                              _pad_lanes(dc_rep[:, ::HEAD_DIM]), fl, bf_pad)
    dp0 = jnp.concatenate([dq0, dk0, dv0, dgate0], axis=1)
    dh0 = _mm("fox_in_dh", dp0, w_main, tb=True, res=_mm("fox_fl_dh", dfl, w_fl, tb=True))
    d_wmain = _mm("fox_in_dw", h0, dp0, ta=True)
    d_wfl = _mm("fox_fl_dw", h0, dfl, ta=True)
    grad_x, d_g0 = _rms_bwd("rms0_bwd", x0, g0, dh0, dx1)

    def grad_columns(a, b):
        out = []
        for lo, hi, src, shift in ((0, 3 * w, d_wmain, 0), (3 * w, 3 * w + hh, d_wfl, 3 * w),
                                   (3 * w + hh, 4 * w + hh, d_wmain, hh)):
            l2, h2 = max(a, lo), min(b, hi)
            if l2 < h2:
                out.append(src[:, l2 - shift:h2 - shift])
        return out

    g2_fox = jnp.stack([jnp.concatenate(grad_columns(j * fox_cols, (j + 1) * fox_cols), axis=1).reshape(2, d // 2, fox_cols)
                        for j in range(4)], axis=1)
    g2_hgrn = d_wh.reshape(2, d // 2, 4, w).transpose(0, 2, 1, 3)
    g2_out = jnp.stack([d_wo0, d_wo1]).reshape(2, 4, wq, d)
    g2s = [g2_fox, g2_hgrn, g2_out]
    core_arr = core.reshape(1).astype(jnp.int32)
    from_sibling = _sibling_other_half("rs_sibling", g2s)
    pairs = [_pair_add("rs_pair_add_" + nm, g2, rcv, core_arr)
             for nm, g2, rcv in zip(("fox", "hgrn", "out"), g2s, from_sibling)]
    from_chips = _scatter_chips("rs_scatter", pairs)
    halves_sum = [_sum_slots("rs_sum_" + nm, fc, core_arr) for nm, fc in zip(("fox", "hgrn", "out"), from_chips)]
    r_fox, r_hgrn, r_out = _sibling_join("rs_join", halves_sum)
    g_fox = r_fox.reshape(d, fox_cols)
    g_hgrn = r_hgrn.reshape(d, w)
    g_out = r_out.reshape(2 * wq, d)

    small = jnp.concatenate([jnp.concatenate([d_g0, d_g1], axis=0).reshape(-1), d_lbl.reshape(-1), d_gf.reshape(-1),
                             d_onorm.reshape(-1), d_bf.reshape(-1)])
    n_small = small.shape[0]
    pad_to = -(-n_small // 1024) * 1024
    small = jnp.pad(small, (0, pad_to - n_small)).reshape(pad_to // 128, 128)
    small = _sum_slots("small_sum", _ag_devices("small_gather", small)).reshape(-1)
    g_norm = small[:2 * d].reshape(2, d)
    g_lbl = small[2 * d:2 * d + 2 * w].reshape(2, w)
    g_gf = small[2 * d + 2 * w:3 * d + 2 * w]
    g_onorm_full = small[3 * d + 2 * w:3 * d + 3 * w]
    g_bf = small[3 * d + 3 * w:3 * d + 3 * w + hh].reshape(1, hh)
    chip = 2 * lax.axis_index("x") + lax.axis_index("y")
    g_onorm = lax.dynamic_slice_in_dim(g_onorm_full, chip * wq, wq).reshape(1, wq)

    loss = lax.psum(loss_tile[0, 0], ("x", "y", "c"))

    def upd(name, wt, g, m, v):
        shp = wt.shape
        two = lambda a: a.reshape(-1, shp[-1])
        dl, nm, nv = _adamw(name, two(wt), two(g), two(m), two(v))
        return g.reshape(shp), dl.reshape(shp), nm.reshape(shp), nv.reshape(shp)

    res = [
        upd("adamw_norm_gains", norm_gains, g_norm, m_norm_gains, v_norm_gains),
        upd("adamw_fox_w_in", fox_w_in, g_fox, m_fox_w_in, v_fox_w_in),
        upd("adamw_fox_b_f", fox_b_f, g_bf, m_fox_b_f, v_fox_b_f),
        upd("adamw_hgrn_w_in", hgrn_w_in, g_hgrn, m_hgrn_w_in, v_hgrn_w_in),
        upd("adamw_lb_logits", hgrn_lb_logits, g_lbl, m_hgrn_lb_logits, v_hgrn_lb_logits),
        upd("adamw_onorm", hgrn_onorm, g_onorm, m_hgrn_onorm, v_hgrn_onorm),
        upd("adamw_w_out", w_out, g_out, m_w_out, v_w_out),
        upd("adamw_final_gain", final_gain.reshape(1, d), g_gf.reshape(1, d), m_final_gain.reshape(1, d),
            v_final_gain.reshape(1, d)),
    ]
    res[-1] = tuple(a.reshape(d) for a in res[-1])
    grads, deltas, new_m, new_v = zip(*res)
    return (loss, grad_x[None], *grads, *deltas, *new_m, *new_v)
```

```python
import functools

import jax
import jax.numpy as jnp
from jax import lax
from jax.experimental import pallas as pl
from jax.experimental.pallas import tpu as pltpu

F32 = jnp.float32
BF16 = jnp.bfloat16
MESH = pl.DeviceIdType.MESH
ANY = pl.BlockSpec(memory_space=pl.ANY)

EPS = 1e-6
HEAD_DIM = 128
HGRN_CHUNK = 64
HGRN_SUB = 32
EXP_CLAMP = 80.0
ATT_BLOCK = 512
HGRN_BLOCK = 512
GATE_BLOCK = 512
ROW_BLOCK = 256
MM_TM, MM_TN, MM_TK = 1024, 1024, 2048
VMEM_LIMIT_V7X = 56 * 1024 * 1024

ADAM_LR, ADAM_B1, ADAM_B2, ADAM_EPS, ADAM_WD, ADAM_STEP = 0.001, 0.9, 0.999, 1e-08, 0.01, 10

NT = (((1,), (1,)), ((), ()))
TN = (((0,), (0,)), ((), ()))
NN = (((1,), (0,)), ((), ()))


def _call(body, **kw):
    return pl.pallas_call(body, **kw)


def _cp(dims=None):
    kw = dict(vmem_limit_bytes=VMEM_LIMIT_V7X)
    if dims is not None:
        kw["dimension_semantics"] = dims
    return pltpu.CompilerParams(**kw)


def _sigmoid(x):
    return 1.0 / (1.0 + jnp.exp(-x))


def _dot(a, b, dn=NN):
    return lax.dot_general(a.astype(BF16), b.astype(BF16), dn, preferred_element_type=F32)


def _split3(x):
    hi = x.astype(BF16)
    r1 = x - hi.astype(F32)
    mid = r1.astype(BF16)
    lo = (r1 - mid.astype(F32)).astype(BF16)
    return hi, mid, lo


def _dot_exact(m01, x, right=False):
    hi, mid, lo = _split3(x)
    if right:
        dot = lambda p: lax.dot_general(p, m01, NN, preferred_element_type=F32)
    else:
        dot = lambda p: lax.dot_general(m01, p, NN, preferred_element_type=F32)
    return dot(hi) + dot(mid) + dot(lo)


def _row_block(rows, cap):
    if rows <= cap:
        return rows
    best = None
    for t in range(16, cap + 1, 16):
        if rows % t == 0:
            best = t
    assert best is not None, rows
    return best


def _mm(name, a, b, *, ta=False, tb=False, out_dtype=F32, res=None, n=None, b_off=0):
    m, k = (a.shape[1], a.shape[0]) if ta else a.shape
    n_full = b.shape[0] if tb else b.shape[1]
    n = n_full if n is None else n
    tm, tn, tk = min(MM_TM, m), min(MM_TN, n), min(MM_TK, k)
    assert m % tm == 0 and n % tn == 0 and k % tk == 0 and b_off % tn == 0
    nk = k // tk
    jo = b_off // tn
    a_spec = pl.BlockSpec((tk, tm), lambda i, j, kk: (kk, i)) if ta else pl.BlockSpec((tm, tk), lambda i, j, kk: (i, kk))
    b_spec = pl.BlockSpec((tn, tk), lambda i, j, kk: (j + jo, kk)) if tb else pl.BlockSpec((tk, tn), lambda i, j, kk: (kk, j + jo))
    o_spec = pl.BlockSpec((tm, tn), lambda i, j, kk: (i, j))
    dn = (((0 if ta else 1,), (1 if tb else 0,)), ((), ()))
    has_res = res is not None

    def body(*refs):
        if has_res:
            a_ref, b_ref, r_ref, o_ref, acc = refs
        else:
            a_ref, b_ref, o_ref, acc = refs
            r_ref = None
        kk = pl.program_id(2)
        p = lax.dot_general(a_ref[...].astype(BF16), b_ref[...].astype(BF16), dn, preferred_element_type=F32)

        def finish(total):
            if r_ref is not None:
                total = total + r_ref[...]
            o_ref[...] = total.astype(out_dtype)

        if nk == 1:
            finish(p)
        else:
            @pl.when(kk == 0)
            def _():
                acc[...] = p

            @pl.when(jnp.logical_and(kk > 0, kk < nk - 1))
            def _():
                acc[...] += p

            @pl.when(kk == nk - 1)
            def _():
                finish(acc[...] + p)

    ins = [a, b] + ([res] if has_res else [])
    in_specs = [a_spec, b_spec] + ([o_spec] if has_res else [])
    return _call(
        body, name=name, grid=(m // tm, n // tn, nk), in_specs=in_specs, out_specs=o_spec,
        out_shape=jax.ShapeDtypeStruct((m, n), out_dtype),
        scratch_shapes=[pltpu.VMEM((tm, tn) if nk > 1 else (8, 128), F32)],
        compiler_params=_cp(("parallel", "parallel", "arbitrary")),
    )(*ins)


def _rms_fwd(name, x, g):
    s, d = x.shape
    tm = min(ROW_BLOCK, s)

    def body(x_ref, g_ref, h_ref):
        xv = x_ref[...]
        r = lax.rsqrt(jnp.mean(xv * xv, axis=-1, keepdims=True) + EPS)
        h_ref[...] = (xv * r * g_ref[...]).astype(BF16)

    row = pl.BlockSpec((tm, d), lambda i: (i, 0))
    vec = pl.BlockSpec((1, d), lambda i: (0, 0))
    return _call(body, name=name, grid=(s // tm,), in_specs=[row, vec], out_specs=row,
                 out_shape=jax.ShapeDtypeStruct((s, d), BF16), compiler_params=_cp(("parallel",)))(x, g)


def _rms_bwd(name, x, g, dh, dres):
    s, d = x.shape
    tm = min(ROW_BLOCK, s)

    def body(x_ref, g_ref, dh_ref, dres_ref, dx_ref, dg_ref):
        @pl.when(pl.program_id(0) == 0)
        def _():
            dg_ref[...] = jnp.zeros_like(dg_ref)

        xv = x_ref[...]
        r = lax.rsqrt(jnp.mean(xv * xv, axis=-1, keepdims=True) + EPS)
        xn = xv * r
        dhv = dh_ref[...]
        dxn = dhv * g_ref[...]
        dx_ref[...] = dres_ref[...] + r * (dxn - xn * jnp.mean(dxn * xn, axis=-1, keepdims=True))
        dg_ref[...] += jnp.sum(dhv * xn, axis=0, keepdims=True)

    row = pl.BlockSpec((tm, d), lambda i: (i, 0))
    vec = pl.BlockSpec((1, d), lambda i: (0, 0))
    return _call(body, name=name, grid=(s // tm,), in_specs=[row, vec, row, row], out_specs=[row, vec],
                 out_shape=[jax.ShapeDtypeStruct((s, d), F32), jax.ShapeDtypeStruct((1, d), F32)],
                 compiler_params=_cp(("arbitrary",)))(x, g, dh, dres)


def _loss_head(name, x, tgt, g):
    s, d = x.shape
    tm = min(ROW_BLOCK, s)
    nb = s // tm

    def body(x_ref, t_ref, g_ref, dx_ref, dg_ref, loss_ref, lacc):
        i = pl.program_id(0)

        @pl.when(i == 0)
        def _():
            dg_ref[...] = jnp.zeros_like(dg_ref)
            lacc[...] = jnp.zeros_like(lacc)

        xv = x_ref[...]
        gv = g_ref[...]
        r = lax.rsqrt(jnp.mean(xv * xv, axis=-1, keepdims=True) + EPS)
        xn = xv * r
        err = xn * gv - t_ref[...]
        lacc[...] += jnp.sum(err * err, axis=0, keepdims=True)
        dout = err * (1.0 / d)
        dg_ref[...] += jnp.sum(dout * xn, axis=0, keepdims=True)
        dxn = dout * gv
        dx_ref[...] = r * (dxn - xn * jnp.mean(dxn * xn, axis=-1, keepdims=True))

        @pl.when(i == nb - 1)
        def _():
            total = jnp.sum(lacc[...], axis=1, keepdims=True) * (0.5 / d)
            loss_ref[...] = jnp.broadcast_to(total, loss_ref.shape)

    row = pl.BlockSpec((tm, d), lambda i: (i, 0))
    vec = pl.BlockSpec((1, d), lambda i: (0, 0))
    one = pl.BlockSpec((1, 128), lambda i: (0, 0))
    return _call(body, name=name, grid=(nb,), in_specs=[row, row, vec], out_specs=[row, vec, one],
                 out_shape=[jax.ShapeDtypeStruct((s, d), F32), jax.ShapeDtypeStruct((1, d), F32),
                            jax.ShapeDtypeStruct((1, 128), F32)],
                 scratch_shapes=[pltpu.VMEM((1, d), F32)], compiler_params=_cp(("arbitrary",)))(x, tgt, g)


def _fox_gate_fwd(name, fl, bf, w):
    s = fl.shape[0]
    tb = min(GATE_BLOCK, s)

    def body(fl_ref, bf_ref, c_ref, crep_ref, carry):
        @pl.when(pl.program_id(0) == 0)
        def _():
            carry[...] = jnp.zeros_like(carry)

        z = fl_ref[...] + bf_ref[...]
        lf = jnp.minimum(z, 0.0) - jnp.log(1.0 + jnp.exp(-jnp.abs(z)))
        rows = lax.broadcasted_iota(jnp.int32, (tb, tb), 0)
        cols = lax.broadcasted_iota(jnp.int32, (tb, tb), 1)
        tri = (rows >= cols).astype(BF16)
        cs = _dot_exact(tri, lf) + carry[...]
        c_ref[...] = cs
        carry[...] = c_ref[tb - 1:tb, :]
        sel_r = lax.broadcasted_iota(jnp.int32, (128, w), 0)
        sel_c = lax.broadcasted_iota(jnp.int32, (128, w), 1)
        sel = (sel_r == sel_c // HEAD_DIM).astype(BF16)
        crep_ref[...] = _dot_exact(sel, cs, right=True)

    blk = pl.BlockSpec((tb, 128), lambda i: (i, 0))
    return _call(body, name=name, grid=(s // tb,),
                 in_specs=[blk, pl.BlockSpec((1, 128), lambda i: (0, 0))],
                 out_specs=[blk, pl.BlockSpec((tb, w), lambda i: (i, 0))],
                 out_shape=[jax.ShapeDtypeStruct((s, 128), F32), jax.ShapeDtypeStruct((s, w), F32)],
                 scratch_shapes=[pltpu.VMEM((1, 128), F32)], compiler_params=_cp(("arbitrary",)))(fl, bf)


def _fox_gate_bwd(name, drow, dcol, fl, bf):
    s = fl.shape[0]
    tb = min(GATE_BLOCK, s)
    nb = s // tb

    def body(dr_ref, dc_ref, fl_ref, bf_ref, dfl_ref, dbf_ref, carry, tmp):
        @pl.when(pl.program_id(0) == 0)
        def _():
            carry[...] = jnp.zeros_like(carry)
            dbf_ref[...] = jnp.zeros_like(dbf_ref)

        rows = lax.broadcasted_iota(jnp.int32, (tb, tb), 0)
        cols = lax.broadcasted_iota(jnp.int32, (tb, tb), 1)
        triu = (rows <= cols).astype(BF16)
        dlf = _dot_exact(triu, dr_ref[...] + dc_ref[...]) + carry[...]
        tmp[...] = dlf
        carry[...] = tmp[0:1, :]
        z = fl_ref[...] + bf_ref[...]
        dfl = dlf * (1.0 / (1.0 + jnp.exp(z)))
        dfl_ref[...] = dfl
        dbf_ref[...] += jnp.sum(dfl, axis=0, keepdims=True)

    blk = pl.BlockSpec((tb, 128), lambda i: (nb - 1 - i, 0))
    vec = pl.BlockSpec((1, 128), lambda i: (0, 0))
    return _call(body, name=name, grid=(nb,), in_specs=[blk, blk, blk, vec], out_specs=[blk, vec],
                 out_shape=[jax.ShapeDtypeStruct((s, 128), F32), jax.ShapeDtypeStruct((1, 128), F32)],
                 scratch_shapes=[pltpu.VMEM((1, 128), F32), pltpu.VMEM((tb, 128), F32)],
                 compiler_params=_cp(("arbitrary",)))(drow, dcol, fl, bf)


def _attn_fwd(name, qkv, c_rep, h_count, gather=()):
    s = qkv.shape[0]
    w = h_count * HEAD_DIM
    t = min(ATT_BLOCK, s)
    scale = HEAD_DIM ** -0.5
    hp = 2 if h_count % 2 == 0 else 1
    wb = hp * HEAD_DIM
    ng = len(gather)
    nh, nq = h_count // hp, s // t

    def body(*refs):
        q_ref, k_ref, v_ref, c_ref = refs[:4]
        h_refs = refs[4:4 + ng]
        o_ref, lse_ref = refs[4 + ng:6 + ng]
        big_refs = refs[6 + ng:6 + 2 * ng]
        m_s, l_s, acc_s = refs[6 + 2 * ng:9 + 2 * ng]
        sems = refs[9 + 2 * ng:]
        hs = pl.program_id(0)
        qi = pl.program_id(1)
        if ng:
            @pl.when(jnp.logical_and(hs == 0, qi == 0))
            def _():
                _ag_phase(0, h_refs, big_refs, sems)

            @pl.when(jnp.logical_and(hs == nh // 2, qi == 0))
            def _():
                _ag_phase(1, h_refs, big_refs, sems)

        m_s[...] = jnp.full(m_s.shape, -jnp.inf, F32)
        l_s[...] = jnp.zeros_like(l_s)
        acc_s[...] = jnp.zeros_like(acc_s)

        def step(kb, masked):
            off = pl.multiple_of(kb * t, t)
            for a in range(hp):
                cols_a = slice(a * HEAD_DIM, (a + 1) * HEAD_DIM)
                kk = k_ref[pl.ds(off, t), cols_a]
                vv = v_ref[pl.ds(off, t), cols_a]
                cc = jnp.tile(c_ref[pl.ds(off, t), cols_a], (1, t // HEAD_DIM))
                st = lax.dot_general(kk, q_ref[:, cols_a], NT, preferred_element_type=F32) * scale - cc
                if masked:
                    rows = lax.broadcasted_iota(jnp.int32, (t, t), 0)
                    cols = lax.broadcasted_iota(jnp.int32, (t, t), 1)
                    st = jnp.where(cols >= rows, st, -jnp.inf)
                m_prev = m_s[a]
                m_new = jnp.maximum(m_prev, jnp.max(st, axis=0, keepdims=True))
                pt = jnp.exp(st - m_new)
                alpha = jnp.exp(m_prev - m_new)
                l_s[a] = alpha * l_s[a] + jnp.sum(pt, axis=0, keepdims=True)
                acc_s[a] = alpha * acc_s[a] + lax.dot_general(vv, pt.astype(BF16), TN, preferred_element_type=F32)
                m_s[a] = m_new

        def loop_body(kb, carry):
            step(kb, False)
            return carry

        lax.fori_loop(0, qi, loop_body, 0)
        step(qi, True)
        for a in range(hp):
            l = l_s[a]
            o_ref[:, a * HEAD_DIM:(a + 1) * HEAD_DIM] = (acc_s[a] / l).T
            lse_ref[a] = m_s[a] + jnp.log(l)

        if ng:
            @pl.when(jnp.logical_and(hs == nh - 1, qi == nq - 1))
            def _():
                _ag_phase(2, h_refs, big_refs, sems)

    blk = lambda off: pl.BlockSpec((t, wb), lambda h, i: (i, off + h))
    whole = lambda off: pl.BlockSpec((s, wb), lambda h, i: (0, off + h))
    rowv = pl.BlockSpec((hp, 1, t), lambda h, i: (h, 0, i))
    return _call(
        body, name=name, grid=(nh, nq),
        in_specs=[blk(0), whole(nh), whole(2 * nh), whole(0)] + [ANY] * ng,
        out_specs=[blk(0), rowv] + [ANY] * ng,
        out_shape=[jax.ShapeDtypeStruct((s, w), F32), jax.ShapeDtypeStruct((h_count, 1, s), F32)] + _ag_out_shapes(gather),
        scratch_shapes=[pltpu.VMEM((hp, 1, t), F32), pltpu.VMEM((hp, 1, t), F32), pltpu.VMEM((hp, HEAD_DIM, t), F32)]
        + (_ag_sems(ng) if ng else []),
        compiler_params=_cp(("arbitrary", "arbitrary") if ng else ("parallel", "arbitrary")),
    )(qkv, qkv, qkv, c_rep, *gather)


def _attn_bwd(name, qkv, do, lse_row, delta_row, c_rep, h_count, scatter=()):
    s = qkv.shape[0]
    w = h_count * HEAD_DIM
    t = min(ATT_BLOCK, s)
    nq = s // t
    scale = HEAD_DIM ** -0.5
    ns = len(scatter)

    def body(*refs):
        k_ref, v_ref, q_ref, do_ref, lse_ref, dl_ref, c_ref = refs[:7]
        p_refs = refs[7:7 + ns]
        dk_ref, dv_ref, dc_ref, dq_ref, rs_ref = refs[7 + ns:12 + ns]
        got_refs = refs[12 + ns:12 + 2 * ns]
        dk_s, dv_s, dc_s, dq_s, rs_s = refs[12 + 2 * ns:17 + 2 * ns]
        sems = refs[17 + 2 * ns:]
        kj = pl.program_id(1)
        if ns:
            @pl.when(jnp.logical_and(pl.program_id(0) == 0, kj == 0))
            def _():
                _scatter_phase(0, p_refs, got_refs, sems)

        kk = k_ref[...]
        vv = v_ref[...]
        ccol = jnp.tile(c_ref[...], (1, t // HEAD_DIM))
        dk_s[...] = jnp.zeros_like(dk_s)
        dv_s[...] = jnp.zeros_like(dv_s)
        dc_s[...] = jnp.zeros_like(dc_s)

        @pl.when(kj == 0)
        def _():
            dq_s[...] = jnp.zeros_like(dq_s)
            rs_s[...] = jnp.zeros_like(rs_s)

        def step(qb, masked):
            off = pl.multiple_of(qb * t, t)
            q = q_ref[pl.ds(off, t), :]
            dov = do_ref[pl.ds(off, t), :]
            st = lax.dot_general(kk, q, NT, preferred_element_type=F32) * scale - ccol
            pt = jnp.exp(st - lse_ref[0, :, pl.ds(off, t)])
            if masked:
                rows = lax.broadcasted_iota(jnp.int32, (t, t), 0)
                cols = lax.broadcasted_iota(jnp.int32, (t, t), 1)
                pt = jnp.where(cols >= rows, pt, 0.0)
            dv_s[...] += lax.dot_general(pt.astype(BF16), dov, NN, preferred_element_type=F32)
            dpt = lax.dot_general(vv, dov, NT, preferred_element_type=F32)
            dst = pt * (dpt - dl_ref[0, :, pl.ds(off, t)])
            dstb = dst.astype(BF16)
            dk_s[...] += lax.dot_general(dstb, q, NN, preferred_element_type=F32)
            dc_s[...] += jnp.sum(dst, axis=1, keepdims=True)
            dq_s[:, pl.ds(off, t)] += lax.dot_general(kk, dstb, TN, preferred_element_type=F32)
            rs_s[:, pl.ds(off, t)] += jnp.sum(dst, axis=0, keepdims=True)

        step(kj, True)

        def loop_body(qb, carry):
            step(qb, False)
            return carry

        lax.fori_loop(kj + 1, nq, loop_body, 0)
        dk_ref[...] = (dk_s[...] * scale).astype(BF16)
        dv_ref[...] = dv_s[...].astype(BF16)
        dc_ref[...] = jnp.broadcast_to(-dc_s[...], dc_ref.shape)

        @pl.when(kj == nq - 1)
        def _():
            for b in range(nq):
                dq_ref[b * t:(b + 1) * t, :] = (dq_s[:, b * t:(b + 1) * t] * scale).T.astype(BF16)
            rs_ref[0] = rs_s[...]

        if ns:
            @pl.when(jnp.logical_and(pl.program_id(0) == h_count - 1, kj == nq - 1))
            def _():
                _scatter_phase(1, p_refs, got_refs, sems)

    hh = h_count
    blk = lambda off: pl.BlockSpec((t, HEAD_DIM), lambda h, j: (j, off + h))
    whole = pl.BlockSpec((s, HEAD_DIM), lambda h, j: (0, h))
    rowv = pl.BlockSpec((1, 1, s), lambda h, j: (h, 0, 0))
    return _call(
        body, name=name, grid=(hh, nq),
        in_specs=[blk(hh), blk(2 * hh), whole, whole, rowv, rowv, blk(0)] + [ANY] * ns,
        out_specs=[blk(0), blk(0), blk(0), whole, rowv] + [ANY] * ns,
        out_shape=[jax.ShapeDtypeStruct((s, w), BF16), jax.ShapeDtypeStruct((s, w), BF16),
                   jax.ShapeDtypeStruct((s, w), F32), jax.ShapeDtypeStruct((s, w), BF16),
                   jax.ShapeDtypeStruct((hh, 1, s), F32)] + [jax.ShapeDtypeStruct(p.shape, p.dtype) for p in scatter],
        scratch_shapes=[pltpu.VMEM((t, HEAD_DIM), F32), pltpu.VMEM((t, HEAD_DIM), F32), pltpu.VMEM((t, 1), F32),
                        pltpu.VMEM((HEAD_DIM, s), F32), pltpu.VMEM((1, s), F32)] + (_scatter_sems(ns) if ns else []),
        compiler_params=_cp(("arbitrary", "arbitrary") if ns else ("parallel", "arbitrary")),
    )(qkv, qkv, qkv, do, lse_row, delta_row, c_rep, *scatter)


def _fox_post(name, o, gate):
    s, w = o.shape
    tm = min(ROW_BLOCK, s)

    def body(o_ref, g_ref, y_ref):
        g = g_ref[...]
        y_ref[...] = (o_ref[...] * (g * _sigmoid(g))).astype(BF16)

    row = pl.BlockSpec((tm, w), lambda i: (i, 0))
    return _call(body, name=name, grid=(s // tm,), in_specs=[row, row], out_specs=row,
                 out_shape=jax.ShapeDtypeStruct((s, w), BF16), compiler_params=_cp(("parallel",)))(o, gate)


def _fox_pre_bwd(name, dy, o, gate):
    s, w = o.shape
    tm = min(ROW_BLOCK, s)

    def body(dy_ref, o_ref, g_ref, do_ref, dg_ref, dl_ref):
        g = g_ref[...]
        sg = _sigmoid(g)
        dyv = dy_ref[...]
        ov = o_ref[...]
        dov = dyv * (g * sg)
        do_ref[...] = dov.astype(BF16)
        dg_ref[...] = (dyv * ov * (sg * (1.0 + g * (1.0 - sg)))).astype(BF16)
        prod = dov * ov
        for h in range(w // HEAD_DIM):
            sl = slice(h * HEAD_DIM, (h + 1) * HEAD_DIM)
            dl_ref[:, sl] = jnp.broadcast_to(jnp.sum(prod[:, sl], axis=1, keepdims=True), (tm, HEAD_DIM))

    row = pl.BlockSpec((tm, w), lambda i: (i, 0))
    return _call(body, name=name, grid=(s // tm,), in_specs=[row, row, row], out_specs=[row, row, row],
                 out_shape=[jax.ShapeDtypeStruct((s, w), BF16), jax.ShapeDtypeStruct((s, w), BF16),
                            jax.ShapeDtypeStruct((s, w), F32)],
                 compiler_params=_cp(("parallel",)))(dy, o, gate)


def _hgrn_chunk_terms(q_c, k_c, b_c, b_s, base):
    nsub = HGRN_CHUNK // HGRN_SUB
    refs = [jnp.zeros((1, HEAD_DIM), F32)]
    for i in range(1, nsub):
        r0 = base + i * HGRN_SUB - 1
        refs.append(b_s[r0:r0 + 1, :])
    rfull = jnp.concatenate([jnp.broadcast_to(r, (HGRN_SUB, HEAD_DIM)) for r in refs], axis=0)
    eq = jnp.exp(b_c - rfull)
    qe = q_c * eq
    es = [jnp.exp(jnp.minimum(r - b_c, EXP_CLAMP)) for r in refs]
    kes = [(k_c * e).astype(BF16) for e in es]
    return eq, qe, es, kes


def _hgrn_block_pre(q_ref, f_ref, lbl_ref, tri, b_s):
    lb = _sigmoid(lbl_ref[1:2, :] - lbl_ref[0:1, :])
    qr = q_ref[...]
    sq = _sigmoid(qr)
    q = qr * sq
    fz = f_ref[...]
    sg = _sigmoid(fz)
    f = lb + (1.0 - lb) * sg
    g = jnp.log(f)
    k = (1.0 - lb) * (1.0 / (1.0 + jnp.exp(fz)))
    b = _dot_exact(tri, g)
    b_s[...] = b
    return lb, qr, sq, q, sg, f, k, b


def _chunk_masks(t):
    rows = lax.broadcasted_iota(jnp.int32, (t, t), 0)
    cols = lax.broadcasted_iota(jnp.int32, (t, t), 1)
    same = (rows // HGRN_CHUNK) == (cols // HGRN_CHUNK)
    return rows, cols, same


def _hgrn_fwd(name, p1, lbl, onorm, h_count):
    s = p1.shape[0]
    w = h_count * HEAD_DIM
    t = min(HGRN_BLOCK, s)
    nc = t // HGRN_CHUNK
    nsub = HGRN_CHUNK // HGRN_SUB
    cc = HGRN_CHUNK

    def body(q_ref, f_ref, i_ref, g_ref, lbl_ref, on_ref, y_ref, o_ref, st_ref, state_s, b_s):
        @pl.when(pl.program_id(1) == 0)
        def _():
            state_s[...] = jnp.zeros_like(state_s)

        rows, cols, same = _chunk_masks(t)
        tri = jnp.logical_and(same, rows >= cols).astype(BF16)
        lb, qr, sq, q, sg, f, k, b = _hgrn_block_pre(q_ref, f_ref, lbl_ref, tri, b_s)
        v = i_ref[...]
        r64 = lax.broadcasted_iota(jnp.int32, (cc, cc), 0)
        c64 = lax.broadcasted_iota(jnp.int32, (cc, cc), 1)
        for n in range(nc):
            sl = slice(n * cc, (n + 1) * cc)
            q_c, k_c, v_c, b_c = q[sl], k[sl], v[sl], b[sl]
            bl = b_s[n * cc + cc - 1:n * cc + cc, :]
            eq, qe, es, kes = _hgrn_chunk_terms(q_c, k_c, b_c, b_s, n * cc)
            qeb = qe.astype(BF16)
            a = jnp.concatenate(
                [lax.dot_general(qeb[i * HGRN_SUB:(i + 1) * HGRN_SUB], kes[i], NT, preferred_element_type=F32)
                 for i in range(nsub)], axis=0)
            a = jnp.where(r64 >= c64, a, 0.0)
            st = state_s[...]
            st_ref[0, n] = st
            inter = _dot(q_c * jnp.exp(b_c), st, NT)
            intra = _dot(a, v_c)
            o_ref[sl, :] = inter + intra
            kb = k_c * jnp.exp(bl - b_c)
            state_s[...] = st * jnp.exp(bl) + _dot(v_c, kb, TN)
        o = o_ref[...]
        rr = lax.rsqrt(jnp.mean(o * o, axis=-1, keepdims=True) + EPS)
        gate = g_ref[...]
        y_ref[...] = ((o * rr) * on_ref[...] * (gate * _sigmoid(gate))).astype(BF16)

    hh = h_count
    blk = lambda off: pl.BlockSpec((t, HEAD_DIM), lambda h, i: (i, off + h))
    return _call(
        body, name=name, grid=(hh, s // t),
        in_specs=[blk(0), blk(hh), blk(2 * hh), blk(3 * hh),
                  pl.BlockSpec((2, HEAD_DIM), lambda h, i: (0, h)), pl.BlockSpec((1, HEAD_DIM), lambda h, i: (0, h))],
        out_specs=[blk(0), blk(0), pl.BlockSpec((1, nc, HEAD_DIM, HEAD_DIM), lambda h, i: (h, i, 0, 0))],
        out_shape=[jax.ShapeDtypeStruct((s, w), BF16), jax.ShapeDtypeStruct((s, w), F32),
                   jax.ShapeDtypeStruct((hh, s // cc, HEAD_DIM, HEAD_DIM), F32)],
        scratch_shapes=[pltpu.VMEM((HEAD_DIM, HEAD_DIM), F32), pltpu.VMEM((t, HEAD_DIM), F32)],
        compiler_params=_cp(("parallel", "arbitrary")),
    )(p1, p1, p1, p1, lbl, onorm)


def _hgrn_bwd(name, p1, lbl, onorm, o, dy, states, h_count):
    s = p1.shape[0]
    w = h_count * HEAD_DIM
    t = min(HGRN_BLOCK, s)
    nb = s // t
    nc = t // HGRN_CHUNK
    nsub = HGRN_CHUNK // HGRN_SUB
    cc = HGRN_CHUNK

    def body(q_ref, f_ref, i_ref, g_ref, lbl_ref, on_ref, o_ref, dy_ref, st_ref,
             dq_ref, df_ref, di_ref, dgate_ref, dlog_ref, don_ref,
             dst_s, b_s, do_s, dqs, dks, dbs, exs):
        @pl.when(pl.program_id(1) == 0)
        def _():
            dst_s[...] = jnp.zeros_like(dst_s)
            dlog_ref[...] = jnp.zeros_like(dlog_ref)
            don_ref[...] = jnp.zeros_like(don_ref)

        rows, cols, same = _chunk_masks(t)
        tri = jnp.logical_and(same, rows >= cols).astype(BF16)
        triu = jnp.logical_and(same, rows <= cols).astype(BF16)
        lb, qr, sq, q, sg, f, k, b = _hgrn_block_pre(q_ref, f_ref, lbl_ref, tri, b_s)
        v = i_ref[...]

        ov = o_ref[...]
        rr = lax.rsqrt(jnp.mean(ov * ov, axis=-1, keepdims=True) + EPS)
        on = ov * rr
        gate = g_ref[...]
        sgt = _sigmoid(gate)
        silu = gate * sgt
        dyv = dy_ref[...]
        gain = on_ref[...]
        dgate_ref[...] = (dyv * on * gain * (sgt * (1.0 + gate * (1.0 - sgt)))).astype(BF16)
        don_ref[...] += jnp.sum(dyv * on * silu, axis=0, keepdims=True)
        d_on = dyv * gain * silu
        do_s[...] = rr * (d_on - on * jnp.mean(d_on * on, axis=-1, keepdims=True))

        r64 = lax.broadcasted_iota(jnp.int32, (cc, cc), 0)
        c64 = lax.broadcasted_iota(jnp.int32, (cc, cc), 1)
        upper = r64 <= c64
        for n in reversed(range(nc)):
            sl = slice(n * cc, (n + 1) * cc)
            q_c, k_c, v_c, b_c = q[sl], k[sl], v[sl], b[sl]
            do_c = do_s[sl, :]
            bl = b_s[n * cc + cc - 1:n * cc + cc, :]
            ebl = jnp.exp(bl)
            eq, qe, es, kes = _hgrn_chunk_terms(q_c, k_c, b_c, b_s, n * cc)
            qeb = qe.astype(BF16)
            dob = do_c.astype(BF16)
            vb = v_c.astype(BF16)
            st = st_ref[0, n]
            dstn = dst_s[...]
            qb_ = q_c * jnp.exp(b_c)
            kb_ = k_c * jnp.exp(bl - b_c)
            at = jnp.zeros((cc, cc), F32)
            for i in range(nsub):
                blk_i = (c64 // HGRN_SUB) == i
                at = at + jnp.where(blk_i, lax.dot_general(kes[i], qeb, NT, preferred_element_type=F32), 0.0)
            at = jnp.where(upper, at, 0.0)
            dv = _dot(at, dob) + _dot(kb_, dstn, NT)
            dqb = _dot(dob, st)
            dkb = _dot(vb, dstn)
            da = jnp.where(r64 >= c64, lax.dot_general(dob, vb, NT, preferred_element_type=F32), 0.0)
            dat = jnp.where(upper, lax.dot_general(vb, dob, NT, preferred_element_type=F32), 0.0)
            dab = da.astype(BF16)
            dq_raw = jnp.concatenate(
                [lax.dot_general(dab[i * HGRN_SUB:(i + 1) * HGRN_SUB], kes[i], NN, preferred_element_type=F32)
                 for i in range(nsub)], axis=0)
            db = qb_.astype(BF16).astype(F32) * dqb + qeb.astype(F32) * dq_raw
            dkbk = dkb * kb_.astype(BF16).astype(F32)
            db = db - dkbk
            dk_in = jnp.zeros((cc, HEAD_DIM), F32)
            for i in range(nsub):
                blk_i = (c64 // HGRN_SUB) == i
                dk_raw = _dot(jnp.where(blk_i, dat, 0.0), qeb)
                dk_in = dk_in + dk_raw * es[i]
                db = db - kes[i].astype(F32) * dk_raw
            dqs[sl, :] = dqb * jnp.exp(b_c) + dq_raw * eq
            dks[sl, :] = dkb * jnp.exp(bl - b_c) + dk_in
            dbs[sl, :] = db
            extra = jnp.sum(dkbk, axis=0, keepdims=True) + jnp.sum(dstn * st, axis=0, keepdims=True) * ebl
            exs[sl, :] = jnp.broadcast_to(extra, (cc, HEAD_DIM))
            di_ref[sl, :] = dv.astype(BF16)
            dst_s[...] = dstn * ebl + _dot(dob, qb_, TN)

        dg = _dot_exact(triu, dbs[...]) + exs[...]
        df = dg / f - dks[...]
        df_ref[...] = (df * (1.0 - lb) * sg * (1.0 - sg)).astype(BF16)
        dq_ref[...] = (dqs[...] * (sq * (1.0 + qr * (1.0 - sq)))).astype(BF16)
        dlb = jnp.sum(df * (1.0 - sg), axis=0, keepdims=True) * (lb * (1.0 - lb))
        dlog_ref[0:1, :] += -dlb
        dlog_ref[1:2, :] += dlb

    hh = h_count
    blk = lambda off: pl.BlockSpec((t, HEAD_DIM), lambda h, i: (nb - 1 - i, off + h))
    two = pl.BlockSpec((2, HEAD_DIM), lambda h, i: (0, h))
    one = pl.BlockSpec((1, HEAD_DIM), lambda h, i: (0, h))
    act = jax.ShapeDtypeStruct((s, w), BF16)
    tile = pltpu.VMEM((t, HEAD_DIM), F32)
    return _call(
        body, name=name, grid=(hh, nb),
        in_specs=[blk(0), blk(hh), blk(2 * hh), blk(3 * hh), two, one, blk(0), blk(0),
                  pl.BlockSpec((1, nc, HEAD_DIM, HEAD_DIM), lambda h, i: (h, nb - 1 - i, 0, 0))],
        out_specs=[blk(0), blk(0), blk(0), blk(0), two, one],
        out_shape=[act, act, act, act, jax.ShapeDtypeStruct((2, w), F32), jax.ShapeDtypeStruct((1, w), F32)],
        scratch_shapes=[pltpu.VMEM((HEAD_DIM, HEAD_DIM), F32), tile, tile, tile, tile, tile, tile],
        compiler_params=_cp(("parallel", "arbitrary")),
    )(p1, p1, p1, p1, lbl, onorm, o, dy, states)


def _adamw(name, w, g, m, v):
    r, c = w.shape
    tr = r if r <= 128 else _row_block(r, 128)
    c1 = 1.0 - ADAM_B1 ** ADAM_STEP
    c2 = 1.0 - ADAM_B2 ** ADAM_STEP

    def body(w_ref, g_ref, m_ref, v_ref, d_ref, nm_ref, nv_ref):
        gv = g_ref[...]
        nm = ADAM_B1 * m_ref[...] + (1.0 - ADAM_B1) * gv
        nv = ADAM_B2 * v_ref[...] + (1.0 - ADAM_B2) * (gv * gv)
        nm_ref[...] = nm
        nv_ref[...] = nv
        d_ref[...] = -ADAM_LR * ((nm / c1) / (jnp.sqrt(nv / c2) + ADAM_EPS) + ADAM_WD * w_ref[...])

    blk = pl.BlockSpec((tr, c), lambda i: (i, 0))
    sh = jax.ShapeDtypeStruct((r, c), F32)
    return _call(body, name=name, grid=(r // tr,), in_specs=[blk] * 4, out_specs=[blk] * 3,
                 out_shape=[sh, sh, sh], compiler_params=_cp(("parallel",)))(w, g, m, v)


SLAB_BLOCK_BYTES = 2 * 1024 * 1024


def _slab_rows(r, c):
    return _row_block(r, max(16, SLAB_BLOCK_BYTES // (4 * c) // 16 * 16))


def _pair_add(name, g2, recv, core):
    _, nch, r, c = g2.shape
    tr = _slab_rows(r, c)

    grid_spec = pltpu.PrefetchScalarGridSpec(
        num_scalar_prefetch=1, grid=(nch, r // tr),
        in_specs=[pl.BlockSpec((1, 1, tr, c), lambda j, i, cr: (cr[0], j, i, 0)),
                  pl.BlockSpec((1, tr, c), lambda j, i, cr: (j, i, 0))],
        out_specs=pl.BlockSpec((1, tr, c), lambda j, i, cr: (j, i, 0)))

    def body(core_ref, a_ref, b_ref, o_ref):
        o_ref[...] = (a_ref[0] + b_ref[...]).astype(BF16)

    return _call(body, name=name, grid_spec=grid_spec, out_shape=jax.ShapeDtypeStruct((nch, r, c), BF16),
                 compiler_params=_cp(("parallel", "parallel")))(core, g2, recv)


def _sum_slots(name, x, core=None):
    n, r, c = x.shape
    tr = _slab_rows(r, c)

    def total(x_ref):
        acc = x_ref[0].astype(F32)
        for j in range(1, n):
            acc = acc + x_ref[j].astype(F32)
        return acc

    if core is None:
        def body(x_ref, o_ref):
            o_ref[...] = total(x_ref)

        return _call(body, name=name, grid=(r // tr,),
                     in_specs=[pl.BlockSpec((n, tr, c), lambda i: (0, i, 0))],
                     out_specs=pl.BlockSpec((tr, c), lambda i: (i, 0)),
                     out_shape=jax.ShapeDtypeStruct((r, c), F32), compiler_params=_cp(("parallel",)))(x)

    def body_half(core_ref, x_ref, o_ref):
        o_ref[0] = total(x_ref)

    grid_spec = pltpu.PrefetchScalarGridSpec(
        num_scalar_prefetch=1, grid=(r // tr,),
        in_specs=[pl.BlockSpec((n, tr, c), lambda i, cr: (0, i, 0))],
        out_specs=pl.BlockSpec((1, tr, c), lambda i, cr: (cr[0], i, 0)))
    return _call(body_half, name=name, grid_spec=grid_spec, out_shape=jax.ShapeDtypeStruct((2, r, c), F32),
                 compiler_params=_cp(("parallel",)))(core, x)


def _pos():
    return lax.axis_index("x"), lax.axis_index("y"), lax.axis_index("c")


def _flip(v, f):
    return (1 - v) if f else v


CHIP_FLIPS = ((0, 1), (1, 0), (1, 1))
DEV_FLIPS = tuple((fx, fy, fc) for fx in (0, 1) for fy in (0, 1) for fc in (0, 1))[1:]


def _remote(src, dst, ssem, rsem, dev):
    return pltpu.make_async_remote_copy(src_ref=src, dst_ref=dst, send_sem=ssem, recv_sem=rsem,
                                        device_id=dev, device_id_type=MESH)


def _ag_weights(name, halves):
    n = len(halves)

    def body(*refs):
        h_refs, big_refs, sems = refs[:n], refs[n:2 * n], refs[2 * n:]
        _ag_phase(0, h_refs, big_refs, sems)
        _ag_phase(1, h_refs, big_refs, sems)
        _ag_phase(2, h_refs, big_refs, sems)

    return _call(body, name=name, in_specs=[ANY] * n, out_specs=[ANY] * n,
                 out_shape=_ag_out_shapes(halves), scratch_shapes=_ag_sems(n))(*halves)


def _ag_out_shapes(halves):
    return [jax.ShapeDtypeStruct((2, 4) + h.shape, h.dtype) for h in halves]


def _ag_sems(n):
    return [pltpu.SemaphoreType.DMA((n,)), pltpu.SemaphoreType.DMA((7 * n,)), pltpu.SemaphoreType.DMA((7 * n,))]


def _ag_phase(phase, h_refs, big_refs, sems):
    lsem, ssem, rsem = sems
    n = len(h_refs)
    x, y, cc = _pos()
    me = 2 * x + y
    sib = (x, y, 1 - cc)
    peers = [(_flip(x, fx), _flip(y, fy), cc) for fx, fy in CHIP_FLIPS]
    chips = [2 * px + py for px, py, _ in peers]
    for a in range(n):
        mine = big_refs[a].at[cc, me]
        first = [pltpu.make_async_copy(h_refs[a], mine, lsem.at[a])]
        first += [_remote(h_refs[a], mine, ssem.at[7 * a + k], rsem.at[7 * a + k], peers[k]) for k in range(3)]
        first += [_remote(h_refs[a], mine, ssem.at[7 * a + 3], rsem.at[7 * a + 3], sib)]
        passed = []
        for k in range(3):
            blk = big_refs[a].at[cc, chips[k]]
            passed.append(_remote(blk, blk, ssem.at[7 * a + 4 + k], rsem.at[7 * a + 4 + k], sib))
        if phase == 0:
            for cp in first:
                cp.start()
        elif phase == 1:
            for k in range(3):
                blk = big_refs[a].at[cc, chips[k]]
                _remote(blk, blk, ssem.at[7 * a + k], rsem.at[7 * a + k], peers[k]).wait_recv()
                passed[k].start()
        else:
            theirs = big_refs[a].at[1 - cc, me]
            _remote(theirs, theirs, ssem.at[7 * a + 3], rsem.at[7 * a + 3], sib).wait_recv()
            for k in range(3):
                blk = big_refs[a].at[1 - cc, chips[k]]
                _remote(blk, blk, ssem.at[7 * a + 4 + k], rsem.at[7 * a + 4 + k], sib).wait_recv()
            first[0].wait()
            for cp in first[1:] + passed:
                cp.wait_send()


def _ag_chips(name, v):
    r, c = v.shape

    def body(v_ref, out_ref, lsem, ssem, rsem):
        x, y, cc = _pos()
        me = 2 * x + y
        loc = pltpu.make_async_copy(v_ref, out_ref.at[me], lsem)
        loc.start()
        started = []
        for k, (fx, fy) in enumerate(CHIP_FLIPS):
            cp = _remote(v_ref, out_ref.at[me], ssem.at[k], rsem.at[k], (_flip(x, fx), _flip(y, fy), cc))
            cp.start()
            started.append(cp)
        for k, (fx, fy) in enumerate(CHIP_FLIPS):
            blk = out_ref.at[2 * _flip(x, fx) + _flip(y, fy)]
            _remote(blk, blk, ssem.at[k], rsem.at[k], (_flip(x, fx), _flip(y, fy), cc)).wait_recv()
        for cp in started:
            cp.wait_send()
        loc.wait()

    return _call(body, name=name, in_specs=[ANY], out_specs=ANY,
                 out_shape=jax.ShapeDtypeStruct((4, r, c), v.dtype),
                 scratch_shapes=[pltpu.SemaphoreType.DMA, pltpu.SemaphoreType.DMA((3,)), pltpu.SemaphoreType.DMA((3,))])(v)


def _ag_devices(name, v):
    r, c = v.shape

    def body(v_ref, out_ref, lsem, ssem, rsem):
        x, y, cc = _pos()
        me = 4 * x + 2 * y + cc
        loc = pltpu.make_async_copy(v_ref, out_ref.at[me], lsem)
        loc.start()
        started = []
        for k, (fx, fy, fc) in enumerate(DEV_FLIPS):
            cp = _remote(v_ref, out_ref.at[me], ssem.at[k], rsem.at[k], (_flip(x, fx), _flip(y, fy), _flip(cc, fc)))
            cp.start()
            started.append(cp)
        for k, (fx, fy, fc) in enumerate(DEV_FLIPS):
            px, py, pc = _flip(x, fx), _flip(y, fy), _flip(cc, fc)
            blk = out_ref.at[4 * px + 2 * py + pc]
            _remote(blk, blk, ssem.at[k], rsem.at[k], (px, py, pc)).wait_recv()
        for cp in started:
            cp.wait_send()
        loc.wait()

    return _call(body, name=name, in_specs=[ANY], out_specs=ANY,
                 out_shape=jax.ShapeDtypeStruct((8, r, c), v.dtype),
                 scratch_shapes=[pltpu.SemaphoreType.DMA, pltpu.SemaphoreType.DMA((7,)), pltpu.SemaphoreType.DMA((7,))])(v)


def _sibling_other_half(name, g2s):
    n = len(g2s)

    def body(*refs):
        g_refs, out_refs, (ssem, rsem) = refs[:n], refs[n:2 * n], refs[2 * n:]
        x, y, cc = _pos()
        cps = [_remote(g_refs[a].at[1 - cc], out_refs[a], ssem.at[a], rsem.at[a], (x, y, 1 - cc)) for a in range(n)]
        for cp in cps:
            cp.start()
        for cp in cps:
            cp.wait()

    return _call(body, name=name, in_specs=[ANY] * n, out_specs=[ANY] * n,
                 out_shape=[jax.ShapeDtypeStruct(g.shape[1:], g.dtype) for g in g2s],
                 scratch_shapes=[pltpu.SemaphoreType.DMA((n,)), pltpu.SemaphoreType.DMA((n,))])(*g2s)


def _scatter_chips(name, ps):
    n = len(ps)

    def body(*refs):
        p_refs, out_refs, sems = refs[:n], refs[n:2 * n], refs[2 * n:]
        _scatter_phase(0, p_refs, out_refs, sems)
        _scatter_phase(1, p_refs, out_refs, sems)

    return _call(body, name=name, in_specs=[ANY] * n, out_specs=[ANY] * n,
                 out_shape=[jax.ShapeDtypeStruct(p.shape, p.dtype) for p in ps], scratch_shapes=_scatter_sems(n))(*ps)


def _scatter_sems(n):
    return [pltpu.SemaphoreType.DMA((n,)), pltpu.SemaphoreType.DMA((3 * n,)), pltpu.SemaphoreType.DMA((3 * n,))]


def _scatter_phase(phase, p_refs, out_refs, sems):
    lsem, ssem, rsem = sems
    x, y, cc = _pos()
    me = 2 * x + y
    peers = [(_flip(x, fx), _flip(y, fy), cc) for fx, fy in CHIP_FLIPS]
    for a in range(len(p_refs)):
        loc = pltpu.make_async_copy(p_refs[a].at[me], out_refs[a].at[me], lsem.at[a])
        sends = [_remote(p_refs[a].at[2 * px + py], out_refs[a].at[me], ssem.at[3 * a + k], rsem.at[3 * a + k], peers[k])
                 for k, (px, py, _) in enumerate(peers)]
        if phase == 0:
            loc.start()
            for cp in sends:
                cp.start()
        else:
            for k, (px, py, _) in enumerate(peers):
                blk = out_refs[a].at[2 * px + py]
                _remote(blk, blk, ssem.at[3 * a + k], rsem.at[3 * a + k], peers[k]).wait_recv()
            loc.wait()
            for cp in sends:
                cp.wait_send()


def _sibling_join(name, bufs):
    n = len(bufs)

    def body(*refs):
        out_refs, (ssem, rsem) = refs[n:2 * n], refs[2 * n:]
        x, y, cc = _pos()
        sib = (x, y, 1 - cc)
        cps = [_remote(out_refs[a].at[cc], out_refs[a].at[cc], ssem.at[a], rsem.at[a], sib) for a in range(n)]
        for cp in cps:
            cp.start()
        for a in range(n):
            theirs = out_refs[a].at[1 - cc]
            _remote(theirs, theirs, ssem.at[a], rsem.at[a], sib).wait_recv()
        for cp in cps:
            cp.wait_send()

    return _call(body, name=name, in_specs=[ANY] * n, out_specs=[ANY] * n,
                 out_shape=[jax.ShapeDtypeStruct(b.shape, b.dtype) for b in bufs],
                 input_output_aliases={a: a for a in range(n)},
                 scratch_shapes=[pltpu.SemaphoreType.DMA((n,)), pltpu.SemaphoreType.DMA((n,))])(*bufs)


def _pad_lanes(a, width=128):
    return jnp.pad(a, ((0, 0), (0, width - a.shape[1])))


def kernel(x, norm_gains, fox_w_in, fox_b_f, hgrn_w_in, hgrn_lb_logits, hgrn_onorm, w_out, final_gain, loss_target, m_norm_gains, m_fox_w_in, m_fox_b_f, m_hgrn_w_in, m_hgrn_lb_logits, m_hgrn_onorm, m_w_out, m_final_gain, v_norm_gains, v_fox_w_in, v_fox_b_f, v_hgrn_w_in, v_hgrn_lb_logits, v_hgrn_onorm, v_w_out, v_final_gain):
    s, d = x.shape[1], x.shape[2]
    wq = w_out.shape[1]
    w = 4 * wq
    hh = w // HEAD_DIM
    fox_cols = fox_w_in.shape[2]
    assert 4 * fox_cols == 4 * w + hh and hgrn_w_in.shape[2] == w
    core = lax.axis_index("c")

    x0 = x[0]
    tgt = loss_target[0]

    def my_half(a):
        return lax.dynamic_index_in_dim(a, core, 0, keepdims=False).astype(BF16)

    (big_fox,) = _ag_weights("ag_weights", [my_half(fox_w_in.reshape(2, d // 2, fox_cols))])
    later_halves = [my_half(hgrn_w_in.reshape(2, d // 2, w)), my_half(w_out)]
    fox_chip = [big_fox[:, j].reshape(d, fox_cols) for j in range(4)]
    def fox_columns(a, b):
        out = []
        for j in range(4):
            lo, hi = max(a, j * fox_cols), min(b, (j + 1) * fox_cols)
            if lo < hi:
                out.append(fox_chip[j][:, lo - j * fox_cols:hi - j * fox_cols])
        return out

    w_main = jnp.concatenate(fox_columns(0, 3 * w) + fox_columns(3 * w + hh, 4 * w + hh), axis=1)
    w_fl = _pad_lanes(jnp.concatenate(fox_columns(3 * w, 3 * w + hh), axis=1))
    onorm_full = _ag_chips("ag_onorm", hgrn_onorm.reshape(wq // 128, 128)).reshape(1, w)

    bf_pad = _pad_lanes(fox_b_f)
    g0, g1 = norm_gains[0:1], norm_gains[1:2]
    gf = final_gain.reshape(1, d)

    h0 = _rms_fwd("rms0_fwd", x0, g0)
    qkv = _mm("fox_qkv", h0, w_main, out_dtype=BF16, n=3 * w)
    gate0 = _mm("fox_gate", h0, w_main, n=w, b_off=3 * w)
    fl = _mm("fox_flogit", h0, w_fl)
    _, c_rep = _fox_gate_fwd("fox_cumsum", fl, bf_pad, w)
    o0, lse_row, big_hgrn, big_out = _attn_fwd("fox_attn_fwd", qkv, c_rep, hh, gather=later_halves)
    w_h = big_hgrn.transpose(0, 2, 1, 3).reshape(d, 4 * w)
    w_o = big_out.reshape(2, w, d)
    y0 = _fox_post("fox_post", o0, gate0)
    x1 = _mm("fox_out", y0, w_o[0], res=x0)
    h1 = _rms_fwd("rms1_fwd", x1, g1)
    p1 = _mm("hgrn_in", h1, w_h)
    y1, o1, states = _hgrn_fwd("hgrn_fwd", p1, hgrn_lb_logits, onorm_full, hh)
    x2 = _mm("hgrn_out", y1, w_o[1], res=x1)
    dx2, d_gf, loss_tile = _loss_head("loss_head", x2, tgt, gf)

    dy1 = _mm("hgrn_out_dy", dx2, w_o[1], tb=True)
    d_wo1 = _mm("hgrn_out_dw", y1, dx2, ta=True)
    dq1, df1, di1, dgate1, d_lbl, d_onorm = _hgrn_bwd("hgrn_bwd", p1, hgrn_lb_logits, onorm_full, o1, dy1, states, hh)
    dp1 = jnp.concatenate([dq1, df1, di1, dgate1], axis=1)
    dh1 = _mm("hgrn_in_dh", dp1, w_h, tb=True)
    d_wh = _mm("hgrn_in_dw", h1, dp1, ta=True)
    dx1, d_g1 = _rms_bwd("rms1_bwd", x1, g1, dh1, dx2)
    dy0 = _mm("fox_out_dy", dx1, w_o[0], tb=True)
    d_wo0 = _mm("fox_out_dw", y0, dx1, ta=True)
    do0, dgate0, delta_rep = _fox_pre_bwd("fox_pre_bwd", dy0, o0, gate0)
    delta_row = delta_rep[:, ::HEAD_DIM].T.reshape(hh, 1, s)
    core_arr = core.reshape(1).astype(jnp.int32)
    g2_hgrn = d_wh.reshape(2, d // 2, 4, w).transpose(0, 2, 1, 3)
    g2_out = jnp.stack([d_wo0, d_wo1]).reshape(2, 4, wq, d)
    sib_hgrn, sib_out = _sibling_other_half("rs_sibling_early", [g2_hgrn, g2_out])
    pairs_early = [_pair_add("rs_pair_add_hgrn", g2_hgrn, sib_hgrn, core_arr),
                   _pair_add("rs_pair_add_out", g2_out, sib_out, core_arr)]
    dk0, dv0, dc_rep, dq0, rowsum_row, got_hgrn, got_out = _attn_bwd(
        "fox_attn_bwd", qkv, do0, lse_row, delta_row, c_rep, hh, scatter=pairs_early)
    dfl, d_bf = _fox_gate_bwd("fox_cumsum_bwd", _pad_lanes(rowsum_row.reshape(hh, s).T),
                              _pad_lanes(dc_rep[:, ::HEAD_DIM]), fl, bf_pad)
    dp0 = jnp.concatenate([dq0, dk0, dv0, dgate0], axis=1)
    dh0 = _mm("fox_in_dh", dp0, w_main, tb=True, res=_mm("fox_fl_dh", dfl, w_fl, tb=True))
    d_wmain = _mm("fox_in_dw", h0, dp0, ta=True)
    d_wfl = _mm("fox_fl_dw", h0, dfl, ta=True)
    grad_x, d_g0 = _rms_bwd("rms0_bwd", x0, g0, dh0, dx1)

    def grad_columns(a, b):
        out = []
        for lo, hi, src, shift in ((0, 3 * w, d_wmain, 0), (3 * w, 3 * w + hh, d_wfl, 3 * w),
                                   (3 * w + hh, 4 * w + hh, d_wmain, hh)):
            l2, h2 = max(a, lo), min(b, hi)
            if l2 < h2:
                out.append(src[:, l2 - shift:h2 - shift])
        return out

    g2_fox = jnp.stack([jnp.concatenate(grad_columns(j * fox_cols, (j + 1) * fox_cols), axis=1).reshape(2, d // 2, fox_cols)
                        for j in range(4)], axis=1)
    (sib_fox,) = _sibling_other_half("rs_sibling", [g2_fox])
    (got_fox,) = _scatter_chips("rs_scatter", [_pair_add("rs_pair_add_fox", g2_fox, sib_fox, core_arr)])
    halves_sum = [_sum_slots("rs_sum_" + nm, got, core_arr)
                  for nm, got in zip(("fox", "hgrn", "out"), (got_fox, got_hgrn, got_out))]
    r_fox, r_hgrn, r_out = _sibling_join("rs_join", halves_sum)
    g_fox = r_fox.reshape(d, fox_cols)
    g_hgrn = r_hgrn.reshape(d, w)
    g_out = r_out.reshape(2 * wq, d)

    small = jnp.concatenate([jnp.concatenate([d_g0, d_g1], axis=0).reshape(-1), d_lbl.reshape(-1), d_gf.reshape(-1),
                             d_onorm.reshape(-1), d_bf.reshape(-1)])
    n_small = small.shape[0]
    pad_to = -(-n_small // 1024) * 1024
    small = jnp.pad(small, (0, pad_to - n_small)).reshape(pad_to // 128, 128)
    small = _sum_slots("small_sum", _ag_devices("small_gather", small)).reshape(-1)
    g_norm = small[:2 * d].reshape(2, d)
    g_lbl = small[2 * d:2 * d + 2 * w].reshape(2, w)
    g_gf = small[2 * d + 2 * w:3 * d + 2 * w]
    g_onorm_full = small[3 * d + 2 * w:3 * d + 3 * w]
    g_bf = small[3 * d + 3 * w:3 * d + 3 * w + hh].reshape(1, hh)
    chip = 2 * lax.axis_index("x") + lax.axis_index("y")
    g_onorm = lax.dynamic_slice_in_dim(g_onorm_full, chip * wq, wq).reshape(1, wq)

    loss = lax.psum(loss_tile[0, 0], ("x", "y", "c"))

    def upd(name, wt, g, m, v):
        shp = wt.shape
        two = lambda a: a.reshape(-1, shp[-1])
        dl, nm, nv = _adamw(name, two(wt), two(g), two(m), two(v))
        return g.reshape(shp), dl.reshape(shp), nm.reshape(shp), nv.reshape(shp)

    res = [
        upd("adamw_norm_gains", norm_gains, g_norm, m_norm_gains, v_norm_gains),
        upd("adamw_fox_w_in", fox_w_in, g_fox, m_fox_w_in, v_fox_w_in),
        upd("adamw_fox_b_f", fox_b_f, g_bf, m_fox_b_f, v_fox_b_f),
        upd("adamw_hgrn_w_in", hgrn_w_in, g_hgrn, m_hgrn_w_in, v_hgrn_w_in),
        upd("adamw_lb_logits", hgrn_lb_logits, g_lbl, m_hgrn_lb_logits, v_hgrn_lb_logits),
        upd("adamw_onorm", hgrn_onorm, g_onorm, m_hgrn_onorm, v_hgrn_onorm),
        upd("adamw_w_out", w_out, g_out, m_w_out, v_w_out),
        upd("adamw_final_gain", final_gain.reshape(1, d), g_gf.reshape(1, d), m_final_gain.reshape(1, d),
            v_final_gain.reshape(1, d)),
    ]
    res[-1] = tuple(a.reshape(d) for a in res[-1])
    grads, deltas, new_m, new_v = zip(*res)
    return (loss, grad_x[None], *grads, *deltas, *new_m, *new_v)
```

```python
import functools

import jax
import jax.numpy as jnp
from jax import lax
from jax.experimental import pallas as pl
from jax.experimental.pallas import tpu as pltpu

F32 = jnp.float32
BF16 = jnp.bfloat16
MESH = pl.DeviceIdType.MESH
ANY = pl.BlockSpec(memory_space=pl.ANY)

EPS = 1e-6
HEAD_DIM = 128
HGRN_CHUNK = 64
HGRN_SUB = 32
EXP_CLAMP = 80.0
ATT_BLOCK = 512
HGRN_BLOCK = 512
GATE_BLOCK = 512
ROW_BLOCK = 256
MM_TM, MM_TN, MM_TK = 1024, 1024, 2048
VMEM_LIMIT_V7X = 56 * 1024 * 1024

ADAM_LR, ADAM_B1, ADAM_B2, ADAM_EPS, ADAM_WD, ADAM_STEP = 0.001, 0.9, 0.999, 1e-08, 0.01, 10

NT = (((1,), (1,)), ((), ()))
TN = (((0,), (0,)), ((), ()))
NN = (((1,), (0,)), ((), ()))


def _call(body, **kw):
    return pl.pallas_call(body, **kw)


def _cp(dims=None):
    kw = dict(vmem_limit_bytes=VMEM_LIMIT_V7X)
    if dims is not None:
        kw["dimension_semantics"] = dims
    return pltpu.CompilerParams(**kw)


def _sigmoid(x):
    return 1.0 / (1.0 + jnp.exp(-x))


def _dot(a, b, dn=NN):
    return lax.dot_general(a.astype(BF16), b.astype(BF16), dn, preferred_element_type=F32)


def _split3(x):
    hi = x.astype(BF16)
    r1 = x - hi.astype(F32)
    mid = r1.astype(BF16)
    lo = (r1 - mid.astype(F32)).astype(BF16)
    return hi, mid, lo


def _dot_exact(m01, x, right=False):
    hi, mid, lo = _split3(x)
    if right:
        dot = lambda p: lax.dot_general(p, m01, NN, preferred_element_type=F32)
    else:
        dot = lambda p: lax.dot_general(m01, p, NN, preferred_element_type=F32)
    return dot(hi) + dot(mid) + dot(lo)


def _row_block(rows, cap):
    if rows <= cap:
        return rows
    best = None
    for t in range(16, cap + 1, 16):
        if rows % t == 0:
            best = t
    assert best is not None, rows
    return best


def _mm(name, a, b, *, ta=False, tb=False, out_dtype=F32, res=None, n=None, b_off=0, b_gathered=False,
        out_gathered=None, scatter=()):
    m, k = (a.shape[1], a.shape[0]) if ta else a.shape
    if b_gathered:
        rh, cw = b.shape[2], b.shape[3]
        n_full = 2 * rh if tb else 4 * cw
    else:
        n_full = b.shape[0] if tb else b.shape[1]
    n = n_full if n is None else n
    tm, tn, tk = min(MM_TM, m), min(MM_TN, n), min(MM_TK, k)
    if b_gathered:
        tn, tk = (min(tn, rh), min(tk, cw)) if tb else (min(tn, cw), min(tk, rh))
    if out_gathered:
        tm, tn = min(tm, out_gathered[0]), min(tn, out_gathered[1])
    assert m % tm == 0 and n % tn == 0 and k % tk == 0 and b_off % tn == 0
    nk = k // tk
    jo = b_off // tn
    a_spec = pl.BlockSpec((tk, tm), lambda i, j, kk: (kk, i)) if ta else pl.BlockSpec((tm, tk), lambda i, j, kk: (i, kk))
    if b_gathered and tb:
        pr, pc = rh // tn, cw // tk
        b_spec = pl.BlockSpec((1, 1, tn, tk), lambda i, j, kk: (j // pr, kk // pc, j % pr, kk % pc))
    elif b_gathered:
        pr, pc = rh // tk, cw // tn
        b_spec = pl.BlockSpec((1, 1, tk, tn), lambda i, j, kk: (kk // pr, j // pc, kk % pr, j % pc))
    elif tb:
        b_spec = pl.BlockSpec((tn, tk), lambda i, j, kk: (j + jo, kk))
    else:
        b_spec = pl.BlockSpec((tk, tn), lambda i, j, kk: (kk, j + jo))
    o_spec = pl.BlockSpec((tm, tn), lambda i, j, kk: (i, j))
    if out_gathered:
        assert m == 2 * out_gathered[0] and n == 4 * out_gathered[1] and res is None
        qr, qc = out_gathered[0] // tm, out_gathered[1] // tn
        o_spec = pl.BlockSpec((1, 1, tm, tn), lambda i, j, kk: (i // qr, j // qc, i % qr, j % qc))
    dn = (((0 if ta else 1,), (1 if tb else 0,)), ((), ()))
    has_res = res is not None
    ns = len(scatter)
    grid = (m // tm, n // tn, nk)

    def body(*refs):
        a_ref, b_ref = refs[:2]
        r_ref = refs[2] if has_res else None
        base = 2 + int(has_res)
        p_refs = refs[base:base + ns]
        o_ref = refs[base + ns]
        got_refs = refs[base + ns + 1:base + 2 * ns + 1]
        acc = refs[base + 2 * ns + 1]
        sems = refs[base + 2 * ns + 2:]
        kk = pl.program_id(2)
        if ns:
            first = jnp.logical_and(jnp.logical_and(pl.program_id(0) == 0, pl.program_id(1) == 0), kk == 0)
            last = jnp.logical_and(jnp.logical_and(pl.program_id(0) == grid[0] - 1, pl.program_id(1) == grid[1] - 1),
                                   kk == nk - 1)

            @pl.when(first)
            def _():
                _scatter_phase(0, p_refs, got_refs, sems)

        bv = b_ref[0, 0] if b_gathered else b_ref[...]
        p = lax.dot_general(a_ref[...].astype(BF16), bv.astype(BF16), dn, preferred_element_type=F32)

        def finish(total):
            if r_ref is not None:
                total = total + r_ref[...]
            if out_gathered:
                o_ref[0, 0] = total.astype(out_dtype)
            else:
                o_ref[...] = total.astype(out_dtype)

        if nk == 1:
            finish(p)
        else:
            @pl.when(kk == 0)
            def _():
                acc[...] = p

            @pl.when(jnp.logical_and(kk > 0, kk < nk - 1))
            def _():
                acc[...] += p

            @pl.when(kk == nk - 1)
            def _():
                finish(acc[...] + p)

        if ns:
            @pl.when(last)
            def _():
                _scatter_phase(1, p_refs, got_refs, sems)

    ins = [a, b] + ([res] if has_res else []) + list(scatter)
    in_specs = [a_spec, b_spec] + ([o_spec] if has_res else []) + [ANY] * ns
    o_shape = jax.ShapeDtypeStruct((2, 4) + tuple(out_gathered) if out_gathered else (m, n), out_dtype)
    out = _call(
        body, name=name, grid=grid, in_specs=in_specs, out_specs=[o_spec] + [ANY] * ns,
        out_shape=[o_shape] + [jax.ShapeDtypeStruct(p.shape, p.dtype) for p in scatter],
        scratch_shapes=[pltpu.VMEM((tm, tn) if nk > 1 else (8, 128), F32)] + (_scatter_sems(ns) if ns else []),
        compiler_params=_cp(("arbitrary",) * 3 if ns else ("parallel", "parallel", "arbitrary")),
    )(*ins)
    return out if ns else out[0]


def _rms_fwd(name, x, g):
    s, d = x.shape
    tm = min(ROW_BLOCK, s)

    def body(x_ref, g_ref, h_ref):
        xv = x_ref[...]
        r = lax.rsqrt(jnp.mean(xv * xv, axis=-1, keepdims=True) + EPS)
        h_ref[...] = (xv * r * g_ref[...]).astype(BF16)

    row = pl.BlockSpec((tm, d), lambda i: (i, 0))
    vec = pl.BlockSpec((1, d), lambda i: (0, 0))
    return _call(body, name=name, grid=(s // tm,), in_specs=[row, vec], out_specs=row,
                 out_shape=jax.ShapeDtypeStruct((s, d), BF16), compiler_params=_cp(("parallel",)))(x, g)


def _rms_bwd(name, x, g, dh, dres):
    s, d = x.shape
    tm = min(ROW_BLOCK, s)

    def body(x_ref, g_ref, dh_ref, dres_ref, dx_ref, dg_ref):
        @pl.when(pl.program_id(0) == 0)
        def _():
            dg_ref[...] = jnp.zeros_like(dg_ref)

        xv = x_ref[...]
        r = lax.rsqrt(jnp.mean(xv * xv, axis=-1, keepdims=True) + EPS)
        xn = xv * r
        dhv = dh_ref[...]
        dxn = dhv * g_ref[...]
        dx_ref[...] = dres_ref[...] + r * (dxn - xn * jnp.mean(dxn * xn, axis=-1, keepdims=True))
        dg_ref[...] += jnp.sum(dhv * xn, axis=0, keepdims=True)

    row = pl.BlockSpec((tm, d), lambda i: (i, 0))
    vec = pl.BlockSpec((1, d), lambda i: (0, 0))
    return _call(body, name=name, grid=(s // tm,), in_specs=[row, vec, row, row], out_specs=[row, vec],
                 out_shape=[jax.ShapeDtypeStruct((s, d), F32), jax.ShapeDtypeStruct((1, d), F32)],
                 compiler_params=_cp(("arbitrary",)))(x, g, dh, dres)


def _loss_head(name, x, tgt, g):
    s, d = x.shape
    tm = min(ROW_BLOCK, s)
    nb = s // tm

    def body(x_ref, t_ref, g_ref, dx_ref, dg_ref, loss_ref, lacc):
        i = pl.program_id(0)

        @pl.when(i == 0)
        def _():
            dg_ref[...] = jnp.zeros_like(dg_ref)
            lacc[...] = jnp.zeros_like(lacc)

        xv = x_ref[...]
        gv = g_ref[...]
        r = lax.rsqrt(jnp.mean(xv * xv, axis=-1, keepdims=True) + EPS)
        xn = xv * r
        err = xn * gv - t_ref[...]
        lacc[...] += jnp.sum(err * err, axis=0, keepdims=True)
        dout = err * (1.0 / d)
        dg_ref[...] += jnp.sum(dout * xn, axis=0, keepdims=True)
        dxn = dout * gv
        dx_ref[...] = r * (dxn - xn * jnp.mean(dxn * xn, axis=-1, keepdims=True))

        @pl.when(i == nb - 1)
        def _():
            total = jnp.sum(lacc[...], axis=1, keepdims=True) * (0.5 / d)
            loss_ref[...] = jnp.broadcast_to(total, loss_ref.shape)

    row = pl.BlockSpec((tm, d), lambda i: (i, 0))
    vec = pl.BlockSpec((1, d), lambda i: (0, 0))
    one = pl.BlockSpec((1, 128), lambda i: (0, 0))
    return _call(body, name=name, grid=(nb,), in_specs=[row, row, vec], out_specs=[row, vec, one],
                 out_shape=[jax.ShapeDtypeStruct((s, d), F32), jax.ShapeDtypeStruct((1, d), F32),
                            jax.ShapeDtypeStruct((1, 128), F32)],
                 scratch_shapes=[pltpu.VMEM((1, d), F32)], compiler_params=_cp(("arbitrary",)))(x, tgt, g)


def _fox_gate_fwd(name, fl, bf, w):
    s = fl.shape[0]
    tb = min(GATE_BLOCK, s)

    def body(fl_ref, bf_ref, c_ref, crep_ref, carry):
        @pl.when(pl.program_id(0) == 0)
        def _():
            carry[...] = jnp.zeros_like(carry)

        z = fl_ref[...] + bf_ref[...]
        lf = jnp.minimum(z, 0.0) - jnp.log(1.0 + jnp.exp(-jnp.abs(z)))
        rows = lax.broadcasted_iota(jnp.int32, (tb, tb), 0)
        cols = lax.broadcasted_iota(jnp.int32, (tb, tb), 1)
        tri = (rows >= cols).astype(BF16)
        cs = _dot_exact(tri, lf) + carry[...]
        c_ref[...] = cs
        carry[...] = c_ref[tb - 1:tb, :]
        sel_r = lax.broadcasted_iota(jnp.int32, (128, w), 0)
        sel_c = lax.broadcasted_iota(jnp.int32, (128, w), 1)
        sel = (sel_r == sel_c // HEAD_DIM).astype(BF16)
        crep_ref[...] = _dot_exact(sel, cs, right=True)

    blk = pl.BlockSpec((tb, 128), lambda i: (i, 0))
    return _call(body, name=name, grid=(s // tb,),
                 in_specs=[blk, pl.BlockSpec((1, 128), lambda i: (0, 0))],
                 out_specs=[blk, pl.BlockSpec((tb, w), lambda i: (i, 0))],
                 out_shape=[jax.ShapeDtypeStruct((s, 128), F32), jax.ShapeDtypeStruct((s, w), F32)],
                 scratch_shapes=[pltpu.VMEM((1, 128), F32)], compiler_params=_cp(("arbitrary",)))(fl, bf)


def _fox_gate_bwd(name, drow, dcol, fl, bf):
    s = fl.shape[0]
    tb = min(GATE_BLOCK, s)
    nb = s // tb

    def body(dr_ref, dc_ref, fl_ref, bf_ref, dfl_ref, dbf_ref, carry, tmp):
        @pl.when(pl.program_id(0) == 0)
        def _():
            carry[...] = jnp.zeros_like(carry)
            dbf_ref[...] = jnp.zeros_like(dbf_ref)

        rows = lax.broadcasted_iota(jnp.int32, (tb, tb), 0)
        cols = lax.broadcasted_iota(jnp.int32, (tb, tb), 1)
        triu = (rows <= cols).astype(BF16)
        dlf = _dot_exact(triu, dr_ref[...] + dc_ref[...]) + carry[...]
        tmp[...] = dlf
        carry[...] = tmp[0:1, :]
        z = fl_ref[...] + bf_ref[...]
        dfl = dlf * (1.0 / (1.0 + jnp.exp(z)))
        dfl_ref[...] = dfl
        dbf_ref[...] += jnp.sum(dfl, axis=0, keepdims=True)

    blk = pl.BlockSpec((tb, 128), lambda i: (nb - 1 - i, 0))
    vec = pl.BlockSpec((1, 128), lambda i: (0, 0))
    return _call(body, name=name, grid=(nb,), in_specs=[blk, blk, blk, vec], out_specs=[blk, vec],
                 out_shape=[jax.ShapeDtypeStruct((s, 128), F32), jax.ShapeDtypeStruct((1, 128), F32)],
                 scratch_shapes=[pltpu.VMEM((1, 128), F32), pltpu.VMEM((tb, 128), F32)],
                 compiler_params=_cp(("arbitrary",)))(drow, dcol, fl, bf)


def _attn_fwd(name, qkv, c_rep, h_count, gather=()):
    s = qkv.shape[0]
    w = h_count * HEAD_DIM
    t = min(ATT_BLOCK, s)
    scale = HEAD_DIM ** -0.5
    hp = 2 if h_count % 2 == 0 else 1
    wb = hp * HEAD_DIM
    ng = len(gather)
    nh, nq = h_count // hp, s // t

    def body(*refs):
        q_ref, k_ref, v_ref, c_ref = refs[:4]
        h_refs = refs[4:4 + ng]
        o_ref, lse_ref = refs[4 + ng:6 + ng]
        big_refs = refs[6 + ng:6 + 2 * ng]
        m_s, l_s, acc_s = refs[6 + 2 * ng:9 + 2 * ng]
        sems = refs[9 + 2 * ng:]
        hs = pl.program_id(0)
        qi = pl.program_id(1)
        if ng:
            @pl.when(jnp.logical_and(hs == 0, qi == 0))
            def _():
                _ag_phase(0, h_refs, big_refs, sems)

            @pl.when(jnp.logical_and(hs == nh // 2, qi == 0))
            def _():
                _ag_phase(1, h_refs, big_refs, sems)

        m_s[...] = jnp.full(m_s.shape, -jnp.inf, F32)
        l_s[...] = jnp.zeros_like(l_s)
        acc_s[...] = jnp.zeros_like(acc_s)

        def step(kb, masked):
            off = pl.multiple_of(kb * t, t)
            for a in range(hp):
                cols_a = slice(a * HEAD_DIM, (a + 1) * HEAD_DIM)
                kk = k_ref[pl.ds(off, t), cols_a]
                vv = v_ref[pl.ds(off, t), cols_a]
                cc = jnp.tile(c_ref[pl.ds(off, t), cols_a], (1, t // HEAD_DIM))
                st = lax.dot_general(kk, q_ref[:, cols_a], NT, preferred_element_type=F32) * scale - cc
                if masked:
                    rows = lax.broadcasted_iota(jnp.int32, (t, t), 0)
                    cols = lax.broadcasted_iota(jnp.int32, (t, t), 1)
                    st = jnp.where(cols >= rows, st, -jnp.inf)
                m_prev = m_s[a]
                m_new = jnp.maximum(m_prev, jnp.max(st, axis=0, keepdims=True))
                pt = jnp.exp(st - m_new)
                alpha = jnp.exp(m_prev - m_new)
                l_s[a] = alpha * l_s[a] + jnp.sum(pt, axis=0, keepdims=True)
                acc_s[a] = alpha * acc_s[a] + lax.dot_general(vv, pt.astype(BF16), TN, preferred_element_type=F32)
                m_s[a] = m_new

        def loop_body(kb, carry):
            step(kb, False)
            return carry

        lax.fori_loop(0, qi, loop_body, 0)
        step(qi, True)
        for a in range(hp):
            l = l_s[a]
            o_ref[:, a * HEAD_DIM:(a + 1) * HEAD_DIM] = (acc_s[a] / l).T
            lse_ref[a] = m_s[a] + jnp.log(l)

        if ng:
            @pl.when(jnp.logical_and(hs == nh - 1, qi == nq - 1))
            def _():
                _ag_phase(2, h_refs, big_refs, sems)

    blk = lambda off: pl.BlockSpec((t, wb), lambda h, i: (i, off + h))
    whole = lambda off: pl.BlockSpec((s, wb), lambda h, i: (0, off + h))
    rowv = pl.BlockSpec((hp, 1, t), lambda h, i: (h, 0, i))
    return _call(
        body, name=name, grid=(nh, nq),
        in_specs=[blk(0), whole(nh), whole(2 * nh), whole(0)] + [ANY] * ng,
        out_specs=[blk(0), rowv] + [ANY] * ng,
        out_shape=[jax.ShapeDtypeStruct((s, w), F32), jax.ShapeDtypeStruct((h_count, 1, s), F32)] + _ag_out_shapes(gather),
        scratch_shapes=[pltpu.VMEM((hp, 1, t), F32), pltpu.VMEM((hp, 1, t), F32), pltpu.VMEM((hp, HEAD_DIM, t), F32)]
        + (_ag_sems(ng) if ng else []),
        compiler_params=_cp(("arbitrary", "arbitrary") if ng else ("parallel", "arbitrary")),
    )(qkv, qkv, qkv, c_rep, *gather)


def _attn_bwd(name, qkv, do, lse_row, delta_row, c_rep, h_count, scatter=()):
    s = qkv.shape[0]
    w = h_count * HEAD_DIM
    t = min(ATT_BLOCK, s)
    nq = s // t
    scale = HEAD_DIM ** -0.5
    ns = len(scatter)

    def body(*refs):
        k_ref, v_ref, q_ref, do_ref, lse_ref, dl_ref, c_ref = refs[:7]
        p_refs = refs[7:7 + ns]
        dk_ref, dv_ref, dc_ref, dq_ref, rs_ref = refs[7 + ns:12 + ns]
        got_refs = refs[12 + ns:12 + 2 * ns]
        dk_s, dv_s, dc_s, dq_s, rs_s = refs[12 + 2 * ns:17 + 2 * ns]
        sems = refs[17 + 2 * ns:]
        kj = pl.program_id(1)
        if ns:
            @pl.when(jnp.logical_and(pl.program_id(0) == 0, kj == 0))
            def _():
                _scatter_phase(0, p_refs, got_refs, sems)

        kk = k_ref[...]
        vv = v_ref[...]
        ccol = jnp.tile(c_ref[...], (1, t // HEAD_DIM))
        dk_s[...] = jnp.zeros_like(dk_s)
        dv_s[...] = jnp.zeros_like(dv_s)
        dc_s[...] = jnp.zeros_like(dc_s)

        @pl.when(kj == 0)
        def _():
            dq_s[...] = jnp.zeros_like(dq_s)
            rs_s[...] = jnp.zeros_like(rs_s)

        def step(qb, masked):
            off = pl.multiple_of(qb * t, t)
            q = q_ref[pl.ds(off, t), :]
            dov = do_ref[pl.ds(off, t), :]
            st = lax.dot_general(kk, q, NT, preferred_element_type=F32) * scale - ccol
            pt = jnp.exp(st - lse_ref[0, :, pl.ds(off, t)])
            if masked:
                rows = lax.broadcasted_iota(jnp.int32, (t, t), 0)
                cols = lax.broadcasted_iota(jnp.int32, (t, t), 1)
                pt = jnp.where(cols >= rows, pt, 0.0)
            dv_s[...] += lax.dot_general(pt.astype(BF16), dov, NN, preferred_element_type=F32)
            dpt = lax.dot_general(vv, dov, NT, preferred_element_type=F32)
            dst = pt * (dpt - dl_ref[0, :, pl.ds(off, t)])
            dstb = dst.astype(BF16)
            dk_s[...] += lax.dot_general(dstb, q, NN, preferred_element_type=F32)
            dc_s[...] += jnp.sum(dst, axis=1, keepdims=True)
            dq_s[:, pl.ds(off, t)] += lax.dot_general(kk, dstb, TN, preferred_element_type=F32)
            rs_s[:, pl.ds(off, t)] += jnp.sum(dst, axis=0, keepdims=True)

        step(kj, True)

        def loop_body(qb, carry):
            step(qb, False)
            return carry

        lax.fori_loop(kj + 1, nq, loop_body, 0)
        dk_ref[...] = (dk_s[...] * scale).astype(BF16)
        dv_ref[...] = dv_s[...].astype(BF16)
        dc_ref[...] = jnp.broadcast_to(-dc_s[...], dc_ref.shape)

        @pl.when(kj == nq - 1)
        def _():
            for b in range(nq):
                dq_ref[b * t:(b + 1) * t, :] = (dq_s[:, b * t:(b + 1) * t] * scale).T.astype(BF16)
            rs_ref[0] = rs_s[...]

        if ns:
            @pl.when(jnp.logical_and(pl.program_id(0) == h_count - 1, kj == nq - 1))
            def _():
                _scatter_phase(1, p_refs, got_refs, sems)

    hh = h_count
    blk = lambda off: pl.BlockSpec((t, HEAD_DIM), lambda h, j: (j, off + h))
    whole = pl.BlockSpec((s, HEAD_DIM), lambda h, j: (0, h))
    rowv = pl.BlockSpec((1, 1, s), lambda h, j: (h, 0, 0))
    return _call(
        body, name=name, grid=(hh, nq),
        in_specs=[blk(hh), blk(2 * hh), whole, whole, rowv, rowv, blk(0)] + [ANY] * ns,
        out_specs=[blk(0), blk(0), blk(0), whole, rowv] + [ANY] * ns,
        out_shape=[jax.ShapeDtypeStruct((s, w), BF16), jax.ShapeDtypeStruct((s, w), BF16),
                   jax.ShapeDtypeStruct((s, w), F32), jax.ShapeDtypeStruct((s, w), BF16),
                   jax.ShapeDtypeStruct((hh, 1, s), F32)] + [jax.ShapeDtypeStruct(p.shape, p.dtype) for p in scatter],
        scratch_shapes=[pltpu.VMEM((t, HEAD_DIM), F32), pltpu.VMEM((t, HEAD_DIM), F32), pltpu.VMEM((t, 1), F32),
                        pltpu.VMEM((HEAD_DIM, s), F32), pltpu.VMEM((1, s), F32)] + (_scatter_sems(ns) if ns else []),
        compiler_params=_cp(("arbitrary", "arbitrary") if ns else ("parallel", "arbitrary")),
    )(qkv, qkv, qkv, do, lse_row, delta_row, c_rep, *scatter)


def _fox_post(name, o, gate):
    s, w = o.shape
    tm = min(ROW_BLOCK, s)

    def body(o_ref, g_ref, y_ref):
        g = g_ref[...]
        y_ref[...] = (o_ref[...] * (g * _sigmoid(g))).astype(BF16)

    row = pl.BlockSpec((tm, w), lambda i: (i, 0))
    return _call(body, name=name, grid=(s // tm,), in_specs=[row, row], out_specs=row,
                 out_shape=jax.ShapeDtypeStruct((s, w), BF16), compiler_params=_cp(("parallel",)))(o, gate)


def _fox_pre_bwd(name, dy, o, gate):
    s, w = o.shape
    tm = min(ROW_BLOCK, s)

    def body(dy_ref, o_ref, g_ref, do_ref, dg_ref, dl_ref):
        g = g_ref[...]
        sg = _sigmoid(g)
        dyv = dy_ref[...]
        ov = o_ref[...]
        dov = dyv * (g * sg)
        do_ref[...] = dov.astype(BF16)
        dg_ref[...] = (dyv * ov * (sg * (1.0 + g * (1.0 - sg)))).astype(BF16)
        prod = dov * ov
        for h in range(w // HEAD_DIM):
            sl = slice(h * HEAD_DIM, (h + 1) * HEAD_DIM)
            dl_ref[:, sl] = jnp.broadcast_to(jnp.sum(prod[:, sl], axis=1, keepdims=True), (tm, HEAD_DIM))

    row = pl.BlockSpec((tm, w), lambda i: (i, 0))
    return _call(body, name=name, grid=(s // tm,), in_specs=[row, row, row], out_specs=[row, row, row],
                 out_shape=[jax.ShapeDtypeStruct((s, w), BF16), jax.ShapeDtypeStruct((s, w), BF16),
                            jax.ShapeDtypeStruct((s, w), F32)],
                 compiler_params=_cp(("parallel",)))(dy, o, gate)


def _hgrn_chunk_terms(q_c, k_c, b_c, b_s, base):
    nsub = HGRN_CHUNK // HGRN_SUB
    refs = [jnp.zeros((1, HEAD_DIM), F32)]
    for i in range(1, nsub):
        r0 = base + i * HGRN_SUB - 1
        refs.append(b_s[r0:r0 + 1, :])
    rfull = jnp.concatenate([jnp.broadcast_to(r, (HGRN_SUB, HEAD_DIM)) for r in refs], axis=0)
    eq = jnp.exp(b_c - rfull)
    qe = q_c * eq
    es = [jnp.exp(jnp.minimum(r - b_c, EXP_CLAMP)) for r in refs]
    kes = [(k_c * e).astype(BF16) for e in es]
    return eq, qe, es, kes


def _chunk_cumsum(x, reverse=False):
    cc = HGRN_CHUNK
    nc = x.shape[0] // cc
    rows = lax.broadcasted_iota(jnp.int32, (cc, cc), 0)
    cols = lax.broadcasted_iota(jnp.int32, (cc, cc), 1)
    tri = ((rows <= cols) if reverse else (rows >= cols)).astype(BF16)
    wide = jnp.concatenate([x[n * cc:(n + 1) * cc] for n in range(nc)], axis=1)
    res = _dot_exact(tri, wide)
    return jnp.concatenate([res[:, n * HEAD_DIM:(n + 1) * HEAD_DIM] for n in range(nc)], axis=0)


def _hgrn_block_pre(q_ref, f_ref, lbl_ref, b_s):
    lb = _sigmoid(lbl_ref[1:2, :] - lbl_ref[0:1, :])
    qr = q_ref[...]
    sq = _sigmoid(qr)
    q = qr * sq
    fz = f_ref[...]
    sg = _sigmoid(fz)
    f = lb + (1.0 - lb) * sg
    g = jnp.log(f)
    k = (1.0 - lb) * (1.0 / (1.0 + jnp.exp(fz)))
    b = _chunk_cumsum(g)
    b_s[...] = b
    return lb, qr, sq, q, sg, f, k, b


def _hgrn_fwd(name, p1, lbl, onorm, h_count):
    s = p1.shape[0]
    w = h_count * HEAD_DIM
    t = min(HGRN_BLOCK, s)
    nc = t // HGRN_CHUNK
    nsub = HGRN_CHUNK // HGRN_SUB
    cc = HGRN_CHUNK

    def body(q_ref, f_ref, i_ref, g_ref, lbl_ref, on_ref, y_ref, o_ref, st_ref, state_s, b_s):
        @pl.when(pl.program_id(1) == 0)
        def _():
            state_s[...] = jnp.zeros_like(state_s)

        lb, qr, sq, q, sg, f, k, b = _hgrn_block_pre(q_ref, f_ref, lbl_ref, b_s)
        v = i_ref[...]
        r64 = lax.broadcasted_iota(jnp.int32, (cc, cc), 0)
        c64 = lax.broadcasted_iota(jnp.int32, (cc, cc), 1)
        for n in range(nc):
            sl = slice(n * cc, (n + 1) * cc)
            q_c, k_c, v_c, b_c = q[sl], k[sl], v[sl], b[sl]
            bl = b_s[n * cc + cc - 1:n * cc + cc, :]
            eq, qe, es, kes = _hgrn_chunk_terms(q_c, k_c, b_c, b_s, n * cc)
            qeb = qe.astype(BF16)
            a = jnp.concatenate(
                [lax.dot_general(qeb[i * HGRN_SUB:(i + 1) * HGRN_SUB], kes[i], NT, preferred_element_type=F32)
                 for i in range(nsub)], axis=0)
            a = jnp.where(r64 >= c64, a, 0.0)
            st = state_s[...]
            st_ref[0, n] = st
            inter = _dot(q_c * jnp.exp(b_c), st, NT)
            intra = _dot(a, v_c)
            o_ref[sl, :] = inter + intra
            kb = k_c * jnp.exp(bl - b_c)
            state_s[...] = st * jnp.exp(bl) + _dot(v_c, kb, TN)
        o = o_ref[...]
        rr = lax.rsqrt(jnp.mean(o * o, axis=-1, keepdims=True) + EPS)
        gate = g_ref[...]
        y_ref[...] = ((o * rr) * on_ref[...] * (gate * _sigmoid(gate))).astype(BF16)

    hh = h_count
    blk = lambda off: pl.BlockSpec((t, HEAD_DIM), lambda h, i: (i, off + h))
    return _call(
        body, name=name, grid=(hh, s // t),
        in_specs=[blk(0), blk(hh), blk(2 * hh), blk(3 * hh),
                  pl.BlockSpec((2, HEAD_DIM), lambda h, i: (0, h)), pl.BlockSpec((1, HEAD_DIM), lambda h, i: (0, h))],
        out_specs=[blk(0), blk(0), pl.BlockSpec((1, nc, HEAD_DIM, HEAD_DIM), lambda h, i: (h, i, 0, 0))],
        out_shape=[jax.ShapeDtypeStruct((s, w), BF16), jax.ShapeDtypeStruct((s, w), F32),
                   jax.ShapeDtypeStruct((hh, s // cc, HEAD_DIM, HEAD_DIM), F32)],
        scratch_shapes=[pltpu.VMEM((HEAD_DIM, HEAD_DIM), F32), pltpu.VMEM((t, HEAD_DIM), F32)],
        compiler_params=_cp(("parallel", "arbitrary")),
    )(p1, p1, p1, p1, lbl, onorm)


def _hgrn_bwd(name, p1, lbl, onorm, o, dy, states, h_count):
    s = p1.shape[0]
    w = h_count * HEAD_DIM
    t = min(HGRN_BLOCK, s)
    nb = s // t
    nc = t // HGRN_CHUNK
    nsub = HGRN_CHUNK // HGRN_SUB
    cc = HGRN_CHUNK

    def body(q_ref, f_ref, i_ref, g_ref, lbl_ref, on_ref, o_ref, dy_ref, st_ref,
             dq_ref, df_ref, di_ref, dgate_ref, dlog_ref, don_ref,
             dst_s, b_s, do_s, dqs, dks, dbs, exs):
        @pl.when(pl.program_id(1) == 0)
        def _():
            dst_s[...] = jnp.zeros_like(dst_s)
            dlog_ref[...] = jnp.zeros_like(dlog_ref)
            don_ref[...] = jnp.zeros_like(don_ref)

        lb, qr, sq, q, sg, f, k, b = _hgrn_block_pre(q_ref, f_ref, lbl_ref, b_s)
        v = i_ref[...]

        ov = o_ref[...]
        rr = lax.rsqrt(jnp.mean(ov * ov, axis=-1, keepdims=True) + EPS)
        on = ov * rr
        gate = g_ref[...]
        sgt = _sigmoid(gate)
        silu = gate * sgt
        dyv = dy_ref[...]
        gain = on_ref[...]
        dgate_ref[...] = (dyv * on * gain * (sgt * (1.0 + gate * (1.0 - sgt)))).astype(BF16)
        don_ref[...] += jnp.sum(dyv * on * silu, axis=0, keepdims=True)
        d_on = dyv * gain * silu
        do_s[...] = rr * (d_on - on * jnp.mean(d_on * on, axis=-1, keepdims=True))

        r64 = lax.broadcasted_iota(jnp.int32, (cc, cc), 0)
        c64 = lax.broadcasted_iota(jnp.int32, (cc, cc), 1)
        upper = r64 <= c64
        for n in reversed(range(nc)):
            sl = slice(n * cc, (n + 1) * cc)
            q_c, k_c, v_c, b_c = q[sl], k[sl], v[sl], b[sl]
            do_c = do_s[sl, :]
            bl = b_s[n * cc + cc - 1:n * cc + cc, :]
            ebl = jnp.exp(bl)
            eq, qe, es, kes = _hgrn_chunk_terms(q_c, k_c, b_c, b_s, n * cc)
            qeb = qe.astype(BF16)
            dob = do_c.astype(BF16)
            vb = v_c.astype(BF16)
            st = st_ref[0, n]
            dstn = dst_s[...]
            qb_ = q_c * jnp.exp(b_c)
            kb_ = k_c * jnp.exp(bl - b_c)
            at = jnp.zeros((cc, cc), F32)
            for i in range(nsub):
                blk_i = (c64 // HGRN_SUB) == i
                at = at + jnp.where(blk_i, lax.dot_general(kes[i], qeb, NT, preferred_element_type=F32), 0.0)
            at = jnp.where(upper, at, 0.0)
            dv = _dot(at, dob) + _dot(kb_, dstn, NT)
            dqb = _dot(dob, st)
            dkb = _dot(vb, dstn)
            da = jnp.where(r64 >= c64, lax.dot_general(dob, vb, NT, preferred_element_type=F32), 0.0)
            dat = jnp.where(upper, lax.dot_general(vb, dob, NT, preferred_element_type=F32), 0.0)
            dab = da.astype(BF16)
            dq_raw = jnp.concatenate(
                [lax.dot_general(dab[i * HGRN_SUB:(i + 1) * HGRN_SUB], kes[i], NN, preferred_element_type=F32)
                 for i in range(nsub)], axis=0)
            db = qb_.astype(BF16).astype(F32) * dqb + qeb.astype(F32) * dq_raw
            dkbk = dkb * kb_.astype(BF16).astype(F32)
            db = db - dkbk
            dk_in = jnp.zeros((cc, HEAD_DIM), F32)
            for i in range(nsub):
                blk_i = (c64 // HGRN_SUB) == i
                dk_raw = _dot(jnp.where(blk_i, dat, 0.0), qeb)
                dk_in = dk_in + dk_raw * es[i]
                db = db - kes[i].astype(F32) * dk_raw
            dqs[sl, :] = dqb * jnp.exp(b_c) + dq_raw * eq
            dks[sl, :] = dkb * jnp.exp(bl - b_c) + dk_in
            dbs[sl, :] = db
            extra = jnp.sum(dkbk, axis=0, keepdims=True) + jnp.sum(dstn * st, axis=0, keepdims=True) * ebl
            exs[sl, :] = jnp.broadcast_to(extra, (cc, HEAD_DIM))
            di_ref[sl, :] = dv.astype(BF16)
            dst_s[...] = dstn * ebl + _dot(dob, qb_, TN)

        dg = _chunk_cumsum(dbs[...], reverse=True) + exs[...]
        df = dg / f - dks[...]
        df_ref[...] = (df * (1.0 - lb) * sg * (1.0 - sg)).astype(BF16)
        dq_ref[...] = (dqs[...] * (sq * (1.0 + qr * (1.0 - sq)))).astype(BF16)
        dlb = jnp.sum(df * (1.0 - sg), axis=0, keepdims=True) * (lb * (1.0 - lb))
        dlog_ref[0:1, :] += -dlb
        dlog_ref[1:2, :] += dlb

    hh = h_count
    blk = lambda off: pl.BlockSpec((t, HEAD_DIM), lambda h, i: (nb - 1 - i, off + h))
    two = pl.BlockSpec((2, HEAD_DIM), lambda h, i: (0, h))
    one = pl.BlockSpec((1, HEAD_DIM), lambda h, i: (0, h))
    act = jax.ShapeDtypeStruct((s, w), BF16)
    tile = pltpu.VMEM((t, HEAD_DIM), F32)
    return _call(
        body, name=name, grid=(hh, nb),
        in_specs=[blk(0), blk(hh), blk(2 * hh), blk(3 * hh), two, one, blk(0), blk(0),
                  pl.BlockSpec((1, nc, HEAD_DIM, HEAD_DIM), lambda h, i: (h, nb - 1 - i, 0, 0))],
        out_specs=[blk(0), blk(0), blk(0), blk(0), two, one],
        out_shape=[act, act, act, act, jax.ShapeDtypeStruct((2, w), F32), jax.ShapeDtypeStruct((1, w), F32)],
        scratch_shapes=[pltpu.VMEM((HEAD_DIM, HEAD_DIM), F32), tile, tile, tile, tile, tile, tile],
        compiler_params=_cp(("parallel", "arbitrary")),
    )(p1, p1, p1, p1, lbl, onorm, o, dy, states)


def _adamw(name, w, g, m, v):
    r, c = w.shape
    tr = r if r <= 128 else _row_block(r, 128)
    c1 = 1.0 - ADAM_B1 ** ADAM_STEP
    c2 = 1.0 - ADAM_B2 ** ADAM_STEP

    def body(w_ref, g_ref, m_ref, v_ref, d_ref, nm_ref, nv_ref):
        gv = g_ref[...]
        nm = ADAM_B1 * m_ref[...] + (1.0 - ADAM_B1) * gv
        nv = ADAM_B2 * v_ref[...] + (1.0 - ADAM_B2) * (gv * gv)
        nm_ref[...] = nm
        nv_ref[...] = nv
        d_ref[...] = -ADAM_LR * ((nm / c1) / (jnp.sqrt(nv / c2) + ADAM_EPS) + ADAM_WD * w_ref[...])

    blk = pl.BlockSpec((tr, c), lambda i: (i, 0))
    sh = jax.ShapeDtypeStruct((r, c), F32)
    return _call(body, name=name, grid=(r // tr,), in_specs=[blk] * 4, out_specs=[blk] * 3,
                 out_shape=[sh, sh, sh], compiler_params=_cp(("parallel",)))(w, g, m, v)


SLAB_BLOCK_BYTES = 2 * 1024 * 1024


def _slab_rows(r, c):
    return _row_block(r, max(16, SLAB_BLOCK_BYTES // (4 * c) // 16 * 16))


def _pair_add(name, g2, recv, core):
    _, nch, r, c = g2.shape
    tr = _slab_rows(r, c)

    grid_spec = pltpu.PrefetchScalarGridSpec(
        num_scalar_prefetch=1, grid=(nch, r // tr),
        in_specs=[pl.BlockSpec((1, 1, tr, c), lambda j, i, cr: (cr[0], j, i, 0)),
                  pl.BlockSpec((1, tr, c), lambda j, i, cr: (j, i, 0))],
        out_specs=pl.BlockSpec((1, tr, c), lambda j, i, cr: (j, i, 0)))

    def body(core_ref, a_ref, b_ref, o_ref):
        o_ref[...] = (a_ref[0] + b_ref[...]).astype(BF16)

    return _call(body, name=name, grid_spec=grid_spec, out_shape=jax.ShapeDtypeStruct((nch, r, c), BF16),
                 compiler_params=_cp(("parallel", "parallel")))(core, g2, recv)


def _sum_slots(name, x, core=None):
    n, r, c = x.shape
    tr = _slab_rows(r, c)

    def total(x_ref):
        acc = x_ref[0].astype(F32)
        for j in range(1, n):
            acc = acc + x_ref[j].astype(F32)
        return acc

    if core is None:
        def body(x_ref, o_ref):
            o_ref[...] = total(x_ref)

        return _call(body, name=name, grid=(r // tr,),
                     in_specs=[pl.BlockSpec((n, tr, c), lambda i: (0, i, 0))],
                     out_specs=pl.BlockSpec((tr, c), lambda i: (i, 0)),
                     out_shape=jax.ShapeDtypeStruct((r, c), F32), compiler_params=_cp(("parallel",)))(x)

    def body_half(core_ref, x_ref, o_ref):
        o_ref[0] = total(x_ref)

    grid_spec = pltpu.PrefetchScalarGridSpec(
        num_scalar_prefetch=1, grid=(r // tr,),
        in_specs=[pl.BlockSpec((n, tr, c), lambda i, cr: (0, i, 0))],
        out_specs=pl.BlockSpec((1, tr, c), lambda i, cr: (cr[0], i, 0)))
    return _call(body_half, name=name, grid_spec=grid_spec, out_shape=jax.ShapeDtypeStruct((2, r, c), F32),
                 compiler_params=_cp(("parallel",)))(core, x)


def _pos():
    return lax.axis_index("x"), lax.axis_index("y"), lax.axis_index("c")


def _flip(v, f):
    return (1 - v) if f else v


CHIP_FLIPS = ((0, 1), (1, 0), (1, 1))
DEV_FLIPS = tuple((fx, fy, fc) for fx in (0, 1) for fy in (0, 1) for fc in (0, 1))[1:]


def _remote(src, dst, ssem, rsem, dev):
    return pltpu.make_async_remote_copy(src_ref=src, dst_ref=dst, send_sem=ssem, recv_sem=rsem,
                                        device_id=dev, device_id_type=MESH)


def _ag_weights(name, halves):
    n = len(halves)

    def body(*refs):
        h_refs, big_refs, sems = refs[:n], refs[n:2 * n], refs[2 * n:]
        _ag_phase(0, h_refs, big_refs, sems)
        _ag_phase(1, h_refs, big_refs, sems)
        _ag_phase(2, h_refs, big_refs, sems)

    return _call(body, name=name, in_specs=[ANY] * n, out_specs=[ANY] * n,
                 out_shape=_ag_out_shapes(halves), scratch_shapes=_ag_sems(n))(*halves)


def _ag_out_shapes(halves):
    return [jax.ShapeDtypeStruct((2, 4) + h.shape, h.dtype) for h in halves]


def _ag_sems(n):
    return [pltpu.SemaphoreType.DMA((n,)), pltpu.SemaphoreType.DMA((7 * n,)), pltpu.SemaphoreType.DMA((7 * n,))]


def _ag_phase(phase, h_refs, big_refs, sems):
    lsem, ssem, rsem = sems
    n = len(h_refs)
    x, y, cc = _pos()
    me = 2 * x + y
    sib = (x, y, 1 - cc)
    peers = [(_flip(x, fx), _flip(y, fy), cc) for fx, fy in CHIP_FLIPS]
    chips = [2 * px + py for px, py, _ in peers]
    for a in range(n):
        mine = big_refs[a].at[cc, me]
        first = [pltpu.make_async_copy(h_refs[a], mine, lsem.at[a])]
        first += [_remote(h_refs[a], mine, ssem.at[7 * a + k], rsem.at[7 * a + k], peers[k]) for k in range(3)]
        first += [_remote(h_refs[a], mine, ssem.at[7 * a + 3], rsem.at[7 * a + 3], sib)]
        passed = []
        for k in range(3):
            blk = big_refs[a].at[cc, chips[k]]
            passed.append(_remote(blk, blk, ssem.at[7 * a + 4 + k], rsem.at[7 * a + 4 + k], sib))
        if phase == 0:
            for cp in first:
                cp.start()
        elif phase == 1:
            for k in range(3):
                blk = big_refs[a].at[cc, chips[k]]
                _remote(blk, blk, ssem.at[7 * a + k], rsem.at[7 * a + k], peers[k]).wait_recv()
                passed[k].start()
        else:
            theirs = big_refs[a].at[1 - cc, me]
            _remote(theirs, theirs, ssem.at[7 * a + 3], rsem.at[7 * a + 3], sib).wait_recv()
            for k in range(3):
                blk = big_refs[a].at[1 - cc, chips[k]]
                _remote(blk, blk, ssem.at[7 * a + 4 + k], rsem.at[7 * a + 4 + k], sib).wait_recv()
            first[0].wait()
            for cp in first[1:] + passed:
                cp.wait_send()


def _ag_chips(name, v):
    r, c = v.shape

    def body(v_ref, out_ref, lsem, ssem, rsem):
        x, y, cc = _pos()
        me = 2 * x + y
        loc = pltpu.make_async_copy(v_ref, out_ref.at[me], lsem)
        loc.start()
        started = []
        for k, (fx, fy) in enumerate(CHIP_FLIPS):
            cp = _remote(v_ref, out_ref.at[me], ssem.at[k], rsem.at[k], (_flip(x, fx), _flip(y, fy), cc))
            cp.start()
            started.append(cp)
        for k, (fx, fy) in enumerate(CHIP_FLIPS):
            blk = out_ref.at[2 * _flip(x, fx) + _flip(y, fy)]
            _remote(blk, blk, ssem.at[k], rsem.at[k], (_flip(x, fx), _flip(y, fy), cc)).wait_recv()
        for cp in started:
            cp.wait_send()
        loc.wait()

    return _call(body, name=name, in_specs=[ANY], out_specs=ANY,
                 out_shape=jax.ShapeDtypeStruct((4, r, c), v.dtype),
                 scratch_shapes=[pltpu.SemaphoreType.DMA, pltpu.SemaphoreType.DMA((3,)), pltpu.SemaphoreType.DMA((3,))])(v)


def _ag_devices(name, v):
    r, c = v.shape

    def body(v_ref, out_ref, lsem, ssem, rsem):
        x, y, cc = _pos()
        me = 4 * x + 2 * y + cc
        loc = pltpu.make_async_copy(v_ref, out_ref.at[me], lsem)
        loc.start()
        started = []
        for k, (fx, fy, fc) in enumerate(DEV_FLIPS):
            cp = _remote(v_ref, out_ref.at[me], ssem.at[k], rsem.at[k], (_flip(x, fx), _flip(y, fy), _flip(cc, fc)))
            cp.start()
            started.append(cp)
        for k, (fx, fy, fc) in enumerate(DEV_FLIPS):
            px, py, pc = _flip(x, fx), _flip(y, fy), _flip(cc, fc)
            blk = out_ref.at[4 * px + 2 * py + pc]
            _remote(blk, blk, ssem.at[k], rsem.at[k], (px, py, pc)).wait_recv()
        for cp in started:
            cp.wait_send()
        loc.wait()

    return _call(body, name=name, in_specs=[ANY], out_specs=ANY,
                 out_shape=jax.ShapeDtypeStruct((8, r, c), v.dtype),
                 scratch_shapes=[pltpu.SemaphoreType.DMA, pltpu.SemaphoreType.DMA((7,)), pltpu.SemaphoreType.DMA((7,))])(v)


def _sibling_other_half(name, g2s):
    n = len(g2s)

    def body(*refs):
        g_refs, out_refs, (ssem, rsem) = refs[:n], refs[n:2 * n], refs[2 * n:]
        x, y, cc = _pos()
        cps = [_remote(g_refs[a].at[1 - cc], out_refs[a], ssem.at[a], rsem.at[a], (x, y, 1 - cc)) for a in range(n)]
        for cp in cps:
            cp.start()
        for cp in cps:
            cp.wait()

    return _call(body, name=name, in_specs=[ANY] * n, out_specs=[ANY] * n,
                 out_shape=[jax.ShapeDtypeStruct(g.shape[1:], g.dtype) for g in g2s],
                 scratch_shapes=[pltpu.SemaphoreType.DMA((n,)), pltpu.SemaphoreType.DMA((n,))])(*g2s)


def _scatter_sems(n):
    return [pltpu.SemaphoreType.DMA((n,)), pltpu.SemaphoreType.DMA((3 * n,)), pltpu.SemaphoreType.DMA((3 * n,))]


def _scatter_phase(phase, p_refs, out_refs, sems):
    lsem, ssem, rsem = sems
    x, y, cc = _pos()
    me = 2 * x + y
    peers = [(_flip(x, fx), _flip(y, fy), cc) for fx, fy in CHIP_FLIPS]
    for a in range(len(p_refs)):
        loc = pltpu.make_async_copy(p_refs[a].at[me], out_refs[a].at[me], lsem.at[a])
        sends = [_remote(p_refs[a].at[2 * px + py], out_refs[a].at[me], ssem.at[3 * a + k], rsem.at[3 * a + k], peers[k])
                 for k, (px, py, _) in enumerate(peers)]
        if phase == 0:
            loc.start()
            for cp in sends:
                cp.start()
        else:
            for k, (px, py, _) in enumerate(peers):
                blk = out_refs[a].at[2 * px + py]
                _remote(blk, blk, ssem.at[3 * a + k], rsem.at[3 * a + k], peers[k]).wait_recv()
            loc.wait()
            for cp in sends:
                cp.wait_send()


def _sibling_join(name, bufs):
    n = len(bufs)

    def body(*refs):
        out_refs, (ssem, rsem) = refs[n:2 * n], refs[2 * n:]
        x, y, cc = _pos()
        sib = (x, y, 1 - cc)
        cps = [_remote(out_refs[a].at[cc], out_refs[a].at[cc], ssem.at[a], rsem.at[a], sib) for a in range(n)]
        for cp in cps:
            cp.start()
        for a in range(n):
            theirs = out_refs[a].at[1 - cc]
            _remote(theirs, theirs, ssem.at[a], rsem.at[a], sib).wait_recv()
        for cp in cps:
            cp.wait_send()

    return _call(body, name=name, in_specs=[ANY] * n, out_specs=[ANY] * n,
                 out_shape=[jax.ShapeDtypeStruct(b.shape, b.dtype) for b in bufs],
                 input_output_aliases={a: a for a in range(n)},
                 scratch_shapes=[pltpu.SemaphoreType.DMA((n,)), pltpu.SemaphoreType.DMA((n,))])(*bufs)


def _pad_lanes(a, width=128):
    return jnp.pad(a, ((0, 0), (0, width - a.shape[1])))


def kernel(x, norm_gains, fox_w_in, fox_b_f, hgrn_w_in, hgrn_lb_logits, hgrn_onorm, w_out, final_gain, loss_target, m_norm_gains, m_fox_w_in, m_fox_b_f, m_hgrn_w_in, m_hgrn_lb_logits, m_hgrn_onorm, m_w_out, m_final_gain, v_norm_gains, v_fox_w_in, v_fox_b_f, v_hgrn_w_in, v_hgrn_lb_logits, v_hgrn_onorm, v_w_out, v_final_gain):
    s, d = x.shape[1], x.shape[2]
    wq = w_out.shape[1]
    w = 4 * wq
    hh = w // HEAD_DIM
    fox_cols = fox_w_in.shape[2]
    assert 4 * fox_cols == 4 * w + hh and hgrn_w_in.shape[2] == w
    core = lax.axis_index("c")

    x0 = x[0]
    tgt = loss_target[0]

    def my_half(a):
        return lax.dynamic_index_in_dim(a, core, 0, keepdims=False).astype(BF16)

    (big_fox,) = _ag_weights("ag_weights", [my_half(fox_w_in.reshape(2, d // 2, fox_cols))])
    later_halves = [my_half(hgrn_w_in.reshape(2, d // 2, w)), my_half(w_out)]
    fox_chip = [big_fox[:, j].reshape(d, fox_cols) for j in range(4)]
    def fox_columns(a, b):
        out = []
        for j in range(4):
            lo, hi = max(a, j * fox_cols), min(b, (j + 1) * fox_cols)
            if lo < hi:
                out.append(fox_chip[j][:, lo - j * fox_cols:hi - j * fox_cols])
        return out

    w_main = jnp.concatenate(fox_columns(0, 3 * w) + fox_columns(3 * w + hh, 4 * w + hh), axis=1)
    w_fl = _pad_lanes(jnp.concatenate(fox_columns(3 * w, 3 * w + hh), axis=1))
    onorm_full = _ag_chips("ag_onorm", hgrn_onorm.reshape(wq // 128, 128)).reshape(1, w)

    bf_pad = _pad_lanes(fox_b_f)
    g0, g1 = norm_gains[0:1], norm_gains[1:2]
    gf = final_gain.reshape(1, d)

    h0 = _rms_fwd("rms0_fwd", x0, g0)
    qkv = _mm("fox_qkv", h0, w_main, out_dtype=BF16, n=3 * w)
    gate0 = _mm("fox_gate", h0, w_main, n=w, b_off=3 * w)
    fl = _mm("fox_flogit", h0, w_fl)
    _, c_rep = _fox_gate_fwd("fox_cumsum", fl, bf_pad, w)
    o0, lse_row, big_hgrn, big_out = _attn_fwd("fox_attn_fwd", qkv, c_rep, hh, gather=later_halves)
    w_o = big_out.reshape(2, w, d)
    y0 = _fox_post("fox_post", o0, gate0)
    x1 = _mm("fox_out", y0, w_o[0], res=x0)
    h1 = _rms_fwd("rms1_fwd", x1, g1)
    p1 = _mm("hgrn_in", h1, big_hgrn, b_gathered=True)
    y1, o1, states = _hgrn_fwd("hgrn_fwd", p1, hgrn_lb_logits, onorm_full, hh)
    x2 = _mm("hgrn_out", y1, w_o[1], res=x1)
    dx2, d_gf, loss_tile = _loss_head("loss_head", x2, tgt, gf)

    dy1 = _mm("hgrn_out_dy", dx2, w_o[1], tb=True)
    d_wo1 = _mm("hgrn_out_dw", y1, dx2, ta=True)
    dq1, df1, di1, dgate1, d_lbl, d_onorm = _hgrn_bwd("hgrn_bwd", p1, hgrn_lb_logits, onorm_full, o1, dy1, states, hh)
    dp1 = jnp.concatenate([dq1, df1, di1, dgate1], axis=1)
    dh1 = _mm("hgrn_in_dh", dp1, big_hgrn, tb=True, b_gathered=True)
    g2_hgrn = _mm("hgrn_in_dw", h1, dp1, ta=True, out_gathered=(d // 2, w))
    dx1, d_g1 = _rms_bwd("rms1_bwd", x1, g1, dh1, dx2)
    dy0 = _mm("fox_out_dy", dx1, w_o[0], tb=True)
    d_wo0 = _mm("fox_out_dw", y0, dx1, ta=True)
    do0, dgate0, delta_rep = _fox_pre_bwd("fox_pre_bwd", dy0, o0, gate0)
    delta_row = delta_rep[:, ::HEAD_DIM].T.reshape(hh, 1, s)
    core_arr = core.reshape(1).astype(jnp.int32)
    g2_out = jnp.stack([d_wo0, d_wo1]).reshape(2, 4, wq, d)
    sib_hgrn, sib_out = _sibling_other_half("rs_sibling_early", [g2_hgrn, g2_out])
    pairs_early = [_pair_add("rs_pair_add_hgrn", g2_hgrn, sib_hgrn, core_arr),
                   _pair_add("rs_pair_add_out", g2_out, sib_out, core_arr)]
    dk0, dv0, dc_rep, dq0, rowsum_row, got_hgrn, got_out = _attn_bwd(
        "fox_attn_bwd", qkv, do0, lse_row, delta_row, c_rep, hh, scatter=pairs_early)
    dfl, d_bf = _fox_gate_bwd("fox_cumsum_bwd", _pad_lanes(rowsum_row.reshape(hh, s).T),
                              _pad_lanes(dc_rep[:, ::HEAD_DIM]), fl, bf_pad)
    dp0 = jnp.concatenate([dq0, dk0, dv0, dgate0], axis=1)
    d_wmain = _mm("fox_in_dw", h0, dp0, ta=True)
    d_wfl = _mm("fox_fl_dw", h0, dfl, ta=True)

    def grad_columns(a, b):
        out = []
        for lo, hi, src, shift in ((0, 3 * w, d_wmain, 0), (3 * w, 3 * w + hh, d_wfl, 3 * w),
                                   (3 * w + hh, 4 * w + hh, d_wmain, hh)):
            l2, h2 = max(a, lo), min(b, hi)
            if l2 < h2:
                out.append(src[:, l2 - shift:h2 - shift])
        return out

    g2_fox = jnp.stack([jnp.concatenate(grad_columns(j * fox_cols, (j + 1) * fox_cols), axis=1).reshape(2, d // 2, fox_cols)
                        for j in range(4)], axis=1)
    (sib_fox,) = _sibling_other_half("rs_sibling", [g2_fox])
    dh0, got_fox = _mm("fox_in_dh", dp0, w_main, tb=True, res=_mm("fox_fl_dh", dfl, w_fl, tb=True),
                       scatter=[_pair_add("rs_pair_add_fox", g2_fox, sib_fox, core_arr)])
    grad_x, d_g0 = _rms_bwd("rms0_bwd", x0, g0, dh0, dx1)
    halves_sum = [_sum_slots("rs_sum_" + nm, got, core_arr)
                  for nm, got in zip(("fox", "hgrn", "out"), (got_fox, got_hgrn, got_out))]
    r_fox, r_hgrn, r_out = _sibling_join("rs_join", halves_sum)
    g_fox = r_fox.reshape(d, fox_cols)
    g_hgrn = r_hgrn.reshape(d, w)
    g_out = r_out.reshape(2 * wq, d)

    small = jnp.concatenate([jnp.concatenate([d_g0, d_g1], axis=0).reshape(-1), d_lbl.reshape(-1), d_gf.reshape(-1),
                             d_onorm.reshape(-1), d_bf.reshape(-1)])
    n_small = small.shape[0]
    pad_to = -(-n_small // 1024) * 1024
    small = jnp.pad(small, (0, pad_to - n_small)).reshape(pad_to // 128, 128)
    small = _sum_slots("small_sum", _ag_devices("small_gather", small)).reshape(-1)
    g_norm = small[:2 * d].reshape(2, d)
    g_lbl = small[2 * d:2 * d + 2 * w].reshape(2, w)
    g_gf = small[2 * d + 2 * w:3 * d + 2 * w]
    g_onorm_full = small[3 * d + 2 * w:3 * d + 3 * w]
    g_bf = small[3 * d + 3 * w:3 * d + 3 * w + hh].reshape(1, hh)
    chip = 2 * lax.axis_index("x") + lax.axis_index("y")
    g_onorm = lax.dynamic_slice_in_dim(g_onorm_full, chip * wq, wq).reshape(1, wq)

    loss = lax.psum(loss_tile[0, 0], ("x", "y", "c"))

    def upd(name, wt, g, m, v):
        shp = wt.shape
        two = lambda a: a.reshape(-1, shp[-1])
        dl, nm, nv = _adamw(name, two(wt), two(g), two(m), two(v))
        return g.reshape(shp), dl.reshape(shp), nm.reshape(shp), nv.reshape(shp)

    res = [
        upd("adamw_norm_gains", norm_gains, g_norm, m_norm_gains, v_norm_gains),
        upd("adamw_fox_w_in", fox_w_in, g_fox, m_fox_w_in, v_fox_w_in),
        upd("adamw_fox_b_f", fox_b_f, g_bf, m_fox_b_f, v_fox_b_f),
        upd("adamw_hgrn_w_in", hgrn_w_in, g_hgrn, m_hgrn_w_in, v_hgrn_w_in),
        upd("adamw_lb_logits", hgrn_lb_logits, g_lbl, m_hgrn_lb_logits, v_hgrn_lb_logits),
        upd("adamw_onorm", hgrn_onorm, g_onorm, m_hgrn_onorm, v_hgrn_onorm),
        upd("adamw_w_out", w_out, g_out, m_w_out, v_w_out),
        upd("adamw_final_gain", final_gain.reshape(1, d), g_gf.reshape(1, d), m_final_gain.reshape(1, d),
            v_final_gain.reshape(1, d)),
    ]
    res[-1] = tuple(a.reshape(d) for a in res[-1])
    grads, deltas, new_m, new_v = zip(*res)
    return (loss, grad_x[None], *grads, *deltas, *new_m, *new_v)
```

```python
import functools

import jax
import jax.numpy as jnp
from jax import lax
from jax.experimental import pallas as pl
from jax.experimental.pallas import tpu as pltpu

F32 = jnp.float32
BF16 = jnp.bfloat16
MESH = pl.DeviceIdType.MESH
ANY = pl.BlockSpec(memory_space=pl.ANY)

EPS = 1e-6
HEAD_DIM = 128
HGRN_CHUNK = 64
HGRN_SUB = 32
EXP_CLAMP = 80.0
ATT_BLOCK = 512
HGRN_BLOCK = 512
GATE_BLOCK = 512
ROW_BLOCK = 256
MM_TM, MM_TN, MM_TK = 1024, 1024, 2048
VMEM_LIMIT_V7X = 56 * 1024 * 1024

ADAM_LR, ADAM_B1, ADAM_B2, ADAM_EPS, ADAM_WD, ADAM_STEP = 0.001, 0.9, 0.999, 1e-08, 0.01, 10

NT = (((1,), (1,)), ((), ()))
TN = (((0,), (0,)), ((), ()))
NN = (((1,), (0,)), ((), ()))


def _call(body, **kw):
    return pl.pallas_call(body, **kw)


def _cp(dims=None):
    kw = dict(vmem_limit_bytes=VMEM_LIMIT_V7X)
    if dims is not None:
        kw["dimension_semantics"] = dims
    return pltpu.CompilerParams(**kw)


def _sigmoid(x):
    return 1.0 / (1.0 + jnp.exp(-x))


def _dot(a, b, dn=NN):
    return lax.dot_general(a.astype(BF16), b.astype(BF16), dn, preferred_element_type=F32)


def _split3(x):
    hi = x.astype(BF16)
    r1 = x - hi.astype(F32)
    mid = r1.astype(BF16)
    lo = (r1 - mid.astype(F32)).astype(BF16)
    return hi, mid, lo


def _dot_exact(m01, x, right=False):
    hi, mid, lo = _split3(x)
    if right:
        dot = lambda p: lax.dot_general(p, m01, NN, preferred_element_type=F32)
    else:
        dot = lambda p: lax.dot_general(m01, p, NN, preferred_element_type=F32)
    return dot(hi) + dot(mid) + dot(lo)


def _row_block(rows, cap):
    if rows <= cap:
        return rows
    best = None
    for t in range(16, cap + 1, 16):
        if rows % t == 0:
            best = t
    assert best is not None, rows
    return best


def _mm(name, a, b, *, ta=False, tb=False, out_dtype=F32, res=None, n=None, b_off=0, b_gathered=False,
        out_gathered=None, scatter=(), a_stacked=False, b_stacked=False):
    if a_stacked:
        assert not ta
        m, k = a.shape[1], a.shape[0] * a.shape[2]
    else:
        m, k = (a.shape[1], a.shape[0]) if ta else a.shape
    if b_gathered:
        rh, cw = b.shape[2], b.shape[3]
        n_full = 2 * rh if tb else 4 * cw
    elif b_stacked:
        assert not tb
        n_full = b.shape[0] * b.shape[2]
    else:
        n_full = b.shape[0] if tb else b.shape[1]
    n = n_full if n is None else n
    tm, tn, tk = min(MM_TM, m), min(MM_TN, n), min(MM_TK, k)
    b_both = b_gathered and not tb and tk == 2 * rh
    if b_gathered and not b_both:
        tn, tk = (min(tn, rh), min(tk, cw)) if tb else (min(tn, cw), min(tk, rh))
    if b_both:
        tn = min(tn, cw)
    if out_gathered:
        tm, tn = min(tm, out_gathered[0]), min(tn, out_gathered[1])
    if a_stacked:
        tk = min(tk, a.shape[2])
    if b_stacked:
        tn = min(tn, b.shape[2])
    assert m % tm == 0 and n % tn == 0 and k % tk == 0 and b_off % tn == 0
    nk = k // tk
    jo = b_off // tn
    if a_stacked:
        pa = a.shape[2] // tk
        a_spec = pl.BlockSpec((1, tm, tk), lambda i, j, kk: (kk // pa, i, kk % pa))
    elif ta:
        a_spec = pl.BlockSpec((tk, tm), lambda i, j, kk: (kk, i))
    else:
        a_spec = pl.BlockSpec((tm, tk), lambda i, j, kk: (i, kk))
    if b_gathered and tb:
        pr, pc = rh // tn, cw // tk
        b_spec = pl.BlockSpec((1, 1, tn, tk), lambda i, j, kk: (j // pr, kk // pc, j % pr, kk % pc))
    elif b_both:
        pc = cw // tn
        b_spec = pl.BlockSpec((2, 1, rh, tn), lambda i, j, kk: (0, j // pc, 0, j % pc))
    elif b_gathered:
        pr, pc = rh // tk, cw // tn
        b_spec = pl.BlockSpec((1, 1, tk, tn), lambda i, j, kk: (kk // pr, j // pc, kk % pr, j % pc))
    elif b_stacked:
        pb = b.shape[2] // tn
        b_spec = pl.BlockSpec((1, tk, tn), lambda i, j, kk: (j // pb, kk, j % pb))
    elif tb:
        b_spec = pl.BlockSpec((tn, tk), lambda i, j, kk: (j + jo, kk))
    else:
        b_spec = pl.BlockSpec((tk, tn), lambda i, j, kk: (kk, j + jo))
    o_spec = pl.BlockSpec((tm, tn), lambda i, j, kk: (i, j))
    if out_gathered:
        assert m == 2 * out_gathered[0] and n == 4 * out_gathered[1] and res is None
        qr, qc = out_gathered[0] // tm, out_gathered[1] // tn
        o_spec = pl.BlockSpec((1, 1, tm, tn), lambda i, j, kk: (i // qr, j // qc, i % qr, j % qc))
    dn = (((0 if ta else 1,), (1 if tb else 0,)), ((), ()))
    has_res = res is not None
    ns = len(scatter)
    grid = (m // tm, n // tn, nk)

    def body(*refs):
        a_ref, b_ref = refs[:2]
        r_ref = refs[2] if has_res else None
        base = 2 + int(has_res)
        p_refs = refs[base:base + ns]
        o_ref = refs[base + ns]
        got_refs = refs[base + ns + 1:base + 2 * ns + 1]
        acc = refs[base + 2 * ns + 1]
        sems = refs[base + 2 * ns + 2:]
        kk = pl.program_id(2)
        if ns:
            first = jnp.logical_and(jnp.logical_and(pl.program_id(0) == 0, pl.program_id(1) == 0), kk == 0)
            last = jnp.logical_and(jnp.logical_and(pl.program_id(0) == grid[0] - 1, pl.program_id(1) == grid[1] - 1),
                                   kk == nk - 1)

            @pl.when(first)
            def _():
                _scatter_phase(0, p_refs, got_refs, sems)

        av = a_ref[0] if a_stacked else a_ref[...]
        if b_both:
            p = (lax.dot_general(av[:, :rh].astype(BF16), b_ref[0, 0].astype(BF16), dn, preferred_element_type=F32)
                 + lax.dot_general(av[:, rh:].astype(BF16), b_ref[1, 0].astype(BF16), dn, preferred_element_type=F32))
        else:
            bv = b_ref[0, 0] if b_gathered else (b_ref[0] if b_stacked else b_ref[...])
            p = lax.dot_general(av.astype(BF16), bv.astype(BF16), dn, preferred_element_type=F32)

        def finish(total):
            if r_ref is not None:
                total = total + r_ref[...]
            if out_gathered:
                o_ref[0, 0] = total.astype(out_dtype)
            else:
                o_ref[...] = total.astype(out_dtype)

        if nk == 1:
            finish(p)
        else:
            @pl.when(kk == 0)
            def _():
                acc[...] = p

            @pl.when(jnp.logical_and(kk > 0, kk < nk - 1))
            def _():
                acc[...] += p

            @pl.when(kk == nk - 1)
            def _():
                finish(acc[...] + p)

        if ns:
            @pl.when(last)
            def _():
                _scatter_phase(1, p_refs, got_refs, sems)

    ins = [a, b] + ([res] if has_res else []) + list(scatter)
    in_specs = [a_spec, b_spec] + ([o_spec] if has_res else []) + [ANY] * ns
    o_shape = jax.ShapeDtypeStruct((2, 4) + tuple(out_gathered) if out_gathered else (m, n), out_dtype)
    out = _call(
        body, name=name, grid=grid, in_specs=in_specs, out_specs=[o_spec] + [ANY] * ns,
        out_shape=[o_shape] + [jax.ShapeDtypeStruct(p.shape, p.dtype) for p in scatter],
        scratch_shapes=[pltpu.VMEM((tm, tn) if nk > 1 else (8, 128), F32)] + (_scatter_sems(ns) if ns else []),
        compiler_params=_cp(("arbitrary",) * 3 if ns else ("parallel", "parallel", "arbitrary")),
    )(*ins)
    return out if ns else out[0]


def _rms_fwd(name, x, g):
    s, d = x.shape
    tm = min(ROW_BLOCK, s)

    def body(x_ref, g_ref, h_ref):
        xv = x_ref[...]
        r = lax.rsqrt(jnp.mean(xv * xv, axis=-1, keepdims=True) + EPS)
        h_ref[...] = (xv * r * g_ref[...]).astype(BF16)

    row = pl.BlockSpec((tm, d), lambda i: (i, 0))
    vec = pl.BlockSpec((1, d), lambda i: (0, 0))
    return _call(body, name=name, grid=(s // tm,), in_specs=[row, vec], out_specs=row,
                 out_shape=jax.ShapeDtypeStruct((s, d), BF16), compiler_params=_cp(("parallel",)))(x, g)


def _rms_bwd(name, x, g, dh, dres):
    s, d = x.shape
    tm = min(ROW_BLOCK, s)

    def body(x_ref, g_ref, dh_ref, dres_ref, dx_ref, dg_ref):
        @pl.when(pl.program_id(0) == 0)
        def _():
            dg_ref[...] = jnp.zeros_like(dg_ref)

        xv = x_ref[...]
        r = lax.rsqrt(jnp.mean(xv * xv, axis=-1, keepdims=True) + EPS)
        xn = xv * r
        dhv = dh_ref[...]
        dxn = dhv * g_ref[...]
        dx_ref[...] = dres_ref[...] + r * (dxn - xn * jnp.mean(dxn * xn, axis=-1, keepdims=True))
        dg_ref[...] += jnp.sum(dhv * xn, axis=0, keepdims=True)

    row = pl.BlockSpec((tm, d), lambda i: (i, 0))
    vec = pl.BlockSpec((1, d), lambda i: (0, 0))
    return _call(body, name=name, grid=(s // tm,), in_specs=[row, vec, row, row], out_specs=[row, vec],
                 out_shape=[jax.ShapeDtypeStruct((s, d), F32), jax.ShapeDtypeStruct((1, d), F32)],
                 compiler_params=_cp(("arbitrary",)))(x, g, dh, dres)


def _loss_head(name, x, tgt, g):
    s, d = x.shape
    tm = min(ROW_BLOCK, s)
    nb = s // tm

    def body(x_ref, t_ref, g_ref, dx_ref, dg_ref, loss_ref, lacc):
        i = pl.program_id(0)

        @pl.when(i == 0)
        def _():
            dg_ref[...] = jnp.zeros_like(dg_ref)
            lacc[...] = jnp.zeros_like(lacc)

        xv = x_ref[...]
        gv = g_ref[...]
        r = lax.rsqrt(jnp.mean(xv * xv, axis=-1, keepdims=True) + EPS)
        xn = xv * r
        err = xn * gv - t_ref[...]
        lacc[...] += jnp.sum(err * err, axis=0, keepdims=True)
        dout = err * (1.0 / d)
        dg_ref[...] += jnp.sum(dout * xn, axis=0, keepdims=True)
        dxn = dout * gv
        dx_ref[...] = r * (dxn - xn * jnp.mean(dxn * xn, axis=-1, keepdims=True))

        @pl.when(i == nb - 1)
        def _():
            total = jnp.sum(lacc[...], axis=1, keepdims=True) * (0.5 / d)
            loss_ref[...] = jnp.broadcast_to(total, loss_ref.shape)

    row = pl.BlockSpec((tm, d), lambda i: (i, 0))
    vec = pl.BlockSpec((1, d), lambda i: (0, 0))
    one = pl.BlockSpec((1, 128), lambda i: (0, 0))
    return _call(body, name=name, grid=(nb,), in_specs=[row, row, vec], out_specs=[row, vec, one],
                 out_shape=[jax.ShapeDtypeStruct((s, d), F32), jax.ShapeDtypeStruct((1, d), F32),
                            jax.ShapeDtypeStruct((1, 128), F32)],
                 scratch_shapes=[pltpu.VMEM((1, d), F32)], compiler_params=_cp(("arbitrary",)))(x, tgt, g)


def _fox_gate_fwd(name, fl, bf, w):
    s = fl.shape[0]
    tb = min(GATE_BLOCK, s)

    def body(fl_ref, bf_ref, c_ref, crep_ref, carry):
        @pl.when(pl.program_id(0) == 0)
        def _():
            carry[...] = jnp.zeros_like(carry)

        z = fl_ref[...] + bf_ref[...]
        lf = jnp.minimum(z, 0.0) - jnp.log(1.0 + jnp.exp(-jnp.abs(z)))
        rows = lax.broadcasted_iota(jnp.int32, (tb, tb), 0)
        cols = lax.broadcasted_iota(jnp.int32, (tb, tb), 1)
        tri = (rows >= cols).astype(BF16)
        cs = _dot_exact(tri, lf) + carry[...]
        c_ref[...] = cs
        carry[...] = c_ref[tb - 1:tb, :]
        sel_r = lax.broadcasted_iota(jnp.int32, (128, w), 0)
        sel_c = lax.broadcasted_iota(jnp.int32, (128, w), 1)
        sel = (sel_r == sel_c // HEAD_DIM).astype(BF16)
        crep_ref[...] = _dot_exact(sel, cs, right=True)

    blk = pl.BlockSpec((tb, 128), lambda i: (i, 0))
    return _call(body, name=name, grid=(s // tb,),
                 in_specs=[blk, pl.BlockSpec((1, 128), lambda i: (0, 0))],
                 out_specs=[blk, pl.BlockSpec((tb, w), lambda i: (i, 0))],
                 out_shape=[jax.ShapeDtypeStruct((s, 128), F32), jax.ShapeDtypeStruct((s, w), F32)],
                 scratch_shapes=[pltpu.VMEM((1, 128), F32)], compiler_params=_cp(("arbitrary",)))(fl, bf)


def _fox_gate_bwd(name, drow, dcol, fl, bf):
    s = fl.shape[0]
    tb = min(GATE_BLOCK, s)
    nb = s // tb

    def body(dr_ref, dc_ref, fl_ref, bf_ref, dfl_ref, dbf_ref, carry, tmp):
        @pl.when(pl.program_id(0) == 0)
        def _():
            carry[...] = jnp.zeros_like(carry)
            dbf_ref[...] = jnp.zeros_like(dbf_ref)

        rows = lax.broadcasted_iota(jnp.int32, (tb, tb), 0)
        cols = lax.broadcasted_iota(jnp.int32, (tb, tb), 1)
        triu = (rows <= cols).astype(BF16)
        dlf = _dot_exact(triu, dr_ref[...] + dc_ref[...]) + carry[...]
        tmp[...] = dlf
        carry[...] = tmp[0:1, :]
        z = fl_ref[...] + bf_ref[...]
        dfl = dlf * (1.0 / (1.0 + jnp.exp(z)))
        dfl_ref[...] = dfl
        dbf_ref[...] += jnp.sum(dfl, axis=0, keepdims=True)

    blk = pl.BlockSpec((tb, 128), lambda i: (nb - 1 - i, 0))
    vec = pl.BlockSpec((1, 128), lambda i: (0, 0))
    return _call(body, name=name, grid=(nb,), in_specs=[blk, blk, blk, vec], out_specs=[blk, vec],
                 out_shape=[jax.ShapeDtypeStruct((s, 128), F32), jax.ShapeDtypeStruct((1, 128), F32)],
                 scratch_shapes=[pltpu.VMEM((1, 128), F32), pltpu.VMEM((tb, 128), F32)],
                 compiler_params=_cp(("arbitrary",)))(drow, dcol, fl, bf)


def _attn_fwd(name, qkv, c_rep, h_count, gather=()):
    s = qkv.shape[0]
    w = h_count * HEAD_DIM
    t = min(ATT_BLOCK, s)
    scale = HEAD_DIM ** -0.5
    hp = 2 if h_count % 2 == 0 else 1
    wb = hp * HEAD_DIM
    ng = len(gather)
    nh, nq = h_count // hp, s // t

    def body(*refs):
        q_ref, k_ref, v_ref, c_ref = refs[:4]
        h_refs = refs[4:4 + ng]
        o_ref, lse_ref = refs[4 + ng:6 + ng]
        big_refs = refs[6 + ng:6 + 2 * ng]
        m_s, l_s, acc_s = refs[6 + 2 * ng:9 + 2 * ng]
        sems = refs[9 + 2 * ng:]
        hs = pl.program_id(0)
        qi = pl.program_id(1)
        if ng:
            @pl.when(jnp.logical_and(hs == 0, qi == 0))
            def _():
                _ag_phase(0, h_refs, big_refs, sems)

            @pl.when(jnp.logical_and(hs == nh // 2, qi == 0))
            def _():
                _ag_phase(1, h_refs, big_refs, sems)

        m_s[...] = jnp.full(m_s.shape, -jnp.inf, F32)
        l_s[...] = jnp.zeros_like(l_s)
        acc_s[...] = jnp.zeros_like(acc_s)

        def step(kb, masked):
            off = pl.multiple_of(kb * t, t)
            for a in range(hp):
                cols_a = slice(a * HEAD_DIM, (a + 1) * HEAD_DIM)
                kk = k_ref[pl.ds(off, t), cols_a]
                vv = v_ref[pl.ds(off, t), cols_a]
                cc = jnp.tile(c_ref[pl.ds(off, t), cols_a], (1, t // HEAD_DIM))
                st = lax.dot_general(kk, q_ref[:, cols_a], NT, preferred_element_type=F32) * scale - cc
                if masked:
                    rows = lax.broadcasted_iota(jnp.int32, (t, t), 0)
                    cols = lax.broadcasted_iota(jnp.int32, (t, t), 1)
                    st = jnp.where(cols >= rows, st, -jnp.inf)
                m_prev = m_s[a]
                m_new = jnp.maximum(m_prev, jnp.max(st, axis=0, keepdims=True))
                pt = jnp.exp(st - m_new)
                alpha = jnp.exp(m_prev - m_new)
                l_s[a] = alpha * l_s[a] + jnp.sum(pt, axis=0, keepdims=True)
                acc_s[a] = alpha * acc_s[a] + lax.dot_general(vv, pt.astype(BF16), TN, preferred_element_type=F32)
                m_s[a] = m_new

        def loop_body(kb, carry):
            step(kb, False)
            return carry

        lax.fori_loop(0, qi, loop_body, 0)
        step(qi, True)
        for a in range(hp):
            l = l_s[a]
            o_ref[:, a * HEAD_DIM:(a + 1) * HEAD_DIM] = (acc_s[a] / l).T
            lse_ref[a] = m_s[a] + jnp.log(l)

        if ng:
            @pl.when(jnp.logical_and(hs == nh - 1, qi == nq - 1))
            def _():
                _ag_phase(2, h_refs, big_refs, sems)

    blk = lambda off: pl.BlockSpec((t, wb), lambda h, i: (i, off + h))
    whole = lambda off: pl.BlockSpec((s, wb), lambda h, i: (0, off + h))
    rowv = pl.BlockSpec((hp, 1, t), lambda h, i: (h, 0, i))
    return _call(
        body, name=name, grid=(nh, nq),
        in_specs=[blk(0), whole(nh), whole(2 * nh), whole(0)] + [ANY] * ng,
        out_specs=[blk(0), rowv] + [ANY] * ng,
        out_shape=[jax.ShapeDtypeStruct((s, w), F32), jax.ShapeDtypeStruct((h_count, 1, s), F32)] + _ag_out_shapes(gather),
        scratch_shapes=[pltpu.VMEM((hp, 1, t), F32), pltpu.VMEM((hp, 1, t), F32), pltpu.VMEM((hp, HEAD_DIM, t), F32)]
        + (_ag_sems(ng) if ng else []),
        compiler_params=_cp(("arbitrary", "arbitrary") if ng else ("parallel", "arbitrary")),
    )(qkv, qkv, qkv, c_rep, *gather)


def _attn_bwd(name, qkv, do, lse_row, delta_row, c_rep, h_count, scatter=()):
    s = qkv.shape[0]
    w = h_count * HEAD_DIM
    t = min(ATT_BLOCK, s)
    nq = s // t
    scale = HEAD_DIM ** -0.5
    ns = len(scatter)

    def body(*refs):
        k_ref, v_ref, q_ref, do_ref, lse_ref, dl_ref, c_ref = refs[:7]
        p_refs = refs[7:7 + ns]
        dk_ref, dv_ref, dc_ref, dq_ref, rs_ref = refs[7 + ns:12 + ns]
        got_refs = refs[12 + ns:12 + 2 * ns]
        dk_s, dv_s, dc_s, dq_s, rs_s = refs[12 + 2 * ns:17 + 2 * ns]
        sems = refs[17 + 2 * ns:]
        kj = pl.program_id(1)
        if ns:
            @pl.when(jnp.logical_and(pl.program_id(0) == 0, kj == 0))
            def _():
                _scatter_phase(0, p_refs, got_refs, sems)

        kk = k_ref[...]
        vv = v_ref[...]
        ccol = jnp.tile(c_ref[...], (1, t // HEAD_DIM))
        dk_s[...] = jnp.zeros_like(dk_s)
        dv_s[...] = jnp.zeros_like(dv_s)
        dc_s[...] = jnp.zeros_like(dc_s)

        @pl.when(kj == 0)
        def _():
            dq_s[...] = jnp.zeros_like(dq_s)
            rs_s[...] = jnp.zeros_like(rs_s)

        def step(qb, masked):
            off = pl.multiple_of(qb * t, t)
            q = q_ref[pl.ds(off, t), :]
            dov = do_ref[pl.ds(off, t), :]
            st = lax.dot_general(kk, q, NT, preferred_element_type=F32) * scale - ccol
            pt = jnp.exp(st - lse_ref[0, :, pl.ds(off, t)])
            if masked:
                rows = lax.broadcasted_iota(jnp.int32, (t, t), 0)
                cols = lax.broadcasted_iota(jnp.int32, (t, t), 1)
                pt = jnp.where(cols >= rows, pt, 0.0)
            dv_s[...] += lax.dot_general(pt.astype(BF16), dov, NN, preferred_element_type=F32)
            dpt = lax.dot_general(vv, dov, NT, preferred_element_type=F32)
            dst = pt * (dpt - dl_ref[0, :, pl.ds(off, t)])
            dstb = dst.astype(BF16)
            dk_s[...] += lax.dot_general(dstb, q, NN, preferred_element_type=F32)
            dc_s[...] += jnp.sum(dst, axis=1, keepdims=True)
            dq_s[:, pl.ds(off, t)] += lax.dot_general(kk, dstb, TN, preferred_element_type=F32)
            rs_s[:, pl.ds(off, t)] += jnp.sum(dst, axis=0, keepdims=True)

        step(kj, True)

        def loop_body(qb, carry):
            step(qb, False)
            return carry

        lax.fori_loop(kj + 1, nq, loop_body, 0)
        dk_ref[...] = (dk_s[...] * scale).astype(BF16)
        dv_ref[...] = dv_s[...].astype(BF16)
        dc_ref[...] = jnp.broadcast_to(-dc_s[...], dc_ref.shape)

        @pl.when(kj == nq - 1)
        def _():
            for b in range(nq):
                dq_ref[b * t:(b + 1) * t, :] = (dq_s[:, b * t:(b + 1) * t] * scale).T.astype(BF16)
            rs_ref[0] = rs_s[...]

        if ns:
            @pl.when(jnp.logical_and(pl.program_id(0) == h_count - 1, kj == nq - 1))
            def _():
                _scatter_phase(1, p_refs, got_refs, sems)

    hh = h_count
    blk = lambda off: pl.BlockSpec((t, HEAD_DIM), lambda h, j: (j, off + h))
    whole = pl.BlockSpec((s, HEAD_DIM), lambda h, j: (0, h))
    rowv = pl.BlockSpec((1, 1, s), lambda h, j: (h, 0, 0))
    return _call(
        body, name=name, grid=(hh, nq),
        in_specs=[blk(hh), blk(2 * hh), whole, whole, rowv, rowv, blk(0)] + [ANY] * ns,
        out_specs=[blk(0), blk(0), blk(0), whole, rowv] + [ANY] * ns,
        out_shape=[jax.ShapeDtypeStruct((s, w), BF16), jax.ShapeDtypeStruct((s, w), BF16),
                   jax.ShapeDtypeStruct((s, w), F32), jax.ShapeDtypeStruct((s, w), BF16),
                   jax.ShapeDtypeStruct((hh, 1, s), F32)] + [jax.ShapeDtypeStruct(p.shape, p.dtype) for p in scatter],
        scratch_shapes=[pltpu.VMEM((t, HEAD_DIM), F32), pltpu.VMEM((t, HEAD_DIM), F32), pltpu.VMEM((t, 1), F32),
                        pltpu.VMEM((HEAD_DIM, s), F32), pltpu.VMEM((1, s), F32)] + (_scatter_sems(ns) if ns else []),
        compiler_params=_cp(("arbitrary", "arbitrary") if ns else ("parallel", "arbitrary")),
    )(qkv, qkv, qkv, do, lse_row, delta_row, c_rep, *scatter)


def _fox_post(name, o, gate):
    s, w = o.shape
    tm = min(ROW_BLOCK, s)

    def body(o_ref, g_ref, y_ref):
        g = g_ref[...]
        y_ref[...] = (o_ref[...] * (g * _sigmoid(g))).astype(BF16)

    row = pl.BlockSpec((tm, w), lambda i: (i, 0))
    return _call(body, name=name, grid=(s // tm,), in_specs=[row, row], out_specs=row,
                 out_shape=jax.ShapeDtypeStruct((s, w), BF16), compiler_params=_cp(("parallel",)))(o, gate)


def _fox_pre_bwd(name, dy, o, gate, exchange=()):
    s, w = o.shape
    tm = min(ROW_BLOCK, s)
    ne = len(exchange)
    nb = s // tm

    def body(*refs):
        dy_ref, o_ref, g_ref = refs[:3]
        x_refs = refs[3:3 + ne]
        do_ref, dg_ref, dl_ref = refs[3 + ne:6 + ne]
        got_refs = refs[6 + ne:6 + 2 * ne]
        sems = refs[6 + 2 * ne:]
        if ne:
            @pl.when(pl.program_id(0) == 0)
            def _():
                _sibling_phase(0, x_refs, got_refs, sems)

            @pl.when(pl.program_id(0) == nb - 1)
            def _():
                _sibling_phase(1, x_refs, got_refs, sems)

        g = g_ref[...]
        sg = _sigmoid(g)
        dyv = dy_ref[...]
        ov = o_ref[...]
        dov = dyv * (g * sg)
        do_ref[...] = dov.astype(BF16)
        dg_ref[...] = (dyv * ov * (sg * (1.0 + g * (1.0 - sg)))).astype(BF16)
        prod = dov * ov
        for h in range(w // HEAD_DIM):
            sl = slice(h * HEAD_DIM, (h + 1) * HEAD_DIM)
            dl_ref[:, sl] = jnp.broadcast_to(jnp.sum(prod[:, sl], axis=1, keepdims=True), (tm, HEAD_DIM))

    row = pl.BlockSpec((tm, w), lambda i: (i, 0))
    return _call(body, name=name, grid=(nb,), in_specs=[row, row, row] + [ANY] * ne, out_specs=[row, row, row] + [ANY] * ne,
                 out_shape=[jax.ShapeDtypeStruct((s, w), BF16), jax.ShapeDtypeStruct((s, w), BF16),
                            jax.ShapeDtypeStruct((s, w), F32)] + _sibling_out_shapes(exchange),
                 scratch_shapes=_sibling_sems(ne) if ne else [],
                 compiler_params=_cp(("arbitrary",) if ne else ("parallel",)))(dy, o, gate, *exchange)


def _hgrn_chunk_terms(q_c, k_c, b_c, b_s, base):
    nsub = HGRN_CHUNK // HGRN_SUB
    refs = [jnp.zeros((1, HEAD_DIM), F32)]
    for i in range(1, nsub):
        r0 = base + i * HGRN_SUB - 1
        refs.append(b_s[r0:r0 + 1, :])
    rfull = jnp.concatenate([jnp.broadcast_to(r, (HGRN_SUB, HEAD_DIM)) for r in refs], axis=0)
    eq = jnp.exp(b_c - rfull)
    qe = q_c * eq
    es = [jnp.exp(jnp.minimum(r - b_c, EXP_CLAMP)) for r in refs]
    kes = [(k_c * e).astype(BF16) for e in es]
    return eq, qe, es, kes


def _chunk_cumsum(x, reverse=False):
    cc = HGRN_CHUNK
    nc = x.shape[0] // cc
    rows = lax.broadcasted_iota(jnp.int32, (cc, cc), 0)
    cols = lax.broadcasted_iota(jnp.int32, (cc, cc), 1)
    tri = ((rows <= cols) if reverse else (rows >= cols)).astype(BF16)
    wide = jnp.concatenate([x[n * cc:(n + 1) * cc] for n in range(nc)], axis=1)
    res = _dot_exact(tri, wide)
    return jnp.concatenate([res[:, n * HEAD_DIM:(n + 1) * HEAD_DIM] for n in range(nc)], axis=0)


def _hgrn_block_pre(q_ref, f_ref, lbl_ref, b_s):
    lb = _sigmoid(lbl_ref[1:2, :] - lbl_ref[0:1, :])
    qr = q_ref[...]
    sq = _sigmoid(qr)
    q = qr * sq
    fz = f_ref[...]
    sg = _sigmoid(fz)
    f = lb + (1.0 - lb) * sg
    g = jnp.log(f)
    k = (1.0 - lb) * (1.0 / (1.0 + jnp.exp(fz)))
    b = _chunk_cumsum(g)
    b_s[...] = b
    return lb, qr, sq, q, sg, f, k, b


def _hgrn_fwd(name, p1, lbl, onorm, h_count):
    s = p1.shape[0]
    w = h_count * HEAD_DIM
    t = min(HGRN_BLOCK, s)
    nc = t // HGRN_CHUNK
    nsub = HGRN_CHUNK // HGRN_SUB
    cc = HGRN_CHUNK

    def body(q_ref, f_ref, i_ref, g_ref, lbl_ref, on_ref, y_ref, o_ref, st_ref, state_s, b_s):
        @pl.when(pl.program_id(1) == 0)
        def _():
            state_s[...] = jnp.zeros_like(state_s)

        lb, qr, sq, q, sg, f, k, b = _hgrn_block_pre(q_ref, f_ref, lbl_ref, b_s)
        v = i_ref[...]
        r64 = lax.broadcasted_iota(jnp.int32, (cc, cc), 0)
        c64 = lax.broadcasted_iota(jnp.int32, (cc, cc), 1)
        for n in range(nc):
            sl = slice(n * cc, (n + 1) * cc)
            q_c, k_c, v_c, b_c = q[sl], k[sl], v[sl], b[sl]
            bl = b_s[n * cc + cc - 1:n * cc + cc, :]
            eq, qe, es, kes = _hgrn_chunk_terms(q_c, k_c, b_c, b_s, n * cc)
            qeb = qe.astype(BF16)
            a = jnp.concatenate(
                [lax.dot_general(qeb[i * HGRN_SUB:(i + 1) * HGRN_SUB], kes[i], NT, preferred_element_type=F32)
                 for i in range(nsub)], axis=0)
            a = jnp.where(r64 >= c64, a, 0.0)
            st = state_s[...]
            st_ref[0, n] = st
            inter = _dot(q_c * jnp.exp(b_c), st, NT)
            intra = _dot(a, v_c)
            o_ref[sl, :] = inter + intra
            kb = k_c * jnp.exp(bl - b_c)
            state_s[...] = st * jnp.exp(bl) + _dot(v_c, kb, TN)
        o = o_ref[...]
        rr = lax.rsqrt(jnp.mean(o * o, axis=-1, keepdims=True) + EPS)
        gate = g_ref[...]
        y_ref[...] = ((o * rr) * on_ref[...] * (gate * _sigmoid(gate))).astype(BF16)

    hh = h_count
    blk = lambda off: pl.BlockSpec((t, HEAD_DIM), lambda h, i: (i, off + h))
    return _call(
        body, name=name, grid=(hh, s // t),
        in_specs=[blk(0), blk(hh), blk(2 * hh), blk(3 * hh),
                  pl.BlockSpec((2, HEAD_DIM), lambda h, i: (0, h)), pl.BlockSpec((1, HEAD_DIM), lambda h, i: (0, h))],
        out_specs=[blk(0), blk(0), pl.BlockSpec((1, nc, HEAD_DIM, HEAD_DIM), lambda h, i: (h, i, 0, 0))],
        out_shape=[jax.ShapeDtypeStruct((s, w), BF16), jax.ShapeDtypeStruct((s, w), F32),
                   jax.ShapeDtypeStruct((hh, s // cc, HEAD_DIM, HEAD_DIM), F32)],
        scratch_shapes=[pltpu.VMEM((HEAD_DIM, HEAD_DIM), F32), pltpu.VMEM((t, HEAD_DIM), F32)],
        compiler_params=_cp(("parallel", "arbitrary")),
    )(p1, p1, p1, p1, lbl, onorm)


def _hgrn_bwd(name, p1, lbl, onorm, o, dy, states, h_count):
    s = p1.shape[0]
    w = h_count * HEAD_DIM
    t = min(HGRN_BLOCK, s)
    nb = s // t
    nc = t // HGRN_CHUNK
    nsub = HGRN_CHUNK // HGRN_SUB
    cc = HGRN_CHUNK

    def body(q_ref, f_ref, i_ref, g_ref, lbl_ref, on_ref, o_ref, dy_ref, st_ref,
             dp_ref, dlog_ref, don_ref,
             dst_s, b_s, do_s, dqs, dks, dbs, exs):
        @pl.when(pl.program_id(1) == 0)
        def _():
            dst_s[...] = jnp.zeros_like(dst_s)
            dlog_ref[...] = jnp.zeros_like(dlog_ref)
            don_ref[...] = jnp.zeros_like(don_ref)

        lb, qr, sq, q, sg, f, k, b = _hgrn_block_pre(q_ref, f_ref, lbl_ref, b_s)
        v = i_ref[...]

        ov = o_ref[...]
        rr = lax.rsqrt(jnp.mean(ov * ov, axis=-1, keepdims=True) + EPS)
        on = ov * rr
        gate = g_ref[...]
        sgt = _sigmoid(gate)
        silu = gate * sgt
        dyv = dy_ref[...]
        gain = on_ref[...]
        dp_ref[3] = (dyv * on * gain * (sgt * (1.0 + gate * (1.0 - sgt)))).astype(BF16)
        don_ref[...] += jnp.sum(dyv * on * silu, axis=0, keepdims=True)
        d_on = dyv * gain * silu
        do_s[...] = rr * (d_on - on * jnp.mean(d_on * on, axis=-1, keepdims=True))

        r64 = lax.broadcasted_iota(jnp.int32, (cc, cc), 0)
        c64 = lax.broadcasted_iota(jnp.int32, (cc, cc), 1)
        upper = r64 <= c64
        for n in reversed(range(nc)):
            sl = slice(n * cc, (n + 1) * cc)
            q_c, k_c, v_c, b_c = q[sl], k[sl], v[sl], b[sl]
            do_c = do_s[sl, :]
            bl = b_s[n * cc + cc - 1:n * cc + cc, :]
            ebl = jnp.exp(bl)
            eq, qe, es, kes = _hgrn_chunk_terms(q_c, k_c, b_c, b_s, n * cc)
            qeb = qe.astype(BF16)
            dob = do_c.astype(BF16)
            vb = v_c.astype(BF16)
            st = st_ref[0, n]
            dstn = dst_s[...]
            qb_ = q_c * jnp.exp(b_c)
            kb_ = k_c * jnp.exp(bl - b_c)
            at = jnp.zeros((cc, cc), F32)
            for i in range(nsub):
                blk_i = (c64 // HGRN_SUB) == i
                at = at + jnp.where(blk_i, lax.dot_general(kes[i], qeb, NT, preferred_element_type=F32), 0.0)
            at = jnp.where(upper, at, 0.0)
            dv = _dot(at, dob) + _dot(kb_, dstn, NT)
            dqb = _dot(dob, st)
            dkb = _dot(vb, dstn)
            da = jnp.where(r64 >= c64, lax.dot_general(dob, vb, NT, preferred_element_type=F32), 0.0)
            dat = jnp.where(upper, lax.dot_general(vb, dob, NT, preferred_element_type=F32), 0.0)
            dab = da.astype(BF16)
            dq_raw = jnp.concatenate(
                [lax.dot_general(dab[i * HGRN_SUB:(i + 1) * HGRN_SUB], kes[i], NN, preferred_element_type=F32)
                 for i in range(nsub)], axis=0)
            db = qb_.astype(BF16).astype(F32) * dqb + qeb.astype(F32) * dq_raw
            dkbk = dkb * kb_.astype(BF16).astype(F32)
            db = db - dkbk
            dk_in = jnp.zeros((cc, HEAD_DIM), F32)
            for i in range(nsub):
                blk_i = (c64 // HGRN_SUB) == i
                dk_raw = _dot(jnp.where(blk_i, dat, 0.0), qeb)
                dk_in = dk_in + dk_raw * es[i]
                db = db - kes[i].astype(F32) * dk_raw
            dqs[sl, :] = dqb * jnp.exp(b_c) + dq_raw * eq
            dks[sl, :] = dkb * jnp.exp(bl - b_c) + dk_in
            dbs[sl, :] = db
            extra = jnp.sum(dkbk, axis=0, keepdims=True) + jnp.sum(dstn * st, axis=0, keepdims=True) * ebl
            exs[sl, :] = jnp.broadcast_to(extra, (cc, HEAD_DIM))
            dp_ref[2, sl, :] = dv.astype(BF16)
            dst_s[...] = dstn * ebl + _dot(dob, qb_, TN)

        dg = _chunk_cumsum(dbs[...], reverse=True) + exs[...]
        df = dg / f - dks[...]
        dp_ref[1] = (df * (1.0 - lb) * sg * (1.0 - sg)).astype(BF16)
        dp_ref[0] = (dqs[...] * (sq * (1.0 + qr * (1.0 - sq)))).astype(BF16)
        dlb = jnp.sum(df * (1.0 - sg), axis=0, keepdims=True) * (lb * (1.0 - lb))
        dlog_ref[0:1, :] += -dlb
        dlog_ref[1:2, :] += dlb

    hh = h_count
    blk = lambda off: pl.BlockSpec((t, HEAD_DIM), lambda h, i: (nb - 1 - i, off + h))
    two = pl.BlockSpec((2, HEAD_DIM), lambda h, i: (0, h))
    one = pl.BlockSpec((1, HEAD_DIM), lambda h, i: (0, h))
    tile = pltpu.VMEM((t, HEAD_DIM), F32)
    return _call(
        body, name=name, grid=(hh, nb),
        in_specs=[blk(0), blk(hh), blk(2 * hh), blk(3 * hh), two, one, blk(0), blk(0),
                  pl.BlockSpec((1, nc, HEAD_DIM, HEAD_DIM), lambda h, i: (h, nb - 1 - i, 0, 0))],
        out_specs=[pl.BlockSpec((4, t, HEAD_DIM), lambda h, i: (0, nb - 1 - i, h)), two, one],
        out_shape=[jax.ShapeDtypeStruct((4, s, w), BF16), jax.ShapeDtypeStruct((2, w), F32),
                   jax.ShapeDtypeStruct((1, w), F32)],
        scratch_shapes=[pltpu.VMEM((HEAD_DIM, HEAD_DIM), F32), tile, tile, tile, tile, tile, tile],
        compiler_params=_cp(("parallel", "arbitrary")),
    )(p1, p1, p1, p1, lbl, onorm, o, dy, states)


def _adamw(name, w, g, m, v):
    r, c = w.shape
    tr = r if r <= 128 else _row_block(r, 128)
    c1 = 1.0 - ADAM_B1 ** ADAM_STEP
    c2 = 1.0 - ADAM_B2 ** ADAM_STEP

    def body(w_ref, g_ref, m_ref, v_ref, d_ref, nm_ref, nv_ref):
        gv = g_ref[...]
        nm = ADAM_B1 * m_ref[...] + (1.0 - ADAM_B1) * gv
        nv = ADAM_B2 * v_ref[...] + (1.0 - ADAM_B2) * (gv * gv)
        nm_ref[...] = nm
        nv_ref[...] = nv
        d_ref[...] = -ADAM_LR * ((nm / c1) / (jnp.sqrt(nv / c2) + ADAM_EPS) + ADAM_WD * w_ref[...])

    blk = pl.BlockSpec((tr, c), lambda i: (i, 0))
    sh = jax.ShapeDtypeStruct((r, c), F32)
    return _call(body, name=name, grid=(r // tr,), in_specs=[blk] * 4, out_specs=[blk] * 3,
                 out_shape=[sh, sh, sh], compiler_params=_cp(("parallel",)))(w, g, m, v)


SLAB_BLOCK_BYTES = 2 * 1024 * 1024


def _slab_rows(r, c):
    return _row_block(r, max(16, SLAB_BLOCK_BYTES // (4 * c) // 16 * 16))


def _pair_add(name, g2, recv, core):
    _, nch, r, c = g2.shape
    tr = _slab_rows(r, c)

    grid_spec = pltpu.PrefetchScalarGridSpec(
        num_scalar_prefetch=1, grid=(nch, r // tr),
        in_specs=[pl.BlockSpec((1, 1, tr, c), lambda j, i, cr: (cr[0], j, i, 0)),
                  pl.BlockSpec((1, tr, c), lambda j, i, cr: (j, i, 0))],
        out_specs=pl.BlockSpec((1, tr, c), lambda j, i, cr: (j, i, 0)))

    def body(core_ref, a_ref, b_ref, o_ref):
        o_ref[...] = (a_ref[0] + b_ref[...]).astype(BF16)

    return _call(body, name=name, grid_spec=grid_spec, out_shape=jax.ShapeDtypeStruct((nch, r, c), BF16),
                 compiler_params=_cp(("parallel", "parallel")))(core, g2, recv)


def _sum_slots(name, x, core=None):
    n, r, c = x.shape
    tr = _slab_rows(r, c)

    def total(x_ref):
        acc = x_ref[0].astype(F32)
        for j in range(1, n):
            acc = acc + x_ref[j].astype(F32)
        return acc

    if core is None:
        def body(x_ref, o_ref):
            o_ref[...] = total(x_ref)

        return _call(body, name=name, grid=(r // tr,),
                     in_specs=[pl.BlockSpec((n, tr, c), lambda i: (0, i, 0))],
                     out_specs=pl.BlockSpec((tr, c), lambda i: (i, 0)),
                     out_shape=jax.ShapeDtypeStruct((r, c), F32), compiler_params=_cp(("parallel",)))(x)

    def body_half(core_ref, x_ref, o_ref):
        o_ref[0] = total(x_ref)

    grid_spec = pltpu.PrefetchScalarGridSpec(
        num_scalar_prefetch=1, grid=(r // tr,),
        in_specs=[pl.BlockSpec((n, tr, c), lambda i, cr: (0, i, 0))],
        out_specs=pl.BlockSpec((1, tr, c), lambda i, cr: (cr[0], i, 0)))
    return _call(body_half, name=name, grid_spec=grid_spec, out_shape=jax.ShapeDtypeStruct((2, r, c), F32),
                 compiler_params=_cp(("parallel",)))(core, x)


def _pos():
    return lax.axis_index("x"), lax.axis_index("y"), lax.axis_index("c")


def _flip(v, f):
    return (1 - v) if f else v


CHIP_FLIPS = ((0, 1), (1, 0), (1, 1))
DEV_FLIPS = tuple((fx, fy, fc) for fx in (0, 1) for fy in (0, 1) for fc in (0, 1))[1:]


def _remote(src, dst, ssem, rsem, dev):
    return pltpu.make_async_remote_copy(src_ref=src, dst_ref=dst, send_sem=ssem, recv_sem=rsem,
                                        device_id=dev, device_id_type=MESH)


def _ag_weights(name, halves):
    n = len(halves)

    def body(*refs):
        h_refs, big_refs, sems = refs[:n], refs[n:2 * n], refs[2 * n:]
        _ag_phase(0, h_refs, big_refs, sems)
        _ag_phase(1, h_refs, big_refs, sems)
        _ag_phase(2, h_refs, big_refs, sems)

    return _call(body, name=name, in_specs=[ANY] * n, out_specs=[ANY] * n,
                 out_shape=_ag_out_shapes(halves), scratch_shapes=_ag_sems(n))(*halves)


def _ag_out_shapes(halves):
    return [jax.ShapeDtypeStruct((2, 4) + h.shape, h.dtype) for h in halves]


def _ag_sems(n):
    return [pltpu.SemaphoreType.DMA((n,)), pltpu.SemaphoreType.DMA((7 * n,)), pltpu.SemaphoreType.DMA((7 * n,))]


def _ag_phase(phase, h_refs, big_refs, sems):
    lsem, ssem, rsem = sems
    n = len(h_refs)
    x, y, cc = _pos()
    me = 2 * x + y
    sib = (x, y, 1 - cc)
    peers = [(_flip(x, fx), _flip(y, fy), cc) for fx, fy in CHIP_FLIPS]
    chips = [2 * px + py for px, py, _ in peers]
    for a in range(n):
        mine = big_refs[a].at[cc, me]
        first = [pltpu.make_async_copy(h_refs[a], mine, lsem.at[a])]
        first += [_remote(h_refs[a], mine, ssem.at[7 * a + k], rsem.at[7 * a + k], peers[k]) for k in range(3)]
        first += [_remote(h_refs[a], mine, ssem.at[7 * a + 3], rsem.at[7 * a + 3], sib)]
        passed = []
        for k in range(3):
            blk = big_refs[a].at[cc, chips[k]]
            passed.append(_remote(blk, blk, ssem.at[7 * a + 4 + k], rsem.at[7 * a + 4 + k], sib))
        if phase == 0:
            for cp in first:
                cp.start()
        elif phase == 1:
            for k in range(3):
                blk = big_refs[a].at[cc, chips[k]]
                _remote(blk, blk, ssem.at[7 * a + k], rsem.at[7 * a + k], peers[k]).wait_recv()
                passed[k].start()
        else:
            theirs = big_refs[a].at[1 - cc, me]
            _remote(theirs, theirs, ssem.at[7 * a + 3], rsem.at[7 * a + 3], sib).wait_recv()
            for k in range(3):
                blk = big_refs[a].at[1 - cc, chips[k]]
                _remote(blk, blk, ssem.at[7 * a + 4 + k], rsem.at[7 * a + 4 + k], sib).wait_recv()
            first[0].wait()
            for cp in first[1:] + passed:
                cp.wait_send()


def _ag_chips(name, v):
    r, c = v.shape

    def body(v_ref, out_ref, lsem, ssem, rsem):
        x, y, cc = _pos()
        me = 2 * x + y
        loc = pltpu.make_async_copy(v_ref, out_ref.at[me], lsem)
        loc.start()
        started = []
        for k, (fx, fy) in enumerate(CHIP_FLIPS):
            cp = _remote(v_ref, out_ref.at[me], ssem.at[k], rsem.at[k], (_flip(x, fx), _flip(y, fy), cc))
            cp.start()
            started.append(cp)
        for k, (fx, fy) in enumerate(CHIP_FLIPS):
            blk = out_ref.at[2 * _flip(x, fx) + _flip(y, fy)]
            _remote(blk, blk, ssem.at[k], rsem.at[k], (_flip(x, fx), _flip(y, fy), cc)).wait_recv()
        for cp in started:
            cp.wait_send()
        loc.wait()

    return _call(body, name=name, in_specs=[ANY], out_specs=ANY,
                 out_shape=jax.ShapeDtypeStruct((4, r, c), v.dtype),
                 scratch_shapes=[pltpu.SemaphoreType.DMA, pltpu.SemaphoreType.DMA((3,)), pltpu.SemaphoreType.DMA((3,))])(v)


def _ag_devices(name, v):
    r, c = v.shape

    def body(v_ref, out_ref, lsem, ssem, rsem):
        x, y, cc = _pos()
        me = 4 * x + 2 * y + cc
        loc = pltpu.make_async_copy(v_ref, out_ref.at[me], lsem)
        loc.start()
        started = []
        for k, (fx, fy, fc) in enumerate(DEV_FLIPS):
            cp = _remote(v_ref, out_ref.at[me], ssem.at[k], rsem.at[k], (_flip(x, fx), _flip(y, fy), _flip(cc, fc)))
            cp.start()
            started.append(cp)
        for k, (fx, fy, fc) in enumerate(DEV_FLIPS):
            px, py, pc = _flip(x, fx), _flip(y, fy), _flip(cc, fc)
            blk = out_ref.at[4 * px + 2 * py + pc]
            _remote(blk, blk, ssem.at[k], rsem.at[k], (px, py, pc)).wait_recv()
        for cp in started:
            cp.wait_send()
        loc.wait()

    return _call(body, name=name, in_specs=[ANY], out_specs=ANY,
                 out_shape=jax.ShapeDtypeStruct((8, r, c), v.dtype),
                 scratch_shapes=[pltpu.SemaphoreType.DMA, pltpu.SemaphoreType.DMA((7,)), pltpu.SemaphoreType.DMA((7,))])(v)


def _sibling_other_half(name, g2s):
    n = len(g2s)

    def body(*refs):
        g_refs, out_refs, sems = refs[:n], refs[n:2 * n], refs[2 * n:]
        _sibling_phase(0, g_refs, out_refs, sems)
        _sibling_phase(1, g_refs, out_refs, sems)

    return _call(body, name=name, in_specs=[ANY] * n, out_specs=[ANY] * n,
                 out_shape=_sibling_out_shapes(g2s), scratch_shapes=_sibling_sems(n))(*g2s)


def _sibling_out_shapes(g2s):
    return [jax.ShapeDtypeStruct(g.shape[1:], g.dtype) for g in g2s]


def _sibling_sems(n):
    return [pltpu.SemaphoreType.DMA((n,)), pltpu.SemaphoreType.DMA((n,))]


def _sibling_phase(phase, g_refs, out_refs, sems):
    ssem, rsem = sems
    x, y, cc = _pos()
    for a in range(len(g_refs)):
        cp = _remote(g_refs[a].at[1 - cc], out_refs[a], ssem.at[a], rsem.at[a], (x, y, 1 - cc))
        if phase == 0:
            cp.start()
        else:
            cp.wait()


def _scatter_sems(n):
    return [pltpu.SemaphoreType.DMA((n,)), pltpu.SemaphoreType.DMA((3 * n,)), pltpu.SemaphoreType.DMA((3 * n,))]


def _scatter_phase(phase, p_refs, out_refs, sems):
    lsem, ssem, rsem = sems
    x, y, cc = _pos()
    me = 2 * x + y
    peers = [(_flip(x, fx), _flip(y, fy), cc) for fx, fy in CHIP_FLIPS]
    for a in range(len(p_refs)):
        loc = pltpu.make_async_copy(p_refs[a].at[me], out_refs[a].at[me], lsem.at[a])
        sends = [_remote(p_refs[a].at[2 * px + py], out_refs[a].at[me], ssem.at[3 * a + k], rsem.at[3 * a + k], peers[k])
                 for k, (px, py, _) in enumerate(peers)]
        if phase == 0:
            loc.start()
            for cp in sends:
                cp.start()
        else:
            for k, (px, py, _) in enumerate(peers):
                blk = out_refs[a].at[2 * px + py]
                _remote(blk, blk, ssem.at[3 * a + k], rsem.at[3 * a + k], peers[k]).wait_recv()
            loc.wait()
            for cp in sends:
                cp.wait_send()


def _sibling_join(name, bufs):
    n = len(bufs)

    def body(*refs):
        out_refs, (ssem, rsem) = refs[n:2 * n], refs[2 * n:]
        x, y, cc = _pos()
        sib = (x, y, 1 - cc)
        cps = [_remote(out_refs[a].at[cc], out_refs[a].at[cc], ssem.at[a], rsem.at[a], sib) for a in range(n)]
        for cp in cps:
            cp.start()
        for a in range(n):
            theirs = out_refs[a].at[1 - cc]
            _remote(theirs, theirs, ssem.at[a], rsem.at[a], sib).wait_recv()
        for cp in cps:
            cp.wait_send()

    return _call(body, name=name, in_specs=[ANY] * n, out_specs=[ANY] * n,
                 out_shape=[jax.ShapeDtypeStruct(b.shape, b.dtype) for b in bufs],
                 input_output_aliases={a: a for a in range(n)},
                 scratch_shapes=[pltpu.SemaphoreType.DMA((n,)), pltpu.SemaphoreType.DMA((n,))])(*bufs)


def _pad_lanes(a, width=128):
    return jnp.pad(a, ((0, 0), (0, width - a.shape[1])))


def kernel(x, norm_gains, fox_w_in, fox_b_f, hgrn_w_in, hgrn_lb_logits, hgrn_onorm, w_out, final_gain, loss_target, m_norm_gains, m_fox_w_in, m_fox_b_f, m_hgrn_w_in, m_hgrn_lb_logits, m_hgrn_onorm, m_w_out, m_final_gain, v_norm_gains, v_fox_w_in, v_fox_b_f, v_hgrn_w_in, v_hgrn_lb_logits, v_hgrn_onorm, v_w_out, v_final_gain):
    s, d = x.shape[1], x.shape[2]
    wq = w_out.shape[1]
    w = 4 * wq
    hh = w // HEAD_DIM
    fox_cols = fox_w_in.shape[2]
    assert 4 * fox_cols == 4 * w + hh and hgrn_w_in.shape[2] == w
    core = lax.axis_index("c")

    x0 = x[0]
    tgt = loss_target[0]

    def my_half(a):
        return lax.dynamic_index_in_dim(a, core, 0, keepdims=False).astype(BF16)

    (big_fox,) = _ag_weights("ag_weights", [my_half(fox_w_in.reshape(2, d // 2, fox_cols))])
    later_halves = [my_half(hgrn_w_in.reshape(2, d // 2, w)), my_half(w_out)]
    fox_chip = [big_fox[:, j].reshape(d, fox_cols) for j in range(4)]
    def fox_columns(a, b):
        out = []
        for j in range(4):
            lo, hi = max(a, j * fox_cols), min(b, (j + 1) * fox_cols)
            if lo < hi:
                out.append(fox_chip[j][:, lo - j * fox_cols:hi - j * fox_cols])
        return out

    w_main = jnp.concatenate(fox_columns(0, 3 * w) + fox_columns(3 * w + hh, 4 * w + hh), axis=1)
    w_fl = _pad_lanes(jnp.concatenate(fox_columns(3 * w, 3 * w + hh), axis=1))
    onorm_full = _ag_chips("ag_onorm", hgrn_onorm.reshape(wq // 128, 128)).reshape(1, w)

    bf_pad = _pad_lanes(fox_b_f)
    g0, g1 = norm_gains[0:1], norm_gains[1:2]
    gf = final_gain.reshape(1, d)

    h0 = _rms_fwd("rms0_fwd", x0, g0)
    qkv = _mm("fox_qkv", h0, w_main, out_dtype=BF16, n=3 * w)
    gate0 = _mm("fox_gate", h0, w_main, n=w, b_off=3 * w)
    fl = _mm("fox_flogit", h0, w_fl)
    _, c_rep = _fox_gate_fwd("fox_cumsum", fl, bf_pad, w)
    o0, lse_row, big_hgrn, big_out = _attn_fwd("fox_attn_fwd", qkv, c_rep, hh, gather=later_halves)
    w_o = big_out.reshape(2, w, d)
    y0 = _fox_post("fox_post", o0, gate0)
    x1 = _mm("fox_out", y0, w_o[0], res=x0)
    h1 = _rms_fwd("rms1_fwd", x1, g1)
    p1 = _mm("hgrn_in", h1, big_hgrn, b_gathered=True)
    y1, o1, states = _hgrn_fwd("hgrn_fwd", p1, hgrn_lb_logits, onorm_full, hh)
    x2 = _mm("hgrn_out", y1, w_o[1], res=x1)
    dx2, d_gf, loss_tile = _loss_head("loss_head", x2, tgt, gf)

    dy1 = _mm("hgrn_out_dy", dx2, w_o[1], tb=True)
    d_wo1 = _mm("hgrn_out_dw", y1, dx2, ta=True)
    dp1, d_lbl, d_onorm = _hgrn_bwd("hgrn_bwd", p1, hgrn_lb_logits, onorm_full, o1, dy1, states, hh)
    dh1 = _mm("hgrn_in_dh", dp1, big_hgrn, tb=True, b_gathered=True, a_stacked=True)
    g2_hgrn = _mm("hgrn_in_dw", h1, dp1, ta=True, out_gathered=(d // 2, w), b_stacked=True)
    dx1, d_g1 = _rms_bwd("rms1_bwd", x1, g1, dh1, dx2)
    dy0 = _mm("fox_out_dy", dx1, w_o[0], tb=True)
    d_wo0 = _mm("fox_out_dw", y0, dx1, ta=True)
    core_arr = core.reshape(1).astype(jnp.int32)
    g2_out = jnp.stack([d_wo0, d_wo1]).reshape(2, 4, wq, d)
    do0, dgate0, delta_rep, sib_hgrn, sib_out = _fox_pre_bwd("fox_pre_bwd", dy0, o0, gate0, exchange=[g2_hgrn, g2_out])
    delta_row = delta_rep[:, ::HEAD_DIM].T.reshape(hh, 1, s)
    pairs_early = [_pair_add("rs_pair_add_hgrn", g2_hgrn, sib_hgrn, core_arr),
                   _pair_add("rs_pair_add_out", g2_out, sib_out, core_arr)]
    dk0, dv0, dc_rep, dq0, rowsum_row, got_hgrn, got_out = _attn_bwd(
        "fox_attn_bwd", qkv, do0, lse_row, delta_row, c_rep, hh, scatter=pairs_early)
    dfl, d_bf = _fox_gate_bwd("fox_cumsum_bwd", _pad_lanes(rowsum_row.reshape(hh, s).T),
                              _pad_lanes(dc_rep[:, ::HEAD_DIM]), fl, bf_pad)
    dp0 = jnp.concatenate([dq0, dk0, dv0, dgate0], axis=1)
    d_wmain = _mm("fox_in_dw", h0, dp0, ta=True)
    d_wfl = _mm("fox_fl_dw", h0, dfl, ta=True)

    def grad_columns(a, b):
        out = []
        for lo, hi, src, shift in ((0, 3 * w, d_wmain, 0), (3 * w, 3 * w + hh, d_wfl, 3 * w),
                                   (3 * w + hh, 4 * w + hh, d_wmain, hh)):
            l2, h2 = max(a, lo), min(b, hi)
            if l2 < h2:
                out.append(src[:, l2 - shift:h2 - shift])
        return out

    g2_fox = jnp.stack([jnp.concatenate(grad_columns(j * fox_cols, (j + 1) * fox_cols), axis=1).reshape(2, d // 2, fox_cols)
                        for j in range(4)], axis=1)
    (sib_fox,) = _sibling_other_half("rs_sibling", [g2_fox])
    dh0, got_fox = _mm("fox_in_dh", dp0, w_main, tb=True, res=_mm("fox_fl_dh", dfl, w_fl, tb=True),
                       scatter=[_pair_add("rs_pair_add_fox", g2_fox, sib_fox, core_arr)])
    grad_x, d_g0 = _rms_bwd("rms0_bwd", x0, g0, dh0, dx1)
    halves_sum = [_sum_slots("rs_sum_" + nm, got, core_arr)
                  for nm, got in zip(("fox", "hgrn", "out"), (got_fox, got_hgrn, got_out))]
    r_fox, r_hgrn, r_out = _sibling_join("rs_join", halves_sum)
    g_fox = r_fox.reshape(d, fox_cols)
    g_hgrn = r_hgrn.reshape(d, w)
    g_out = r_out.reshape(2 * wq, d)

    small = jnp.concatenate([jnp.concatenate([d_g0, d_g1], axis=0).reshape(-1), d_lbl.reshape(-1), d_gf.reshape(-1),
                             d_onorm.reshape(-1), d_bf.reshape(-1)])
    n_small = small.shape[0]
    pad_to = -(-n_small // 1024) * 1024
    small = jnp.pad(small, (0, pad_to - n_small)).reshape(pad_to // 128, 128)
    small = _sum_slots("small_sum", _ag_devices("small_gather", small)).reshape(-1)
    g_norm = small[:2 * d].reshape(2, d)
    g_lbl = small[2 * d:2 * d + 2 * w].reshape(2, w)
    g_gf = small[2 * d + 2 * w:3 * d + 2 * w]
    g_onorm_full = small[3 * d + 2 * w:3 * d + 3 * w]
    g_bf = small[3 * d + 3 * w:3 * d + 3 * w + hh].reshape(1, hh)
    chip = 2 * lax.axis_index("x") + lax.axis_index("y")
    g_onorm = lax.dynamic_slice_in_dim(g_onorm_full, chip * wq, wq).reshape(1, wq)

    loss = lax.psum(loss_tile[0, 0], ("x", "y", "c"))

    def upd(name, wt, g, m, v):
        shp = wt.shape
        two = lambda a: a.reshape(-1, shp[-1])
        dl, nm, nv = _adamw(name, two(wt), two(g), two(m), two(v))
        return g.reshape(shp), dl.reshape(shp), nm.reshape(shp), nv.reshape(shp)

    res = [
        upd("adamw_norm_gains", norm_gains, g_norm, m_norm_gains, v_norm_gains),
        upd("adamw_fox_w_in", fox_w_in, g_fox, m_fox_w_in, v_fox_w_in),
        upd("adamw_fox_b_f", fox_b_f, g_bf, m_fox_b_f, v_fox_b_f),
        upd("adamw_hgrn_w_in", hgrn_w_in, g_hgrn, m_hgrn_w_in, v_hgrn_w_in),
        upd("adamw_lb_logits", hgrn_lb_logits, g_lbl, m_hgrn_lb_logits, v_hgrn_lb_logits),
        upd("adamw_onorm", hgrn_onorm, g_onorm, m_hgrn_onorm, v_hgrn_onorm),
        upd("adamw_w_out", w_out, g_out, m_w_out, v_w_out),
        upd("adamw_final_gain", final_gain.reshape(1, d), g_gf.reshape(1, d), m_final_gain.reshape(1, d),
            v_final_gain.reshape(1, d)),
    ]
    res[-1] = tuple(a.reshape(d) for a in res[-1])
    grads, deltas, new_m, new_v = zip(*res)
    return (loss, grad_x[None], *grads, *deltas, *new_m, *new_v)
```

```python
import functools

import jax
import jax.numpy as jnp
from jax import lax
from jax.experimental import pallas as pl
from jax.experimental.pallas import tpu as pltpu

F32 = jnp.float32
BF16 = jnp.bfloat16
MESH = pl.DeviceIdType.MESH
ANY = pl.BlockSpec(memory_space=pl.ANY)

EPS = 1e-6
HEAD_DIM = 128
HGRN_CHUNK = 64
HGRN_SUB = 32
EXP_CLAMP = 80.0
ATT_BLOCK = 512
HGRN_BLOCK = 512
GATE_BLOCK = 512
ROW_BLOCK = 256
MM_TM, MM_TN, MM_TK = 1024, 1024, 2048
VMEM_LIMIT_V7X = 56 * 1024 * 1024

ADAM_LR, ADAM_B1, ADAM_B2, ADAM_EPS, ADAM_WD, ADAM_STEP = 0.001, 0.9, 0.999, 1e-08, 0.01, 10

NT = (((1,), (1,)), ((), ()))
TN = (((0,), (0,)), ((), ()))
NN = (((1,), (0,)), ((), ()))


def _call(body, **kw):
    return pl.pallas_call(body, **kw)


def _cp(dims=None):
    kw = dict(vmem_limit_bytes=VMEM_LIMIT_V7X)
    if dims is not None:
        kw["dimension_semantics"] = dims
    return pltpu.CompilerParams(**kw)


def _sigmoid(x):
    return 1.0 / (1.0 + jnp.exp(-x))


def _dot(a, b, dn=NN):
    return lax.dot_general(a.astype(BF16), b.astype(BF16), dn, preferred_element_type=F32)


def _split3(x):
    hi = x.astype(BF16)
    r1 = x - hi.astype(F32)
    mid = r1.astype(BF16)
    lo = (r1 - mid.astype(F32)).astype(BF16)
    return hi, mid, lo


def _dot_exact(m01, x, right=False):
    hi, mid, lo = _split3(x)
    if right:
        dot = lambda p: lax.dot_general(p, m01, NN, preferred_element_type=F32)
    else:
        dot = lambda p: lax.dot_general(m01, p, NN, preferred_element_type=F32)
    return dot(hi) + dot(mid) + dot(lo)


def _row_block(rows, cap):
    if rows <= cap:
        return rows
    best = None
    for t in range(16, cap + 1, 16):
        if rows % t == 0:
            best = t
    assert best is not None, rows
    return best


def _mm(name, a, b, *, ta=False, tb=False, out_dtype=F32, res=None, n=None, b_off=0, b_gathered=False,
        out_gathered=None, scatter=(), a_stacked=False, b_stacked=False):
    if a_stacked:
        assert not ta
        m, k = a.shape[1], a.shape[0] * a.shape[2]
    else:
        m, k = (a.shape[1], a.shape[0]) if ta else a.shape
    if b_gathered:
        rh, cw = b.shape[2], b.shape[3]
        n_full = 2 * rh if tb else 4 * cw
    elif b_stacked:
        assert not tb
        n_full = b.shape[0] * b.shape[2]
    else:
        n_full = b.shape[0] if tb else b.shape[1]
    n = n_full if n is None else n
    tm, tn, tk = min(MM_TM, m), min(MM_TN, n), min(MM_TK, k)
    b_both = b_gathered and not tb and tk == 2 * rh
    if b_gathered and not b_both:
        tn, tk = (min(tn, rh), min(tk, cw)) if tb else (min(tn, cw), min(tk, rh))
    if b_both:
        tn = min(tn, cw)
    if out_gathered:
        tm, tn = min(tm, out_gathered[0]), min(tn, out_gathered[1])
    if a_stacked:
        tk = min(tk, a.shape[2])
    if b_stacked:
        tn = min(tn, b.shape[2])
    assert m % tm == 0 and n % tn == 0 and k % tk == 0 and b_off % tn == 0
    nk = k // tk
    jo = b_off // tn
    if a_stacked:
        pa = a.shape[2] // tk
        a_spec = pl.BlockSpec((1, tm, tk), lambda i, j, kk: (kk // pa, i, kk % pa))
    elif ta:
        a_spec = pl.BlockSpec((tk, tm), lambda i, j, kk: (kk, i))
    else:
        a_spec = pl.BlockSpec((tm, tk), lambda i, j, kk: (i, kk))
    if b_gathered and tb:
        pr, pc = rh // tn, cw // tk
        b_spec = pl.BlockSpec((1, 1, tn, tk), lambda i, j, kk: (j // pr, kk // pc, j % pr, kk % pc))
    elif b_both:
        pc = cw // tn
        b_spec = pl.BlockSpec((2, 1, rh, tn), lambda i, j, kk: (0, j // pc, 0, j % pc))
    elif b_gathered:
        pr, pc = rh // tk, cw // tn
        b_spec = pl.BlockSpec((1, 1, tk, tn), lambda i, j, kk: (kk // pr, j // pc, kk % pr, j % pc))
    elif b_stacked:
        pb = b.shape[2] // tn
        b_spec = pl.BlockSpec((1, tk, tn), lambda i, j, kk: (j // pb, kk, j % pb))
    elif tb:
        b_spec = pl.BlockSpec((tn, tk), lambda i, j, kk: (j + jo, kk))
    else:
        b_spec = pl.BlockSpec((tk, tn), lambda i, j, kk: (kk, j + jo))
    o_spec = pl.BlockSpec((tm, tn), lambda i, j, kk: (i, j))
    if out_gathered:
        assert m == 2 * out_gathered[0] and n == 4 * out_gathered[1] and res is None
        qr, qc = out_gathered[0] // tm, out_gathered[1] // tn
        o_spec = pl.BlockSpec((1, 1, tm, tn), lambda i, j, kk: (i // qr, j // qc, i % qr, j % qc))
    dn = (((0 if ta else 1,), (1 if tb else 0,)), ((), ()))
    has_res = res is not None
    ns = len(scatter)
    grid = (m // tm, n // tn, nk)

    def body(*refs):
        a_ref, b_ref = refs[:2]
        r_ref = refs[2] if has_res else None
        base = 2 + int(has_res)
        p_refs = refs[base:base + ns]
        o_ref = refs[base + ns]
        got_refs = refs[base + ns + 1:base + 2 * ns + 1]
        acc = refs[base + 2 * ns + 1]
        sems = refs[base + 2 * ns + 2:]
        kk = pl.program_id(2)
        if ns:
            first = jnp.logical_and(jnp.logical_and(pl.program_id(0) == 0, pl.program_id(1) == 0), kk == 0)
            last = jnp.logical_and(jnp.logical_and(pl.program_id(0) == grid[0] - 1, pl.program_id(1) == grid[1] - 1),
                                   kk == nk - 1)

            @pl.when(first)
            def _():
                _scatter_phase(0, p_refs, got_refs, sems)

        av = a_ref[0] if a_stacked else a_ref[...]
        if b_both:
            p = (lax.dot_general(av[:, :rh].astype(BF16), b_ref[0, 0].astype(BF16), dn, preferred_element_type=F32)
                 + lax.dot_general(av[:, rh:].astype(BF16), b_ref[1, 0].astype(BF16), dn, preferred_element_type=F32))
        else:
            bv = b_ref[0, 0] if b_gathered else (b_ref[0] if b_stacked else b_ref[...])
            p = lax.dot_general(av.astype(BF16), bv.astype(BF16), dn, preferred_element_type=F32)

        def finish(total):
            if r_ref is not None:
                total = total + r_ref[...]
            if out_gathered:
                o_ref[0, 0] = total.astype(out_dtype)
            else:
                o_ref[...] = total.astype(out_dtype)

        if nk == 1:
            finish(p)
        else:
            @pl.when(kk == 0)
            def _():
                acc[...] = p

            @pl.when(jnp.logical_and(kk > 0, kk < nk - 1))
            def _():
                acc[...] += p

            @pl.when(kk == nk - 1)
            def _():
                finish(acc[...] + p)

        if ns:
            @pl.when(last)
            def _():
                _scatter_phase(1, p_refs, got_refs, sems)

    ins = [a, b] + ([res] if has_res else []) + list(scatter)
    in_specs = [a_spec, b_spec] + ([o_spec] if has_res else []) + [ANY] * ns
    o_shape = jax.ShapeDtypeStruct((2, 4) + tuple(out_gathered) if out_gathered else (m, n), out_dtype)
    out = _call(
        body, name=name, grid=grid, in_specs=in_specs, out_specs=[o_spec] + [ANY] * ns,
        out_shape=[o_shape] + [jax.ShapeDtypeStruct(p.shape, p.dtype) for p in scatter],
        scratch_shapes=[pltpu.VMEM((tm, tn) if nk > 1 else (8, 128), F32)] + (_scatter_sems(ns) if ns else []),
        compiler_params=_cp(("arbitrary",) * 3 if ns else ("parallel", "parallel", "arbitrary")),
    )(*ins)
    return out if ns else out[0]


def _rms_fwd(name, x, g):
    s, d = x.shape
    tm = min(ROW_BLOCK, s)

    def body(x_ref, g_ref, h_ref):
        xv = x_ref[...]
        r = lax.rsqrt(jnp.mean(xv * xv, axis=-1, keepdims=True) + EPS)
        h_ref[...] = (xv * r * g_ref[...]).astype(BF16)

    row = pl.BlockSpec((tm, d), lambda i: (i, 0))
    vec = pl.BlockSpec((1, d), lambda i: (0, 0))
    return _call(body, name=name, grid=(s // tm,), in_specs=[row, vec], out_specs=row,
                 out_shape=jax.ShapeDtypeStruct((s, d), BF16), compiler_params=_cp(("parallel",)))(x, g)


def _rms_bwd(name, x, g, dh, dres):
    s, d = x.shape
    tm = min(ROW_BLOCK, s)

    def body(x_ref, g_ref, dh_ref, dres_ref, dx_ref, dg_ref):
        @pl.when(pl.program_id(0) == 0)
        def _():
            dg_ref[...] = jnp.zeros_like(dg_ref)

        xv = x_ref[...]
        r = lax.rsqrt(jnp.mean(xv * xv, axis=-1, keepdims=True) + EPS)
        xn = xv * r
        dhv = dh_ref[...]
        dxn = dhv * g_ref[...]
        dx_ref[...] = dres_ref[...] + r * (dxn - xn * jnp.mean(dxn * xn, axis=-1, keepdims=True))
        dg_ref[...] += jnp.sum(dhv * xn, axis=0, keepdims=True)

    row = pl.BlockSpec((tm, d), lambda i: (i, 0))
    vec = pl.BlockSpec((1, d), lambda i: (0, 0))
    return _call(body, name=name, grid=(s // tm,), in_specs=[row, vec, row, row], out_specs=[row, vec],
                 out_shape=[jax.ShapeDtypeStruct((s, d), F32), jax.ShapeDtypeStruct((1, d), F32)],
                 compiler_params=_cp(("arbitrary",)))(x, g, dh, dres)


def _loss_head(name, x, tgt, g):
    s, d = x.shape
    tm = min(ROW_BLOCK, s)
    nb = s // tm

    def body(x_ref, t_ref, g_ref, dx_ref, dg_ref, loss_ref, lacc):
        i = pl.program_id(0)

        @pl.when(i == 0)
        def _():
            dg_ref[...] = jnp.zeros_like(dg_ref)
            lacc[...] = jnp.zeros_like(lacc)

        xv = x_ref[...]
        gv = g_ref[...]
        r = lax.rsqrt(jnp.mean(xv * xv, axis=-1, keepdims=True) + EPS)
        xn = xv * r
        err = xn * gv - t_ref[...]
        lacc[...] += jnp.sum(err * err, axis=0, keepdims=True)
        dout = err * (1.0 / d)
        dg_ref[...] += jnp.sum(dout * xn, axis=0, keepdims=True)
        dxn = dout * gv
        dx_ref[...] = r * (dxn - xn * jnp.mean(dxn * xn, axis=-1, keepdims=True))

        @pl.when(i == nb - 1)
        def _():
            total = jnp.sum(lacc[...], axis=1, keepdims=True) * (0.5 / d)
            loss_ref[...] = jnp.broadcast_to(total, loss_ref.shape)

    row = pl.BlockSpec((tm, d), lambda i: (i, 0))
    vec = pl.BlockSpec((1, d), lambda i: (0, 0))
    one = pl.BlockSpec((1, 128), lambda i: (0, 0))
    return _call(body, name=name, grid=(nb,), in_specs=[row, row, vec], out_specs=[row, vec, one],
                 out_shape=[jax.ShapeDtypeStruct((s, d), F32), jax.ShapeDtypeStruct((1, d), F32),
                            jax.ShapeDtypeStruct((1, 128), F32)],
                 scratch_shapes=[pltpu.VMEM((1, d), F32)], compiler_params=_cp(("arbitrary",)))(x, tgt, g)


def _fox_gate_fwd(name, fl, bf, w):
    s = fl.shape[0]
    tb = min(GATE_BLOCK, s)

    def body(fl_ref, bf_ref, c_ref, crep_ref, carry):
        @pl.when(pl.program_id(0) == 0)
        def _():
            carry[...] = jnp.zeros_like(carry)

        z = fl_ref[...] + bf_ref[...]
        lf = jnp.minimum(z, 0.0) - jnp.log(1.0 + jnp.exp(-jnp.abs(z)))
        rows = lax.broadcasted_iota(jnp.int32, (tb, tb), 0)
        cols = lax.broadcasted_iota(jnp.int32, (tb, tb), 1)
        tri = (rows >= cols).astype(BF16)
        cs = _dot_exact(tri, lf) + carry[...]
        c_ref[...] = cs
        carry[...] = c_ref[tb - 1:tb, :]
        sel_r = lax.broadcasted_iota(jnp.int32, (128, w), 0)
        sel_c = lax.broadcasted_iota(jnp.int32, (128, w), 1)
        sel = (sel_r == sel_c // HEAD_DIM).astype(BF16)
        crep_ref[...] = _dot_exact(sel, cs, right=True)

    blk = pl.BlockSpec((tb, 128), lambda i: (i, 0))
    return _call(body, name=name, grid=(s // tb,),
                 in_specs=[blk, pl.BlockSpec((1, 128), lambda i: (0, 0))],
                 out_specs=[blk, pl.BlockSpec((tb, w), lambda i: (i, 0))],
                 out_shape=[jax.ShapeDtypeStruct((s, 128), F32), jax.ShapeDtypeStruct((s, w), F32)],
                 scratch_shapes=[pltpu.VMEM((1, 128), F32)], compiler_params=_cp(("arbitrary",)))(fl, bf)


def _fox_gate_bwd(name, drow, dcol, fl, bf):
    s = fl.shape[0]
    tb = min(GATE_BLOCK, s)
    nb = s // tb

    def body(dr_ref, dc_ref, fl_ref, bf_ref, dfl_ref, dbf_ref, carry, tmp):
        @pl.when(pl.program_id(0) == 0)
        def _():
            carry[...] = jnp.zeros_like(carry)
            dbf_ref[...] = jnp.zeros_like(dbf_ref)

        rows = lax.broadcasted_iota(jnp.int32, (tb, tb), 0)
        cols = lax.broadcasted_iota(jnp.int32, (tb, tb), 1)
        triu = (rows <= cols).astype(BF16)
        dlf = _dot_exact(triu, dr_ref[...] + dc_ref[...]) + carry[...]
        tmp[...] = dlf
        carry[...] = tmp[0:1, :]
        z = fl_ref[...] + bf_ref[...]
        dfl = dlf * (1.0 / (1.0 + jnp.exp(z)))
        dfl_ref[...] = dfl
        dbf_ref[...] += jnp.sum(dfl, axis=0, keepdims=True)

    blk = pl.BlockSpec((tb, 128), lambda i: (nb - 1 - i, 0))
    vec = pl.BlockSpec((1, 128), lambda i: (0, 0))
    return _call(body, name=name, grid=(nb,), in_specs=[blk, blk, blk, vec], out_specs=[blk, vec],
                 out_shape=[jax.ShapeDtypeStruct((s, 128), F32), jax.ShapeDtypeStruct((1, 128), F32)],
                 scratch_shapes=[pltpu.VMEM((1, 128), F32), pltpu.VMEM((tb, 128), F32)],
                 compiler_params=_cp(("arbitrary",)))(drow, dcol, fl, bf)


def _attn_fwd(name, qkv, c_rep, gate, h_count, gather=()):
    s = qkv.shape[0]
    w = h_count * HEAD_DIM
    t = min(ATT_BLOCK, s)
    scale = HEAD_DIM ** -0.5
    hp = 2 if h_count % 2 == 0 else 1
    wb = hp * HEAD_DIM
    ng = len(gather)
    nh, nq = h_count // hp, s // t

    def body(*refs):
        q_ref, k_ref, v_ref, c_ref, gate_ref = refs[:5]
        h_refs = refs[5:5 + ng]
        o_ref, lse_ref, y_ref = refs[5 + ng:8 + ng]
        big_refs = refs[8 + ng:8 + 2 * ng]
        m_s, l_s, acc_s = refs[8 + 2 * ng:11 + 2 * ng]
        sems = refs[11 + 2 * ng:]
        hs = pl.program_id(0)
        qi = pl.program_id(1)
        if ng:
            @pl.when(jnp.logical_and(hs == 0, qi == 0))
            def _():
                _ag_phase(0, h_refs, big_refs, sems)

            @pl.when(jnp.logical_and(hs == nh // 2, qi == 0))
            def _():
                _ag_phase(1, h_refs, big_refs, sems)

        m_s[...] = jnp.full(m_s.shape, -jnp.inf, F32)
        l_s[...] = jnp.zeros_like(l_s)
        acc_s[...] = jnp.zeros_like(acc_s)

        def step(kb, masked):
            off = pl.multiple_of(kb * t, t)
            for a in range(hp):
                cols_a = slice(a * HEAD_DIM, (a + 1) * HEAD_DIM)
                kk = k_ref[pl.ds(off, t), cols_a]
                vv = v_ref[pl.ds(off, t), cols_a]
                cc = jnp.tile(c_ref[pl.ds(off, t), cols_a], (1, t // HEAD_DIM))
                st = lax.dot_general(kk, q_ref[:, cols_a], NT, preferred_element_type=F32) * scale - cc
                if masked:
                    rows = lax.broadcasted_iota(jnp.int32, (t, t), 0)
                    cols = lax.broadcasted_iota(jnp.int32, (t, t), 1)
                    st = jnp.where(cols >= rows, st, -jnp.inf)
                m_prev = m_s[a]
                m_new = jnp.maximum(m_prev, jnp.max(st, axis=0, keepdims=True))
                pt = jnp.exp(st - m_new)
                alpha = jnp.exp(m_prev - m_new)
                l_s[a] = alpha * l_s[a] + jnp.sum(pt, axis=0, keepdims=True)
                acc_s[a] = alpha * acc_s[a] + lax.dot_general(vv, pt.astype(BF16), TN, preferred_element_type=F32)
                m_s[a] = m_new

        def loop_body(kb, carry):
            step(kb, False)
            return carry

        lax.fori_loop(0, qi, loop_body, 0)
        step(qi, True)
        for a in range(hp):
            l = l_s[a]
            cols_a = slice(a * HEAD_DIM, (a + 1) * HEAD_DIM)
            oa = (acc_s[a] / l).T
            o_ref[:, cols_a] = oa
            g = gate_ref[:, cols_a]
            y_ref[:, cols_a] = (oa * (g * _sigmoid(g))).astype(BF16)
            lse_ref[a] = m_s[a] + jnp.log(l)

        if ng:
            @pl.when(jnp.logical_and(hs == nh - 1, qi == nq - 1))
            def _():
                _ag_phase(2, h_refs, big_refs, sems)

    blk = lambda off: pl.BlockSpec((t, wb), lambda h, i: (i, off + h))
    whole = lambda off: pl.BlockSpec((s, wb), lambda h, i: (0, off + h))
    rowv = pl.BlockSpec((hp, 1, t), lambda h, i: (h, 0, i))
    return _call(
        body, name=name, grid=(nh, nq),
        in_specs=[blk(0), whole(nh), whole(2 * nh), whole(0), blk(0)] + [ANY] * ng,
        out_specs=[blk(0), rowv, blk(0)] + [ANY] * ng,
        out_shape=[jax.ShapeDtypeStruct((s, w), F32), jax.ShapeDtypeStruct((h_count, 1, s), F32),
                   jax.ShapeDtypeStruct((s, w), BF16)] + _ag_out_shapes(gather),
        scratch_shapes=[pltpu.VMEM((hp, 1, t), F32), pltpu.VMEM((hp, 1, t), F32), pltpu.VMEM((hp, HEAD_DIM, t), F32)]
        + (_ag_sems(ng) if ng else []),
        compiler_params=_cp(("arbitrary", "arbitrary") if ng else ("parallel", "arbitrary")),
    )(qkv, qkv, qkv, c_rep, gate, *gather)


def _attn_bwd(name, qkv, do, lse_row, delta_row, c_rep, h_count, scatter=()):
    s = qkv.shape[0]
    w = h_count * HEAD_DIM
    t = min(ATT_BLOCK, s)
    nq = s // t
    scale = HEAD_DIM ** -0.5
    ns = len(scatter)

    def body(*refs):
        k_ref, v_ref, q_ref, do_ref, lse_ref, dl_ref, c_ref = refs[:7]
        p_refs = refs[7:7 + ns]
        dk_ref, dv_ref, dc_ref, dq_ref, rs_ref = refs[7 + ns:12 + ns]
        got_refs = refs[12 + ns:12 + 2 * ns]
        dk_s, dv_s, dc_s, dq_s, rs_s = refs[12 + 2 * ns:17 + 2 * ns]
        sems = refs[17 + 2 * ns:]
        kj = pl.program_id(1)
        if ns:
            @pl.when(jnp.logical_and(pl.program_id(0) == 0, kj == 0))
            def _():
                _scatter_phase(0, p_refs, got_refs, sems)

        kk = k_ref[...]
        vv = v_ref[...]
        ccol = jnp.tile(c_ref[...], (1, t // HEAD_DIM))
        dk_s[...] = jnp.zeros_like(dk_s)
        dv_s[...] = jnp.zeros_like(dv_s)
        dc_s[...] = jnp.zeros_like(dc_s)

        @pl.when(kj == 0)
        def _():
            dq_s[...] = jnp.zeros_like(dq_s)
            rs_s[...] = jnp.zeros_like(rs_s)

        def step(qb, masked):
            off = pl.multiple_of(qb * t, t)
            q = q_ref[pl.ds(off, t), :]
            dov = do_ref[pl.ds(off, t), :]
            st = lax.dot_general(kk, q, NT, preferred_element_type=F32) * scale - ccol
            pt = jnp.exp(st - lse_ref[0, :, pl.ds(off, t)])
            if masked:
                rows = lax.broadcasted_iota(jnp.int32, (t, t), 0)
                cols = lax.broadcasted_iota(jnp.int32, (t, t), 1)
                pt = jnp.where(cols >= rows, pt, 0.0)
            dv_s[...] += lax.dot_general(pt.astype(BF16), dov, NN, preferred_element_type=F32)
            dpt = lax.dot_general(vv, dov, NT, preferred_element_type=F32)
            dst = pt * (dpt - dl_ref[0, :, pl.ds(off, t)])
            dstb = dst.astype(BF16)
            dk_s[...] += lax.dot_general(dstb, q, NN, preferred_element_type=F32)
            dc_s[...] += jnp.sum(dst, axis=1, keepdims=True)
            dq_s[:, pl.ds(off, t)] += lax.dot_general(kk, dstb, TN, preferred_element_type=F32)
            rs_s[:, pl.ds(off, t)] += jnp.sum(dst, axis=0, keepdims=True)

        step(kj, True)

        def loop_body(qb, carry):
            step(qb, False)
            return carry

        lax.fori_loop(kj + 1, nq, loop_body, 0)
        dk_ref[...] = (dk_s[...] * scale).astype(BF16)
        dv_ref[...] = dv_s[...].astype(BF16)
        dc_ref[...] = jnp.broadcast_to(-dc_s[...], dc_ref.shape)

        @pl.when(kj == nq - 1)
        def _():
            for b in range(nq):
                dq_ref[b * t:(b + 1) * t, :] = (dq_s[:, b * t:(b + 1) * t] * scale).T.astype(BF16)
            rs_ref[0] = rs_s[...]

        if ns:
            @pl.when(jnp.logical_and(pl.program_id(0) == h_count - 1, kj == nq - 1))
            def _():
                _scatter_phase(1, p_refs, got_refs, sems)

    hh = h_count
    blk = lambda off: pl.BlockSpec((t, HEAD_DIM), lambda h, j: (j, off + h))
    whole = pl.BlockSpec((s, HEAD_DIM), lambda h, j: (0, h))
    rowv = pl.BlockSpec((1, 1, s), lambda h, j: (h, 0, 0))
    return _call(
        body, name=name, grid=(hh, nq),
        in_specs=[blk(hh), blk(2 * hh), whole, whole, rowv, rowv, blk(0)] + [ANY] * ns,
        out_specs=[blk(0), blk(0), blk(0), whole, rowv] + [ANY] * ns,
        out_shape=[jax.ShapeDtypeStruct((s, w), BF16), jax.ShapeDtypeStruct((s, w), BF16),
                   jax.ShapeDtypeStruct((s, w), F32), jax.ShapeDtypeStruct((s, w), BF16),
                   jax.ShapeDtypeStruct((hh, 1, s), F32)] + [jax.ShapeDtypeStruct(p.shape, p.dtype) for p in scatter],
        scratch_shapes=[pltpu.VMEM((t, HEAD_DIM), F32), pltpu.VMEM((t, HEAD_DIM), F32), pltpu.VMEM((t, 1), F32),
                        pltpu.VMEM((HEAD_DIM, s), F32), pltpu.VMEM((1, s), F32)] + (_scatter_sems(ns) if ns else []),
        compiler_params=_cp(("arbitrary", "arbitrary") if ns else ("parallel", "arbitrary")),
    )(qkv, qkv, qkv, do, lse_row, delta_row, c_rep, *scatter)


def _fox_pre_bwd(name, dy, o, gate, exchange=()):
    s, w = o.shape
    tm = min(ROW_BLOCK, s)
    ne = len(exchange)
    nb = s // tm

    def body(*refs):
        dy_ref, o_ref, g_ref = refs[:3]
        x_refs = refs[3:3 + ne]
        do_ref, dg_ref, dl_ref = refs[3 + ne:6 + ne]
        got_refs = refs[6 + ne:6 + 2 * ne]
        sems = refs[6 + 2 * ne:]
        if ne:
            @pl.when(pl.program_id(0) == 0)
            def _():
                _sibling_phase(0, x_refs, got_refs, sems)

            @pl.when(pl.program_id(0) == nb - 1)
            def _():
                _sibling_phase(1, x_refs, got_refs, sems)

        g = g_ref[...]
        sg = _sigmoid(g)
        dyv = dy_ref[...]
        ov = o_ref[...]
        dov = dyv * (g * sg)
        do_ref[...] = dov.astype(BF16)
        dg_ref[...] = (dyv * ov * (sg * (1.0 + g * (1.0 - sg)))).astype(BF16)
        sel_r = lax.broadcasted_iota(jnp.int32, (w, 128), 0)
        sel_c = lax.broadcasted_iota(jnp.int32, (w, 128), 1)
        sel = (sel_r // HEAD_DIM == sel_c).astype(BF16)
        dl_ref[...] = _dot_exact(sel, dov * ov, right=True)

    row = pl.BlockSpec((tm, w), lambda i: (i, 0))
    lanes = pl.BlockSpec((tm, 128), lambda i: (i, 0))
    return _call(body, name=name, grid=(nb,), in_specs=[row, row, row] + [ANY] * ne,
                 out_specs=[row, row, lanes] + [ANY] * ne,
                 out_shape=[jax.ShapeDtypeStruct((s, w), BF16), jax.ShapeDtypeStruct((s, w), BF16),
                            jax.ShapeDtypeStruct((s, 128), F32)] + _sibling_out_shapes(exchange),
                 scratch_shapes=_sibling_sems(ne) if ne else [],
                 compiler_params=_cp(("arbitrary",) if ne else ("parallel",)))(dy, o, gate, *exchange)


def _hgrn_chunk_terms(q_c, k_c, b_c, b_s, base):
    nsub = HGRN_CHUNK // HGRN_SUB
    refs = [jnp.zeros((1, HEAD_DIM), F32)]
    for i in range(1, nsub):
        r0 = base + i * HGRN_SUB - 1
        refs.append(b_s[r0:r0 + 1, :])
    rfull = jnp.concatenate([jnp.broadcast_to(r, (HGRN_SUB, HEAD_DIM)) for r in refs], axis=0)
    eq = jnp.exp(b_c - rfull)
    qe = q_c * eq
    es = [jnp.exp(jnp.minimum(r - b_c, EXP_CLAMP)) for r in refs]
    kes = [(k_c * e).astype(BF16) for e in es]
    return eq, qe, es, kes


def _chunk_cumsum(x, reverse=False):
    cc = HGRN_CHUNK
    nc = x.shape[0] // cc
    rows = lax.broadcasted_iota(jnp.int32, (cc, cc), 0)
    cols = lax.broadcasted_iota(jnp.int32, (cc, cc), 1)
    tri = ((rows <= cols) if reverse else (rows >= cols)).astype(BF16)
    wide = jnp.concatenate([x[n * cc:(n + 1) * cc] for n in range(nc)], axis=1)
    res = _dot_exact(tri, wide)
    return jnp.concatenate([res[:, n * HEAD_DIM:(n + 1) * HEAD_DIM] for n in range(nc)], axis=0)


def _hgrn_block_pre(q_ref, f_ref, lbl_ref, b_s):
    lb = _sigmoid(lbl_ref[1:2, :] - lbl_ref[0:1, :])
    qr = q_ref[...]
    sq = _sigmoid(qr)
    q = qr * sq
    fz = f_ref[...]
    sg = _sigmoid(fz)
    f = lb + (1.0 - lb) * sg
    g = jnp.log(f)
    k = (1.0 - lb) * (1.0 / (1.0 + jnp.exp(fz)))
    b = _chunk_cumsum(g)
    b_s[...] = b
    return lb, qr, sq, q, sg, f, k, b


def _hgrn_fwd(name, p1, lbl, onorm, h_count):
    s = p1.shape[0]
    w = h_count * HEAD_DIM
    t = min(HGRN_BLOCK, s)
    nc = t // HGRN_CHUNK
    nsub = HGRN_CHUNK // HGRN_SUB
    cc = HGRN_CHUNK

    def body(q_ref, f_ref, i_ref, g_ref, lbl_ref, on_ref, y_ref, o_ref, st_ref, state_s, b_s):
        @pl.when(pl.program_id(1) == 0)
        def _():
            state_s[...] = jnp.zeros_like(state_s)

        lb, qr, sq, q, sg, f, k, b = _hgrn_block_pre(q_ref, f_ref, lbl_ref, b_s)
        v = i_ref[...]
        r64 = lax.broadcasted_iota(jnp.int32, (cc, cc), 0)
        c64 = lax.broadcasted_iota(jnp.int32, (cc, cc), 1)
        for n in range(nc):
            sl = slice(n * cc, (n + 1) * cc)
            q_c, k_c, v_c, b_c = q[sl], k[sl], v[sl], b[sl]
            bl = b_s[n * cc + cc - 1:n * cc + cc, :]
            eq, qe, es, kes = _hgrn_chunk_terms(q_c, k_c, b_c, b_s, n * cc)
            qeb = qe.astype(BF16)
            a = jnp.concatenate(
                [lax.dot_general(qeb[i * HGRN_SUB:(i + 1) * HGRN_SUB], kes[i], NT, preferred_element_type=F32)
                 for i in range(nsub)], axis=0)
            a = jnp.where(r64 >= c64, a, 0.0)
            st = state_s[...]
            st_ref[0, n] = st
            inter = _dot(q_c * jnp.exp(b_c), st, NT)
            intra = _dot(a, v_c)
            o_ref[sl, :] = inter + intra
            kb = k_c * jnp.exp(bl - b_c)
            state_s[...] = st * jnp.exp(bl) + _dot(v_c, kb, TN)
        o = o_ref[...]
        rr = lax.rsqrt(jnp.mean(o * o, axis=-1, keepdims=True) + EPS)
        gate = g_ref[...]
        y_ref[...] = ((o * rr) * on_ref[...] * (gate * _sigmoid(gate))).astype(BF16)

    hh = h_count
    blk = lambda off: pl.BlockSpec((t, HEAD_DIM), lambda h, i: (i, off + h))
    return _call(
        body, name=name, grid=(hh, s // t),
        in_specs=[blk(0), blk(hh), blk(2 * hh), blk(3 * hh),
                  pl.BlockSpec((2, HEAD_DIM), lambda h, i: (0, h)), pl.BlockSpec((1, HEAD_DIM), lambda h, i: (0, h))],
        out_specs=[blk(0), blk(0), pl.BlockSpec((1, nc, HEAD_DIM, HEAD_DIM), lambda h, i: (h, i, 0, 0))],
        out_shape=[jax.ShapeDtypeStruct((s, w), BF16), jax.ShapeDtypeStruct((s, w), F32),
                   jax.ShapeDtypeStruct((hh, s // cc, HEAD_DIM, HEAD_DIM), F32)],
        scratch_shapes=[pltpu.VMEM((HEAD_DIM, HEAD_DIM), F32), pltpu.VMEM((t, HEAD_DIM), F32)],
        compiler_params=_cp(("parallel", "arbitrary")),
    )(p1, p1, p1, p1, lbl, onorm)


def _hgrn_bwd(name, p1, lbl, onorm, o, dy, states, h_count):
    s = p1.shape[0]
    w = h_count * HEAD_DIM
    t = min(HGRN_BLOCK, s)
    nb = s // t
    nc = t // HGRN_CHUNK
    nsub = HGRN_CHUNK // HGRN_SUB
    cc = HGRN_CHUNK

    def body(q_ref, f_ref, i_ref, g_ref, lbl_ref, on_ref, o_ref, dy_ref, st_ref,
             dp_ref, dlog_ref, don_ref,
             dst_s, b_s, do_s, dqs, dks, dbs, exs):
        @pl.when(pl.program_id(1) == 0)
        def _():
            dst_s[...] = jnp.zeros_like(dst_s)
            dlog_ref[...] = jnp.zeros_like(dlog_ref)
            don_ref[...] = jnp.zeros_like(don_ref)

        lb, qr, sq, q, sg, f, k, b = _hgrn_block_pre(q_ref, f_ref, lbl_ref, b_s)
        v = i_ref[...]

        ov = o_ref[...]
        rr = lax.rsqrt(jnp.mean(ov * ov, axis=-1, keepdims=True) + EPS)
        on = ov * rr
        gate = g_ref[...]
        sgt = _sigmoid(gate)
        silu = gate * sgt
        dyv = dy_ref[...]
        gain = on_ref[...]
        dp_ref[3] = (dyv * on * gain * (sgt * (1.0 + gate * (1.0 - sgt)))).astype(BF16)
        don_ref[...] += jnp.sum(dyv * on * silu, axis=0, keepdims=True)
        d_on = dyv * gain * silu
        do_s[...] = rr * (d_on - on * jnp.mean(d_on * on, axis=-1, keepdims=True))

        r64 = lax.broadcasted_iota(jnp.int32, (cc, cc), 0)
        c64 = lax.broadcasted_iota(jnp.int32, (cc, cc), 1)
        upper = r64 <= c64
        for n in reversed(range(nc)):
            sl = slice(n * cc, (n + 1) * cc)
            q_c, k_c, v_c, b_c = q[sl], k[sl], v[sl], b[sl]
            do_c = do_s[sl, :]
            bl = b_s[n * cc + cc - 1:n * cc + cc, :]
            ebl = jnp.exp(bl)
            eq, qe, es, kes = _hgrn_chunk_terms(q_c, k_c, b_c, b_s, n * cc)
            qeb = qe.astype(BF16)
            dob = do_c.astype(BF16)
            vb = v_c.astype(BF16)
            st = st_ref[0, n]
            dstn = dst_s[...]
            qb_ = q_c * jnp.exp(b_c)
            kb_ = k_c * jnp.exp(bl - b_c)
            at = jnp.zeros((cc, cc), F32)
            for i in range(nsub):
                blk_i = (c64 // HGRN_SUB) == i
                at = at + jnp.where(blk_i, lax.dot_general(kes[i], qeb, NT, preferred_element_type=F32), 0.0)
            at = jnp.where(upper, at, 0.0)
            dv = _dot(at, dob) + _dot(kb_, dstn, NT)
            dqb = _dot(dob, st)
            dkb = _dot(vb, dstn)
            da = jnp.where(r64 >= c64, lax.dot_general(dob, vb, NT, preferred_element_type=F32), 0.0)
            dat = jnp.where(upper, lax.dot_general(vb, dob, NT, preferred_element_type=F32), 0.0)
            dab = da.astype(BF16)
            dq_raw = jnp.concatenate(
                [lax.dot_general(dab[i * HGRN_SUB:(i + 1) * HGRN_SUB], kes[i], NN, preferred_element_type=F32)
                 for i in range(nsub)], axis=0)
            db = qb_.astype(BF16).astype(F32) * dqb + qeb.astype(F32) * dq_raw
            dkbk = dkb * kb_.astype(BF16).astype(F32)
            db = db - dkbk
            dk_in = jnp.zeros((cc, HEAD_DIM), F32)
            for i in range(nsub):
                blk_i = (c64 // HGRN_SUB) == i
                dk_raw = _dot(jnp.where(blk_i, dat, 0.0), qeb)
                dk_in = dk_in + dk_raw * es[i]
                db = db - kes[i].astype(F32) * dk_raw
            dqs[sl, :] = dqb * jnp.exp(b_c) + dq_raw * eq
            dks[sl, :] = dkb * jnp.exp(bl - b_c) + dk_in
            dbs[sl, :] = db
            extra = jnp.sum(dkbk, axis=0, keepdims=True) + jnp.sum(dstn * st, axis=0, keepdims=True) * ebl
            exs[sl, :] = jnp.broadcast_to(extra, (cc, HEAD_DIM))
            dp_ref[2, sl, :] = dv.astype(BF16)
            dst_s[...] = dstn * ebl + _dot(dob, qb_, TN)

        dg = _chunk_cumsum(dbs[...], reverse=True) + exs[...]
        df = dg / f - dks[...]
        dp_ref[1] = (df * (1.0 - lb) * sg * (1.0 - sg)).astype(BF16)
        dp_ref[0] = (dqs[...] * (sq * (1.0 + qr * (1.0 - sq)))).astype(BF16)
        dlb = jnp.sum(df * (1.0 - sg), axis=0, keepdims=True) * (lb * (1.0 - lb))
        dlog_ref[0:1, :] += -dlb
        dlog_ref[1:2, :] += dlb

    hh = h_count
    blk = lambda off: pl.BlockSpec((t, HEAD_DIM), lambda h, i: (nb - 1 - i, off + h))
    two = pl.BlockSpec((2, HEAD_DIM), lambda h, i: (0, h))
    one = pl.BlockSpec((1, HEAD_DIM), lambda h, i: (0, h))
    tile = pltpu.VMEM((t, HEAD_DIM), F32)
    return _call(
        body, name=name, grid=(hh, nb),
        in_specs=[blk(0), blk(hh), blk(2 * hh), blk(3 * hh), two, one, blk(0), blk(0),
                  pl.BlockSpec((1, nc, HEAD_DIM, HEAD_DIM), lambda h, i: (h, nb - 1 - i, 0, 0))],
        out_specs=[pl.BlockSpec((4, t, HEAD_DIM), lambda h, i: (0, nb - 1 - i, h)), two, one],
        out_shape=[jax.ShapeDtypeStruct((4, s, w), BF16), jax.ShapeDtypeStruct((2, w), F32),
                   jax.ShapeDtypeStruct((1, w), F32)],
        scratch_shapes=[pltpu.VMEM((HEAD_DIM, HEAD_DIM), F32), tile, tile, tile, tile, tile, tile],
        compiler_params=_cp(("parallel", "arbitrary")),
    )(p1, p1, p1, p1, lbl, onorm, o, dy, states)


def _adamw(name, w, g, m, v):
    r, c = w.shape
    tr = r if r <= 128 else _row_block(r, 128)
    c1 = 1.0 - ADAM_B1 ** ADAM_STEP
    c2 = 1.0 - ADAM_B2 ** ADAM_STEP

    def body(w_ref, g_ref, m_ref, v_ref, d_ref, nm_ref, nv_ref):
        gv = g_ref[...]
        nm = ADAM_B1 * m_ref[...] + (1.0 - ADAM_B1) * gv
        nv = ADAM_B2 * v_ref[...] + (1.0 - ADAM_B2) * (gv * gv)
        nm_ref[...] = nm
        nv_ref[...] = nv
        d_ref[...] = -ADAM_LR * ((nm / c1) / (jnp.sqrt(nv / c2) + ADAM_EPS) + ADAM_WD * w_ref[...])

    blk = pl.BlockSpec((tr, c), lambda i: (i, 0))
    sh = jax.ShapeDtypeStruct((r, c), F32)
    return _call(body, name=name, grid=(r // tr,), in_specs=[blk] * 4, out_specs=[blk] * 3,
                 out_shape=[sh, sh, sh], compiler_params=_cp(("parallel",)))(w, g, m, v)


SLAB_BLOCK_BYTES = 2 * 1024 * 1024


def _slab_rows(r, c):
    return _row_block(r, max(16, SLAB_BLOCK_BYTES // (4 * c) // 16 * 16))


def _pair_add(name, g2, recv, core):
    _, nch, r, c = g2.shape
    tr = _slab_rows(r, c)

    grid_spec = pltpu.PrefetchScalarGridSpec(
        num_scalar_prefetch=1, grid=(nch, r // tr),
        in_specs=[pl.BlockSpec((1, 1, tr, c), lambda j, i, cr: (cr[0], j, i, 0)),
                  pl.BlockSpec((1, tr, c), lambda j, i, cr: (j, i, 0))],
        out_specs=pl.BlockSpec((1, tr, c), lambda j, i, cr: (j, i, 0)))

    def body(core_ref, a_ref, b_ref, o_ref):
        o_ref[...] = (a_ref[0] + b_ref[...]).astype(BF16)

    return _call(body, name=name, grid_spec=grid_spec, out_shape=jax.ShapeDtypeStruct((nch, r, c), BF16),
                 compiler_params=_cp(("parallel", "parallel")))(core, g2, recv)


def _sum_slots(name, x, core=None):
    n, r, c = x.shape
    tr = _slab_rows(r, c)

    def total(x_ref):
        acc = x_ref[0].astype(F32)
        for j in range(1, n):
            acc = acc + x_ref[j].astype(F32)
        return acc

    if core is None:
        def body(x_ref, o_ref):
            o_ref[...] = total(x_ref)

        return _call(body, name=name, grid=(r // tr,),
                     in_specs=[pl.BlockSpec((n, tr, c), lambda i: (0, i, 0))],
                     out_specs=pl.BlockSpec((tr, c), lambda i: (i, 0)),
                     out_shape=jax.ShapeDtypeStruct((r, c), F32), compiler_params=_cp(("parallel",)))(x)

    def body_half(core_ref, x_ref, o_ref):
        o_ref[0] = total(x_ref)

    grid_spec = pltpu.PrefetchScalarGridSpec(
        num_scalar_prefetch=1, grid=(r // tr,),
        in_specs=[pl.BlockSpec((n, tr, c), lambda i, cr: (0, i, 0))],
        out_specs=pl.BlockSpec((1, tr, c), lambda i, cr: (cr[0], i, 0)))
    return _call(body_half, name=name, grid_spec=grid_spec, out_shape=jax.ShapeDtypeStruct((2, r, c), F32),
                 compiler_params=_cp(("parallel",)))(core, x)


def _pos():
    return lax.axis_index("x"), lax.axis_index("y"), lax.axis_index("c")


def _flip(v, f):
    return (1 - v) if f else v


CHIP_FLIPS = ((0, 1), (1, 0), (1, 1))
DEV_FLIPS = tuple((fx, fy, fc) for fx in (0, 1) for fy in (0, 1) for fc in (0, 1))[1:]


def _remote(src, dst, ssem, rsem, dev):
    return pltpu.make_async_remote_copy(src_ref=src, dst_ref=dst, send_sem=ssem, recv_sem=rsem,
                                        device_id=dev, device_id_type=MESH)


def _ag_weights(name, halves):
    n = len(halves)

    def body(*refs):
        h_refs, big_refs, sems = refs[:n], refs[n:2 * n], refs[2 * n:]
        _ag_phase(0, h_refs, big_refs, sems)
        _ag_phase(1, h_refs, big_refs, sems)
        _ag_phase(2, h_refs, big_refs, sems)

    return _call(body, name=name, in_specs=[ANY] * n, out_specs=[ANY] * n,
                 out_shape=_ag_out_shapes(halves), scratch_shapes=_ag_sems(n))(*halves)


def _ag_out_shapes(halves):
    return [jax.ShapeDtypeStruct((2, 4) + h.shape, h.dtype) for h in halves]


def _ag_sems(n):
    return [pltpu.SemaphoreType.DMA((n,)), pltpu.SemaphoreType.DMA((7 * n,)), pltpu.SemaphoreType.DMA((7 * n,))]


def _ag_phase(phase, h_refs, big_refs, sems):
    lsem, ssem, rsem = sems
    n = len(h_refs)
    x, y, cc = _pos()
    me = 2 * x + y
    sib = (x, y, 1 - cc)
    peers = [(_flip(x, fx), _flip(y, fy), cc) for fx, fy in CHIP_FLIPS]
    chips = [2 * px + py for px, py, _ in peers]
    for a in range(n):
        mine = big_refs[a].at[cc, me]
        first = [pltpu.make_async_copy(h_refs[a], mine, lsem.at[a])]
        first += [_remote(h_refs[a], mine, ssem.at[7 * a + k], rsem.at[7 * a + k], peers[k]) for k in range(3)]
        first += [_remote(h_refs[a], mine, ssem.at[7 * a + 3], rsem.at[7 * a + 3], sib)]
        passed = []
        for k in range(3):
            blk = big_refs[a].at[cc, chips[k]]
            passed.append(_remote(blk, blk, ssem.at[7 * a + 4 + k], rsem.at[7 * a + 4 + k], sib))
        if phase == 0:
            for cp in first:
                cp.start()
        elif phase == 1:
            for k in range(3):
                blk = big_refs[a].at[cc, chips[k]]
                _remote(blk, blk, ssem.at[7 * a + k], rsem.at[7 * a + k], peers[k]).wait_recv()
                passed[k].start()
        else:
            theirs = big_refs[a].at[1 - cc, me]
            _remote(theirs, theirs, ssem.at[7 * a + 3], rsem.at[7 * a + 3], sib).wait_recv()
            for k in range(3):
                blk = big_refs[a].at[1 - cc, chips[k]]
                _remote(blk, blk, ssem.at[7 * a + 4 + k], rsem.at[7 * a + 4 + k], sib).wait_recv()
            first[0].wait()
            for cp in first[1:] + passed:
                cp.wait_send()


def _ag_devices(name, v):
    r, c = v.shape

    def body(v_ref, out_ref, lsem, ssem, rsem):
        x, y, cc = _pos()
        me = 4 * x + 2 * y + cc
        loc = pltpu.make_async_copy(v_ref, out_ref.at[me], lsem)
        loc.start()
        started = []
        for k, (fx, fy, fc) in enumerate(DEV_FLIPS):
            cp = _remote(v_ref, out_ref.at[me], ssem.at[k], rsem.at[k], (_flip(x, fx), _flip(y, fy), _flip(cc, fc)))
            cp.start()
            started.append(cp)
        for k, (fx, fy, fc) in enumerate(DEV_FLIPS):
            px, py, pc = _flip(x, fx), _flip(y, fy), _flip(cc, fc)
            blk = out_ref.at[4 * px + 2 * py + pc]
            _remote(blk, blk, ssem.at[k], rsem.at[k], (px, py, pc)).wait_recv()
        for cp in started:
            cp.wait_send()
        loc.wait()

    return _call(body, name=name, in_specs=[ANY], out_specs=ANY,
                 out_shape=jax.ShapeDtypeStruct((8, r, c), v.dtype),
                 scratch_shapes=[pltpu.SemaphoreType.DMA, pltpu.SemaphoreType.DMA((7,)), pltpu.SemaphoreType.DMA((7,))])(v)


def _sibling_other_half(name, g2s):
    n = len(g2s)

    def body(*refs):
        g_refs, out_refs, sems = refs[:n], refs[n:2 * n], refs[2 * n:]
        _sibling_phase(0, g_refs, out_refs, sems)
        _sibling_phase(1, g_refs, out_refs, sems)

    return _call(body, name=name, in_specs=[ANY] * n, out_specs=[ANY] * n,
                 out_shape=_sibling_out_shapes(g2s), scratch_shapes=_sibling_sems(n))(*g2s)


def _sibling_out_shapes(g2s):
    return [jax.ShapeDtypeStruct(g.shape[1:], g.dtype) for g in g2s]


def _sibling_sems(n):
    return [pltpu.SemaphoreType.DMA((n,)), pltpu.SemaphoreType.DMA((n,))]


def _sibling_phase(phase, g_refs, out_refs, sems):
    ssem, rsem = sems
    x, y, cc = _pos()
    for a in range(len(g_refs)):
        cp = _remote(g_refs[a].at[1 - cc], out_refs[a], ssem.at[a], rsem.at[a], (x, y, 1 - cc))
        if phase == 0:
            cp.start()
        else:
            cp.wait()


def _scatter_sems(n):
    return [pltpu.SemaphoreType.DMA((n,)), pltpu.SemaphoreType.DMA((3 * n,)), pltpu.SemaphoreType.DMA((3 * n,))]


def _scatter_phase(phase, p_refs, out_refs, sems):
    lsem, ssem, rsem = sems
    x, y, cc = _pos()
    me = 2 * x + y
    peers = [(_flip(x, fx), _flip(y, fy), cc) for fx, fy in CHIP_FLIPS]
    for a in range(len(p_refs)):
        loc = pltpu.make_async_copy(p_refs[a].at[me], out_refs[a].at[me], lsem.at[a])
        sends = [_remote(p_refs[a].at[2 * px + py], out_refs[a].at[me], ssem.at[3 * a + k], rsem.at[3 * a + k], peers[k])
                 for k, (px, py, _) in enumerate(peers)]
        if phase == 0:
            loc.start()
            for cp in sends:
                cp.start()
        else:
            for k, (px, py, _) in enumerate(peers):
                blk = out_refs[a].at[2 * px + py]
                _remote(blk, blk, ssem.at[3 * a + k], rsem.at[3 * a + k], peers[k]).wait_recv()
            loc.wait()
            for cp in sends:
                cp.wait_send()


def _sibling_join(name, bufs):
    n = len(bufs)

    def body(*refs):
        out_refs, (ssem, rsem) = refs[n:2 * n], refs[2 * n:]
        x, y, cc = _pos()
        sib = (x, y, 1 - cc)
        cps = [_remote(out_refs[a].at[cc], out_refs[a].at[cc], ssem.at[a], rsem.at[a], sib) for a in range(n)]
        for cp in cps:
            cp.start()
        for a in range(n):
            theirs = out_refs[a].at[1 - cc]
            _remote(theirs, theirs, ssem.at[a], rsem.at[a], sib).wait_recv()
        for cp in cps:
            cp.wait_send()

    return _call(body, name=name, in_specs=[ANY] * n, out_specs=[ANY] * n,
                 out_shape=[jax.ShapeDtypeStruct(b.shape, b.dtype) for b in bufs],
                 input_output_aliases={a: a for a in range(n)},
                 scratch_shapes=[pltpu.SemaphoreType.DMA((n,)), pltpu.SemaphoreType.DMA((n,))])(*bufs)


def _pad_lanes(a, width=128):
    return jnp.pad(a, ((0, 0), (0, width - a.shape[1])))


def kernel(x, norm_gains, fox_w_in, fox_b_f, hgrn_w_in, hgrn_lb_logits, hgrn_onorm, w_out, final_gain, loss_target, m_norm_gains, m_fox_w_in, m_fox_b_f, m_hgrn_w_in, m_hgrn_lb_logits, m_hgrn_onorm, m_w_out, m_final_gain, v_norm_gains, v_fox_w_in, v_fox_b_f, v_hgrn_w_in, v_hgrn_lb_logits, v_hgrn_onorm, v_w_out, v_final_gain):
    s, d = x.shape[1], x.shape[2]
    wq = w_out.shape[1]
    w = 4 * wq
    hh = w // HEAD_DIM
    fox_cols = fox_w_in.shape[2]
    assert 4 * fox_cols == 4 * w + hh and hgrn_w_in.shape[2] == w
    core = lax.axis_index("c")

    x0 = x[0]
    tgt = loss_target[0]

    def my_half(a):
        return lax.dynamic_index_in_dim(a, core, 0, keepdims=False).astype(BF16)

    (big_fox,) = _ag_weights("ag_weights", [my_half(fox_w_in.reshape(2, d // 2, fox_cols))])
    later_halves = [my_half(hgrn_w_in.reshape(2, d // 2, w)), my_half(w_out),
                    lax.dynamic_index_in_dim(hgrn_onorm.reshape(2, 1, wq // 2), core, 0, keepdims=False)]
    fox_chip = [big_fox[:, j].reshape(d, fox_cols) for j in range(4)]
    def fox_columns(a, b):
        out = []
        for j in range(4):
            lo, hi = max(a, j * fox_cols), min(b, (j + 1) * fox_cols)
            if lo < hi:
                out.append(fox_chip[j][:, lo - j * fox_cols:hi - j * fox_cols])
        return out

    w_main = jnp.concatenate(fox_columns(0, 3 * w) + fox_columns(3 * w + hh, 4 * w + hh), axis=1)
    w_fl = _pad_lanes(jnp.concatenate(fox_columns(3 * w, 3 * w + hh), axis=1))

    bf_pad = _pad_lanes(fox_b_f)
    g0, g1 = norm_gains[0:1], norm_gains[1:2]
    gf = final_gain.reshape(1, d)

    h0 = _rms_fwd("rms0_fwd", x0, g0)
    qkv = _mm("fox_qkv", h0, w_main, out_dtype=BF16, n=3 * w)
    gate0 = _mm("fox_gate", h0, w_main, n=w, b_off=3 * w)
    fl = _mm("fox_flogit", h0, w_fl)
    _, c_rep = _fox_gate_fwd("fox_cumsum", fl, bf_pad, w)
    o0, lse_row, y0, big_hgrn, big_out, big_onorm = _attn_fwd("fox_attn_fwd", qkv, c_rep, gate0, hh, gather=later_halves)
    onorm_full = big_onorm.transpose(1, 0, 2, 3).reshape(1, w)
    w_o = big_out.reshape(2, w, d)
    x1 = _mm("fox_out", y0, w_o[0], res=x0)
    h1 = _rms_fwd("rms1_fwd", x1, g1)
    p1 = _mm("hgrn_in", h1, big_hgrn, b_gathered=True)
    y1, o1, states = _hgrn_fwd("hgrn_fwd", p1, hgrn_lb_logits, onorm_full, hh)
    x2 = _mm("hgrn_out", y1, w_o[1], res=x1)
    dx2, d_gf, loss_tile = _loss_head("loss_head", x2, tgt, gf)

    dy1 = _mm("hgrn_out_dy", dx2, w_o[1], tb=True)
    d_wo1 = _mm("hgrn_out_dw", y1, dx2, ta=True)
    dp1, d_lbl, d_onorm = _hgrn_bwd("hgrn_bwd", p1, hgrn_lb_logits, onorm_full, o1, dy1, states, hh)
    dh1 = _mm("hgrn_in_dh", dp1, big_hgrn, tb=True, b_gathered=True, a_stacked=True)
    g2_hgrn = _mm("hgrn_in_dw", h1, dp1, ta=True, out_gathered=(d // 2, w), b_stacked=True)
    dx1, d_g1 = _rms_bwd("rms1_bwd", x1, g1, dh1, dx2)
    dy0 = _mm("fox_out_dy", dx1, w_o[0], tb=True)
    d_wo0 = _mm("fox_out_dw", y0, dx1, ta=True)
    core_arr = core.reshape(1).astype(jnp.int32)
    g2_out = jnp.stack([d_wo0, d_wo1]).reshape(2, 4, wq, d)
    do0, dgate0, delta, sib_hgrn, sib_out = _fox_pre_bwd("fox_pre_bwd", dy0, o0, gate0, exchange=[g2_hgrn, g2_out])
    delta_row = delta[:, :hh].T.reshape(hh, 1, s)
    pairs_early = [_pair_add("rs_pair_add_hgrn", g2_hgrn, sib_hgrn, core_arr),
                   _pair_add("rs_pair_add_out", g2_out, sib_out, core_arr)]
    dk0, dv0, dc_rep, dq0, rowsum_row, got_hgrn, got_out = _attn_bwd(
        "fox_attn_bwd", qkv, do0, lse_row, delta_row, c_rep, hh, scatter=pairs_early)
    dfl, d_bf = _fox_gate_bwd("fox_cumsum_bwd", _pad_lanes(rowsum_row.reshape(hh, s).T),
                              _pad_lanes(dc_rep[:, ::HEAD_DIM]), fl, bf_pad)
    dp0 = jnp.concatenate([dq0, dk0, dv0, dgate0], axis=1)
    d_wmain = _mm("fox_in_dw", h0, dp0, ta=True)
    d_wfl = _mm("fox_fl_dw", h0, dfl, ta=True)

    def grad_columns(a, b):
        out = []
        for lo, hi, src, shift in ((0, 3 * w, d_wmain, 0), (3 * w, 3 * w + hh, d_wfl, 3 * w),
                                   (3 * w + hh, 4 * w + hh, d_wmain, hh)):
            l2, h2 = max(a, lo), min(b, hi)
            if l2 < h2:
                out.append(src[:, l2 - shift:h2 - shift])
        return out

    g2_fox = jnp.stack([jnp.concatenate(grad_columns(j * fox_cols, (j + 1) * fox_cols), axis=1).reshape(2, d // 2, fox_cols)
                        for j in range(4)], axis=1)
    (sib_fox,) = _sibling_other_half("rs_sibling", [g2_fox])
    dh0, got_fox = _mm("fox_in_dh", dp0, w_main, tb=True, res=_mm("fox_fl_dh", dfl, w_fl, tb=True),
                       scatter=[_pair_add("rs_pair_add_fox", g2_fox, sib_fox, core_arr)])
    grad_x, d_g0 = _rms_bwd("rms0_bwd", x0, g0, dh0, dx1)
    halves_sum = [_sum_slots("rs_sum_" + nm, got, core_arr)
                  for nm, got in zip(("fox", "hgrn", "out"), (got_fox, got_hgrn, got_out))]
    r_fox, r_hgrn, r_out = _sibling_join("rs_join", halves_sum)
    g_fox = r_fox.reshape(d, fox_cols)
    g_hgrn = r_hgrn.reshape(d, w)
    g_out = r_out.reshape(2 * wq, d)

    small = jnp.concatenate([jnp.concatenate([d_g0, d_g1], axis=0).reshape(-1), d_lbl.reshape(-1), d_gf.reshape(-1),
                             d_onorm.reshape(-1), d_bf.reshape(-1)])
    n_small = small.shape[0]
    pad_to = -(-n_small // 1024) * 1024
    small = jnp.pad(small, (0, pad_to - n_small)).reshape(pad_to // 128, 128)
    small = _sum_slots("small_sum", _ag_devices("small_gather", small)).reshape(-1)
    g_norm = small[:2 * d].reshape(2, d)
    g_lbl = small[2 * d:2 * d + 2 * w].reshape(2, w)
    g_gf = small[2 * d + 2 * w:3 * d + 2 * w]
    g_onorm_full = small[3 * d + 2 * w:3 * d + 3 * w]
    g_bf = small[3 * d + 3 * w:3 * d + 3 * w + hh].reshape(1, hh)
    chip = 2 * lax.axis_index("x") + lax.axis_index("y")
    g_onorm = lax.dynamic_slice_in_dim(g_onorm_full, chip * wq, wq).reshape(1, wq)

    loss = lax.psum(loss_tile[0, 0], ("x", "y", "c"))

    def upd(name, wt, g, m, v):
        shp = wt.shape
        two = lambda a: a.reshape(-1, shp[-1])
        dl, nm, nv = _adamw(name, two(wt), two(g), two(m), two(v))
        return g.reshape(shp), dl.reshape(shp), nm.reshape(shp), nv.reshape(shp)

    res = [
        upd("adamw_norm_gains", norm_gains, g_norm, m_norm_gains, v_norm_gains),
        upd("adamw_fox_w_in", fox_w_in, g_fox, m_fox_w_in, v_fox_w_in),
        upd("adamw_fox_b_f", fox_b_f, g_bf, m_fox_b_f, v_fox_b_f),
        upd("adamw_hgrn_w_in", hgrn_w_in, g_hgrn, m_hgrn_w_in, v_hgrn_w_in),
        upd("adamw_lb_logits", hgrn_lb_logits, g_lbl, m_hgrn_lb_logits, v_hgrn_lb_logits),
        upd("adamw_onorm", hgrn_onorm, g_onorm, m_hgrn_onorm, v_hgrn_onorm),
        upd("adamw_w_out", w_out, g_out, m_w_out, v_w_out),
        upd("adamw_final_gain", final_gain.reshape(1, d), g_gf.reshape(1, d), m_final_gain.reshape(1, d),
            v_final_gain.reshape(1, d)),
    ]
    res[-1] = tuple(a.reshape(d) for a in res[-1])
    grads, deltas, new_m, new_v = zip(*res)
    return (loss, grad_x[None], *grads, *deltas, *new_m, *new_v)
```

```python
import functools

import jax
import jax.numpy as jnp
from jax import lax
from jax.experimental import pallas as pl
from jax.experimental.pallas import tpu as pltpu

F32 = jnp.float32
BF16 = jnp.bfloat16
MESH = pl.DeviceIdType.MESH
ANY = pl.BlockSpec(memory_space=pl.ANY)

EPS = 1e-6
HEAD_DIM = 128
HGRN_CHUNK = 64
HGRN_SUB = 32
EXP_CLAMP = 80.0
ATT_BLOCK = 512
HGRN_BLOCK = 512
GATE_BLOCK = 512
ROW_BLOCK = 256
MM_TM, MM_TN, MM_TK = 1024, 1024, 2048
VMEM_LIMIT_V7X = 56 * 1024 * 1024

ADAM_LR, ADAM_B1, ADAM_B2, ADAM_EPS, ADAM_WD, ADAM_STEP = 0.001, 0.9, 0.999, 1e-08, 0.01, 10

NT = (((1,), (1,)), ((), ()))
TN = (((0,), (0,)), ((), ()))
NN = (((1,), (0,)), ((), ()))


def _call(body, **kw):
    return pl.pallas_call(body, **kw)


def _cp(dims=None):
    kw = dict(vmem_limit_bytes=VMEM_LIMIT_V7X)
    if dims is not None:
        kw["dimension_semantics"] = dims
    return pltpu.CompilerParams(**kw)


def _sigmoid(x):
    return 1.0 / (1.0 + jnp.exp(-x))


def _dot(a, b, dn=NN):
    return lax.dot_general(a.astype(BF16), b.astype(BF16), dn, preferred_element_type=F32)


def _split3(x):
    hi = x.astype(BF16)
    r1 = x - hi.astype(F32)
    mid = r1.astype(BF16)
    lo = (r1 - mid.astype(F32)).astype(BF16)
    return hi, mid, lo


def _dot_exact(m01, x, right=False):
    hi, mid, lo = _split3(x)
    if right:
        dot = lambda p: lax.dot_general(p, m01, NN, preferred_element_type=F32)
    else:
        dot = lambda p: lax.dot_general(m01, p, NN, preferred_element_type=F32)
    return dot(hi) + dot(mid) + dot(lo)


def _row_block(rows, cap):
    if rows <= cap:
        return rows
    best = None
    for t in range(16, cap + 1, 16):
        if rows % t == 0:
            best = t
    assert best is not None, rows
    return best


def _mm(name, a, b, *, ta=False, tb=False, out_dtype=F32, res=None, n=None, b_off=0, b_gathered=False,
        out_gathered=None, scatter=(), a_stacked=False, b_stacked=False, col_scale=None):
    if a_stacked:
        assert not ta
        m, k = a.shape[1], a.shape[0] * a.shape[2]
    else:
        m, k = (a.shape[1], a.shape[0]) if ta else a.shape
    if b_gathered:
        rh, cw = b.shape[2], b.shape[3]
        n_full = 2 * rh if tb else 4 * cw
    elif b_stacked:
        assert not tb
        n_full = b.shape[0] * b.shape[2]
    else:
        n_full = b.shape[0] if tb else b.shape[1]
    n = n_full if n is None else n
    tm, tn, tk = min(MM_TM, m), min(MM_TN, n), min(MM_TK, k)
    b_both = b_gathered and not tb and tk == 2 * rh
    if b_gathered and not b_both:
        tn, tk = (min(tn, rh), min(tk, cw)) if tb else (min(tn, cw), min(tk, rh))
    if b_both:
        tn = min(tn, cw)
    if out_gathered:
        tm, tn = min(tm, out_gathered[0]), min(tn, out_gathered[1])
    if a_stacked:
        tk = min(tk, a.shape[2])
    if b_stacked:
        tn = min(tn, b.shape[2])
    assert m % tm == 0 and n % tn == 0 and k % tk == 0 and b_off % tn == 0
    nk = k // tk
    jo = b_off // tn
    if a_stacked:
        pa = a.shape[2] // tk
        a_spec = pl.BlockSpec((1, tm, tk), lambda i, j, kk: (kk // pa, i, kk % pa))
    elif ta:
        a_spec = pl.BlockSpec((tk, tm), lambda i, j, kk: (kk, i))
    else:
        a_spec = pl.BlockSpec((tm, tk), lambda i, j, kk: (i, kk))
    if b_gathered and tb:
        pr, pc = rh // tn, cw // tk
        b_spec = pl.BlockSpec((1, 1, tn, tk), lambda i, j, kk: (j // pr, kk // pc, j % pr, kk % pc))
    elif b_both:
        pc = cw // tn
        b_spec = pl.BlockSpec((2, 1, rh, tn), lambda i, j, kk: (0, j // pc, 0, j % pc))
    elif b_gathered:
        pr, pc = rh // tk, cw // tn
        b_spec = pl.BlockSpec((1, 1, tk, tn), lambda i, j, kk: (kk // pr, j // pc, kk % pr, j % pc))
    elif b_stacked:
        pb = b.shape[2] // tn
        b_spec = pl.BlockSpec((1, tk, tn), lambda i, j, kk: (j // pb, kk, j % pb))
    elif tb:
        b_spec = pl.BlockSpec((tn, tk), lambda i, j, kk: (j + jo, kk))
    else:
        b_spec = pl.BlockSpec((tk, tn), lambda i, j, kk: (kk, j + jo))
    o_spec = pl.BlockSpec((tm, tn), lambda i, j, kk: (i, j))
    if out_gathered:
        assert m == 2 * out_gathered[0] and n == 4 * out_gathered[1] and res is None
        qr, qc = out_gathered[0] // tm, out_gathered[1] // tn
        o_spec = pl.BlockSpec((1, 1, tm, tn), lambda i, j, kk: (i // qr, j // qc, i % qr, j % qc))
    dn = (((0 if ta else 1,), (1 if tb else 0,)), ((), ()))
    has_res = res is not None
    ns = len(scatter)
    grid = (m // tm, n // tn, nk)

    def body(*refs):
        a_ref, b_ref = refs[:2]
        r_ref = refs[2] if has_res else None
        base = 2 + int(has_res)
        p_refs = refs[base:base + ns]
        o_ref = refs[base + ns]
        got_refs = refs[base + ns + 1:base + 2 * ns + 1]
        acc = refs[base + 2 * ns + 1]
        sems = refs[base + 2 * ns + 2:]
        kk = pl.program_id(2)
        if ns:
            first = jnp.logical_and(jnp.logical_and(pl.program_id(0) == 0, pl.program_id(1) == 0), kk == 0)
            last = jnp.logical_and(jnp.logical_and(pl.program_id(0) == grid[0] - 1, pl.program_id(1) == grid[1] - 1),
                                   kk == nk - 1)

            @pl.when(first)
            def _():
                _scatter_phase(0, p_refs, got_refs, sems)

        av = a_ref[0] if a_stacked else a_ref[...]
        if b_both:
            p = (lax.dot_general(av[:, :rh].astype(BF16), b_ref[0, 0].astype(BF16), dn, preferred_element_type=F32)
                 + lax.dot_general(av[:, rh:].astype(BF16), b_ref[1, 0].astype(BF16), dn, preferred_element_type=F32))
        else:
            bv = b_ref[0, 0] if b_gathered else (b_ref[0] if b_stacked else b_ref[...])
            p = lax.dot_general(av.astype(BF16), bv.astype(BF16), dn, preferred_element_type=F32)

        def finish(total):
            if r_ref is not None:
                total = total + r_ref[...]
            if col_scale is not None:
                total = total * jnp.where(pl.program_id(1) < col_scale[0] // tn, col_scale[1], 1.0)
            if out_gathered:
                o_ref[0, 0] = total.astype(out_dtype)
            else:
                o_ref[...] = total.astype(out_dtype)

        if nk == 1:
            finish(p)
        else:
            @pl.when(kk == 0)
            def _():
                acc[...] = p

            @pl.when(jnp.logical_and(kk > 0, kk < nk - 1))
            def _():
                acc[...] += p

            @pl.when(kk == nk - 1)
            def _():
                finish(acc[...] + p)

        if ns:
            @pl.when(last)
            def _():
                _scatter_phase(1, p_refs, got_refs, sems)

    ins = [a, b] + ([res] if has_res else []) + list(scatter)
    in_specs = [a_spec, b_spec] + ([o_spec] if has_res else []) + [ANY] * ns
    o_shape = jax.ShapeDtypeStruct((2, 4) + tuple(out_gathered) if out_gathered else (m, n), out_dtype)
    out = _call(
        body, name=name, grid=grid, in_specs=in_specs, out_specs=[o_spec] + [ANY] * ns,
        out_shape=[o_shape] + [jax.ShapeDtypeStruct(p.shape, p.dtype) for p in scatter],
        scratch_shapes=[pltpu.VMEM((tm, tn) if nk > 1 else (8, 128), F32)] + (_scatter_sems(ns) if ns else []),
        compiler_params=_cp(("arbitrary",) * 3 if ns else ("parallel", "parallel", "arbitrary")),
    )(*ins)
    return out if ns else out[0]


def _rms_fwd(name, x, g):
    s, d = x.shape
    tm = min(ROW_BLOCK, s)

    def body(x_ref, g_ref, h_ref):
        xv = x_ref[...]
        r = lax.rsqrt(jnp.mean(xv * xv, axis=-1, keepdims=True) + EPS)
        h_ref[...] = (xv * r * g_ref[...]).astype(BF16)

    row = pl.BlockSpec((tm, d), lambda i: (i, 0))
    vec = pl.BlockSpec((1, d), lambda i: (0, 0))
    return _call(body, name=name, grid=(s // tm,), in_specs=[row, vec], out_specs=row,
                 out_shape=jax.ShapeDtypeStruct((s, d), BF16), compiler_params=_cp(("parallel",)))(x, g)


def _rms_bwd(name, x, g, dh, dres):
    s, d = x.shape
    tm = min(ROW_BLOCK, s)

    def body(x_ref, g_ref, dh_ref, dres_ref, dx_ref, dg_ref):
        @pl.when(pl.program_id(0) == 0)
        def _():
            dg_ref[...] = jnp.zeros_like(dg_ref)

        xv = x_ref[...]
        r = lax.rsqrt(jnp.mean(xv * xv, axis=-1, keepdims=True) + EPS)
        xn = xv * r
        dhv = dh_ref[...]
        dxn = dhv * g_ref[...]
        dx_ref[...] = dres_ref[...] + r * (dxn - xn * jnp.mean(dxn * xn, axis=-1, keepdims=True))
        dg_ref[...] += jnp.sum(dhv * xn, axis=0, keepdims=True)

    row = pl.BlockSpec((tm, d), lambda i: (i, 0))
    vec = pl.BlockSpec((1, d), lambda i: (0, 0))
    return _call(body, name=name, grid=(s // tm,), in_specs=[row, vec, row, row], out_specs=[row, vec],
                 out_shape=[jax.ShapeDtypeStruct((s, d), F32), jax.ShapeDtypeStruct((1, d), F32)],
                 compiler_params=_cp(("arbitrary",)))(x, g, dh, dres)


def _loss_head(name, x, tgt, g):
    s, d = x.shape
    tm = min(ROW_BLOCK, s)
    nb = s // tm

    def body(x_ref, t_ref, g_ref, dx_ref, dg_ref, loss_ref, lacc):
        i = pl.program_id(0)

        @pl.when(i == 0)
        def _():
            dg_ref[...] = jnp.zeros_like(dg_ref)
            lacc[...] = jnp.zeros_like(lacc)

        xv = x_ref[...]
        gv = g_ref[...]
        r = lax.rsqrt(jnp.mean(xv * xv, axis=-1, keepdims=True) + EPS)
        xn = xv * r
        err = xn * gv - t_ref[...]
        lacc[...] += jnp.sum(err * err, axis=0, keepdims=True)
        dout = err * (1.0 / d)
        dg_ref[...] += jnp.sum(dout * xn, axis=0, keepdims=True)
        dxn = dout * gv
        dx_ref[...] = r * (dxn - xn * jnp.mean(dxn * xn, axis=-1, keepdims=True))

        @pl.when(i == nb - 1)
        def _():
            total = jnp.sum(lacc[...], axis=1, keepdims=True) * (0.5 / d)
            loss_ref[...] = jnp.broadcast_to(total, loss_ref.shape)

    row = pl.BlockSpec((tm, d), lambda i: (i, 0))
    vec = pl.BlockSpec((1, d), lambda i: (0, 0))
    one = pl.BlockSpec((1, 128), lambda i: (0, 0))
    return _call(body, name=name, grid=(nb,), in_specs=[row, row, vec], out_specs=[row, vec, one],
                 out_shape=[jax.ShapeDtypeStruct((s, d), F32), jax.ShapeDtypeStruct((1, d), F32),
                            jax.ShapeDtypeStruct((1, 128), F32)],
                 scratch_shapes=[pltpu.VMEM((1, d), F32)], compiler_params=_cp(("arbitrary",)))(x, tgt, g)


def _fox_gate_fwd(name, fl, bf, w, rep_scale=1.0):
    s = fl.shape[0]
    tb = min(GATE_BLOCK, s)

    def body(fl_ref, bf_ref, c_ref, crep_ref, carry):
        @pl.when(pl.program_id(0) == 0)
        def _():
            carry[...] = jnp.zeros_like(carry)

        z = fl_ref[...] + bf_ref[...]
        lf = jnp.minimum(z, 0.0) - jnp.log(1.0 + jnp.exp(-jnp.abs(z)))
        rows = lax.broadcasted_iota(jnp.int32, (tb, tb), 0)
        cols = lax.broadcasted_iota(jnp.int32, (tb, tb), 1)
        tri = (rows >= cols).astype(BF16)
        cs = _dot_exact(tri, lf) + carry[...]
        c_ref[...] = cs
        carry[...] = c_ref[tb - 1:tb, :]
        sel_r = lax.broadcasted_iota(jnp.int32, (128, w), 0)
        sel_c = lax.broadcasted_iota(jnp.int32, (128, w), 1)
        sel = (sel_r == sel_c // HEAD_DIM).astype(BF16)
        crep_ref[...] = _dot_exact(sel, cs * rep_scale, right=True)

    blk = pl.BlockSpec((tb, 128), lambda i: (i, 0))
    return _call(body, name=name, grid=(s // tb,),
                 in_specs=[blk, pl.BlockSpec((1, 128), lambda i: (0, 0))],
                 out_specs=[blk, pl.BlockSpec((tb, w), lambda i: (i, 0))],
                 out_shape=[jax.ShapeDtypeStruct((s, 128), F32), jax.ShapeDtypeStruct((s, w), F32)],
                 scratch_shapes=[pltpu.VMEM((1, 128), F32)], compiler_params=_cp(("arbitrary",)))(fl, bf)


def _fox_gate_bwd(name, drow, dcol, fl, bf):
    s = fl.shape[0]
    tb = min(GATE_BLOCK, s)
    nb = s // tb

    def body(dr_ref, dc_ref, fl_ref, bf_ref, dfl_ref, dbf_ref, carry, tmp):
        @pl.when(pl.program_id(0) == 0)
        def _():
            carry[...] = jnp.zeros_like(carry)
            dbf_ref[...] = jnp.zeros_like(dbf_ref)

        rows = lax.broadcasted_iota(jnp.int32, (tb, tb), 0)
        cols = lax.broadcasted_iota(jnp.int32, (tb, tb), 1)
        triu = (rows <= cols).astype(BF16)
        dlf = _dot_exact(triu, dr_ref[...] + dc_ref[...]) + carry[...]
        tmp[...] = dlf
        carry[...] = tmp[0:1, :]
        z = fl_ref[...] + bf_ref[...]
        dfl = dlf * (1.0 / (1.0 + jnp.exp(z)))
        dfl_ref[...] = dfl
        dbf_ref[...] += jnp.sum(dfl, axis=0, keepdims=True)

    blk = pl.BlockSpec((tb, 128), lambda i: (nb - 1 - i, 0))
    vec = pl.BlockSpec((1, 128), lambda i: (0, 0))
    return _call(body, name=name, grid=(nb,), in_specs=[blk, blk, blk, vec], out_specs=[blk, vec],
                 out_shape=[jax.ShapeDtypeStruct((s, 128), F32), jax.ShapeDtypeStruct((1, 128), F32)],
                 scratch_shapes=[pltpu.VMEM((1, 128), F32), pltpu.VMEM((tb, 128), F32)],
                 compiler_params=_cp(("arbitrary",)))(drow, dcol, fl, bf)


LOG2E = 1.4426950408889634
ATT_SCALE = HEAD_DIM ** -0.5
Q_FOLD = ATT_SCALE * LOG2E


def _attn_fwd(name, qkv, c_rep, gate, h_count, gather=()):
    s = qkv.shape[0]
    w = h_count * HEAD_DIM
    t = min(ATT_BLOCK, s)
    hp = 2 if h_count % 2 == 0 else 1
    wb = hp * HEAD_DIM
    ng = len(gather)
    nh, nq = h_count // hp, s // t

    def body(*refs):
        q_ref, k_ref, v_ref, c_ref, gate_ref = refs[:5]
        h_refs = refs[5:5 + ng]
        o_ref, lse_ref, y_ref = refs[5 + ng:8 + ng]
        big_refs = refs[8 + ng:8 + 2 * ng]
        m_s, l_s, acc_s = refs[8 + 2 * ng:11 + 2 * ng]
        sems = refs[11 + 2 * ng:]
        hs = pl.program_id(0)
        qi = pl.program_id(1)
        if ng:
            @pl.when(jnp.logical_and(hs == 0, qi == 0))
            def _():
                _ag_phase(0, h_refs, big_refs, sems)

            @pl.when(jnp.logical_and(hs == nh // 2, qi == 0))
            def _():
                _ag_phase(1, h_refs, big_refs, sems)

        m_s[...] = jnp.full(m_s.shape, -jnp.inf, F32)
        l_s[...] = jnp.zeros_like(l_s)
        acc_s[...] = jnp.zeros_like(acc_s)

        def step(kb, masked):
            off = pl.multiple_of(kb * t, t)
            for a in range(hp):
                cols_a = slice(a * HEAD_DIM, (a + 1) * HEAD_DIM)
                kk = k_ref[pl.ds(off, t), cols_a]
                vv = v_ref[pl.ds(off, t), cols_a]
                cc = jnp.tile(c_ref[pl.ds(off, t), cols_a], (1, t // HEAD_DIM))
                st = lax.dot_general(kk, q_ref[:, cols_a], NT, preferred_element_type=F32) - cc
                if masked:
                    rows = lax.broadcasted_iota(jnp.int32, (t, t), 0)
                    cols = lax.broadcasted_iota(jnp.int32, (t, t), 1)
                    st = jnp.where(cols >= rows, st, -jnp.inf)
                m_prev = m_s[a]
                m_new = jnp.maximum(m_prev, jnp.max(st, axis=0, keepdims=True))
                pt = jnp.exp2(st - m_new)
                alpha = jnp.exp2(m_prev - m_new)
                l_s[a] = alpha * l_s[a] + jnp.sum(pt, axis=0, keepdims=True)
                acc_s[a] = alpha * acc_s[a] + lax.dot_general(vv, pt.astype(BF16), TN, preferred_element_type=F32)
                m_s[a] = m_new

        def loop_body(kb, carry):
            step(kb, False)
            return carry

        lax.fori_loop(0, qi, loop_body, 0)
        step(qi, True)
        for a in range(hp):
            l = l_s[a]
            cols_a = slice(a * HEAD_DIM, (a + 1) * HEAD_DIM)
            oa = (acc_s[a] / l).T
            o_ref[:, cols_a] = oa
            g = gate_ref[:, cols_a]
            y_ref[:, cols_a] = (oa * (g * _sigmoid(g))).astype(BF16)
            lse_ref[a] = m_s[a] + jnp.log2(l)

        if ng:
            @pl.when(jnp.logical_and(hs == nh - 1, qi == nq - 1))
            def _():
                _ag_phase(2, h_refs, big_refs, sems)

    blk = lambda off: pl.BlockSpec((t, wb), lambda h, i: (i, off + h))
    whole = lambda off: pl.BlockSpec((s, wb), lambda h, i: (0, off + h))
    rowv = pl.BlockSpec((hp, 1, t), lambda h, i: (h, 0, i))
    return _call(
        body, name=name, grid=(nh, nq),
        in_specs=[blk(0), whole(nh), whole(2 * nh), whole(0), blk(0)] + [ANY] * ng,
        out_specs=[blk(0), rowv, blk(0)] + [ANY] * ng,
        out_shape=[jax.ShapeDtypeStruct((s, w), F32), jax.ShapeDtypeStruct((h_count, 1, s), F32),
                   jax.ShapeDtypeStruct((s, w), BF16)] + _ag_out_shapes(gather),
        scratch_shapes=[pltpu.VMEM((hp, 1, t), F32), pltpu.VMEM((hp, 1, t), F32), pltpu.VMEM((hp, HEAD_DIM, t), F32)]
        + (_ag_sems(ng) if ng else []),
        compiler_params=_cp(("arbitrary", "arbitrary") if ng else ("parallel", "arbitrary")),
    )(qkv, qkv, qkv, c_rep, gate, *gather)


def _attn_bwd(name, qkv, do, lse_row, delta_row, c_rep, h_count, scatter=()):
    s = qkv.shape[0]
    w = h_count * HEAD_DIM
    t = min(ATT_BLOCK, s)
    nq = s // t
    scale = HEAD_DIM ** -0.5
    ns = len(scatter)

    def body(*refs):
        k_ref, v_ref, q_ref, do_ref, lse_ref, dl_ref, c_ref = refs[:7]
        p_refs = refs[7:7 + ns]
        dk_ref, dv_ref, dc_ref, dq_ref, rs_ref = refs[7 + ns:12 + ns]
        got_refs = refs[12 + ns:12 + 2 * ns]
        dk_s, dv_s, dc_s, dq_s, rs_s = refs[12 + 2 * ns:17 + 2 * ns]
        sems = refs[17 + 2 * ns:]
        kj = pl.program_id(1)
        if ns:
            @pl.when(jnp.logical_and(pl.program_id(0) == 0, kj == 0))
            def _():
                _scatter_phase(0, p_refs, got_refs, sems)

        kk = k_ref[...]
        vv = v_ref[...]
        ccol = jnp.tile(c_ref[...], (1, t // HEAD_DIM))
        dk_s[...] = jnp.zeros_like(dk_s)
        dv_s[...] = jnp.zeros_like(dv_s)
        dc_s[...] = jnp.zeros_like(dc_s)

        @pl.when(kj == 0)
        def _():
            dq_s[...] = jnp.zeros_like(dq_s)
            rs_s[...] = jnp.zeros_like(rs_s)

        def step(qb, masked):
            off = pl.multiple_of(qb * t, t)
            q = q_ref[pl.ds(off, t), :]
            dov = do_ref[pl.ds(off, t), :]
            st = lax.dot_general(kk, q, NT, preferred_element_type=F32) - ccol
            pt = jnp.exp2(st - lse_ref[0, :, pl.ds(off, t)])
            if masked:
                rows = lax.broadcasted_iota(jnp.int32, (t, t), 0)
                cols = lax.broadcasted_iota(jnp.int32, (t, t), 1)
                pt = jnp.where(cols >= rows, pt, 0.0)
            dv_s[...] += lax.dot_general(pt.astype(BF16), dov, NN, preferred_element_type=F32)
            dpt = lax.dot_general(vv, dov, NT, preferred_element_type=F32)
            dst = pt * (dpt - dl_ref[0, :, pl.ds(off, t)])
            dstb = dst.astype(BF16)
            dk_s[...] += lax.dot_general(dstb, q, NN, preferred_element_type=F32)
            dc_s[...] += jnp.sum(dst, axis=1, keepdims=True)
            dq_s[:, pl.ds(off, t)] += lax.dot_general(kk, dstb, TN, preferred_element_type=F32)
            rs_s[:, pl.ds(off, t)] += jnp.sum(dst, axis=0, keepdims=True)

        step(kj, True)

        def loop_body(qb, carry):
            step(qb, False)
            return carry

        lax.fori_loop(kj + 1, nq, loop_body, 0)
        dk_ref[...] = (dk_s[...] * (1.0 / LOG2E)).astype(BF16)
        dv_ref[...] = dv_s[...].astype(BF16)
        dc_ref[...] = jnp.broadcast_to(-dc_s[...], dc_ref.shape)

        @pl.when(kj == nq - 1)
        def _():
            for b in range(nq):
                dq_ref[b * t:(b + 1) * t, :] = (dq_s[:, b * t:(b + 1) * t] * scale).T.astype(BF16)
            rs_ref[0] = rs_s[...]

        if ns:
            @pl.when(jnp.logical_and(pl.program_id(0) == h_count - 1, kj == nq - 1))
            def _():
                _scatter_phase(1, p_refs, got_refs, sems)

    hh = h_count
    blk = lambda off: pl.BlockSpec((t, HEAD_DIM), lambda h, j: (j, off + h))
    whole = pl.BlockSpec((s, HEAD_DIM), lambda h, j: (0, h))
    rowv = pl.BlockSpec((1, 1, s), lambda h, j: (h, 0, 0))
    return _call(
        body, name=name, grid=(hh, nq),
        in_specs=[blk(hh), blk(2 * hh), whole, whole, rowv, rowv, blk(0)] + [ANY] * ns,
        out_specs=[blk(0), blk(0), blk(0), whole, rowv] + [ANY] * ns,
        out_shape=[jax.ShapeDtypeStruct((s, w), BF16), jax.ShapeDtypeStruct((s, w), BF16),
                   jax.ShapeDtypeStruct((s, w), F32), jax.ShapeDtypeStruct((s, w), BF16),
                   jax.ShapeDtypeStruct((hh, 1, s), F32)] + [jax.ShapeDtypeStruct(p.shape, p.dtype) for p in scatter],
        scratch_shapes=[pltpu.VMEM((t, HEAD_DIM), F32), pltpu.VMEM((t, HEAD_DIM), F32), pltpu.VMEM((t, 1), F32),
                        pltpu.VMEM((HEAD_DIM, s), F32), pltpu.VMEM((1, s), F32)] + (_scatter_sems(ns) if ns else []),
        compiler_params=_cp(("arbitrary", "arbitrary") if ns else ("parallel", "arbitrary")),
    )(qkv, qkv, qkv, do, lse_row, delta_row, c_rep, *scatter)


def _fox_pre_bwd(name, dy, o, gate, exchange=()):
    s, w = o.shape
    tm = min(ROW_BLOCK, s)
    ne = len(exchange)
    nb = s // tm

    def body(*refs):
        dy_ref, o_ref, g_ref = refs[:3]
        x_refs = refs[3:3 + ne]
        do_ref, dg_ref, dl_ref = refs[3 + ne:6 + ne]
        got_refs = refs[6 + ne:6 + 2 * ne]
        sems = refs[6 + 2 * ne:]
        if ne:
            @pl.when(pl.program_id(0) == 0)
            def _():
                _sibling_phase(0, x_refs, got_refs, sems)

            @pl.when(pl.program_id(0) == nb - 1)
            def _():
                _sibling_phase(1, x_refs, got_refs, sems)

        g = g_ref[...]
        sg = _sigmoid(g)
        dyv = dy_ref[...]
        ov = o_ref[...]
        dov = dyv * (g * sg)
        do_ref[...] = dov.astype(BF16)
        dg_ref[...] = (dyv * ov * (sg * (1.0 + g * (1.0 - sg)))).astype(BF16)
        sel_r = lax.broadcasted_iota(jnp.int32, (w, 128), 0)
        sel_c = lax.broadcasted_iota(jnp.int32, (w, 128), 1)
        sel = (sel_r // HEAD_DIM == sel_c).astype(BF16)
        dl_ref[...] = _dot_exact(sel, dov * ov, right=True)

    row = pl.BlockSpec((tm, w), lambda i: (i, 0))
    lanes = pl.BlockSpec((tm, 128), lambda i: (i, 0))
    return _call(body, name=name, grid=(nb,), in_specs=[row, row, row] + [ANY] * ne,
                 out_specs=[row, row, lanes] + [ANY] * ne,
                 out_shape=[jax.ShapeDtypeStruct((s, w), BF16), jax.ShapeDtypeStruct((s, w), BF16),
                            jax.ShapeDtypeStruct((s, 128), F32)] + _sibling_out_shapes(exchange),
                 scratch_shapes=_sibling_sems(ne) if ne else [],
                 compiler_params=_cp(("arbitrary",) if ne else ("parallel",)))(dy, o, gate, *exchange)


def _hgrn_chunk_terms(q_c, k_c, b_c, b_s, base):
    nsub = HGRN_CHUNK // HGRN_SUB
    refs = [jnp.zeros((1, HEAD_DIM), F32)]
    for i in range(1, nsub):
        r0 = base + i * HGRN_SUB - 1
        refs.append(b_s[r0:r0 + 1, :])
    rfull = jnp.concatenate([jnp.broadcast_to(r, (HGRN_SUB, HEAD_DIM)) for r in refs], axis=0)
    eq = jnp.exp(b_c - rfull)
    qe = q_c * eq
    es = [jnp.exp(jnp.minimum(r - b_c, EXP_CLAMP)) for r in refs]
    kes = [(k_c * e).astype(BF16) for e in es]
    return eq, qe, es, kes


def _chunk_cumsum(x, reverse=False):
    cc = HGRN_CHUNK
    nc = x.shape[0] // cc
    rows = lax.broadcasted_iota(jnp.int32, (cc, cc), 0)
    cols = lax.broadcasted_iota(jnp.int32, (cc, cc), 1)
    tri = ((rows <= cols) if reverse else (rows >= cols)).astype(BF16)
    wide = jnp.concatenate([x[n * cc:(n + 1) * cc] for n in range(nc)], axis=1)
    res = _dot_exact(tri, wide)
    return jnp.concatenate([res[:, n * HEAD_DIM:(n + 1) * HEAD_DIM] for n in range(nc)], axis=0)


def _hgrn_block_pre(q_ref, f_ref, lbl_ref, b_s):
    lb = _sigmoid(lbl_ref[1:2, :] - lbl_ref[0:1, :])
    qr = q_ref[...]
    sq = _sigmoid(qr)
    q = qr * sq
    fz = f_ref[...]
    sg = _sigmoid(fz)
    f = lb + (1.0 - lb) * sg
    g = jnp.log(f)
    k = (1.0 - lb) * (1.0 / (1.0 + jnp.exp(fz)))
    b = _chunk_cumsum(g)
    b_s[...] = b
    return lb, qr, sq, q, sg, f, k, b


def _hgrn_fwd(name, p1, lbl, onorm, h_count):
    s = p1.shape[0]
    w = h_count * HEAD_DIM
    t = min(HGRN_BLOCK, s)
    nc = t // HGRN_CHUNK
    nsub = HGRN_CHUNK // HGRN_SUB
    cc = HGRN_CHUNK

    def body(q_ref, f_ref, i_ref, g_ref, lbl_ref, on_ref, y_ref, o_ref, st_ref, state_s, b_s):
        @pl.when(pl.program_id(1) == 0)
        def _():
            state_s[...] = jnp.zeros_like(state_s)

        lb, qr, sq, q, sg, f, k, b = _hgrn_block_pre(q_ref, f_ref, lbl_ref, b_s)
        v = i_ref[...]
        r64 = lax.broadcasted_iota(jnp.int32, (cc, cc), 0)
        c64 = lax.broadcasted_iota(jnp.int32, (cc, cc), 1)
        for n in range(nc):
            sl = slice(n * cc, (n + 1) * cc)
            q_c, k_c, v_c, b_c = q[sl], k[sl], v[sl], b[sl]
            bl = b_s[n * cc + cc - 1:n * cc + cc, :]
            eq, qe, es, kes = _hgrn_chunk_terms(q_c, k_c, b_c, b_s, n * cc)
            qeb = qe.astype(BF16)
            a = jnp.concatenate(
                [lax.dot_general(qeb[i * HGRN_SUB:(i + 1) * HGRN_SUB], kes[i], NT, preferred_element_type=F32)
                 for i in range(nsub)], axis=0)
            a = jnp.where(r64 >= c64, a, 0.0)
            st = state_s[...]
            st_ref[0, n] = st
            inter = _dot(q_c * jnp.exp(b_c), st, NT)
            intra = _dot(a, v_c)
            o_ref[sl, :] = inter + intra
            kb = k_c * jnp.exp(bl - b_c)
            state_s[...] = st * jnp.exp(bl) + _dot(v_c, kb, TN)
        o = o_ref[...]
        rr = lax.rsqrt(jnp.mean(o * o, axis=-1, keepdims=True) + EPS)
        gate = g_ref[...]
        y_ref[...] = ((o * rr) * on_ref[...] * (gate * _sigmoid(gate))).astype(BF16)

    hh = h_count
    blk = lambda off: pl.BlockSpec((t, HEAD_DIM), lambda h, i: (i, off + h))
    return _call(
        body, name=name, grid=(hh, s // t),
        in_specs=[blk(0), blk(hh), blk(2 * hh), blk(3 * hh),
                  pl.BlockSpec((2, HEAD_DIM), lambda h, i: (0, h)), pl.BlockSpec((1, HEAD_DIM), lambda h, i: (0, h))],
        out_specs=[blk(0), blk(0), pl.BlockSpec((1, nc, HEAD_DIM, HEAD_DIM), lambda h, i: (h, i, 0, 0))],
        out_shape=[jax.ShapeDtypeStruct((s, w), BF16), jax.ShapeDtypeStruct((s, w), F32),
                   jax.ShapeDtypeStruct((hh, s // cc, HEAD_DIM, HEAD_DIM), F32)],
        scratch_shapes=[pltpu.VMEM((HEAD_DIM, HEAD_DIM), F32), pltpu.VMEM((t, HEAD_DIM), F32)],
        compiler_params=_cp(("parallel", "arbitrary")),
    )(p1, p1, p1, p1, lbl, onorm)


def _hgrn_bwd(name, p1, lbl, onorm, o, dy, states, h_count):
    s = p1.shape[0]
    w = h_count * HEAD_DIM
    t = min(HGRN_BLOCK, s)
    nb = s // t
    nc = t // HGRN_CHUNK
    nsub = HGRN_CHUNK // HGRN_SUB
    cc = HGRN_CHUNK

    def body(q_ref, f_ref, i_ref, g_ref, lbl_ref, on_ref, o_ref, dy_ref, st_ref,
             dp_ref, dlog_ref, don_ref,
             dst_s, b_s, do_s, dqs, dks, dbs, exs):
        @pl.when(pl.program_id(1) == 0)
        def _():
            dst_s[...] = jnp.zeros_like(dst_s)
            dlog_ref[...] = jnp.zeros_like(dlog_ref)
            don_ref[...] = jnp.zeros_like(don_ref)

        lb, qr, sq, q, sg, f, k, b = _hgrn_block_pre(q_ref, f_ref, lbl_ref, b_s)
        v = i_ref[...]

        ov = o_ref[...]
        rr = lax.rsqrt(jnp.mean(ov * ov, axis=-1, keepdims=True) + EPS)
        on = ov * rr
        gate = g_ref[...]
        sgt = _sigmoid(gate)
        silu = gate * sgt
        dyv = dy_ref[...]
        gain = on_ref[...]
        dp_ref[3] = (dyv * on * gain * (sgt * (1.0 + gate * (1.0 - sgt)))).astype(BF16)
        don_ref[...] += jnp.sum(dyv * on * silu, axis=0, keepdims=True)
        d_on = dyv * gain * silu
        do_s[...] = rr * (d_on - on * jnp.mean(d_on * on, axis=-1, keepdims=True))

        r64 = lax.broadcasted_iota(jnp.int32, (cc, cc), 0)
        c64 = lax.broadcasted_iota(jnp.int32, (cc, cc), 1)
        upper = r64 <= c64
        for n in reversed(range(nc)):
            sl = slice(n * cc, (n + 1) * cc)
            q_c, k_c, v_c, b_c = q[sl], k[sl], v[sl], b[sl]
            do_c = do_s[sl, :]
            bl = b_s[n * cc + cc - 1:n * cc + cc, :]
            ebl = jnp.exp(bl)
            eq, qe, es, kes = _hgrn_chunk_terms(q_c, k_c, b_c, b_s, n * cc)
            qeb = qe.astype(BF16)
            dob = do_c.astype(BF16)
            vb = v_c.astype(BF16)
            st = st_ref[0, n]
            dstn = dst_s[...]
            qb_ = q_c * jnp.exp(b_c)
            kb_ = k_c * jnp.exp(bl - b_c)
            at = jnp.zeros((cc, cc), F32)
            for i in range(nsub):
                blk_i = (c64 // HGRN_SUB) == i
                at = at + jnp.where(blk_i, lax.dot_general(kes[i], qeb, NT, preferred_element_type=F32), 0.0)
            at = jnp.where(upper, at, 0.0)
            dv = _dot(at, dob) + _dot(kb_, dstn, NT)
            dqb = _dot(dob, st)
            dkb = _dot(vb, dstn)
            da = jnp.where(r64 >= c64, lax.dot_general(dob, vb, NT, preferred_element_type=F32), 0.0)
            dat = jnp.where(upper, lax.dot_general(vb, dob, NT, preferred_element_type=F32), 0.0)
            dab = da.astype(BF16)
            dq_raw = jnp.concatenate(
                [lax.dot_general(dab[i * HGRN_SUB:(i + 1) * HGRN_SUB], kes[i], NN, preferred_element_type=F32)
                 for i in range(nsub)], axis=0)
            db = qb_.astype(BF16).astype(F32) * dqb + qeb.astype(F32) * dq_raw
            dkbk = dkb * kb_.astype(BF16).astype(F32)
            db = db - dkbk
            dk_in = jnp.zeros((cc, HEAD_DIM), F32)
            for i in range(nsub):
                blk_i = (c64 // HGRN_SUB) == i
                dk_raw = _dot(jnp.where(blk_i, dat, 0.0), qeb)
                dk_in = dk_in + dk_raw * es[i]
                db = db - kes[i].astype(F32) * dk_raw
            dqs[sl, :] = dqb * jnp.exp(b_c) + dq_raw * eq
            dks[sl, :] = dkb * jnp.exp(bl - b_c) + dk_in
            dbs[sl, :] = db
            extra = jnp.sum(dkbk, axis=0, keepdims=True) + jnp.sum(dstn * st, axis=0, keepdims=True) * ebl
            exs[sl, :] = jnp.broadcast_to(extra, (cc, HEAD_DIM))
            dp_ref[2, sl, :] = dv.astype(BF16)
            dst_s[...] = dstn * ebl + _dot(dob, qb_, TN)

        dg = _chunk_cumsum(dbs[...], reverse=True) + exs[...]
        df = dg / f - dks[...]
        dp_ref[1] = (df * (1.0 - lb) * sg * (1.0 - sg)).astype(BF16)
        dp_ref[0] = (dqs[...] * (sq * (1.0 + qr * (1.0 - sq)))).astype(BF16)
        dlb = jnp.sum(df * (1.0 - sg), axis=0, keepdims=True) * (lb * (1.0 - lb))
        dlog_ref[0:1, :] += -dlb
        dlog_ref[1:2, :] += dlb

    hh = h_count
    blk = lambda off: pl.BlockSpec((t, HEAD_DIM), lambda h, i: (nb - 1 - i, off + h))
    two = pl.BlockSpec((2, HEAD_DIM), lambda h, i: (0, h))
    one = pl.BlockSpec((1, HEAD_DIM), lambda h, i: (0, h))
    tile = pltpu.VMEM((t, HEAD_DIM), F32)
    return _call(
        body, name=name, grid=(hh, nb),
        in_specs=[blk(0), blk(hh), blk(2 * hh), blk(3 * hh), two, one, blk(0), blk(0),
                  pl.BlockSpec((1, nc, HEAD_DIM, HEAD_DIM), lambda h, i: (h, nb - 1 - i, 0, 0))],
        out_specs=[pl.BlockSpec((4, t, HEAD_DIM), lambda h, i: (0, nb - 1 - i, h)), two, one],
        out_shape=[jax.ShapeDtypeStruct((4, s, w), BF16), jax.ShapeDtypeStruct((2, w), F32),
                   jax.ShapeDtypeStruct((1, w), F32)],
        scratch_shapes=[pltpu.VMEM((HEAD_DIM, HEAD_DIM), F32), tile, tile, tile, tile, tile, tile],
        compiler_params=_cp(("parallel", "arbitrary")),
    )(p1, p1, p1, p1, lbl, onorm, o, dy, states)


def _adamw(name, w, g, m, v):
    r, c = w.shape
    tr = r if r <= 128 else _row_block(r, 128)
    c1 = 1.0 - ADAM_B1 ** ADAM_STEP
    c2 = 1.0 - ADAM_B2 ** ADAM_STEP

    def body(w_ref, g_ref, m_ref, v_ref, d_ref, nm_ref, nv_ref):
        gv = g_ref[...]
        nm = ADAM_B1 * m_ref[...] + (1.0 - ADAM_B1) * gv
        nv = ADAM_B2 * v_ref[...] + (1.0 - ADAM_B2) * (gv * gv)
        nm_ref[...] = nm
        nv_ref[...] = nv
        d_ref[...] = -ADAM_LR * ((nm / c1) / (jnp.sqrt(nv / c2) + ADAM_EPS) + ADAM_WD * w_ref[...])

    blk = pl.BlockSpec((tr, c), lambda i: (i, 0))
    sh = jax.ShapeDtypeStruct((r, c), F32)
    return _call(body, name=name, grid=(r // tr,), in_specs=[blk] * 4, out_specs=[blk] * 3,
                 out_shape=[sh, sh, sh], compiler_params=_cp(("parallel",)))(w, g, m, v)


SLAB_BLOCK_BYTES = 2 * 1024 * 1024


def _slab_rows(r, c):
    return _row_block(r, max(16, SLAB_BLOCK_BYTES // (4 * c) // 16 * 16))


def _pair_add(name, g2, recv, core):
    _, nch, r, c = g2.shape
    tr = _slab_rows(r, c)

    grid_spec = pltpu.PrefetchScalarGridSpec(
        num_scalar_prefetch=1, grid=(nch, r // tr),
        in_specs=[pl.BlockSpec((1, 1, tr, c), lambda j, i, cr: (cr[0], j, i, 0)),
                  pl.BlockSpec((1, tr, c), lambda j, i, cr: (j, i, 0))],
        out_specs=pl.BlockSpec((1, tr, c), lambda j, i, cr: (j, i, 0)))

    def body(core_ref, a_ref, b_ref, o_ref):
        o_ref[...] = (a_ref[0] + b_ref[...]).astype(BF16)

    return _call(body, name=name, grid_spec=grid_spec, out_shape=jax.ShapeDtypeStruct((nch, r, c), BF16),
                 compiler_params=_cp(("parallel", "parallel")))(core, g2, recv)


def _sum_slots(name, x, core=None):
    n, r, c = x.shape
    tr = _slab_rows(r, c)

    def total(x_ref):
        acc = x_ref[0].astype(F32)
        for j in range(1, n):
            acc = acc + x_ref[j].astype(F32)
        return acc

    if core is None:
        def body(x_ref, o_ref):
            o_ref[...] = total(x_ref)

        return _call(body, name=name, grid=(r // tr,),
                     in_specs=[pl.BlockSpec((n, tr, c), lambda i: (0, i, 0))],
                     out_specs=pl.BlockSpec((tr, c), lambda i: (i, 0)),
                     out_shape=jax.ShapeDtypeStruct((r, c), F32), compiler_params=_cp(("parallel",)))(x)

    def body_half(core_ref, x_ref, o_ref):
        o_ref[0] = total(x_ref)

    grid_spec = pltpu.PrefetchScalarGridSpec(
        num_scalar_prefetch=1, grid=(r // tr,),
        in_specs=[pl.BlockSpec((n, tr, c), lambda i, cr: (0, i, 0))],
        out_specs=pl.BlockSpec((1, tr, c), lambda i, cr: (cr[0], i, 0)))
    return _call(body_half, name=name, grid_spec=grid_spec, out_shape=jax.ShapeDtypeStruct((2, r, c), F32),
                 compiler_params=_cp(("parallel",)))(core, x)


def _pos():
    return lax.axis_index("x"), lax.axis_index("y"), lax.axis_index("c")


def _flip(v, f):
    return (1 - v) if f else v


CHIP_FLIPS = ((0, 1), (1, 0), (1, 1))
DEV_FLIPS = tuple((fx, fy, fc) for fx in (0, 1) for fy in (0, 1) for fc in (0, 1))[1:]


def _remote(src, dst, ssem, rsem, dev):
    return pltpu.make_async_remote_copy(src_ref=src, dst_ref=dst, send_sem=ssem, recv_sem=rsem,
                                        device_id=dev, device_id_type=MESH)


def _ag_weights(name, halves):
    n = len(halves)

    def body(*refs):
        h_refs, big_refs, sems = refs[:n], refs[n:2 * n], refs[2 * n:]
        _ag_phase(0, h_refs, big_refs, sems)
        _ag_phase(1, h_refs, big_refs, sems)
        _ag_phase(2, h_refs, big_refs, sems)

    return _call(body, name=name, in_specs=[ANY] * n, out_specs=[ANY] * n,
                 out_shape=_ag_out_shapes(halves), scratch_shapes=_ag_sems(n))(*halves)


def _ag_out_shapes(halves):
    return [jax.ShapeDtypeStruct((2, 4) + h.shape, h.dtype) for h in halves]


def _ag_sems(n):
    return [pltpu.SemaphoreType.DMA((n,)), pltpu.SemaphoreType.DMA((7 * n,)), pltpu.SemaphoreType.DMA((7 * n,))]


def _ag_phase(phase, h_refs, big_refs, sems):
    lsem, ssem, rsem = sems
    n = len(h_refs)
    x, y, cc = _pos()
    me = 2 * x + y
    sib = (x, y, 1 - cc)
    peers = [(_flip(x, fx), _flip(y, fy), cc) for fx, fy in CHIP_FLIPS]
    chips = [2 * px + py for px, py, _ in peers]
    for a in range(n):
        mine = big_refs[a].at[cc, me]
        first = [pltpu.make_async_copy(h_refs[a], mine, lsem.at[a])]
        first += [_remote(h_refs[a], mine, ssem.at[7 * a + k], rsem.at[7 * a + k], peers[k]) for k in range(3)]
        first += [_remote(h_refs[a], mine, ssem.at[7 * a + 3], rsem.at[7 * a + 3], sib)]
        passed = []
        for k in range(3):
            blk = big_refs[a].at[cc, chips[k]]
            passed.append(_remote(blk, blk, ssem.at[7 * a + 4 + k], rsem.at[7 * a + 4 + k], sib))
        if phase == 0:
            for cp in first:
                cp.start()
        elif phase == 1:
            for k in range(3):
                blk = big_refs[a].at[cc, chips[k]]
                _remote(blk, blk, ssem.at[7 * a + k], rsem.at[7 * a + k], peers[k]).wait_recv()
                passed[k].start()
        else:
            theirs = big_refs[a].at[1 - cc, me]
            _remote(theirs, theirs, ssem.at[7 * a + 3], rsem.at[7 * a + 3], sib).wait_recv()
            for k in range(3):
                blk = big_refs[a].at[1 - cc, chips[k]]
                _remote(blk, blk, ssem.at[7 * a + 4 + k], rsem.at[7 * a + 4 + k], sib).wait_recv()
            first[0].wait()
            for cp in first[1:] + passed:
                cp.wait_send()


def _ag_devices(name, v):
    r, c = v.shape

    def body(v_ref, out_ref, lsem, ssem, rsem):
        x, y, cc = _pos()
        me = 4 * x + 2 * y + cc
        loc = pltpu.make_async_copy(v_ref, out_ref.at[me], lsem)
        loc.start()
        started = []
        for k, (fx, fy, fc) in enumerate(DEV_FLIPS):
            cp = _remote(v_ref, out_ref.at[me], ssem.at[k], rsem.at[k], (_flip(x, fx), _flip(y, fy), _flip(cc, fc)))
            cp.start()
            started.append(cp)
        for k, (fx, fy, fc) in enumerate(DEV_FLIPS):
            px, py, pc = _flip(x, fx), _flip(y, fy), _flip(cc, fc)
            blk = out_ref.at[4 * px + 2 * py + pc]
            _remote(blk, blk, ssem.at[k], rsem.at[k], (px, py, pc)).wait_recv()
        for cp in started:
            cp.wait_send()
        loc.wait()

    return _call(body, name=name, in_specs=[ANY], out_specs=ANY,
                 out_shape=jax.ShapeDtypeStruct((8, r, c), v.dtype),
                 scratch_shapes=[pltpu.SemaphoreType.DMA, pltpu.SemaphoreType.DMA((7,)), pltpu.SemaphoreType.DMA((7,))])(v)


def _sibling_other_half(name, g2s):
    n = len(g2s)

    def body(*refs):
        g_refs, out_refs, sems = refs[:n], refs[n:2 * n], refs[2 * n:]
        _sibling_phase(0, g_refs, out_refs, sems)
        _sibling_phase(1, g_refs, out_refs, sems)

    return _call(body, name=name, in_specs=[ANY] * n, out_specs=[ANY] * n,
                 out_shape=_sibling_out_shapes(g2s), scratch_shapes=_sibling_sems(n))(*g2s)


def _sibling_out_shapes(g2s):
    return [jax.ShapeDtypeStruct(g.shape[1:], g.dtype) for g in g2s]


def _sibling_sems(n):
    return [pltpu.SemaphoreType.DMA((n,)), pltpu.SemaphoreType.DMA((n,))]


def _sibling_phase(phase, g_refs, out_refs, sems):
    ssem, rsem = sems
    x, y, cc = _pos()
    for a in range(len(g_refs)):
        cp = _remote(g_refs[a].at[1 - cc], out_refs[a], ssem.at[a], rsem.at[a], (x, y, 1 - cc))
        if phase == 0:
            cp.start()
        else:
            cp.wait()


def _scatter_sems(n):
    return [pltpu.SemaphoreType.DMA((n,)), pltpu.SemaphoreType.DMA((3 * n,)), pltpu.SemaphoreType.DMA((3 * n,))]


def _scatter_phase(phase, p_refs, out_refs, sems):
    lsem, ssem, rsem = sems
    x, y, cc = _pos()
    me = 2 * x + y
    peers = [(_flip(x, fx), _flip(y, fy), cc) for fx, fy in CHIP_FLIPS]
    for a in range(len(p_refs)):
        loc = pltpu.make_async_copy(p_refs[a].at[me], out_refs[a].at[me], lsem.at[a])
        sends = [_remote(p_refs[a].at[2 * px + py], out_refs[a].at[me], ssem.at[3 * a + k], rsem.at[3 * a + k], peers[k])
                 for k, (px, py, _) in enumerate(peers)]
        if phase == 0:
            loc.start()
            for cp in sends:
                cp.start()
        else:
            for k, (px, py, _) in enumerate(peers):
                blk = out_refs[a].at[2 * px + py]
                _remote(blk, blk, ssem.at[3 * a + k], rsem.at[3 * a + k], peers[k]).wait_recv()
            loc.wait()
            for cp in sends:
                cp.wait_send()


def _sibling_join(name, bufs):
    n = len(bufs)

    def body(*refs):
        out_refs, (ssem, rsem) = refs[n:2 * n], refs[2 * n:]
        x, y, cc = _pos()
        sib = (x, y, 1 - cc)
        cps = [_remote(out_refs[a].at[cc], out_refs[a].at[cc], ssem.at[a], rsem.at[a], sib) for a in range(n)]
        for cp in cps:
            cp.start()
        for a in range(n):
            theirs = out_refs[a].at[1 - cc]
            _remote(theirs, theirs, ssem.at[a], rsem.at[a], sib).wait_recv()
        for cp in cps:
            cp.wait_send()

    return _call(body, name=name, in_specs=[ANY] * n, out_specs=[ANY] * n,
                 out_shape=[jax.ShapeDtypeStruct(b.shape, b.dtype) for b in bufs],
                 input_output_aliases={a: a for a in range(n)},
                 scratch_shapes=[pltpu.SemaphoreType.DMA((n,)), pltpu.SemaphoreType.DMA((n,))])(*bufs)


def _pad_lanes(a, width=128):
    return jnp.pad(a, ((0, 0), (0, width - a.shape[1])))


def kernel(x, norm_gains, fox_w_in, fox_b_f, hgrn_w_in, hgrn_lb_logits, hgrn_onorm, w_out, final_gain, loss_target, m_norm_gains, m_fox_w_in, m_fox_b_f, m_hgrn_w_in, m_hgrn_lb_logits, m_hgrn_onorm, m_w_out, m_final_gain, v_norm_gains, v_fox_w_in, v_fox_b_f, v_hgrn_w_in, v_hgrn_lb_logits, v_hgrn_onorm, v_w_out, v_final_gain):
    s, d = x.shape[1], x.shape[2]
    wq = w_out.shape[1]
    w = 4 * wq
    hh = w // HEAD_DIM
    fox_cols = fox_w_in.shape[2]
    assert 4 * fox_cols == 4 * w + hh and hgrn_w_in.shape[2] == w
    core = lax.axis_index("c")

    x0 = x[0]
    tgt = loss_target[0]

    def my_half(a):
        return lax.dynamic_index_in_dim(a, core, 0, keepdims=False).astype(BF16)

    (big_fox,) = _ag_weights("ag_weights", [my_half(fox_w_in.reshape(2, d // 2, fox_cols))])
    later_halves = [my_half(hgrn_w_in.reshape(2, d // 2, w)), my_half(w_out),
                    lax.dynamic_index_in_dim(hgrn_onorm.reshape(2, 1, wq // 2), core, 0, keepdims=False)]
    fox_chip = [big_fox[:, j].reshape(d, fox_cols) for j in range(4)]
    def fox_columns(a, b):
        out = []
        for j in range(4):
            lo, hi = max(a, j * fox_cols), min(b, (j + 1) * fox_cols)
            if lo < hi:
                out.append(fox_chip[j][:, lo - j * fox_cols:hi - j * fox_cols])
        return out

    w_main = jnp.concatenate(fox_columns(0, 3 * w) + fox_columns(3 * w + hh, 4 * w + hh), axis=1)
    w_fl = _pad_lanes(jnp.concatenate(fox_columns(3 * w, 3 * w + hh), axis=1))

    bf_pad = _pad_lanes(fox_b_f)
    g0, g1 = norm_gains[0:1], norm_gains[1:2]
    gf = final_gain.reshape(1, d)

    h0 = _rms_fwd("rms0_fwd", x0, g0)
    qkv = _mm("fox_qkv", h0, w_main, out_dtype=BF16, n=3 * w, col_scale=(w, Q_FOLD))
    gate0 = _mm("fox_gate", h0, w_main, n=w, b_off=3 * w)
    fl = _mm("fox_flogit", h0, w_fl)
    _, c_rep = _fox_gate_fwd("fox_cumsum", fl, bf_pad, w, rep_scale=LOG2E)
    o0, lse_row, y0, big_hgrn, big_out, big_onorm = _attn_fwd("fox_attn_fwd", qkv, c_rep, gate0, hh, gather=later_halves)
    onorm_full = big_onorm.transpose(1, 0, 2, 3).reshape(1, w)
    w_o = big_out.reshape(2, w, d)
    x1 = _mm("fox_out", y0, w_o[0], res=x0)
    h1 = _rms_fwd("rms1_fwd", x1, g1)
    p1 = _mm("hgrn_in", h1, big_hgrn, b_gathered=True)
    y1, o1, states = _hgrn_fwd("hgrn_fwd", p1, hgrn_lb_logits, onorm_full, hh)
    x2 = _mm("hgrn_out", y1, w_o[1], res=x1)
    dx2, d_gf, loss_tile = _loss_head("loss_head", x2, tgt, gf)

    dy1 = _mm("hgrn_out_dy", dx2, w_o[1], tb=True)
    d_wo1 = _mm("hgrn_out_dw", y1, dx2, ta=True)
    dp1, d_lbl, d_onorm = _hgrn_bwd("hgrn_bwd", p1, hgrn_lb_logits, onorm_full, o1, dy1, states, hh)
    dh1 = _mm("hgrn_in_dh", dp1, big_hgrn, tb=True, b_gathered=True, a_stacked=True)
    g2_hgrn = _mm("hgrn_in_dw", h1, dp1, ta=True, out_gathered=(d // 2, w), b_stacked=True)
    dx1, d_g1 = _rms_bwd("rms1_bwd", x1, g1, dh1, dx2)
    dy0 = _mm("fox_out_dy", dx1, w_o[0], tb=True)
    d_wo0 = _mm("fox_out_dw", y0, dx1, ta=True)
    core_arr = core.reshape(1).astype(jnp.int32)
    g2_out = jnp.stack([d_wo0, d_wo1]).reshape(2, 4, wq, d)
    do0, dgate0, delta, sib_hgrn, sib_out = _fox_pre_bwd("fox_pre_bwd", dy0, o0, gate0, exchange=[g2_hgrn, g2_out])
    delta_row = delta[:, :hh].T.reshape(hh, 1, s)
    pairs_early = [_pair_add("rs_pair_add_hgrn", g2_hgrn, sib_hgrn, core_arr),
                   _pair_add("rs_pair_add_out", g2_out, sib_out, core_arr)]
    dk0, dv0, dc_rep, dq0, rowsum_row, got_hgrn, got_out = _attn_bwd(
        "fox_attn_bwd", qkv, do0, lse_row, delta_row, c_rep, hh, scatter=pairs_early)
    dfl, d_bf = _fox_gate_bwd("fox_cumsum_bwd", _pad_lanes(rowsum_row.reshape(hh, s).T),
                              _pad_lanes(dc_rep[:, ::HEAD_DIM]), fl, bf_pad)
    dp0 = jnp.concatenate([dq0, dk0, dv0, dgate0], axis=1)
    d_wmain = _mm("fox_in_dw", h0, dp0, ta=True)
    d_wfl = _mm("fox_fl_dw", h0, dfl, ta=True)

    def grad_columns(a, b):
        out = []
        for lo, hi, src, shift in ((0, 3 * w, d_wmain, 0), (3 * w, 3 * w + hh, d_wfl, 3 * w),
                                   (3 * w + hh, 4 * w + hh, d_wmain, hh)):
            l2, h2 = max(a, lo), min(b, hi)
            if l2 < h2:
                out.append(src[:, l2 - shift:h2 - shift])
        return out

    g2_fox = jnp.stack([jnp.concatenate(grad_columns(j * fox_cols, (j + 1) * fox_cols), axis=1).reshape(2, d // 2, fox_cols)
                        for j in range(4)], axis=1)
    (sib_fox,) = _sibling_other_half("rs_sibling", [g2_fox])
    dh0, got_fox = _mm("fox_in_dh", dp0, w_main, tb=True, res=_mm("fox_fl_dh", dfl, w_fl, tb=True),
                       scatter=[_pair_add("rs_pair_add_fox", g2_fox, sib_fox, core_arr)])
    grad_x, d_g0 = _rms_bwd("rms0_bwd", x0, g0, dh0, dx1)
    halves_sum = [_sum_slots("rs_sum_" + nm, got, core_arr)
                  for nm, got in zip(("fox", "hgrn", "out"), (got_fox, got_hgrn, got_out))]
    r_fox, r_hgrn, r_out = _sibling_join("rs_join", halves_sum)
    g_fox = r_fox.reshape(d, fox_cols)
    g_hgrn = r_hgrn.reshape(d, w)
    g_out = r_out.reshape(2 * wq, d)

    small = jnp.concatenate([jnp.concatenate([d_g0, d_g1], axis=0).reshape(-1), d_lbl.reshape(-1), d_gf.reshape(-1),
                             d_onorm.reshape(-1), d_bf.reshape(-1)])
    n_small = small.shape[0]
    pad_to = -(-n_small // 1024) * 1024
    small = jnp.pad(small, (0, pad_to - n_small)).reshape(pad_to // 128, 128)
    small = _sum_slots("small_sum", _ag_devices("small_gather", small)).reshape(-1)
    g_norm = small[:2 * d].reshape(2, d)
    g_lbl = small[2 * d:2 * d + 2 * w].reshape(2, w)
    g_gf = small[2 * d + 2 * w:3 * d + 2 * w]
    g_onorm_full = small[3 * d + 2 * w:3 * d + 3 * w]
    g_bf = small[3 * d + 3 * w:3 * d + 3 * w + hh].reshape(1, hh)
    chip = 2 * lax.axis_index("x") + lax.axis_index("y")
    g_onorm = lax.dynamic_slice_in_dim(g_onorm_full, chip * wq, wq).reshape(1, wq)

    loss = lax.psum(loss_tile[0, 0], ("x", "y", "c"))

    def upd(name, wt, g, m, v):
        shp = wt.shape
        two = lambda a: a.reshape(-1, shp[-1])
        dl, nm, nv = _adamw(name, two(wt), two(g), two(m), two(v))
        return g.reshape(shp), dl.reshape(shp), nm.reshape(shp), nv.reshape(shp)

    res = [
        upd("adamw_norm_gains", norm_gains, g_norm, m_norm_gains, v_norm_gains),
        upd("adamw_fox_w_in", fox_w_in, g_fox, m_fox_w_in, v_fox_w_in),
        upd("adamw_fox_b_f", fox_b_f, g_bf, m_fox_b_f, v_fox_b_f),
        upd("adamw_hgrn_w_in", hgrn_w_in, g_hgrn, m_hgrn_w_in, v_hgrn_w_in),
        upd("adamw_lb_logits", hgrn_lb_logits, g_lbl, m_hgrn_lb_logits, v_hgrn_lb_logits),
        upd("adamw_onorm", hgrn_onorm, g_onorm, m_hgrn_onorm, v_hgrn_onorm),
        upd("adamw_w_out", w_out, g_out, m_w_out, v_w_out),
        upd("adamw_final_gain", final_gain.reshape(1, d), g_gf.reshape(1, d), m_final_gain.reshape(1, d),
            v_final_gain.reshape(1, d)),
    ]
    res[-1] = tuple(a.reshape(d) for a in res[-1])
    grads, deltas, new_m, new_v = zip(*res)
    return (loss, grad_x[None], *grads, *deltas, *new_m, *new_v)
```

```python
import functools

import jax
import jax.numpy as jnp
from jax import lax
from jax.experimental import pallas as pl
from jax.experimental.pallas import tpu as pltpu

F32 = jnp.float32
BF16 = jnp.bfloat16
MESH = pl.DeviceIdType.MESH
ANY = pl.BlockSpec(memory_space=pl.ANY)

EPS = 1e-6
HEAD_DIM = 128
HGRN_CHUNK = 64
HGRN_SUB = 32
EXP_CLAMP = 80.0
ATT_BLOCK = 512
HGRN_BLOCK = 2048
GATE_BLOCK = 512
ROW_BLOCK = 256
MM_TM, MM_TN, MM_TK = 1024, 1024, 2048
VMEM_LIMIT_V7X = 56 * 1024 * 1024

ADAM_LR, ADAM_B1, ADAM_B2, ADAM_EPS, ADAM_WD, ADAM_STEP = 0.001, 0.9, 0.999, 1e-08, 0.01, 10

NT = (((1,), (1,)), ((), ()))
TN = (((0,), (0,)), ((), ()))
NN = (((1,), (0,)), ((), ()))


def _call(body, **kw):
    return pl.pallas_call(body, **kw)


def _cp(dims=None):
    kw = dict(vmem_limit_bytes=VMEM_LIMIT_V7X)
    if dims is not None:
        kw["dimension_semantics"] = dims
    return pltpu.CompilerParams(**kw)


def _sigmoid(x):
    return 1.0 / (1.0 + jnp.exp(-x))


def _dot(a, b, dn=NN):
    return lax.dot_general(a.astype(BF16), b.astype(BF16), dn, preferred_element_type=F32)


def _split3(x):
    hi = x.astype(BF16)
    r1 = x - hi.astype(F32)
    mid = r1.astype(BF16)
    lo = (r1 - mid.astype(F32)).astype(BF16)
    return hi, mid, lo


def _dot_exact(m01, x, right=False):
    hi, mid, lo = _split3(x)
    if right:
        dot = lambda p: lax.dot_general(p, m01, NN, preferred_element_type=F32)
    else:
        dot = lambda p: lax.dot_general(m01, p, NN, preferred_element_type=F32)
    return dot(hi) + dot(mid) + dot(lo)


def _row_block(rows, cap):
    if rows <= cap:
        return rows
    best = None
    for t in range(16, cap + 1, 16):
        if rows % t == 0:
            best = t
    assert best is not None, rows
    return best


def _mm(name, a, b, *, ta=False, tb=False, out_dtype=F32, res=None, n=None, b_off=0, b_gathered=False,
        out_gathered=None, scatter=(), a_stacked=False, b_stacked=False, col_scale=None):
    if a_stacked:
        assert not ta
        m, k = a.shape[1], a.shape[0] * a.shape[2]
    else:
        m, k = (a.shape[1], a.shape[0]) if ta else a.shape
    if b_gathered:
        rh, cw = b.shape[2], b.shape[3]
        n_full = 2 * rh if tb else 4 * cw
    elif b_stacked:
        assert not tb
        n_full = b.shape[0] * b.shape[2]
    else:
        n_full = b.shape[0] if tb else b.shape[1]
    n = n_full if n is None else n
    tm, tn, tk = min(MM_TM, m), min(MM_TN, n), min(MM_TK, k)
    b_both = b_gathered and not tb and tk == 2 * rh
    if b_gathered and not b_both:
        tn, tk = (min(tn, rh), min(tk, cw)) if tb else (min(tn, cw), min(tk, rh))
    if b_both:
        tn = min(tn, cw)
    if out_gathered:
        tm, tn = min(tm, out_gathered[0]), min(tn, out_gathered[1])
    if a_stacked:
        tk = min(tk, a.shape[2])
    if b_stacked:
        tn = min(tn, b.shape[2])
    assert m % tm == 0 and n % tn == 0 and k % tk == 0 and b_off % tn == 0
    nk = k // tk
    jo = b_off // tn
    if a_stacked:
        pa = a.shape[2] // tk
        a_spec = pl.BlockSpec((1, tm, tk), lambda i, j, kk: (kk // pa, i, kk % pa))
    elif ta:
        a_spec = pl.BlockSpec((tk, tm), lambda i, j, kk: (kk, i))
    else:
        a_spec = pl.BlockSpec((tm, tk), lambda i, j, kk: (i, kk))
    if b_gathered and tb:
        pr, pc = rh // tn, cw // tk
        b_spec = pl.BlockSpec((1, 1, tn, tk), lambda i, j, kk: (j // pr, kk // pc, j % pr, kk % pc))
    elif b_both:
        pc = cw // tn
        b_spec = pl.BlockSpec((2, 1, rh, tn), lambda i, j, kk: (0, j // pc, 0, j % pc))
    elif b_gathered:
        pr, pc = rh // tk, cw // tn
        b_spec = pl.BlockSpec((1, 1, tk, tn), lambda i, j, kk: (kk // pr, j // pc, kk % pr, j % pc))
    elif b_stacked:
        pb = b.shape[2] // tn
        b_spec = pl.BlockSpec((1, tk, tn), lambda i, j, kk: (j // pb, kk, j % pb))
    elif tb:
        b_spec = pl.BlockSpec((tn, tk), lambda i, j, kk: (j + jo, kk))
    else:
        b_spec = pl.BlockSpec((tk, tn), lambda i, j, kk: (kk, j + jo))
    o_spec = pl.BlockSpec((tm, tn), lambda i, j, kk: (i, j))
    if out_gathered:
        assert m == 2 * out_gathered[0] and n == 4 * out_gathered[1] and res is None
        qr, qc = out_gathered[0] // tm, out_gathered[1] // tn
        o_spec = pl.BlockSpec((1, 1, tm, tn), lambda i, j, kk: (i // qr, j // qc, i % qr, j % qc))
    dn = (((0 if ta else 1,), (1 if tb else 0,)), ((), ()))
    has_res = res is not None
    ns = len(scatter)
    grid = (m // tm, n // tn, nk)

    def body(*refs):
        a_ref, b_ref = refs[:2]
        r_ref = refs[2] if has_res else None
        base = 2 + int(has_res)
        p_refs = refs[base:base + ns]
        o_ref = refs[base + ns]
        got_refs = refs[base + ns + 1:base + 2 * ns + 1]
        acc = refs[base + 2 * ns + 1]
        sems = refs[base + 2 * ns + 2:]
        kk = pl.program_id(2)
        if ns:
            first = jnp.logical_and(jnp.logical_and(pl.program_id(0) == 0, pl.program_id(1) == 0), kk == 0)
            last = jnp.logical_and(jnp.logical_and(pl.program_id(0) == grid[0] - 1, pl.program_id(1) == grid[1] - 1),
                                   kk == nk - 1)

            @pl.when(first)
            def _():
                _scatter_phase(0, p_refs, got_refs, sems)

        av = a_ref[0] if a_stacked else a_ref[...]
        if b_both:
            p = (lax.dot_general(av[:, :rh].astype(BF16), b_ref[0, 0].astype(BF16), dn, preferred_element_type=F32)
                 + lax.dot_general(av[:, rh:].astype(BF16), b_ref[1, 0].astype(BF16), dn, preferred_element_type=F32))
        else:
            bv = b_ref[0, 0] if b_gathered else (b_ref[0] if b_stacked else b_ref[...])
            p = lax.dot_general(av.astype(BF16), bv.astype(BF16), dn, preferred_element_type=F32)

        def finish(total):
            if r_ref is not None:
                total = total + r_ref[...]
            if col_scale is not None:
                total = total * jnp.where(pl.program_id(1) < col_scale[0] // tn, col_scale[1], 1.0)
            if out_gathered:
                o_ref[0, 0] = total.astype(out_dtype)
            else:
                o_ref[...] = total.astype(out_dtype)

        if nk == 1:
            finish(p)
        else:
            @pl.when(kk == 0)
            def _():
                acc[...] = p

            @pl.when(jnp.logical_and(kk > 0, kk < nk - 1))
            def _():
                acc[...] += p

            @pl.when(kk == nk - 1)
            def _():
                finish(acc[...] + p)

        if ns:
            @pl.when(last)
            def _():
                _scatter_phase(1, p_refs, got_refs, sems)

    ins = [a, b] + ([res] if has_res else []) + list(scatter)
    in_specs = [a_spec, b_spec] + ([o_spec] if has_res else []) + [ANY] * ns
    o_shape = jax.ShapeDtypeStruct((2, 4) + tuple(out_gathered) if out_gathered else (m, n), out_dtype)
    out = _call(
        body, name=name, grid=grid, in_specs=in_specs, out_specs=[o_spec] + [ANY] * ns,
        out_shape=[o_shape] + [jax.ShapeDtypeStruct(p.shape, p.dtype) for p in scatter],
        scratch_shapes=[pltpu.VMEM((tm, tn) if nk > 1 else (8, 128), F32)] + (_scatter_sems(ns) if ns else []),
        compiler_params=_cp(("arbitrary",) * 3 if ns else ("parallel", "parallel", "arbitrary")),
    )(*ins)
    return out if ns else out[0]


def _rms_fwd(name, x, g):
    s, d = x.shape
    tm = min(ROW_BLOCK, s)

    def body(x_ref, g_ref, h_ref):
        xv = x_ref[...]
        r = lax.rsqrt(jnp.mean(xv * xv, axis=-1, keepdims=True) + EPS)
        h_ref[...] = (xv * r * g_ref[...]).astype(BF16)

    row = pl.BlockSpec((tm, d), lambda i: (i, 0))
    vec = pl.BlockSpec((1, d), lambda i: (0, 0))
    return _call(body, name=name, grid=(s // tm,), in_specs=[row, vec], out_specs=row,
                 out_shape=jax.ShapeDtypeStruct((s, d), BF16), compiler_params=_cp(("parallel",)))(x, g)


def _rms_bwd(name, x, g, dh, dres):
    s, d = x.shape
    tm = min(ROW_BLOCK, s)

    def body(x_ref, g_ref, dh_ref, dres_ref, dx_ref, dg_ref):
        @pl.when(pl.program_id(0) == 0)
        def _():
            dg_ref[...] = jnp.zeros_like(dg_ref)

        xv = x_ref[...]
        r = lax.rsqrt(jnp.mean(xv * xv, axis=-1, keepdims=True) + EPS)
        xn = xv * r
        dhv = dh_ref[...]
        dxn = dhv * g_ref[...]
        dx_ref[...] = dres_ref[...] + r * (dxn - xn * jnp.mean(dxn * xn, axis=-1, keepdims=True))
        dg_ref[...] += jnp.sum(dhv * xn, axis=0, keepdims=True)

    row = pl.BlockSpec((tm, d), lambda i: (i, 0))
    vec = pl.BlockSpec((1, d), lambda i: (0, 0))
    return _call(body, name=name, grid=(s // tm,), in_specs=[row, vec, row, row], out_specs=[row, vec],
                 out_shape=[jax.ShapeDtypeStruct((s, d), F32), jax.ShapeDtypeStruct((1, d), F32)],
                 compiler_params=_cp(("arbitrary",)))(x, g, dh, dres)


def _loss_head(name, x, tgt, g):
    s, d = x.shape
    tm = min(ROW_BLOCK, s)
    nb = s // tm

    def body(x_ref, t_ref, g_ref, dx_ref, dg_ref, loss_ref, lacc):
        i = pl.program_id(0)

        @pl.when(i == 0)
        def _():
            dg_ref[...] = jnp.zeros_like(dg_ref)
            lacc[...] = jnp.zeros_like(lacc)

        xv = x_ref[...]
        gv = g_ref[...]
        r = lax.rsqrt(jnp.mean(xv * xv, axis=-1, keepdims=True) + EPS)
        xn = xv * r
        err = xn * gv - t_ref[...]
        lacc[...] += jnp.sum(err * err, axis=0, keepdims=True)
        dout = err * (1.0 / d)
        dg_ref[...] += jnp.sum(dout * xn, axis=0, keepdims=True)
        dxn = dout * gv
        dx_ref[...] = r * (dxn - xn * jnp.mean(dxn * xn, axis=-1, keepdims=True))

        @pl.when(i == nb - 1)
        def _():
            total = jnp.sum(lacc[...], axis=1, keepdims=True) * (0.5 / d)
            loss_ref[...] = jnp.broadcast_to(total, loss_ref.shape)

    row = pl.BlockSpec((tm, d), lambda i: (i, 0))
    vec = pl.BlockSpec((1, d), lambda i: (0, 0))
    one = pl.BlockSpec((1, 128), lambda i: (0, 0))
    return _call(body, name=name, grid=(nb,), in_specs=[row, row, vec], out_specs=[row, vec, one],
                 out_shape=[jax.ShapeDtypeStruct((s, d), F32), jax.ShapeDtypeStruct((1, d), F32),
                            jax.ShapeDtypeStruct((1, 128), F32)],
                 scratch_shapes=[pltpu.VMEM((1, d), F32)], compiler_params=_cp(("arbitrary",)))(x, tgt, g)


def _fox_gate_fwd(name, fl, bf, w, rep_scale=1.0):
    s = fl.shape[0]
    tb = min(GATE_BLOCK, s)

    def body(fl_ref, bf_ref, c_ref, crep_ref, carry):
        @pl.when(pl.program_id(0) == 0)
        def _():
            carry[...] = jnp.zeros_like(carry)

        z = fl_ref[...] + bf_ref[...]
        lf = jnp.minimum(z, 0.0) - jnp.log(1.0 + jnp.exp(-jnp.abs(z)))
        rows = lax.broadcasted_iota(jnp.int32, (tb, tb), 0)
        cols = lax.broadcasted_iota(jnp.int32, (tb, tb), 1)
        tri = (rows >= cols).astype(BF16)
        cs = _dot_exact(tri, lf) + carry[...]
        c_ref[...] = cs
        carry[...] = c_ref[tb - 1:tb, :]
        sel_r = lax.broadcasted_iota(jnp.int32, (128, w), 0)
        sel_c = lax.broadcasted_iota(jnp.int32, (128, w), 1)
        sel = (sel_r == sel_c // HEAD_DIM).astype(BF16)
        crep_ref[...] = _dot_exact(sel, cs * rep_scale, right=True)

    blk = pl.BlockSpec((tb, 128), lambda i: (i, 0))
    return _call(body, name=name, grid=(s // tb,),
                 in_specs=[blk, pl.BlockSpec((1, 128), lambda i: (0, 0))],
                 out_specs=[blk, pl.BlockSpec((tb, w), lambda i: (i, 0))],
                 out_shape=[jax.ShapeDtypeStruct((s, 128), F32), jax.ShapeDtypeStruct((s, w), F32)],
                 scratch_shapes=[pltpu.VMEM((1, 128), F32)], compiler_params=_cp(("arbitrary",)))(fl, bf)


def _fox_gate_bwd(name, drow, dcol, fl, bf):
    s = fl.shape[0]
    tb = min(GATE_BLOCK, s)
    nb = s // tb

    def body(dr_ref, dc_ref, fl_ref, bf_ref, dfl_ref, dbf_ref, carry, tmp):
        @pl.when(pl.program_id(0) == 0)
        def _():
            carry[...] = jnp.zeros_like(carry)
            dbf_ref[...] = jnp.zeros_like(dbf_ref)

        rows = lax.broadcasted_iota(jnp.int32, (tb, tb), 0)
        cols = lax.broadcasted_iota(jnp.int32, (tb, tb), 1)
        triu = (rows <= cols).astype(BF16)
        dlf = _dot_exact(triu, dr_ref[...] + dc_ref[...]) + carry[...]
        tmp[...] = dlf
        carry[...] = tmp[0:1, :]
        z = fl_ref[...] + bf_ref[...]
        dfl = dlf * (1.0 / (1.0 + jnp.exp(z)))
        dfl_ref[...] = dfl
        dbf_ref[...] += jnp.sum(dfl, axis=0, keepdims=True)

    blk = pl.BlockSpec((tb, 128), lambda i: (nb - 1 - i, 0))
    vec = pl.BlockSpec((1, 128), lambda i: (0, 0))
    return _call(body, name=name, grid=(nb,), in_specs=[blk, blk, blk, vec], out_specs=[blk, vec],
                 out_shape=[jax.ShapeDtypeStruct((s, 128), F32), jax.ShapeDtypeStruct((1, 128), F32)],
                 scratch_shapes=[pltpu.VMEM((1, 128), F32), pltpu.VMEM((tb, 128), F32)],
                 compiler_params=_cp(("arbitrary",)))(drow, dcol, fl, bf)


LOG2E = 1.4426950408889634
ATT_SCALE = HEAD_DIM ** -0.5
Q_FOLD = ATT_SCALE * LOG2E


def _attn_fwd(name, qkv, c_rep, gate, h_count, gather=()):
    s = qkv.shape[0]
    w = h_count * HEAD_DIM
    t = min(ATT_BLOCK, s)
    hp = 2 if h_count % 2 == 0 else 1
    wb = hp * HEAD_DIM
    ng = len(gather)
    nh, nq = h_count // hp, s // t

    def body(*refs):
        q_ref, k_ref, v_ref, c_ref, gate_ref = refs[:5]
        h_refs = refs[5:5 + ng]
        o_ref, lse_ref, y_ref = refs[5 + ng:8 + ng]
        big_refs = refs[8 + ng:8 + 2 * ng]
        m_s, l_s, acc_s = refs[8 + 2 * ng:11 + 2 * ng]
        sems = refs[11 + 2 * ng:]
        hs = pl.program_id(0)
        qi = pl.program_id(1)
        if ng:
            @pl.when(jnp.logical_and(hs == 0, qi == 0))
            def _():
                _ag_phase(0, h_refs, big_refs, sems)

            @pl.when(jnp.logical_and(hs == nh // 2, qi == 0))
            def _():
                _ag_phase(1, h_refs, big_refs, sems)

        m_s[...] = jnp.full(m_s.shape, -jnp.inf, F32)
        l_s[...] = jnp.zeros_like(l_s)
        acc_s[...] = jnp.zeros_like(acc_s)

        def step(kb, masked):
            off = pl.multiple_of(kb * t, t)
            for a in range(hp):
                cols_a = slice(a * HEAD_DIM, (a + 1) * HEAD_DIM)
                kk = k_ref[pl.ds(off, t), cols_a]
                vv = v_ref[pl.ds(off, t), cols_a]
                cc = jnp.tile(c_ref[pl.ds(off, t), cols_a], (1, t // HEAD_DIM))
                st = lax.dot_general(kk, q_ref[:, cols_a], NT, preferred_element_type=F32) - cc
                if masked:
                    rows = lax.broadcasted_iota(jnp.int32, (t, t), 0)
                    cols = lax.broadcasted_iota(jnp.int32, (t, t), 1)
                    st = jnp.where(cols >= rows, st, -jnp.inf)
                m_prev = m_s[a]
                m_new = jnp.maximum(m_prev, jnp.max(st, axis=0, keepdims=True))
                pt = jnp.exp2(st - m_new)
                alpha = jnp.exp2(m_prev - m_new)
                l_s[a] = alpha * l_s[a] + jnp.sum(pt, axis=0, keepdims=True)
                acc_s[a] = alpha * acc_s[a] + lax.dot_general(vv, pt.astype(BF16), TN, preferred_element_type=F32)
                m_s[a] = m_new

        def loop_body(pair, carry):
            step(2 * pair, False)
            step(2 * pair + 1, False)
            return carry

        lax.fori_loop(0, qi // 2, loop_body, 0)

        @pl.when(lax.rem(qi, 2) == 1)
        def _():
            step(qi - 1, False)

        step(qi, True)
        for a in range(hp):
            l = l_s[a]
            cols_a = slice(a * HEAD_DIM, (a + 1) * HEAD_DIM)
            oa = (acc_s[a] / l).T
            o_ref[:, cols_a] = oa
            g = gate_ref[:, cols_a]
            y_ref[:, cols_a] = (oa * (g * _sigmoid(g))).astype(BF16)
            lse_ref[a] = m_s[a] + jnp.log2(l)

        if ng:
            @pl.when(jnp.logical_and(hs == nh - 1, qi == nq - 1))
            def _():
                _ag_phase(2, h_refs, big_refs, sems)

    blk = lambda off: pl.BlockSpec((t, wb), lambda h, i: (i, off + h))
    whole = lambda off: pl.BlockSpec((s, wb), lambda h, i: (0, off + h))
    rowv = pl.BlockSpec((hp, 1, t), lambda h, i: (h, 0, i))
    return _call(
        body, name=name, grid=(nh, nq),
        in_specs=[blk(0), whole(nh), whole(2 * nh), whole(0), blk(0)] + [ANY] * ng,
        out_specs=[blk(0), rowv, blk(0)] + [ANY] * ng,
        out_shape=[jax.ShapeDtypeStruct((s, w), F32), jax.ShapeDtypeStruct((h_count, 1, s), F32),
                   jax.ShapeDtypeStruct((s, w), BF16)] + _ag_out_shapes(gather),
        scratch_shapes=[pltpu.VMEM((hp, 1, t), F32), pltpu.VMEM((hp, 1, t), F32), pltpu.VMEM((hp, HEAD_DIM, t), F32)]
        + (_ag_sems(ng) if ng else []),
        compiler_params=_cp(("arbitrary", "arbitrary") if ng else ("parallel", "arbitrary")),
    )(qkv, qkv, qkv, c_rep, gate, *gather)


def _attn_bwd(name, qkv, do, lse_row, delta_row, c_rep, h_count, scatter=()):
    s = qkv.shape[0]
    w = h_count * HEAD_DIM
    t = min(ATT_BLOCK, s)
    nq = s // t
    scale = HEAD_DIM ** -0.5
    ns = len(scatter)

    def body(*refs):
        k_ref, v_ref, q_ref, do_ref, lse_ref, dl_ref, c_ref = refs[:7]
        p_refs = refs[7:7 + ns]
        dk_ref, dv_ref, dc_ref, dq_ref, rs_ref = refs[7 + ns:12 + ns]
        got_refs = refs[12 + ns:12 + 2 * ns]
        dk_s, dv_s, dc_s, dq_s, rs_s = refs[12 + 2 * ns:17 + 2 * ns]
        sems = refs[17 + 2 * ns:]
        kj = pl.program_id(1)
        if ns:
            @pl.when(jnp.logical_and(pl.program_id(0) == 0, kj == 0))
            def _():
                _scatter_phase(0, p_refs, got_refs, sems)

        kk = k_ref[...]
        vv = v_ref[...]
        ccol = jnp.tile(c_ref[...], (1, t // HEAD_DIM))
        dk_s[...] = jnp.zeros_like(dk_s)
        dv_s[...] = jnp.zeros_like(dv_s)
        dc_s[...] = jnp.zeros_like(dc_s)

        @pl.when(kj == 0)
        def _():
            dq_s[...] = jnp.zeros_like(dq_s)
            rs_s[...] = jnp.zeros_like(rs_s)

        def step(qb, masked):
            off = pl.multiple_of(qb * t, t)
            q = q_ref[pl.ds(off, t), :]
            dov = do_ref[pl.ds(off, t), :]
            st = lax.dot_general(kk, q, NT, preferred_element_type=F32) - ccol
            pt = jnp.exp2(st - lse_ref[0, :, pl.ds(off, t)])
            if masked:
                rows = lax.broadcasted_iota(jnp.int32, (t, t), 0)
                cols = lax.broadcasted_iota(jnp.int32, (t, t), 1)
                pt = jnp.where(cols >= rows, pt, 0.0)
            dv_s[...] += lax.dot_general(pt.astype(BF16), dov, NN, preferred_element_type=F32)
            dpt = lax.dot_general(vv, dov, NT, preferred_element_type=F32)
            dst = pt * (dpt - dl_ref[0, :, pl.ds(off, t)])
            dstb = dst.astype(BF16)
            dk_s[...] += lax.dot_general(dstb, q, NN, preferred_element_type=F32)
            dc_s[...] += jnp.sum(dst, axis=1, keepdims=True)
            dq_s[:, pl.ds(off, t)] += lax.dot_general(kk, dstb, TN, preferred_element_type=F32)
            rs_s[:, pl.ds(off, t)] += jnp.sum(dst, axis=0, keepdims=True)

        step(kj, True)

        def loop_body(qb, carry):
            step(qb, False)
            return carry

        lax.fori_loop(kj + 1, nq, loop_body, 0)
        dk_ref[...] = (dk_s[...] * (1.0 / LOG2E)).astype(BF16)
        dv_ref[...] = dv_s[...].astype(BF16)
        dc_ref[...] = jnp.broadcast_to(-dc_s[...], dc_ref.shape)

        @pl.when(kj == nq - 1)
        def _():
            for b in range(nq):
                dq_ref[b * t:(b + 1) * t, :] = (dq_s[:, b * t:(b + 1) * t] * scale).T.astype(BF16)
            rs_ref[0] = rs_s[...]

        if ns:
            @pl.when(jnp.logical_and(pl.program_id(0) == h_count - 1, kj == nq - 1))
            def _():
                _scatter_phase(1, p_refs, got_refs, sems)

    hh = h_count
    blk = lambda off: pl.BlockSpec((t, HEAD_DIM), lambda h, j: (j, off + h))
    whole = pl.BlockSpec((s, HEAD_DIM), lambda h, j: (0, h))
    rowv = pl.BlockSpec((1, 1, s), lambda h, j: (h, 0, 0))
    return _call(
        body, name=name, grid=(hh, nq),
        in_specs=[blk(hh), blk(2 * hh), whole, whole, rowv, rowv, blk(0)] + [ANY] * ns,
        out_specs=[blk(0), blk(0), blk(0), whole, rowv] + [ANY] * ns,
        out_shape=[jax.ShapeDtypeStruct((s, w), BF16), jax.ShapeDtypeStruct((s, w), BF16),
                   jax.ShapeDtypeStruct((s, w), F32), jax.ShapeDtypeStruct((s, w), BF16),
                   jax.ShapeDtypeStruct((hh, 1, s), F32)] + [jax.ShapeDtypeStruct(p.shape, p.dtype) for p in scatter],
        scratch_shapes=[pltpu.VMEM((t, HEAD_DIM), F32), pltpu.VMEM((t, HEAD_DIM), F32), pltpu.VMEM((t, 1), F32),
                        pltpu.VMEM((HEAD_DIM, s), F32), pltpu.VMEM((1, s), F32)] + (_scatter_sems(ns) if ns else []),
        compiler_params=_cp(("arbitrary", "arbitrary") if ns else ("parallel", "arbitrary")),
    )(qkv, qkv, qkv, do, lse_row, delta_row, c_rep, *scatter)


def _fox_pre_bwd(name, dy, o, gate, exchange=()):
    s, w = o.shape
    tm = min(ROW_BLOCK, s)
    ne = len(exchange)
    nb = s // tm

    def body(*refs):
        dy_ref, o_ref, g_ref = refs[:3]
        x_refs = refs[3:3 + ne]
        do_ref, dg_ref, dl_ref = refs[3 + ne:6 + ne]
        got_refs = refs[6 + ne:6 + 2 * ne]
        sems = refs[6 + 2 * ne:]
        if ne:
            @pl.when(pl.program_id(0) == 0)
            def _():
                _sibling_phase(0, x_refs, got_refs, sems)

            @pl.when(pl.program_id(0) == nb - 1)
            def _():
                _sibling_phase(1, x_refs, got_refs, sems)

        g = g_ref[...]
        sg = _sigmoid(g)
        dyv = dy_ref[...]
        ov = o_ref[...]
        dov = dyv * (g * sg)
        do_ref[...] = dov.astype(BF16)
        dg_ref[...] = (dyv * ov * (sg * (1.0 + g * (1.0 - sg)))).astype(BF16)
        sel_r = lax.broadcasted_iota(jnp.int32, (w, 128), 0)
        sel_c = lax.broadcasted_iota(jnp.int32, (w, 128), 1)
        sel = (sel_r // HEAD_DIM == sel_c).astype(BF16)
        dl_ref[...] = _dot_exact(sel, dov * ov, right=True)

    row = pl.BlockSpec((tm, w), lambda i: (i, 0))
    lanes = pl.BlockSpec((tm, 128), lambda i: (i, 0))
    return _call(body, name=name, grid=(nb,), in_specs=[row, row, row] + [ANY] * ne,
                 out_specs=[row, row, lanes] + [ANY] * ne,
                 out_shape=[jax.ShapeDtypeStruct((s, w), BF16), jax.ShapeDtypeStruct((s, w), BF16),
                            jax.ShapeDtypeStruct((s, 128), F32)] + _sibling_out_shapes(exchange),
                 scratch_shapes=_sibling_sems(ne) if ne else [],
                 compiler_params=_cp(("arbitrary",) if ne else ("parallel",)))(dy, o, gate, *exchange)


def _hgrn_chunk_terms(q_c, k_c, b_c, b_s, base):
    nsub = HGRN_CHUNK // HGRN_SUB
    refs = [jnp.zeros((1, HEAD_DIM), F32)]
    for i in range(1, nsub):
        r0 = base + i * HGRN_SUB - 1
        refs.append(b_s[r0:r0 + 1, :])
    rfull = jnp.concatenate([jnp.broadcast_to(r, (HGRN_SUB, HEAD_DIM)) for r in refs], axis=0)
    eq = jnp.exp(b_c - rfull)
    qe = q_c * eq
    es = [jnp.exp(jnp.minimum(r - b_c, EXP_CLAMP)) for r in refs]
    kes = [(k_c * e).astype(BF16) for e in es]
    return eq, qe, es, kes


def _chunk_cumsum(x, reverse=False):
    cc = HGRN_CHUNK
    nc = x.shape[0] // cc
    rows = lax.broadcasted_iota(jnp.int32, (cc, cc), 0)
    cols = lax.broadcasted_iota(jnp.int32, (cc, cc), 1)
    tri = ((rows <= cols) if reverse else (rows >= cols)).astype(BF16)
    wide = jnp.concatenate([x[n * cc:(n + 1) * cc] for n in range(nc)], axis=1)
    res = _dot_exact(tri, wide)
    return jnp.concatenate([res[:, n * HEAD_DIM:(n + 1) * HEAD_DIM] for n in range(nc)], axis=0)


def _hgrn_block_pre(q_ref, f_ref, lbl_ref, b_s):
    lb = _sigmoid(lbl_ref[1:2, :] - lbl_ref[0:1, :])
    qr = q_ref[...]
    sq = _sigmoid(qr)
    q = qr * sq
    fz = f_ref[...]
    sg = _sigmoid(fz)
    f = lb + (1.0 - lb) * sg
    g = jnp.log(f)
    k = (1.0 - lb) * (1.0 / (1.0 + jnp.exp(fz)))
    b = _chunk_cumsum(g)
    b_s[...] = b
    return lb, qr, sq, q, sg, f, k, b


def _hgrn_fwd(name, p1, lbl, onorm, h_count):
    s = p1.shape[0]
    w = h_count * HEAD_DIM
    t = min(HGRN_BLOCK, s)
    nc = t // HGRN_CHUNK
    nsub = HGRN_CHUNK // HGRN_SUB
    cc = HGRN_CHUNK

    def body(q_ref, f_ref, i_ref, g_ref, lbl_ref, on_ref, y_ref, o_ref, st_ref, state_s, b_s):
        @pl.when(pl.program_id(1) == 0)
        def _():
            state_s[...] = jnp.zeros_like(state_s)

        lb, qr, sq, q, sg, f, k, b = _hgrn_block_pre(q_ref, f_ref, lbl_ref, b_s)
        v = i_ref[...]
        r64 = lax.broadcasted_iota(jnp.int32, (cc, cc), 0)
        c64 = lax.broadcasted_iota(jnp.int32, (cc, cc), 1)
        for n in range(nc):
            sl = slice(n * cc, (n + 1) * cc)
            q_c, k_c, v_c, b_c = q[sl], k[sl], v[sl], b[sl]
            bl = b_s[n * cc + cc - 1:n * cc + cc, :]
            eq, qe, es, kes = _hgrn_chunk_terms(q_c, k_c, b_c, b_s, n * cc)
            qeb = qe.astype(BF16)
            a = jnp.concatenate(
                [lax.dot_general(qeb[i * HGRN_SUB:(i + 1) * HGRN_SUB], kes[i], NT, preferred_element_type=F32)
                 for i in range(nsub)], axis=0)
            a = jnp.where(r64 >= c64, a, 0.0)
            st = state_s[...]
            st_ref[0, n] = st
            inter = _dot(q_c * jnp.exp(b_c), st, NT)
            intra = _dot(a, v_c)
            o_ref[sl, :] = inter + intra
            kb = k_c * jnp.exp(bl - b_c)
            state_s[...] = st * jnp.exp(bl) + _dot(v_c, kb, TN)
        o = o_ref[...]
        rr = lax.rsqrt(jnp.mean(o * o, axis=-1, keepdims=True) + EPS)
        gate = g_ref[...]
        y_ref[...] = ((o * rr) * on_ref[...] * (gate * _sigmoid(gate))).astype(BF16)

    hh = h_count
    blk = lambda off: pl.BlockSpec((t, HEAD_DIM), lambda h, i: (i, off + h))
    return _call(
        body, name=name, grid=(hh, s // t),
        in_specs=[blk(0), blk(hh), blk(2 * hh), blk(3 * hh),
                  pl.BlockSpec((2, HEAD_DIM), lambda h, i: (0, h)), pl.BlockSpec((1, HEAD_DIM), lambda h, i: (0, h))],
        out_specs=[blk(0), blk(0), pl.BlockSpec((1, nc, HEAD_DIM, HEAD_DIM), lambda h, i: (h, i, 0, 0))],
        out_shape=[jax.ShapeDtypeStruct((s, w), BF16), jax.ShapeDtypeStruct((s, w), F32),
                   jax.ShapeDtypeStruct((hh, s // cc, HEAD_DIM, HEAD_DIM), F32)],
        scratch_shapes=[pltpu.VMEM((HEAD_DIM, HEAD_DIM), F32), pltpu.VMEM((t, HEAD_DIM), F32)],
        compiler_params=_cp(("parallel", "arbitrary")),
    )(p1, p1, p1, p1, lbl, onorm)


def _hgrn_bwd(name, p1, lbl, onorm, o, dy, states, h_count):
    s = p1.shape[0]
    w = h_count * HEAD_DIM
    t = min(HGRN_BLOCK, s)
    nb = s // t
    nc = t // HGRN_CHUNK
    nsub = HGRN_CHUNK // HGRN_SUB
    cc = HGRN_CHUNK

    def body(q_ref, f_ref, i_ref, g_ref, lbl_ref, on_ref, o_ref, dy_ref, st_ref,
             dp_ref, dlog_ref, don_ref,
             dst_s, b_s, do_s, dqs, dks, dbs, exs):
        @pl.when(pl.program_id(1) == 0)
        def _():
            dst_s[...] = jnp.zeros_like(dst_s)
            dlog_ref[...] = jnp.zeros_like(dlog_ref)
            don_ref[...] = jnp.zeros_like(don_ref)

        lb, qr, sq, q, sg, f, k, b = _hgrn_block_pre(q_ref, f_ref, lbl_ref, b_s)
        v = i_ref[...]

        ov = o_ref[...]
        rr = lax.rsqrt(jnp.mean(ov * ov, axis=-1, keepdims=True) + EPS)
        on = ov * rr
        gate = g_ref[...]
        sgt = _sigmoid(gate)
        silu = gate * sgt
        dyv = dy_ref[...]
        gain = on_ref[...]
        dp_ref[3] = (dyv * on * gain * (sgt * (1.0 + gate * (1.0 - sgt)))).astype(BF16)
        don_ref[...] += jnp.sum(dyv * on * silu, axis=0, keepdims=True)
        d_on = dyv * gain * silu
        do_s[...] = rr * (d_on - on * jnp.mean(d_on * on, axis=-1, keepdims=True))

        r64 = lax.broadcasted_iota(jnp.int32, (cc, cc), 0)
        c64 = lax.broadcasted_iota(jnp.int32, (cc, cc), 1)
        upper = r64 <= c64
        for n in reversed(range(nc)):
            sl = slice(n * cc, (n + 1) * cc)
            q_c, k_c, v_c, b_c = q[sl], k[sl], v[sl], b[sl]
            do_c = do_s[sl, :]
            bl = b_s[n * cc + cc - 1:n * cc + cc, :]
            ebl = jnp.exp(bl)
            eq, qe, es, kes = _hgrn_chunk_terms(q_c, k_c, b_c, b_s, n * cc)
            qeb = qe.astype(BF16)
            dob = do_c.astype(BF16)
            vb = v_c.astype(BF16)
            st = st_ref[0, n]
            dstn = dst_s[...]
            qb_ = q_c * jnp.exp(b_c)
            kb_ = k_c * jnp.exp(bl - b_c)
            at = jnp.zeros((cc, cc), F32)
            for i in range(nsub):
                blk_i = (c64 // HGRN_SUB) == i
                at = at + jnp.where(blk_i, lax.dot_general(kes[i], qeb, NT, preferred_element_type=F32), 0.0)
            at = jnp.where(upper, at, 0.0)
            dv = _dot(at, dob) + _dot(kb_, dstn, NT)
            dqb = _dot(dob, st)
            dkb = _dot(vb, dstn)
            da = jnp.where(r64 >= c64, lax.dot_general(dob, vb, NT, preferred_element_type=F32), 0.0)
            dat = jnp.where(upper, lax.dot_general(vb, dob, NT, preferred_element_type=F32), 0.0)
            dab = da.astype(BF16)
            dq_raw = jnp.concatenate(
                [lax.dot_general(dab[i * HGRN_SUB:(i + 1) * HGRN_SUB], kes[i], NN, preferred_element_type=F32)
                 for i in range(nsub)], axis=0)
            db = qb_.astype(BF16).astype(F32) * dqb + qeb.astype(F32) * dq_raw
            dkbk = dkb * kb_.astype(BF16).astype(F32)
            db = db - dkbk
            dk_in = jnp.zeros((cc, HEAD_DIM), F32)
            for i in range(nsub):
                blk_i = (c64 // HGRN_SUB) == i
                dk_raw = _dot(jnp.where(blk_i, dat, 0.0), qeb)
                dk_in = dk_in + dk_raw * es[i]
                db = db - kes[i].astype(F32) * dk_raw
            dqs[sl, :] = dqb * jnp.exp(b_c) + dq_raw * eq
            dks[sl, :] = dkb * jnp.exp(bl - b_c) + dk_in
            dbs[sl, :] = db
            extra = jnp.sum(dkbk, axis=0, keepdims=True) + jnp.sum(dstn * st, axis=0, keepdims=True) * ebl
            exs[sl, :] = jnp.broadcast_to(extra, (cc, HEAD_DIM))
            dp_ref[2, sl, :] = dv.astype(BF16)
            dst_s[...] = dstn * ebl + _dot(dob, qb_, TN)

        dg = _chunk_cumsum(dbs[...], reverse=True) + exs[...]
        df = dg / f - dks[...]
        dp_ref[1] = (df * (1.0 - lb) * sg * (1.0 - sg)).astype(BF16)
        dp_ref[0] = (dqs[...] * (sq * (1.0 + qr * (1.0 - sq)))).astype(BF16)
        dlb = jnp.sum(df * (1.0 - sg), axis=0, keepdims=True) * (lb * (1.0 - lb))
        dlog_ref[0:1, :] += -dlb
        dlog_ref[1:2, :] += dlb

    hh = h_count
    blk = lambda off: pl.BlockSpec((t, HEAD_DIM), lambda h, i: (nb - 1 - i, off + h))
    two = pl.BlockSpec((2, HEAD_DIM), lambda h, i: (0, h))
    one = pl.BlockSpec((1, HEAD_DIM), lambda h, i: (0, h))
    tile = pltpu.VMEM((t, HEAD_DIM), F32)
    return _call(
        body, name=name, grid=(hh, nb),
        in_specs=[blk(0), blk(hh), blk(2 * hh), blk(3 * hh), two, one, blk(0), blk(0),
                  pl.BlockSpec((1, nc, HEAD_DIM, HEAD_DIM), lambda h, i: (h, nb - 1 - i, 0, 0))],
        out_specs=[pl.BlockSpec((4, t, HEAD_DIM), lambda h, i: (0, nb - 1 - i, h)), two, one],
        out_shape=[jax.ShapeDtypeStruct((4, s, w), BF16), jax.ShapeDtypeStruct((2, w), F32),
                   jax.ShapeDtypeStruct((1, w), F32)],
        scratch_shapes=[pltpu.VMEM((HEAD_DIM, HEAD_DIM), F32), tile, tile, tile, tile, tile, tile],
        compiler_params=_cp(("parallel", "arbitrary")),
    )(p1, p1, p1, p1, lbl, onorm, o, dy, states)


def _adamw(name, w, g, m, v):
    r, c = w.shape
    tr = r if r <= 128 else _row_block(r, 128)
    c1 = 1.0 - ADAM_B1 ** ADAM_STEP
    c2 = 1.0 - ADAM_B2 ** ADAM_STEP

    def body(w_ref, g_ref, m_ref, v_ref, d_ref, nm_ref, nv_ref):
        gv = g_ref[...]
        nm = ADAM_B1 * m_ref[...] + (1.0 - ADAM_B1) * gv
        nv = ADAM_B2 * v_ref[...] + (1.0 - ADAM_B2) * (gv * gv)
        nm_ref[...] = nm
        nv_ref[...] = nv
        d_ref[...] = -ADAM_LR * ((nm / c1) / (jnp.sqrt(nv / c2) + ADAM_EPS) + ADAM_WD * w_ref[...])

    blk = pl.BlockSpec((tr, c), lambda i: (i, 0))
    sh = jax.ShapeDtypeStruct((r, c), F32)
    return _call(body, name=name, grid=(r // tr,), in_specs=[blk] * 4, out_specs=[blk] * 3,
                 out_shape=[sh, sh, sh], compiler_params=_cp(("parallel",)))(w, g, m, v)


SLAB_BLOCK_BYTES = 2 * 1024 * 1024


def _slab_rows(r, c):
    return _row_block(r, max(16, SLAB_BLOCK_BYTES // (4 * c) // 16 * 16))


def _pair_add(name, g2, recv, core):
    _, nch, r, c = g2.shape
    tr = _slab_rows(r, c)

    grid_spec = pltpu.PrefetchScalarGridSpec(
        num_scalar_prefetch=1, grid=(nch, r // tr),
        in_specs=[pl.BlockSpec((1, 1, tr, c), lambda j, i, cr: (cr[0], j, i, 0)),
                  pl.BlockSpec((1, tr, c), lambda j, i, cr: (j, i, 0))],
        out_specs=pl.BlockSpec((1, tr, c), lambda j, i, cr: (j, i, 0)))

    def body(core_ref, a_ref, b_ref, o_ref):
        o_ref[...] = (a_ref[0] + b_ref[...]).astype(BF16)

    return _call(body, name=name, grid_spec=grid_spec, out_shape=jax.ShapeDtypeStruct((nch, r, c), BF16),
                 compiler_params=_cp(("parallel", "parallel")))(core, g2, recv)


def _sum_slots(name, x, core=None):
    n, r, c = x.shape
    tr = _slab_rows(r, c)

    def total(x_ref):
        acc = x_ref[0].astype(F32)
        for j in range(1, n):
            acc = acc + x_ref[j].astype(F32)
        return acc

    if core is None:
        def body(x_ref, o_ref):
            o_ref[...] = total(x_ref)

        return _call(body, name=name, grid=(r // tr,),
                     in_specs=[pl.BlockSpec((n, tr, c), lambda i: (0, i, 0))],
                     out_specs=pl.BlockSpec((tr, c), lambda i: (i, 0)),
                     out_shape=jax.ShapeDtypeStruct((r, c), F32), compiler_params=_cp(("parallel",)))(x)

    def body_half(core_ref, x_ref, o_ref):
        o_ref[0] = total(x_ref)

    grid_spec = pltpu.PrefetchScalarGridSpec(
        num_scalar_prefetch=1, grid=(r // tr,),
        in_specs=[pl.BlockSpec((n, tr, c), lambda i, cr: (0, i, 0))],
        out_specs=pl.BlockSpec((1, tr, c), lambda i, cr: (cr[0], i, 0)))
    return _call(body_half, name=name, grid_spec=grid_spec, out_shape=jax.ShapeDtypeStruct((2, r, c), F32),
                 compiler_params=_cp(("parallel",)))(core, x)


def _pos():
    return lax.axis_index("x"), lax.axis_index("y"), lax.axis_index("c")


def _flip(v, f):
    return (1 - v) if f else v


CHIP_FLIPS = ((0, 1), (1, 0), (1, 1))
DEV_FLIPS = tuple((fx, fy, fc) for fx in (0, 1) for fy in (0, 1) for fc in (0, 1))[1:]


def _remote(src, dst, ssem, rsem, dev):
    return pltpu.make_async_remote_copy(src_ref=src, dst_ref=dst, send_sem=ssem, recv_sem=rsem,
                                        device_id=dev, device_id_type=MESH)


def _ag_weights(name, halves):
    n = len(halves)

    def body(*refs):
        h_refs, big_refs, sems = refs[:n], refs[n:2 * n], refs[2 * n:]
        _ag_phase(0, h_refs, big_refs, sems)
        _ag_phase(1, h_refs, big_refs, sems)
        _ag_phase(2, h_refs, big_refs, sems)

    return _call(body, name=name, in_specs=[ANY] * n, out_specs=[ANY] * n,
                 out_shape=_ag_out_shapes(halves), scratch_shapes=_ag_sems(n))(*halves)


def _ag_out_shapes(halves):
    return [jax.ShapeDtypeStruct((2, 4) + h.shape, h.dtype) for h in halves]


def _ag_sems(n):
    return [pltpu.SemaphoreType.DMA((n,)), pltpu.SemaphoreType.DMA((7 * n,)), pltpu.SemaphoreType.DMA((7 * n,))]


def _ag_phase(phase, h_refs, big_refs, sems):
    lsem, ssem, rsem = sems
    n = len(h_refs)
    x, y, cc = _pos()
    me = 2 * x + y
    sib = (x, y, 1 - cc)
    peers = [(_flip(x, fx), _flip(y, fy), cc) for fx, fy in CHIP_FLIPS]
    chips = [2 * px + py for px, py, _ in peers]
    for a in range(n):
        mine = big_refs[a].at[cc, me]
        first = [pltpu.make_async_copy(h_refs[a], mine, lsem.at[a])]
        first += [_remote(h_refs[a], mine, ssem.at[7 * a + k], rsem.at[7 * a + k], peers[k]) for k in range(3)]
        first += [_remote(h_refs[a], mine, ssem.at[7 * a + 3], rsem.at[7 * a + 3], sib)]
        passed = []
        for k in range(3):
            blk = big_refs[a].at[cc, chips[k]]
            passed.append(_remote(blk, blk, ssem.at[7 * a + 4 + k], rsem.at[7 * a + 4 + k], sib))
        if phase == 0:
            for cp in first:
                cp.start()
        elif phase == 1:
            for k in range(3):
                blk = big_refs[a].at[cc, chips[k]]
                _remote(blk, blk, ssem.at[7 * a + k], rsem.at[7 * a + k], peers[k]).wait_recv()
                passed[k].start()
        else:
            theirs = big_refs[a].at[1 - cc, me]
            _remote(theirs, theirs, ssem.at[7 * a + 3], rsem.at[7 * a + 3], sib).wait_recv()
            for k in range(3):
                blk = big_refs[a].at[1 - cc, chips[k]]
                _remote(blk, blk, ssem.at[7 * a + 4 + k], rsem.at[7 * a + 4 + k], sib).wait_recv()
            first[0].wait()
            for cp in first[1:] + passed:
                cp.wait_send()


def _ag_devices(name, v):
    r, c = v.shape

    def body(v_ref, out_ref, lsem, ssem, rsem):
        x, y, cc = _pos()
        me = 4 * x + 2 * y + cc
        loc = pltpu.make_async_copy(v_ref, out_ref.at[me], lsem)
        loc.start()
        started = []
        for k, (fx, fy, fc) in enumerate(DEV_FLIPS):
            cp = _remote(v_ref, out_ref.at[me], ssem.at[k], rsem.at[k], (_flip(x, fx), _flip(y, fy), _flip(cc, fc)))
            cp.start()
            started.append(cp)
        for k, (fx, fy, fc) in enumerate(DEV_FLIPS):
            px, py, pc = _flip(x, fx), _flip(y, fy), _flip(cc, fc)
            blk = out_ref.at[4 * px + 2 * py + pc]
            _remote(blk, blk, ssem.at[k], rsem.at[k], (px, py, pc)).wait_recv()
        for cp in started:
            cp.wait_send()
        loc.wait()

    return _call(body, name=name, in_specs=[ANY], out_specs=ANY,
                 out_shape=jax.ShapeDtypeStruct((8, r, c), v.dtype),
                 scratch_shapes=[pltpu.SemaphoreType.DMA, pltpu.SemaphoreType.DMA((7,)), pltpu.SemaphoreType.DMA((7,))])(v)


def _sibling_other_half(name, g2s):
    n = len(g2s)

    def body(*refs):
        g_refs, out_refs, sems = refs[:n], refs[n:2 * n], refs[2 * n:]
        _sibling_phase(0, g_refs, out_refs, sems)
        _sibling_phase(1, g_refs, out_refs, sems)

    return _call(body, name=name, in_specs=[ANY] * n, out_specs=[ANY] * n,
                 out_shape=_sibling_out_shapes(g2s), scratch_shapes=_sibling_sems(n))(*g2s)


def _sibling_out_shapes(g2s):
    return [jax.ShapeDtypeStruct(g.shape[1:], g.dtype) for g in g2s]


def _sibling_sems(n):
    return [pltpu.SemaphoreType.DMA((n,)), pltpu.SemaphoreType.DMA((n,))]


def _sibling_phase(phase, g_refs, out_refs, sems):
    ssem, rsem = sems
    x, y, cc = _pos()
    for a in range(len(g_refs)):
        cp = _remote(g_refs[a].at[1 - cc], out_refs[a], ssem.at[a], rsem.at[a], (x, y, 1 - cc))
        if phase == 0:
            cp.start()
        else:
            cp.wait()


def _scatter_sems(n):
    return [pltpu.SemaphoreType.DMA((n,)), pltpu.SemaphoreType.DMA((3 * n,)), pltpu.SemaphoreType.DMA((3 * n,))]


def _scatter_phase(phase, p_refs, out_refs, sems):
    lsem, ssem, rsem = sems
    x, y, cc = _pos()
    me = 2 * x + y
    peers = [(_flip(x, fx), _flip(y, fy), cc) for fx, fy in CHIP_FLIPS]
    for a in range(len(p_refs)):
        loc = pltpu.make_async_copy(p_refs[a].at[me], out_refs[a].at[me], lsem.at[a])
        sends = [_remote(p_refs[a].at[2 * px + py], out_refs[a].at[me], ssem.at[3 * a + k], rsem.at[3 * a + k], peers[k])
                 for k, (px, py, _) in enumerate(peers)]
        if phase == 0:
            loc.start()
            for cp in sends:
                cp.start()
        else:
            for k, (px, py, _) in enumerate(peers):
                blk = out_refs[a].at[2 * px + py]
                _remote(blk, blk, ssem.at[3 * a + k], rsem.at[3 * a + k], peers[k]).wait_recv()
            loc.wait()
            for cp in sends:
                cp.wait_send()


def _sibling_join(name, bufs):
    n = len(bufs)

    def body(*refs):
        out_refs, (ssem, rsem) = refs[n:2 * n], refs[2 * n:]
        x, y, cc = _pos()
        sib = (x, y, 1 - cc)
        cps = [_remote(out_refs[a].at[cc], out_refs[a].at[cc], ssem.at[a], rsem.at[a], sib) for a in range(n)]
        for cp in cps:
            cp.start()
        for a in range(n):
            theirs = out_refs[a].at[1 - cc]
            _remote(theirs, theirs, ssem.at[a], rsem.at[a], sib).wait_recv()
        for cp in cps:
            cp.wait_send()

    return _call(body, name=name, in_specs=[ANY] * n, out_specs=[ANY] * n,
                 out_shape=[jax.ShapeDtypeStruct(b.shape, b.dtype) for b in bufs],
                 input_output_aliases={a: a for a in range(n)},
                 scratch_shapes=[pltpu.SemaphoreType.DMA((n,)), pltpu.SemaphoreType.DMA((n,))])(*bufs)


def _pad_lanes(a, width=128):
    return jnp.pad(a, ((0, 0), (0, width - a.shape[1])))


def kernel(x, norm_gains, fox_w_in, fox_b_f, hgrn_w_in, hgrn_lb_logits, hgrn_onorm, w_out, final_gain, loss_target, m_norm_gains, m_fox_w_in, m_fox_b_f, m_hgrn_w_in, m_hgrn_lb_logits, m_hgrn_onorm, m_w_out, m_final_gain, v_norm_gains, v_fox_w_in, v_fox_b_f, v_hgrn_w_in, v_hgrn_lb_logits, v_hgrn_onorm, v_w_out, v_final_gain):
    s, d = x.shape[1], x.shape[2]
    wq = w_out.shape[1]
    w = 4 * wq
    hh = w // HEAD_DIM
    fox_cols = fox_w_in.shape[2]
    assert 4 * fox_cols == 4 * w + hh and hgrn_w_in.shape[2] == w
    core = lax.axis_index("c")

    x0 = x[0]
    tgt = loss_target[0]

    def my_half(a):
        return lax.dynamic_index_in_dim(a, core, 0, keepdims=False).astype(BF16)

    (big_fox,) = _ag_weights("ag_weights", [my_half(fox_w_in.reshape(2, d // 2, fox_cols))])
    later_halves = [my_half(hgrn_w_in.reshape(2, d // 2, w)), my_half(w_out),
                    lax.dynamic_index_in_dim(hgrn_onorm.reshape(2, 1, wq // 2), core, 0, keepdims=False)]
    fox_chip = [big_fox[:, j].reshape(d, fox_cols) for j in range(4)]
    def fox_columns(a, b):
        out = []
        for j in range(4):
            lo, hi = max(a, j * fox_cols), min(b, (j + 1) * fox_cols)
            if lo < hi:
                out.append(fox_chip[j][:, lo - j * fox_cols:hi - j * fox_cols])
        return out

    w_main = jnp.concatenate(fox_columns(0, 3 * w) + fox_columns(3 * w + hh, 4 * w + hh), axis=1)
    w_fl = _pad_lanes(jnp.concatenate(fox_columns(3 * w, 3 * w + hh), axis=1))

    bf_pad = _pad_lanes(fox_b_f)
    g0, g1 = norm_gains[0:1], norm_gains[1:2]
    gf = final_gain.reshape(1, d)

    h0 = _rms_fwd("rms0_fwd", x0, g0)
    qkv = _mm("fox_qkv", h0, w_main, out_dtype=BF16, n=3 * w, col_scale=(w, Q_FOLD))
    gate0 = _mm("fox_gate", h0, w_main, n=w, b_off=3 * w)
    fl = _mm("fox_flogit", h0, w_fl)
    _, c_rep = _fox_gate_fwd("fox_cumsum", fl, bf_pad, w, rep_scale=LOG2E)
    o0, lse_row, y0, big_hgrn, big_out, big_onorm = _attn_fwd("fox_attn_fwd", qkv, c_rep, gate0, hh, gather=later_halves)
    onorm_full = big_onorm.transpose(1, 0, 2, 3).reshape(1, w)
    w_o = big_out.reshape(2, w, d)
    x1 = _mm("fox_out", y0, w_o[0], res=x0)
    h1 = _rms_fwd("rms1_fwd", x1, g1)
    p1 = _mm("hgrn_in", h1, big_hgrn, b_gathered=True)
    y1, o1, states = _hgrn_fwd("hgrn_fwd", p1, hgrn_lb_logits, onorm_full, hh)
    x2 = _mm("hgrn_out", y1, w_o[1], res=x1)
    dx2, d_gf, loss_tile = _loss_head("loss_head", x2, tgt, gf)

    dy1 = _mm("hgrn_out_dy", dx2, w_o[1], tb=True)
    d_wo1 = _mm("hgrn_out_dw", y1, dx2, ta=True)
    dp1, d_lbl, d_onorm = _hgrn_bwd("hgrn_bwd", p1, hgrn_lb_logits, onorm_full, o1, dy1, states, hh)
    dh1 = _mm("hgrn_in_dh", dp1, big_hgrn, tb=True, b_gathered=True, a_stacked=True)
    g2_hgrn = _mm("hgrn_in_dw", h1, dp1, ta=True, out_gathered=(d // 2, w), b_stacked=True)
    dx1, d_g1 = _rms_bwd("rms1_bwd", x1, g1, dh1, dx2)
    dy0 = _mm("fox_out_dy", dx1, w_o[0], tb=True)
    d_wo0 = _mm("fox_out_dw", y0, dx1, ta=True)
    core_arr = core.reshape(1).astype(jnp.int32)
    g2_out = jnp.stack([d_wo0, d_wo1]).reshape(2, 4, wq, d)
    do0, dgate0, delta, sib_hgrn, sib_out = _fox_pre_bwd("fox_pre_bwd", dy0, o0, gate0, exchange=[g2_hgrn, g2_out])
    delta_row = delta[:, :hh].T.reshape(hh, 1, s)
    pairs_early = [_pair_add("rs_pair_add_hgrn", g2_hgrn, sib_hgrn, core_arr),
                   _pair_add("rs_pair_add_out", g2_out, sib_out, core_arr)]
    dk0, dv0, dc_rep, dq0, rowsum_row, got_hgrn, got_out = _attn_bwd(
        "fox_attn_bwd", qkv, do0, lse_row, delta_row, c_rep, hh, scatter=pairs_early)
    dfl, d_bf = _fox_gate_bwd("fox_cumsum_bwd", _pad_lanes(rowsum_row.reshape(hh, s).T),
                              _pad_lanes(dc_rep[:, ::HEAD_DIM]), fl, bf_pad)
    dp0 = jnp.concatenate([dq0, dk0, dv0, dgate0], axis=1)
    d_wmain = _mm("fox_in_dw", h0, dp0, ta=True)
    d_wfl = _mm("fox_fl_dw", h0, dfl, ta=True)

    def grad_columns(a, b):
        out = []
        for lo, hi, src, shift in ((0, 3 * w, d_wmain, 0), (3 * w, 3 * w + hh, d_wfl, 3 * w),
                                   (3 * w + hh, 4 * w + hh, d_wmain, hh)):
            l2, h2 = max(a, lo), min(b, hi)
            if l2 < h2:
                out.append(src[:, l2 - shift:h2 - shift])
        return out

    g2_fox = jnp.stack([jnp.concatenate(grad_columns(j * fox_cols, (j + 1) * fox_cols), axis=1).reshape(2, d // 2, fox_cols)
                        for j in range(4)], axis=1)
    (sib_fox,) = _sibling_other_half("rs_sibling", [g2_fox])
    dh0, got_fox = _mm("fox_in_dh", dp0, w_main, tb=True, res=_mm("fox_fl_dh", dfl, w_fl, tb=True),
                       scatter=[_pair_add("rs_pair_add_fox", g2_fox, sib_fox, core_arr)])
    grad_x, d_g0 = _rms_bwd("rms0_bwd", x0, g0, dh0, dx1)
    halves_sum = [_sum_slots("rs_sum_" + nm, got, core_arr)
                  for nm, got in zip(("fox", "hgrn", "out"), (got_fox, got_hgrn, got_out))]
    r_fox, r_hgrn, r_out = _sibling_join("rs_join", halves_sum)
    g_fox = r_fox.reshape(d, fox_cols)
    g_hgrn = r_hgrn.reshape(d, w)
    g_out = r_out.reshape(2 * wq, d)

    small = jnp.concatenate([jnp.concatenate([d_g0, d_g1], axis=0).reshape(-1), d_lbl.reshape(-1), d_gf.reshape(-1),
                             d_onorm.reshape(-1), d_bf.reshape(-1)])
    n_small = small.shape[0]
    pad_to = -(-n_small // 1024) * 1024
    small = jnp.pad(small, (0, pad_to - n_small)).reshape(pad_to // 128, 128)
    small = _sum_slots("small_sum", _ag_devices("small_gather", small)).reshape(-1)
    g_norm = small[:2 * d].reshape(2, d)
    g_lbl = small[2 * d:2 * d + 2 * w].reshape(2, w)
    g_gf = small[2 * d + 2 * w:3 * d + 2 * w]
    g_onorm_full = small[3 * d + 2 * w:3 * d + 3 * w]
    g_bf = small[3 * d + 3 * w:3 * d + 3 * w + hh].reshape(1, hh)
    chip = 2 * lax.axis_index("x") + lax.axis_index("y")
    g_onorm = lax.dynamic_slice_in_dim(g_onorm_full, chip * wq, wq).reshape(1, wq)

    loss = lax.psum(loss_tile[0, 0], ("x", "y", "c"))

    def upd(name, wt, g, m, v):
        shp = wt.shape
        two = lambda a: a.reshape(-1, shp[-1])
        dl, nm, nv = _adamw(name, two(wt), two(g), two(m), two(v))
        return g.reshape(shp), dl.reshape(shp), nm.reshape(shp), nv.reshape(shp)

    res = [
        upd("adamw_norm_gains", norm_gains, g_norm, m_norm_gains, v_norm_gains),
        upd("adamw_fox_w_in", fox_w_in, g_fox, m_fox_w_in, v_fox_w_in),
        upd("adamw_fox_b_f", fox_b_f, g_bf, m_fox_b_f, v_fox_b_f),
        upd("adamw_hgrn_w_in", hgrn_w_in, g_hgrn, m_hgrn_w_in, v_hgrn_w_in),
        upd("adamw_lb_logits", hgrn_lb_logits, g_lbl, m_hgrn_lb_logits, v_hgrn_lb_logits),
        upd("adamw_onorm", hgrn_onorm, g_onorm, m_hgrn_onorm, v_hgrn_onorm),
        upd("adamw_w_out", w_out, g_out, m_w_out, v_w_out),
        upd("adamw_final_gain", final_gain.reshape(1, d), g_gf.reshape(1, d), m_final_gain.reshape(1, d),
            v_final_gain.reshape(1, d)),
    ]
    res[-1] = tuple(a.reshape(d) for a in res[-1])
    grads, deltas, new_m, new_v = zip(*res)
    return (loss, grad_x[None], *grads, *deltas, *new_m, *new_v)
```

```python
import functools

import jax
import jax.numpy as jnp
from jax import lax
from jax.experimental import pallas as pl
from jax.experimental.pallas import tpu as pltpu

F32 = jnp.float32
BF16 = jnp.bfloat16
MESH = pl.DeviceIdType.MESH
ANY = pl.BlockSpec(memory_space=pl.ANY)

EPS = 1e-6
HEAD_DIM = 128
HGRN_CHUNK = 64
HGRN_SUB = 32
EXP_CLAMP = 80.0
ATT_BLOCK = 512
HGRN_BLOCK = 2048
GATE_BLOCK = 512
ROW_BLOCK = 256
MM_TM, MM_TN, MM_TK = 1024, 1024, 2048
VMEM_LIMIT_V7X = 56 * 1024 * 1024

ADAM_LR, ADAM_B1, ADAM_B2, ADAM_EPS, ADAM_WD, ADAM_STEP = 0.001, 0.9, 0.999, 1e-08, 0.01, 10

NT = (((1,), (1,)), ((), ()))
TN = (((0,), (0,)), ((), ()))
NN = (((1,), (0,)), ((), ()))


def _call(body, **kw):
    return pl.pallas_call(body, **kw)


def _cp(dims=None):
    kw = dict(vmem_limit_bytes=VMEM_LIMIT_V7X)
    if dims is not None:
        kw["dimension_semantics"] = dims
    return pltpu.CompilerParams(**kw)


def _sigmoid(x):
    return 1.0 / (1.0 + jnp.exp(-x))


def _dot(a, b, dn=NN):
    return lax.dot_general(a.astype(BF16), b.astype(BF16), dn, preferred_element_type=F32)


def _split3(x):
    hi = x.astype(BF16)
    r1 = x - hi.astype(F32)
    mid = r1.astype(BF16)
    lo = (r1 - mid.astype(F32)).astype(BF16)
    return hi, mid, lo


def _dot_exact(m01, x, right=False):
    hi, mid, lo = _split3(x)
    if right:
        dot = lambda p: lax.dot_general(p, m01, NN, preferred_element_type=F32)
    else:
        dot = lambda p: lax.dot_general(m01, p, NN, preferred_element_type=F32)
    return dot(hi) + dot(mid) + dot(lo)


def _row_block(rows, cap):
    if rows <= cap:
        return rows
    best = None
    for t in range(16, cap + 1, 16):
        if rows % t == 0:
            best = t
    assert best is not None, rows
    return best


def _mm(name, a, b, *, ta=False, tb=False, out_dtype=F32, res=None, n=None, b_off=0, b_gathered=False,
        out_gathered=None, scatter=(), a_stacked=False, b_stacked=False, col_scale=None):
    if a_stacked:
        assert not ta
        m, k = a.shape[1], a.shape[0] * a.shape[2]
    else:
        m, k = (a.shape[1], a.shape[0]) if ta else a.shape
    if b_gathered:
        rh, cw = b.shape[2], b.shape[3]
        n_full = 2 * rh if tb else 4 * cw
    elif b_stacked:
        assert not tb
        n_full = b.shape[0] * b.shape[2]
    else:
        n_full = b.shape[0] if tb else b.shape[1]
    n = n_full if n is None else n
    tm, tn, tk = min(MM_TM, m), min(MM_TN, n), min(MM_TK, k)
    b_both = b_gathered and not tb and tk == 2 * rh
    if b_gathered and not b_both:
        tn, tk = (min(tn, rh), min(tk, cw)) if tb else (min(tn, cw), min(tk, rh))
    if b_both:
        tn = min(tn, cw)
    if out_gathered:
        tm, tn = min(tm, out_gathered[0]), min(tn, out_gathered[1])
    if a_stacked:
        tk = min(tk, a.shape[2])
    if b_stacked:
        tn = min(tn, b.shape[2])
    assert m % tm == 0 and n % tn == 0 and k % tk == 0 and b_off % tn == 0
    nk = k // tk
    jo = b_off // tn
    if a_stacked:
        pa = a.shape[2] // tk
        a_spec = pl.BlockSpec((1, tm, tk), lambda i, j, kk: (kk // pa, i, kk % pa))
    elif ta:
        a_spec = pl.BlockSpec((tk, tm), lambda i, j, kk: (kk, i))
    else:
        a_spec = pl.BlockSpec((tm, tk), lambda i, j, kk: (i, kk))
    if b_gathered and tb:
        pr, pc = rh // tn, cw // tk
        b_spec = pl.BlockSpec((1, 1, tn, tk), lambda i, j, kk: (j // pr, kk // pc, j % pr, kk % pc))
    elif b_both:
        pc = cw // tn
        b_spec = pl.BlockSpec((2, 1, rh, tn), lambda i, j, kk: (0, j // pc, 0, j % pc))
    elif b_gathered:
        pr, pc = rh // tk, cw // tn
        b_spec = pl.BlockSpec((1, 1, tk, tn), lambda i, j, kk: (kk // pr, j // pc, kk % pr, j % pc))
    elif b_stacked:
        pb = b.shape[2] // tn
        b_spec = pl.BlockSpec((1, tk, tn), lambda i, j, kk: (j // pb, kk, j % pb))
    elif tb:
        b_spec = pl.BlockSpec((tn, tk), lambda i, j, kk: (j + jo, kk))
    else:
        b_spec = pl.BlockSpec((tk, tn), lambda i, j, kk: (kk, j + jo))
    o_spec = pl.BlockSpec((tm, tn), lambda i, j, kk: (i, j))
    if out_gathered:
        assert m == 2 * out_gathered[0] and n == 4 * out_gathered[1] and res is None
        qr, qc = out_gathered[0] // tm, out_gathered[1] // tn
        o_spec = pl.BlockSpec((1, 1, tm, tn), lambda i, j, kk: (i // qr, j // qc, i % qr, j % qc))
    dn = (((0 if ta else 1,), (1 if tb else 0,)), ((), ()))
    has_res = res is not None
    ns = len(scatter)
    grid = (m // tm, n // tn, nk)

    def body(*refs):
        a_ref, b_ref = refs[:2]
        r_ref = refs[2] if has_res else None
        base = 2 + int(has_res)
        p_refs = refs[base:base + ns]
        o_ref = refs[base + ns]
        got_refs = refs[base + ns + 1:base + 2 * ns + 1]
        acc = refs[base + 2 * ns + 1]
        sems = refs[base + 2 * ns + 2:]
        kk = pl.program_id(2)
        if ns:
            first = jnp.logical_and(jnp.logical_and(pl.program_id(0) == 0, pl.program_id(1) == 0), kk == 0)
            last = jnp.logical_and(jnp.logical_and(pl.program_id(0) == grid[0] - 1, pl.program_id(1) == grid[1] - 1),
                                   kk == nk - 1)

            @pl.when(first)
            def _():
                _scatter_phase(0, p_refs, got_refs, sems)

        av = a_ref[0] if a_stacked else a_ref[...]
        if b_both:
            p = (lax.dot_general(av[:, :rh].astype(BF16), b_ref[0, 0].astype(BF16), dn, preferred_element_type=F32)
                 + lax.dot_general(av[:, rh:].astype(BF16), b_ref[1, 0].astype(BF16), dn, preferred_element_type=F32))
        else:
            bv = b_ref[0, 0] if b_gathered else (b_ref[0] if b_stacked else b_ref[...])
            p = lax.dot_general(av.astype(BF16), bv.astype(BF16), dn, preferred_element_type=F32)

        def finish(total):
            if r_ref is not None:
                total = total + r_ref[...]
            if col_scale is not None:
                total = total * jnp.where(pl.program_id(1) < col_scale[0] // tn, col_scale[1], 1.0)
            if out_gathered:
                o_ref[0, 0] = total.astype(out_dtype)
            else:
                o_ref[...] = total.astype(out_dtype)

        if nk == 1:
            finish(p)
        else:
            @pl.when(kk == 0)
            def _():
                acc[...] = p

            @pl.when(jnp.logical_and(kk > 0, kk < nk - 1))
            def _():
                acc[...] += p

            @pl.when(kk == nk - 1)
            def _():
                finish(acc[...] + p)

        if ns:
            @pl.when(last)
            def _():
                _scatter_phase(1, p_refs, got_refs, sems)

    ins = [a, b] + ([res] if has_res else []) + list(scatter)
    in_specs = [a_spec, b_spec] + ([o_spec] if has_res else []) + [ANY] * ns
    o_shape = jax.ShapeDtypeStruct((2, 4) + tuple(out_gathered) if out_gathered else (m, n), out_dtype)
    out = _call(
        body, name=name, grid=grid, in_specs=in_specs, out_specs=[o_spec] + [ANY] * ns,
        out_shape=[o_shape] + [jax.ShapeDtypeStruct(p.shape, p.dtype) for p in scatter],
        scratch_shapes=[pltpu.VMEM((tm, tn) if nk > 1 else (8, 128), F32)] + (_scatter_sems(ns) if ns else []),
        compiler_params=_cp(("arbitrary",) * 3 if ns else ("parallel", "parallel", "arbitrary")),
    )(*ins)
    return out if ns else out[0]


def _rms_fwd(name, x, g):
    s, d = x.shape
    tm = min(ROW_BLOCK, s)

    def body(x_ref, g_ref, h_ref):
        xv = x_ref[...]
        r = lax.rsqrt(jnp.mean(xv * xv, axis=-1, keepdims=True) + EPS)
        h_ref[...] = (xv * r * g_ref[...]).astype(BF16)

    row = pl.BlockSpec((tm, d), lambda i: (i, 0))
    vec = pl.BlockSpec((1, d), lambda i: (0, 0))
    return _call(body, name=name, grid=(s // tm,), in_specs=[row, vec], out_specs=row,
                 out_shape=jax.ShapeDtypeStruct((s, d), BF16), compiler_params=_cp(("parallel",)))(x, g)


def _rms_bwd(name, x, g, dh, dres):
    s, d = x.shape
    tm = min(ROW_BLOCK, s)

    def body(x_ref, g_ref, dh_ref, dres_ref, dx_ref, dg_ref):
        @pl.when(pl.program_id(0) == 0)
        def _():
            dg_ref[...] = jnp.zeros_like(dg_ref)

        xv = x_ref[...]
        r = lax.rsqrt(jnp.mean(xv * xv, axis=-1, keepdims=True) + EPS)
        xn = xv * r
        dhv = dh_ref[...]
        dxn = dhv * g_ref[...]
        dx_ref[...] = dres_ref[...] + r * (dxn - xn * jnp.mean(dxn * xn, axis=-1, keepdims=True))
        dg_ref[...] += jnp.sum(dhv * xn, axis=0, keepdims=True)

    row = pl.BlockSpec((tm, d), lambda i: (i, 0))
    vec = pl.BlockSpec((1, d), lambda i: (0, 0))
    return _call(body, name=name, grid=(s // tm,), in_specs=[row, vec, row, row], out_specs=[row, vec],
                 out_shape=[jax.ShapeDtypeStruct((s, d), F32), jax.ShapeDtypeStruct((1, d), F32)],
                 compiler_params=_cp(("arbitrary",)))(x, g, dh, dres)


def _loss_head(name, x, tgt, g):
    s, d = x.shape
    tm = min(ROW_BLOCK, s)
    nb = s // tm

    def body(x_ref, t_ref, g_ref, dx_ref, dg_ref, loss_ref, lacc):
        i = pl.program_id(0)

        @pl.when(i == 0)
        def _():
            dg_ref[...] = jnp.zeros_like(dg_ref)
            lacc[...] = jnp.zeros_like(lacc)

        xv = x_ref[...]
        gv = g_ref[...]
        r = lax.rsqrt(jnp.mean(xv * xv, axis=-1, keepdims=True) + EPS)
        xn = xv * r
        err = xn * gv - t_ref[...]
        lacc[...] += jnp.sum(err * err, axis=0, keepdims=True)
        dout = err * (1.0 / d)
        dg_ref[...] += jnp.sum(dout * xn, axis=0, keepdims=True)
        dxn = dout * gv
        dx_ref[...] = r * (dxn - xn * jnp.mean(dxn * xn, axis=-1, keepdims=True))

        @pl.when(i == nb - 1)
        def _():
            total = jnp.sum(lacc[...], axis=1, keepdims=True) * (0.5 / d)
            loss_ref[...] = jnp.broadcast_to(total, loss_ref.shape)

    row = pl.BlockSpec((tm, d), lambda i: (i, 0))
    vec = pl.BlockSpec((1, d), lambda i: (0, 0))
    one = pl.BlockSpec((1, 128), lambda i: (0, 0))
    return _call(body, name=name, grid=(nb,), in_specs=[row, row, vec], out_specs=[row, vec, one],
                 out_shape=[jax.ShapeDtypeStruct((s, d), F32), jax.ShapeDtypeStruct((1, d), F32),
                            jax.ShapeDtypeStruct((1, 128), F32)],
                 scratch_shapes=[pltpu.VMEM((1, d), F32)], compiler_params=_cp(("arbitrary",)))(x, tgt, g)


def _fox_gate_fwd(name, fl, bf, w, rep_scale=1.0):
    s = fl.shape[0]
    tb = min(GATE_BLOCK, s)

    def body(fl_ref, bf_ref, c_ref, crep_ref, carry):
        @pl.when(pl.program_id(0) == 0)
        def _():
            carry[...] = jnp.zeros_like(carry)

        z = fl_ref[...] + bf_ref[...]
        lf = jnp.minimum(z, 0.0) - jnp.log(1.0 + jnp.exp(-jnp.abs(z)))
        rows = lax.broadcasted_iota(jnp.int32, (tb, tb), 0)
        cols = lax.broadcasted_iota(jnp.int32, (tb, tb), 1)
        tri = (rows >= cols).astype(BF16)
        cs = _dot_exact(tri, lf) + carry[...]
        c_ref[...] = cs
        carry[...] = c_ref[tb - 1:tb, :]
        sel_r = lax.broadcasted_iota(jnp.int32, (128, w), 0)
        sel_c = lax.broadcasted_iota(jnp.int32, (128, w), 1)
        sel = (sel_r == sel_c // HEAD_DIM).astype(BF16)
        crep_ref[...] = _dot_exact(sel, cs * rep_scale, right=True)

    blk = pl.BlockSpec((tb, 128), lambda i: (i, 0))
    return _call(body, name=name, grid=(s // tb,),
                 in_specs=[blk, pl.BlockSpec((1, 128), lambda i: (0, 0))],
                 out_specs=[blk, pl.BlockSpec((tb, w), lambda i: (i, 0))],
                 out_shape=[jax.ShapeDtypeStruct((s, 128), F32), jax.ShapeDtypeStruct((s, w), F32)],
                 scratch_shapes=[pltpu.VMEM((1, 128), F32)], compiler_params=_cp(("arbitrary",)))(fl, bf)


def _fox_gate_bwd(name, drow, dcol, fl, bf):
    s = fl.shape[0]
    tb = min(GATE_BLOCK, s)
    nb = s // tb

    def body(dr_ref, dc_ref, fl_ref, bf_ref, dfl_ref, dbf_ref, carry, tmp):
        @pl.when(pl.program_id(0) == 0)
        def _():
            carry[...] = jnp.zeros_like(carry)
            dbf_ref[...] = jnp.zeros_like(dbf_ref)

        rows = lax.broadcasted_iota(jnp.int32, (tb, tb), 0)
        cols = lax.broadcasted_iota(jnp.int32, (tb, tb), 1)
        triu = (rows <= cols).astype(BF16)
        dlf = _dot_exact(triu, dr_ref[...] + dc_ref[...]) + carry[...]
        tmp[...] = dlf
        carry[...] = tmp[0:1, :]
        z = fl_ref[...] + bf_ref[...]
        dfl = dlf * (1.0 / (1.0 + jnp.exp(z)))
        dfl_ref[...] = dfl
        dbf_ref[...] += jnp.sum(dfl, axis=0, keepdims=True)

    blk = pl.BlockSpec((tb, 128), lambda i: (nb - 1 - i, 0))
    vec = pl.BlockSpec((1, 128), lambda i: (0, 0))
    return _call(body, name=name, grid=(nb,), in_specs=[blk, blk, blk, vec], out_specs=[blk, vec],
                 out_shape=[jax.ShapeDtypeStruct((s, 128), F32), jax.ShapeDtypeStruct((1, 128), F32)],
                 scratch_shapes=[pltpu.VMEM((1, 128), F32), pltpu.VMEM((tb, 128), F32)],
                 compiler_params=_cp(("arbitrary",)))(drow, dcol, fl, bf)


LOG2E = 1.4426950408889634
ATT_SCALE = HEAD_DIM ** -0.5
Q_FOLD = ATT_SCALE * LOG2E


def _attn_fwd(name, qkv, c_rep, gate, h_count, gather=()):
    s = qkv.shape[0]
    w = h_count * HEAD_DIM
    t = min(ATT_BLOCK, s)
    hp = 2 if h_count % 2 == 0 else 1
    wb = hp * HEAD_DIM
    ng = len(gather)
    nh, nq = h_count // hp, s // t

    def body(*refs):
        q_ref, k_ref, v_ref, c_ref, gate_ref = refs[:5]
        h_refs = refs[5:5 + ng]
        o_ref, lse_ref, y_ref = refs[5 + ng:8 + ng]
        big_refs = refs[8 + ng:8 + 2 * ng]
        m_s, l_s, acc_s = refs[8 + 2 * ng:11 + 2 * ng]
        sems = refs[11 + 2 * ng:]
        hs = pl.program_id(0)
        qi = pl.program_id(1)
        if ng:
            @pl.when(jnp.logical_and(hs == 0, qi == 0))
            def _():
                _ag_phase(0, h_refs, big_refs, sems)

            @pl.when(jnp.logical_and(hs == nh // 2, qi == 0))
            def _():
                _ag_phase(1, h_refs, big_refs, sems)

        m_s[...] = jnp.full(m_s.shape, -jnp.inf, F32)
        l_s[...] = jnp.zeros_like(l_s)
        acc_s[...] = jnp.zeros_like(acc_s)

        def step(kb, masked):
            off = pl.multiple_of(kb * t, t)
            for a in range(hp):
                cols_a = slice(a * HEAD_DIM, (a + 1) * HEAD_DIM)
                kk = k_ref[pl.ds(off, t), cols_a]
                vv = v_ref[pl.ds(off, t), cols_a]
                cc = jnp.tile(c_ref[pl.ds(off, t), cols_a], (1, t // HEAD_DIM))
                st = lax.dot_general(kk, q_ref[:, cols_a], NT, preferred_element_type=F32) - cc
                if masked:
                    rows = lax.broadcasted_iota(jnp.int32, (t, t), 0)
                    cols = lax.broadcasted_iota(jnp.int32, (t, t), 1)
                    st = jnp.where(cols >= rows, st, -jnp.inf)
                m_prev = m_s[a]
                m_new = jnp.maximum(m_prev, jnp.max(st, axis=0, keepdims=True))
                pt = jnp.exp2(st - m_new)
                alpha = jnp.exp2(m_prev - m_new)
                l_s[a] = alpha * l_s[a] + jnp.sum(pt, axis=0, keepdims=True)
                acc_s[a] = alpha * acc_s[a] + lax.dot_general(vv, pt.astype(BF16), TN, preferred_element_type=F32)
                m_s[a] = m_new

        def loop_body(pair, carry):
            step(2 * pair, False)
            step(2 * pair + 1, False)
            return carry

        lax.fori_loop(0, qi // 2, loop_body, 0)

        @pl.when(lax.rem(qi, 2) == 1)
        def _():
            step(qi - 1, False)

        step(qi, True)
        for a in range(hp):
            l = l_s[a]
            cols_a = slice(a * HEAD_DIM, (a + 1) * HEAD_DIM)
            oa = (acc_s[a] / l).T
            o_ref[:, cols_a] = oa
            g = gate_ref[:, cols_a]
            y_ref[:, cols_a] = (oa * (g * _sigmoid(g))).astype(BF16)
            lse_ref[a] = m_s[a] + jnp.log2(l)

        if ng:
            @pl.when(jnp.logical_and(hs == nh - 1, qi == nq - 1))
            def _():
                _ag_phase(2, h_refs, big_refs, sems)

    blk = lambda off: pl.BlockSpec((t, wb), lambda h, i: (i, off + h))
    whole = lambda off: pl.BlockSpec((s, wb), lambda h, i: (0, off + h))
    rowv = pl.BlockSpec((hp, 1, t), lambda h, i: (h, 0, i))
    return _call(
        body, name=name, grid=(nh, nq),
        in_specs=[blk(0), whole(nh), whole(2 * nh), whole(0), blk(0)] + [ANY] * ng,
        out_specs=[blk(0), rowv, blk(0)] + [ANY] * ng,
        out_shape=[jax.ShapeDtypeStruct((s, w), F32), jax.ShapeDtypeStruct((h_count, 1, s), F32),
                   jax.ShapeDtypeStruct((s, w), BF16)] + _ag_out_shapes(gather),
        scratch_shapes=[pltpu.VMEM((hp, 1, t), F32), pltpu.VMEM((hp, 1, t), F32), pltpu.VMEM((hp, HEAD_DIM, t), F32)]
        + (_ag_sems(ng) if ng else []),
        compiler_params=_cp(("arbitrary", "arbitrary") if ng else ("parallel", "arbitrary")),
    )(qkv, qkv, qkv, c_rep, gate, *gather)


def _attn_bwd(name, qkv, do, dgate, lse_row, delta_row, c_rep, h_count, scatter=()):
    s = qkv.shape[0]
    w = h_count * HEAD_DIM
    t = min(ATT_BLOCK, s)
    nq = s // t
    scale = HEAD_DIM ** -0.5
    ns = len(scatter)

    def body(*refs):
        k_ref, v_ref, q_ref, do_ref, lse_ref, dl_ref, c_ref, dgate_ref = refs[:8]
        p_refs = refs[8:8 + ns]
        dp_ref, dc_ref, rs_ref = refs[8 + ns:11 + ns]
        got_refs = refs[11 + ns:11 + 2 * ns]
        dk_s, dv_s, dc_s, dq_s, rs_s = refs[11 + 2 * ns:16 + 2 * ns]
        sems = refs[16 + 2 * ns:]
        kj = pl.program_id(1)
        if ns:
            @pl.when(jnp.logical_and(pl.program_id(0) == 0, kj == 0))
            def _():
                _scatter_phase(0, p_refs, got_refs, sems)

        kk = k_ref[...]
        vv = v_ref[...]
        ccol = jnp.tile(c_ref[...], (1, t // HEAD_DIM))
        dk_s[...] = jnp.zeros_like(dk_s)
        dv_s[...] = jnp.zeros_like(dv_s)
        dc_s[...] = jnp.zeros_like(dc_s)

        @pl.when(kj == 0)
        def _():
            dq_s[...] = jnp.zeros_like(dq_s)
            rs_s[...] = jnp.zeros_like(rs_s)

        def step(qb, masked):
            off = pl.multiple_of(qb * t, t)
            q = q_ref[pl.ds(off, t), :]
            dov = do_ref[pl.ds(off, t), :]
            st = lax.dot_general(kk, q, NT, preferred_element_type=F32) - ccol
            pt = jnp.exp2(st - lse_ref[0, :, pl.ds(off, t)])
            if masked:
                rows = lax.broadcasted_iota(jnp.int32, (t, t), 0)
                cols = lax.broadcasted_iota(jnp.int32, (t, t), 1)
                pt = jnp.where(cols >= rows, pt, 0.0)
            dv_s[...] += lax.dot_general(pt.astype(BF16), dov, NN, preferred_element_type=F32)
            dpt = lax.dot_general(vv, dov, NT, preferred_element_type=F32)
            dst = pt * (dpt - dl_ref[0, :, pl.ds(off, t)])
            dstb = dst.astype(BF16)
            dk_s[...] += lax.dot_general(dstb, q, NN, preferred_element_type=F32)
            dc_s[...] += jnp.sum(dst, axis=1, keepdims=True)
            dq_s[:, pl.ds(off, t)] += lax.dot_general(kk, dstb, TN, preferred_element_type=F32)
            rs_s[:, pl.ds(off, t)] += jnp.sum(dst, axis=0, keepdims=True)

        step(kj, True)

        def loop_body(qb, carry):
            step(qb, False)
            return carry

        lax.fori_loop(kj + 1, nq, loop_body, 0)
        rows_j = pl.ds(pl.multiple_of(kj * t, t), t)
        dp_ref[1, rows_j, :] = (dk_s[...] * (1.0 / LOG2E)).astype(BF16)
        dp_ref[2, rows_j, :] = dv_s[...].astype(BF16)
        dc_ref[...] = jnp.broadcast_to(-dc_s[...], dc_ref.shape)

        @pl.when(kj == nq - 1)
        def _():
            for b in range(nq):
                dp_ref[0, b * t:(b + 1) * t, :] = (dq_s[:, b * t:(b + 1) * t] * scale).T.astype(BF16)
            dp_ref[3] = dgate_ref[...]
            rs_ref[0] = rs_s[...]

        if ns:
            @pl.when(jnp.logical_and(pl.program_id(0) == h_count - 1, kj == nq - 1))
            def _():
                _scatter_phase(1, p_refs, got_refs, sems)

    hh = h_count
    blk = lambda off: pl.BlockSpec((t, HEAD_DIM), lambda h, j: (j, off + h))
    whole = pl.BlockSpec((s, HEAD_DIM), lambda h, j: (0, h))
    rowv = pl.BlockSpec((1, 1, s), lambda h, j: (h, 0, 0))
    return _call(
        body, name=name, grid=(hh, nq),
        in_specs=[blk(hh), blk(2 * hh), whole, whole, rowv, rowv, blk(0), whole] + [ANY] * ns,
        out_specs=[pl.BlockSpec((4, s, HEAD_DIM), lambda h, j: (0, 0, h)), blk(0), rowv] + [ANY] * ns,
        out_shape=[jax.ShapeDtypeStruct((4, s, w), BF16), jax.ShapeDtypeStruct((s, w), F32),
                   jax.ShapeDtypeStruct((hh, 1, s), F32)] + [jax.ShapeDtypeStruct(p.shape, p.dtype) for p in scatter],
        scratch_shapes=[pltpu.VMEM((t, HEAD_DIM), F32), pltpu.VMEM((t, HEAD_DIM), F32), pltpu.VMEM((t, 1), F32),
                        pltpu.VMEM((HEAD_DIM, s), F32), pltpu.VMEM((1, s), F32)] + (_scatter_sems(ns) if ns else []),
        compiler_params=_cp(("arbitrary", "arbitrary") if ns else ("parallel", "arbitrary")),
    )(qkv, qkv, qkv, do, lse_row, delta_row, c_rep, dgate, *scatter)


def _fox_pre_bwd(name, dy, o, gate, exchange=()):
    s, w = o.shape
    tm = min(ROW_BLOCK, s)
    ne = len(exchange)
    nb = s // tm

    def body(*refs):
        dy_ref, o_ref, g_ref = refs[:3]
        x_refs = refs[3:3 + ne]
        do_ref, dg_ref, dl_ref = refs[3 + ne:6 + ne]
        got_refs = refs[6 + ne:6 + 2 * ne]
        sems = refs[6 + 2 * ne:]
        if ne:
            @pl.when(pl.program_id(0) == 0)
            def _():
                _sibling_phase(0, x_refs, got_refs, sems)

            @pl.when(pl.program_id(0) == nb - 1)
            def _():
                _sibling_phase(1, x_refs, got_refs, sems)

        g = g_ref[...]
        sg = _sigmoid(g)
        dyv = dy_ref[...]
        ov = o_ref[...]
        dov = dyv * (g * sg)
        do_ref[...] = dov.astype(BF16)
        dg_ref[...] = (dyv * ov * (sg * (1.0 + g * (1.0 - sg)))).astype(BF16)
        sel_r = lax.broadcasted_iota(jnp.int32, (w, 128), 0)
        sel_c = lax.broadcasted_iota(jnp.int32, (w, 128), 1)
        sel = (sel_r // HEAD_DIM == sel_c).astype(BF16)
        dl_ref[...] = _dot_exact(sel, dov * ov, right=True)

    row = pl.BlockSpec((tm, w), lambda i: (i, 0))
    lanes = pl.BlockSpec((tm, 128), lambda i: (i, 0))
    return _call(body, name=name, grid=(nb,), in_specs=[row, row, row] + [ANY] * ne,
                 out_specs=[row, row, lanes] + [ANY] * ne,
                 out_shape=[jax.ShapeDtypeStruct((s, w), BF16), jax.ShapeDtypeStruct((s, w), BF16),
                            jax.ShapeDtypeStruct((s, 128), F32)] + _sibling_out_shapes(exchange),
                 scratch_shapes=_sibling_sems(ne) if ne else [],
                 compiler_params=_cp(("arbitrary",) if ne else ("parallel",)))(dy, o, gate, *exchange)


def _hgrn_chunk_terms(q_c, k_c, b_c, b_s, base):
    nsub = HGRN_CHUNK // HGRN_SUB
    refs = [jnp.zeros((1, HEAD_DIM), F32)]
    for i in range(1, nsub):
        r0 = base + i * HGRN_SUB - 1
        refs.append(b_s[r0:r0 + 1, :])
    rfull = jnp.concatenate([jnp.broadcast_to(r, (HGRN_SUB, HEAD_DIM)) for r in refs], axis=0)
    eq = jnp.exp(b_c - rfull)
    qe = q_c * eq
    es = [jnp.exp(jnp.minimum(r - b_c, EXP_CLAMP)) for r in refs]
    kes = [(k_c * e).astype(BF16) for e in es]
    return eq, qe, es, kes


def _chunk_cumsum(x, reverse=False):
    cc = HGRN_CHUNK
    nc = x.shape[0] // cc
    rows = lax.broadcasted_iota(jnp.int32, (cc, cc), 0)
    cols = lax.broadcasted_iota(jnp.int32, (cc, cc), 1)
    tri = ((rows <= cols) if reverse else (rows >= cols)).astype(BF16)
    wide = jnp.concatenate([x[n * cc:(n + 1) * cc] for n in range(nc)], axis=1)
    res = _dot_exact(tri, wide)
    return jnp.concatenate([res[:, n * HEAD_DIM:(n + 1) * HEAD_DIM] for n in range(nc)], axis=0)


def _hgrn_block_pre(q_ref, f_ref, lbl_ref, b_s):
    lb = _sigmoid(lbl_ref[1:2, :] - lbl_ref[0:1, :])
    qr = q_ref[...]
    sq = _sigmoid(qr)
    q = qr * sq
    fz = f_ref[...]
    sg = _sigmoid(fz)
    f = lb + (1.0 - lb) * sg
    g = jnp.log(f)
    k = (1.0 - lb) * (1.0 / (1.0 + jnp.exp(fz)))
    b = _chunk_cumsum(g)
    b_s[...] = b
    return lb, qr, sq, q, sg, f, k, b


def _hgrn_fwd(name, p1, lbl, onorm, h_count):
    s = p1.shape[0]
    w = h_count * HEAD_DIM
    t = min(HGRN_BLOCK, s)
    nc = t // HGRN_CHUNK
    nsub = HGRN_CHUNK // HGRN_SUB
    cc = HGRN_CHUNK

    def body(q_ref, f_ref, i_ref, g_ref, lbl_ref, on_ref, y_ref, o_ref, st_ref, state_s, b_s):
        @pl.when(pl.program_id(1) == 0)
        def _():
            state_s[...] = jnp.zeros_like(state_s)

        lb, qr, sq, q, sg, f, k, b = _hgrn_block_pre(q_ref, f_ref, lbl_ref, b_s)
        v = i_ref[...]
        r64 = lax.broadcasted_iota(jnp.int32, (cc, cc), 0)
        c64 = lax.broadcasted_iota(jnp.int32, (cc, cc), 1)
        for n in range(nc):
            sl = slice(n * cc, (n + 1) * cc)
            q_c, k_c, v_c, b_c = q[sl], k[sl], v[sl], b[sl]
            bl = b_s[n * cc + cc - 1:n * cc + cc, :]
            eq, qe, es, kes = _hgrn_chunk_terms(q_c, k_c, b_c, b_s, n * cc)
            qeb = qe.astype(BF16)
            a = jnp.concatenate(
                [lax.dot_general(qeb[i * HGRN_SUB:(i + 1) * HGRN_SUB], kes[i], NT, preferred_element_type=F32)
                 for i in range(nsub)], axis=0)
            a = jnp.where(r64 >= c64, a, 0.0)
            st = state_s[...]
            st_ref[0, n] = st.astype(BF16)
            inter = _dot(q_c * jnp.exp(b_c), st, NT)
            intra = _dot(a, v_c)
            o_ref[sl, :] = inter + intra
            kb = k_c * jnp.exp(bl - b_c)
            state_s[...] = st * jnp.exp(bl) + _dot(v_c, kb, TN)
        o = o_ref[...]
        rr = lax.rsqrt(jnp.mean(o * o, axis=-1, keepdims=True) + EPS)
        gate = g_ref[...]
        y_ref[...] = ((o * rr) * on_ref[...] * (gate * _sigmoid(gate))).astype(BF16)

    hh = h_count
    blk = lambda off: pl.BlockSpec((t, HEAD_DIM), lambda h, i: (i, off + h))
    return _call(
        body, name=name, grid=(hh, s // t),
        in_specs=[blk(0), blk(hh), blk(2 * hh), blk(3 * hh),
                  pl.BlockSpec((2, HEAD_DIM), lambda h, i: (0, h)), pl.BlockSpec((1, HEAD_DIM), lambda h, i: (0, h))],
        out_specs=[blk(0), blk(0), pl.BlockSpec((1, nc, HEAD_DIM, HEAD_DIM), lambda h, i: (h, i, 0, 0))],
        out_shape=[jax.ShapeDtypeStruct((s, w), BF16), jax.ShapeDtypeStruct((s, w), F32),
                   jax.ShapeDtypeStruct((hh, s // cc, HEAD_DIM, HEAD_DIM), BF16)],
        scratch_shapes=[pltpu.VMEM((HEAD_DIM, HEAD_DIM), F32), pltpu.VMEM((t, HEAD_DIM), F32)],
        compiler_params=_cp(("parallel", "arbitrary")),
    )(p1, p1, p1, p1, lbl, onorm)


def _hgrn_bwd(name, p1, lbl, onorm, o, dy, states, h_count):
    s = p1.shape[0]
    w = h_count * HEAD_DIM
    t = min(HGRN_BLOCK, s)
    nb = s // t
    nc = t // HGRN_CHUNK
    nsub = HGRN_CHUNK // HGRN_SUB
    cc = HGRN_CHUNK

    def body(q_ref, f_ref, i_ref, g_ref, lbl_ref, on_ref, o_ref, dy_ref, st_ref,
             dp_ref, dlog_ref, don_ref,
             dst_s, b_s, do_s, dqs, dks, dbs, exs):
        @pl.when(pl.program_id(1) == 0)
        def _():
            dst_s[...] = jnp.zeros_like(dst_s)
            dlog_ref[...] = jnp.zeros_like(dlog_ref)
            don_ref[...] = jnp.zeros_like(don_ref)

        lb, qr, sq, q, sg, f, k, b = _hgrn_block_pre(q_ref, f_ref, lbl_ref, b_s)
        v = i_ref[...]

        ov = o_ref[...]
        rr = lax.rsqrt(jnp.mean(ov * ov, axis=-1, keepdims=True) + EPS)
        on = ov * rr
        gate = g_ref[...]
        sgt = _sigmoid(gate)
        silu = gate * sgt
        dyv = dy_ref[...]
        gain = on_ref[...]
        dp_ref[3] = (dyv * on * gain * (sgt * (1.0 + gate * (1.0 - sgt)))).astype(BF16)
        don_ref[...] += jnp.sum(dyv * on * silu, axis=0, keepdims=True)
        d_on = dyv * gain * silu
        do_s[...] = rr * (d_on - on * jnp.mean(d_on * on, axis=-1, keepdims=True))

        r64 = lax.broadcasted_iota(jnp.int32, (cc, cc), 0)
        c64 = lax.broadcasted_iota(jnp.int32, (cc, cc), 1)
        upper = r64 <= c64
        for n in reversed(range(nc)):
            sl = slice(n * cc, (n + 1) * cc)
            q_c, k_c, v_c, b_c = q[sl], k[sl], v[sl], b[sl]
            do_c = do_s[sl, :]
            bl = b_s[n * cc + cc - 1:n * cc + cc, :]
            ebl = jnp.exp(bl)
            eq, qe, es, kes = _hgrn_chunk_terms(q_c, k_c, b_c, b_s, n * cc)
            qeb = qe.astype(BF16)
            dob = do_c.astype(BF16)
            vb = v_c.astype(BF16)
            st = st_ref[0, n].astype(F32)
            dstn = dst_s[...]
            qb_ = q_c * jnp.exp(b_c)
            kb_ = k_c * jnp.exp(bl - b_c)
            at = jnp.zeros((cc, cc), F32)
            for i in range(nsub):
                blk_i = (c64 // HGRN_SUB) == i
                at = at + jnp.where(blk_i, lax.dot_general(kes[i], qeb, NT, preferred_element_type=F32), 0.0)
            at = jnp.where(upper, at, 0.0)
            dv = _dot(at, dob) + _dot(kb_, dstn, NT)
            dqb = _dot(dob, st)
            dkb = _dot(vb, dstn)
            da = jnp.where(r64 >= c64, lax.dot_general(dob, vb, NT, preferred_element_type=F32), 0.0)
            dat = jnp.where(upper, lax.dot_general(vb, dob, NT, preferred_element_type=F32), 0.0)
            dab = da.astype(BF16)
            dq_raw = jnp.concatenate(
                [lax.dot_general(dab[i * HGRN_SUB:(i + 1) * HGRN_SUB], kes[i], NN, preferred_element_type=F32)
                 for i in range(nsub)], axis=0)
            db = qb_.astype(BF16).astype(F32) * dqb + qeb.astype(F32) * dq_raw
            dkbk = dkb * kb_.astype(BF16).astype(F32)
            db = db - dkbk
            dk_in = jnp.zeros((cc, HEAD_DIM), F32)
            for i in range(nsub):
                blk_i = (c64 // HGRN_SUB) == i
                dk_raw = _dot(jnp.where(blk_i, dat, 0.0), qeb)
                dk_in = dk_in + dk_raw * es[i]
                db = db - kes[i].astype(F32) * dk_raw
            dqs[sl, :] = dqb * jnp.exp(b_c) + dq_raw * eq
            dks[sl, :] = dkb * jnp.exp(bl - b_c) + dk_in
            dbs[sl, :] = db
            extra = jnp.sum(dkbk, axis=0, keepdims=True) + jnp.sum(dstn * st, axis=0, keepdims=True) * ebl
            exs[sl, :] = jnp.broadcast_to(extra, (cc, HEAD_DIM))
            dp_ref[2, sl, :] = dv.astype(BF16)
            dst_s[...] = dstn * ebl + _dot(dob, qb_, TN)

        dg = _chunk_cumsum(dbs[...], reverse=True) + exs[...]
        df = dg / f - dks[...]
        dp_ref[1] = (df * (1.0 - lb) * sg * (1.0 - sg)).astype(BF16)
        dp_ref[0] = (dqs[...] * (sq * (1.0 + qr * (1.0 - sq)))).astype(BF16)
        dlb = jnp.sum(df * (1.0 - sg), axis=0, keepdims=True) * (lb * (1.0 - lb))
        dlog_ref[0:1, :] += -dlb
        dlog_ref[1:2, :] += dlb

    hh = h_count
    blk = lambda off: pl.BlockSpec((t, HEAD_DIM), lambda h, i: (nb - 1 - i, off + h))
    two = pl.BlockSpec((2, HEAD_DIM), lambda h, i: (0, h))
    one = pl.BlockSpec((1, HEAD_DIM), lambda h, i: (0, h))
    tile = pltpu.VMEM((t, HEAD_DIM), F32)
    return _call(
        body, name=name, grid=(hh, nb),
        in_specs=[blk(0), blk(hh), blk(2 * hh), blk(3 * hh), two, one, blk(0), blk(0),
                  pl.BlockSpec((1, nc, HEAD_DIM, HEAD_DIM), lambda h, i: (h, nb - 1 - i, 0, 0))],
        out_specs=[pl.BlockSpec((4, t, HEAD_DIM), lambda h, i: (0, nb - 1 - i, h)), two, one],
        out_shape=[jax.ShapeDtypeStruct((4, s, w), BF16), jax.ShapeDtypeStruct((2, w), F32),
                   jax.ShapeDtypeStruct((1, w), F32)],
        scratch_shapes=[pltpu.VMEM((HEAD_DIM, HEAD_DIM), F32), tile, tile, tile, tile, tile, tile],
        compiler_params=_cp(("parallel", "arbitrary")),
    )(p1, p1, p1, p1, lbl, onorm, o, dy, states)


def _adamw(name, w, g, m, v):
    r, c = w.shape
    tr = r if r <= 128 else _row_block(r, 128)
    c1 = 1.0 - ADAM_B1 ** ADAM_STEP
    c2 = 1.0 - ADAM_B2 ** ADAM_STEP

    def body(w_ref, g_ref, m_ref, v_ref, d_ref, nm_ref, nv_ref):
        gv = g_ref[...]
        nm = ADAM_B1 * m_ref[...] + (1.0 - ADAM_B1) * gv
        nv = ADAM_B2 * v_ref[...] + (1.0 - ADAM_B2) * (gv * gv)
        nm_ref[...] = nm
        nv_ref[...] = nv
        d_ref[...] = -ADAM_LR * ((nm / c1) / (jnp.sqrt(nv / c2) + ADAM_EPS) + ADAM_WD * w_ref[...])

    blk = pl.BlockSpec((tr, c), lambda i: (i, 0))
    sh = jax.ShapeDtypeStruct((r, c), F32)
    return _call(body, name=name, grid=(r // tr,), in_specs=[blk] * 4, out_specs=[blk] * 3,
                 out_shape=[sh, sh, sh], compiler_params=_cp(("parallel",)))(w, g, m, v)


SLAB_BLOCK_BYTES = 2 * 1024 * 1024


def _slab_rows(r, c):
    return _row_block(r, max(16, SLAB_BLOCK_BYTES // (4 * c) // 16 * 16))


def _pair_add(name, g2, recv, core):
    _, nch, r, c = g2.shape
    tr = _slab_rows(r, c)

    grid_spec = pltpu.PrefetchScalarGridSpec(
        num_scalar_prefetch=1, grid=(nch, r // tr),
        in_specs=[pl.BlockSpec((1, 1, tr, c), lambda j, i, cr: (cr[0], j, i, 0)),
                  pl.BlockSpec((1, tr, c), lambda j, i, cr: (j, i, 0))],
        out_specs=pl.BlockSpec((1, tr, c), lambda j, i, cr: (j, i, 0)))

    def body(core_ref, a_ref, b_ref, o_ref):
        o_ref[...] = (a_ref[0] + b_ref[...]).astype(BF16)

    return _call(body, name=name, grid_spec=grid_spec, out_shape=jax.ShapeDtypeStruct((nch, r, c), BF16),
                 compiler_params=_cp(("parallel", "parallel")))(core, g2, recv)


def _sum_slots(name, x, core=None):
    n, r, c = x.shape
    tr = _slab_rows(r, c)

    def total(x_ref):
        acc = x_ref[0].astype(F32)
        for j in range(1, n):
            acc = acc + x_ref[j].astype(F32)
        return acc

    if core is None:
        def body(x_ref, o_ref):
            o_ref[...] = total(x_ref)

        return _call(body, name=name, grid=(r // tr,),
                     in_specs=[pl.BlockSpec((n, tr, c), lambda i: (0, i, 0))],
                     out_specs=pl.BlockSpec((tr, c), lambda i: (i, 0)),
                     out_shape=jax.ShapeDtypeStruct((r, c), F32), compiler_params=_cp(("parallel",)))(x)

    def body_half(core_ref, x_ref, o_ref):
        o_ref[0] = total(x_ref)

    grid_spec = pltpu.PrefetchScalarGridSpec(
        num_scalar_prefetch=1, grid=(r // tr,),
        in_specs=[pl.BlockSpec((n, tr, c), lambda i, cr: (0, i, 0))],
        out_specs=pl.BlockSpec((1, tr, c), lambda i, cr: (cr[0], i, 0)))
    return _call(body_half, name=name, grid_spec=grid_spec, out_shape=jax.ShapeDtypeStruct((2, r, c), F32),
                 compiler_params=_cp(("parallel",)))(core, x)


def _pos():
    return lax.axis_index("x"), lax.axis_index("y"), lax.axis_index("c")


def _flip(v, f):
    return (1 - v) if f else v


CHIP_FLIPS = ((0, 1), (1, 0), (1, 1))
DEV_FLIPS = tuple((fx, fy, fc) for fx in (0, 1) for fy in (0, 1) for fc in (0, 1))[1:]


def _remote(src, dst, ssem, rsem, dev):
    return pltpu.make_async_remote_copy(src_ref=src, dst_ref=dst, send_sem=ssem, recv_sem=rsem,
                                        device_id=dev, device_id_type=MESH)


def _ag_weights(name, halves):
    n = len(halves)

    def body(*refs):
        h_refs, big_refs, sems = refs[:n], refs[n:2 * n], refs[2 * n:]
        _ag_phase(0, h_refs, big_refs, sems)
        _ag_phase(1, h_refs, big_refs, sems)
        _ag_phase(2, h_refs, big_refs, sems)

    return _call(body, name=name, in_specs=[ANY] * n, out_specs=[ANY] * n,
                 out_shape=_ag_out_shapes(halves), scratch_shapes=_ag_sems(n))(*halves)


def _ag_out_shapes(halves):
    return [jax.ShapeDtypeStruct((2, 4) + h.shape, h.dtype) for h in halves]


def _ag_sems(n):
    return [pltpu.SemaphoreType.DMA((n,)), pltpu.SemaphoreType.DMA((7 * n,)), pltpu.SemaphoreType.DMA((7 * n,))]


def _ag_phase(phase, h_refs, big_refs, sems):
    lsem, ssem, rsem = sems
    n = len(h_refs)
    x, y, cc = _pos()
    me = 2 * x + y
    sib = (x, y, 1 - cc)
    peers = [(_flip(x, fx), _flip(y, fy), cc) for fx, fy in CHIP_FLIPS]
    chips = [2 * px + py for px, py, _ in peers]
    for a in range(n):
        mine = big_refs[a].at[cc, me]
        first = [pltpu.make_async_copy(h_refs[a], mine, lsem.at[a])]
        first += [_remote(h_refs[a], mine, ssem.at[7 * a + k], rsem.at[7 * a + k], peers[k]) for k in range(3)]
        first += [_remote(h_refs[a], mine, ssem.at[7 * a + 3], rsem.at[7 * a + 3], sib)]
        passed = []
        for k in range(3):
            blk = big_refs[a].at[cc, chips[k]]
            passed.append(_remote(blk, blk, ssem.at[7 * a + 4 + k], rsem.at[7 * a + 4 + k], sib))
        if phase == 0:
            for cp in first:
                cp.start()
        elif phase == 1:
            for k in range(3):
                blk = big_refs[a].at[cc, chips[k]]
                _remote(blk, blk, ssem.at[7 * a + k], rsem.at[7 * a + k], peers[k]).wait_recv()
                passed[k].start()
        else:
            theirs = big_refs[a].at[1 - cc, me]
            _remote(theirs, theirs, ssem.at[7 * a + 3], rsem.at[7 * a + 3], sib).wait_recv()
            for k in range(3):
                blk = big_refs[a].at[1 - cc, chips[k]]
                _remote(blk, blk, ssem.at[7 * a + 4 + k], rsem.at[7 * a + 4 + k], sib).wait_recv()
            first[0].wait()
            for cp in first[1:] + passed:
                cp.wait_send()


def _ag_devices(name, v):
    r, c = v.shape

    def body(v_ref, out_ref, lsem, ssem, rsem):
        x, y, cc = _pos()
        me = 4 * x + 2 * y + cc
        loc = pltpu.make_async_copy(v_ref, out_ref.at[me], lsem)
        loc.start()
        started = []
        for k, (fx, fy, fc) in enumerate(DEV_FLIPS):
            cp = _remote(v_ref, out_ref.at[me], ssem.at[k], rsem.at[k], (_flip(x, fx), _flip(y, fy), _flip(cc, fc)))
            cp.start()
            started.append(cp)
        for k, (fx, fy, fc) in enumerate(DEV_FLIPS):
            px, py, pc = _flip(x, fx), _flip(y, fy), _flip(cc, fc)
            blk = out_ref.at[4 * px + 2 * py + pc]
            _remote(blk, blk, ssem.at[k], rsem.at[k], (px, py, pc)).wait_recv()
        for cp in started:
            cp.wait_send()
        loc.wait()

    return _call(body, name=name, in_specs=[ANY], out_specs=ANY,
                 out_shape=jax.ShapeDtypeStruct((8, r, c), v.dtype),
                 scratch_shapes=[pltpu.SemaphoreType.DMA, pltpu.SemaphoreType.DMA((7,)), pltpu.SemaphoreType.DMA((7,))])(v)


def _sibling_other_half(name, g2s):
    n = len(g2s)

    def body(*refs):
        g_refs, out_refs, sems = refs[:n], refs[n:2 * n], refs[2 * n:]
        _sibling_phase(0, g_refs, out_refs, sems)
        _sibling_phase(1, g_refs, out_refs, sems)

    return _call(body, name=name, in_specs=[ANY] * n, out_specs=[ANY] * n,
                 out_shape=_sibling_out_shapes(g2s), scratch_shapes=_sibling_sems(n))(*g2s)


def _sibling_out_shapes(g2s):
    return [jax.ShapeDtypeStruct(g.shape[1:], g.dtype) for g in g2s]


def _sibling_sems(n):
    return [pltpu.SemaphoreType.DMA((n,)), pltpu.SemaphoreType.DMA((n,))]


def _sibling_phase(phase, g_refs, out_refs, sems):
    ssem, rsem = sems
    x, y, cc = _pos()
    for a in range(len(g_refs)):
        cp = _remote(g_refs[a].at[1 - cc], out_refs[a], ssem.at[a], rsem.at[a], (x, y, 1 - cc))
        if phase == 0:
            cp.start()
        else:
            cp.wait()


def _scatter_sems(n):
    return [pltpu.SemaphoreType.DMA((n,)), pltpu.SemaphoreType.DMA((3 * n,)), pltpu.SemaphoreType.DMA((3 * n,))]


def _scatter_phase(phase, p_refs, out_refs, sems):
    lsem, ssem, rsem = sems
    x, y, cc = _pos()
    me = 2 * x + y
    peers = [(_flip(x, fx), _flip(y, fy), cc) for fx, fy in CHIP_FLIPS]
    for a in range(len(p_refs)):
        loc = pltpu.make_async_copy(p_refs[a].at[me], out_refs[a].at[me], lsem.at[a])
        sends = [_remote(p_refs[a].at[2 * px + py], out_refs[a].at[me], ssem.at[3 * a + k], rsem.at[3 * a + k], peers[k])
                 for k, (px, py, _) in enumerate(peers)]
        if phase == 0:
            loc.start()
            for cp in sends:
                cp.start()
        else:
            for k, (px, py, _) in enumerate(peers):
                blk = out_refs[a].at[2 * px + py]
                _remote(blk, blk, ssem.at[3 * a + k], rsem.at[3 * a + k], peers[k]).wait_recv()
            loc.wait()
            for cp in sends:
                cp.wait_send()


def _sibling_join(name, bufs):
    n = len(bufs)

    def body(*refs):
        out_refs, (ssem, rsem) = refs[n:2 * n], refs[2 * n:]
        x, y, cc = _pos()
        sib = (x, y, 1 - cc)
        cps = [_remote(out_refs[a].at[cc], out_refs[a].at[cc], ssem.at[a], rsem.at[a], sib) for a in range(n)]
        for cp in cps:
            cp.start()
        for a in range(n):
            theirs = out_refs[a].at[1 - cc]
            _remote(theirs, theirs, ssem.at[a], rsem.at[a], sib).wait_recv()
        for cp in cps:
            cp.wait_send()

    return _call(body, name=name, in_specs=[ANY] * n, out_specs=[ANY] * n,
                 out_shape=[jax.ShapeDtypeStruct(b.shape, b.dtype) for b in bufs],
                 input_output_aliases={a: a for a in range(n)},
                 scratch_shapes=[pltpu.SemaphoreType.DMA((n,)), pltpu.SemaphoreType.DMA((n,))])(*bufs)


def _pad_lanes(a, width=128):
    return jnp.pad(a, ((0, 0), (0, width - a.shape[1])))


def kernel(x, norm_gains, fox_w_in, fox_b_f, hgrn_w_in, hgrn_lb_logits, hgrn_onorm, w_out, final_gain, loss_target, m_norm_gains, m_fox_w_in, m_fox_b_f, m_hgrn_w_in, m_hgrn_lb_logits, m_hgrn_onorm, m_w_out, m_final_gain, v_norm_gains, v_fox_w_in, v_fox_b_f, v_hgrn_w_in, v_hgrn_lb_logits, v_hgrn_onorm, v_w_out, v_final_gain):
    s, d = x.shape[1], x.shape[2]
    wq = w_out.shape[1]
    w = 4 * wq
    hh = w // HEAD_DIM
    fox_cols = fox_w_in.shape[2]
    assert 4 * fox_cols == 4 * w + hh and hgrn_w_in.shape[2] == w
    core = lax.axis_index("c")

    x0 = x[0]
    tgt = loss_target[0]

    def my_half(a):
        return lax.dynamic_index_in_dim(a, core, 0, keepdims=False).astype(BF16)

    (big_fox,) = _ag_weights("ag_weights", [my_half(fox_w_in.reshape(2, d // 2, fox_cols))])
    later_halves = [my_half(hgrn_w_in.reshape(2, d // 2, w)), my_half(w_out),
                    lax.dynamic_index_in_dim(hgrn_onorm.reshape(2, 1, wq // 2), core, 0, keepdims=False)]
    fox_chip = [big_fox[:, j].reshape(d, fox_cols) for j in range(4)]
    def fox_columns(a, b):
        out = []
        for j in range(4):
            lo, hi = max(a, j * fox_cols), min(b, (j + 1) * fox_cols)
            if lo < hi:
                out.append(fox_chip[j][:, lo - j * fox_cols:hi - j * fox_cols])
        return out

    w_main = jnp.concatenate(fox_columns(0, 3 * w) + fox_columns(3 * w + hh, 4 * w + hh), axis=1)
    w_fl = _pad_lanes(jnp.concatenate(fox_columns(3 * w, 3 * w + hh), axis=1))

    bf_pad = _pad_lanes(fox_b_f)
    g0, g1 = norm_gains[0:1], norm_gains[1:2]
    gf = final_gain.reshape(1, d)

    h0 = _rms_fwd("rms0_fwd", x0, g0)
    qkv = _mm("fox_qkv", h0, w_main, out_dtype=BF16, n=3 * w, col_scale=(w, Q_FOLD))
    gate0 = _mm("fox_gate", h0, w_main, n=w, b_off=3 * w)
    fl = _mm("fox_flogit", h0, w_fl)
    _, c_rep = _fox_gate_fwd("fox_cumsum", fl, bf_pad, w, rep_scale=LOG2E)
    o0, lse_row, y0, big_hgrn, big_out, big_onorm = _attn_fwd("fox_attn_fwd", qkv, c_rep, gate0, hh, gather=later_halves)
    onorm_full = big_onorm.transpose(1, 0, 2, 3).reshape(1, w)
    w_o = big_out.reshape(2, w, d)
    x1 = _mm("fox_out", y0, w_o[0], res=x0)
    h1 = _rms_fwd("rms1_fwd", x1, g1)
    p1 = _mm("hgrn_in", h1, big_hgrn, b_gathered=True)
    y1, o1, states = _hgrn_fwd("hgrn_fwd", p1, hgrn_lb_logits, onorm_full, hh)
    x2 = _mm("hgrn_out", y1, w_o[1], res=x1)
    dx2, d_gf, loss_tile = _loss_head("loss_head", x2, tgt, gf)

    dy1 = _mm("hgrn_out_dy", dx2, w_o[1], tb=True)
    d_wo1 = _mm("hgrn_out_dw", y1, dx2, ta=True)
    dp1, d_lbl, d_onorm = _hgrn_bwd("hgrn_bwd", p1, hgrn_lb_logits, onorm_full, o1, dy1, states, hh)
    dh1 = _mm("hgrn_in_dh", dp1, big_hgrn, tb=True, b_gathered=True, a_stacked=True)
    g2_hgrn = _mm("hgrn_in_dw", h1, dp1, ta=True, out_gathered=(d // 2, w), b_stacked=True)
    dx1, d_g1 = _rms_bwd("rms1_bwd", x1, g1, dh1, dx2)
    dy0 = _mm("fox_out_dy", dx1, w_o[0], tb=True)
    d_wo0 = _mm("fox_out_dw", y0, dx1, ta=True)
    core_arr = core.reshape(1).astype(jnp.int32)
    g2_out = jnp.stack([d_wo0, d_wo1]).reshape(2, 4, wq, d)
    do0, dgate0, delta, sib_hgrn, sib_out = _fox_pre_bwd("fox_pre_bwd", dy0, o0, gate0, exchange=[g2_hgrn, g2_out])
    delta_row = delta[:, :hh].T.reshape(hh, 1, s)
    pairs_early = [_pair_add("rs_pair_add_hgrn", g2_hgrn, sib_hgrn, core_arr),
                   _pair_add("rs_pair_add_out", g2_out, sib_out, core_arr)]
    dp0, dc_rep, rowsum_row, got_hgrn, got_out = _attn_bwd(
        "fox_attn_bwd", qkv, do0, dgate0, lse_row, delta_row, c_rep, hh, scatter=pairs_early)
    dfl, d_bf = _fox_gate_bwd("fox_cumsum_bwd", _pad_lanes(rowsum_row.reshape(hh, s).T),
                              _pad_lanes(dc_rep[:, ::HEAD_DIM]), fl, bf_pad)
    d_wmain = _mm("fox_in_dw", h0, dp0, ta=True, b_stacked=True)
    d_wfl = _mm("fox_fl_dw", h0, dfl, ta=True)

    def grad_columns(a, b):
        out = []
        for lo, hi, src, shift in ((0, 3 * w, d_wmain, 0), (3 * w, 3 * w + hh, d_wfl, 3 * w),
                                   (3 * w + hh, 4 * w + hh, d_wmain, hh)):
            l2, h2 = max(a, lo), min(b, hi)
            if l2 < h2:
                out.append(src[:, l2 - shift:h2 - shift])
        return out

    g2_fox = jnp.stack([jnp.concatenate(grad_columns(j * fox_cols, (j + 1) * fox_cols), axis=1).reshape(2, d // 2, fox_cols)
                        for j in range(4)], axis=1)
    (sib_fox,) = _sibling_other_half("rs_sibling", [g2_fox])
    dh0, got_fox = _mm("fox_in_dh", dp0, w_main, tb=True, res=_mm("fox_fl_dh", dfl, w_fl, tb=True), a_stacked=True,
                       scatter=[_pair_add("rs_pair_add_fox", g2_fox, sib_fox, core_arr)])
    grad_x, d_g0 = _rms_bwd("rms0_bwd", x0, g0, dh0, dx1)
    halves_sum = [_sum_slots("rs_sum_" + nm, got, core_arr)
                  for nm, got in zip(("fox", "hgrn", "out"), (got_fox, got_hgrn, got_out))]
    r_fox, r_hgrn, r_out = _sibling_join("rs_join", halves_sum)
    g_fox = r_fox.reshape(d, fox_cols)
    g_hgrn = r_hgrn.reshape(d, w)
    g_out = r_out.reshape(2 * wq, d)

    small = jnp.concatenate([jnp.concatenate([d_g0, d_g1], axis=0).reshape(-1), d_lbl.reshape(-1), d_gf.reshape(-1),
                             d_onorm.reshape(-1), d_bf.reshape(-1)])
    n_small = small.shape[0]
    pad_to = -(-n_small // 1024) * 1024
    small = jnp.pad(small, (0, pad_to - n_small)).reshape(pad_to // 128, 128)
    small = _sum_slots("small_sum", _ag_devices("small_gather", small)).reshape(-1)
    g_norm = small[:2 * d].reshape(2, d)
    g_lbl = small[2 * d:2 * d + 2 * w].reshape(2, w)
    g_gf = small[2 * d + 2 * w:3 * d + 2 * w]
    g_onorm_full = small[3 * d + 2 * w:3 * d + 3 * w]
    g_bf = small[3 * d + 3 * w:3 * d + 3 * w + hh].reshape(1, hh)
    chip = 2 * lax.axis_index("x") + lax.axis_index("y")
    g_onorm = lax.dynamic_slice_in_dim(g_onorm_full, chip * wq, wq).reshape(1, wq)

    loss = lax.psum(loss_tile[0, 0], ("x", "y", "c"))

    def upd(name, wt, g, m, v):
        shp = wt.shape
        two = lambda a: a.reshape(-1, shp[-1])
        dl, nm, nv = _adamw(name, two(wt), two(g), two(m), two(v))
        return g.reshape(shp), dl.reshape(shp), nm.reshape(shp), nv.reshape(shp)

    res = [
        upd("adamw_norm_gains", norm_gains, g_norm, m_norm_gains, v_norm_gains),
        upd("adamw_fox_w_in", fox_w_in, g_fox, m_fox_w_in, v_fox_w_in),
        upd("adamw_fox_b_f", fox_b_f, g_bf, m_fox_b_f, v_fox_b_f),
        upd("adamw_hgrn_w_in", hgrn_w_in, g_hgrn, m_hgrn_w_in, v_hgrn_w_in),
        upd("adamw_lb_logits", hgrn_lb_logits, g_lbl, m_hgrn_lb_logits, v_hgrn_lb_logits),
        upd("adamw_onorm", hgrn_onorm, g_onorm, m_hgrn_onorm, v_hgrn_onorm),
        upd("adamw_w_out", w_out, g_out, m_w_out, v_w_out),
        upd("adamw_final_gain", final_gain.reshape(1, d), g_gf.reshape(1, d), m_final_gain.reshape(1, d),
            v_final_gain.reshape(1, d)),
    ]
    res[-1] = tuple(a.reshape(d) for a in res[-1])
    grads, deltas, new_m, new_v = zip(*res)
    return (loss, grad_x[None], *grads, *deltas, *new_m, *new_v)
```

```python
import functools

import jax
import jax.numpy as jnp
from jax import lax
from jax.experimental import pallas as pl
from jax.experimental.pallas import tpu as pltpu

F32 = jnp.float32
BF16 = jnp.bfloat16
MESH = pl.DeviceIdType.MESH
ANY = pl.BlockSpec(memory_space=pl.ANY)

EPS = 1e-6
HEAD_DIM = 128
HGRN_CHUNK = 64
HGRN_SUB = 32
EXP_CLAMP = 80.0
ATT_BLOCK = 1024
HGRN_BLOCK = 2048
GATE_BLOCK = 512
ROW_BLOCK = 256
MM_TM, MM_TN, MM_TK = 1024, 1024, 2048
VMEM_LIMIT_V7X = 56 * 1024 * 1024

ADAM_LR, ADAM_B1, ADAM_B2, ADAM_EPS, ADAM_WD, ADAM_STEP = 0.001, 0.9, 0.999, 1e-08, 0.01, 10

NT = (((1,), (1,)), ((), ()))
TN = (((0,), (0,)), ((), ()))
NN = (((1,), (0,)), ((), ()))


def _call(body, **kw):
    return pl.pallas_call(body, **kw)


def _cp(dims=None):
    kw = dict(vmem_limit_bytes=VMEM_LIMIT_V7X)
    if dims is not None:
        kw["dimension_semantics"] = dims
    return pltpu.CompilerParams(**kw)


def _sigmoid(x):
    return 1.0 / (1.0 + jnp.exp(-x))


def _dot(a, b, dn=NN):
    return lax.dot_general(a.astype(BF16), b.astype(BF16), dn, preferred_element_type=F32)


def _split3(x):
    hi = x.astype(BF16)
    r1 = x - hi.astype(F32)
    mid = r1.astype(BF16)
    lo = (r1 - mid.astype(F32)).astype(BF16)
    return hi, mid, lo


def _dot_exact(m01, x, right=False):
    hi, mid, lo = _split3(x)
    if right:
        dot = lambda p: lax.dot_general(p, m01, NN, preferred_element_type=F32)
    else:
        dot = lambda p: lax.dot_general(m01, p, NN, preferred_element_type=F32)
    return dot(hi) + dot(mid) + dot(lo)


def _row_block(rows, cap):
    if rows <= cap:
        return rows
    best = None
    for t in range(16, cap + 1, 16):
        if rows % t == 0:
            best = t
    assert best is not None, rows
    return best


def _mm(name, a, b, *, ta=False, tb=False, out_dtype=F32, res=None, n=None, b_off=0, b_gathered=False,
        out_gathered=None, scatter=(), a_stacked=False, b_stacked=False, col_scale=None):
    if a_stacked:
        assert not ta
        m, k = a.shape[1], a.shape[0] * a.shape[2]
    else:
        m, k = (a.shape[1], a.shape[0]) if ta else a.shape
    if b_gathered:
        rh, cw = b.shape[2], b.shape[3]
        n_full = 2 * rh if tb else 4 * cw
    elif b_stacked:
        assert not tb
        n_full = b.shape[0] * b.shape[2]
    else:
        n_full = b.shape[0] if tb else b.shape[1]
    n = n_full if n is None else n
    tm, tn, tk = min(MM_TM, m), min(MM_TN, n), min(MM_TK, k)
    b_both = b_gathered and not tb and tk == 2 * rh
    if b_gathered and not b_both:
        tn, tk = (min(tn, rh), min(tk, cw)) if tb else (min(tn, cw), min(tk, rh))
    if b_both:
        tn = min(tn, cw)
    if out_gathered:
        tm, tn = min(tm, out_gathered[0]), min(tn, out_gathered[1])
    if a_stacked:
        tk = min(tk, a.shape[2])
    if b_stacked:
        tn = min(tn, b.shape[2])
    assert m % tm == 0 and n % tn == 0 and k % tk == 0 and b_off % tn == 0
    nk = k // tk
    jo = b_off // tn
    if a_stacked:
        pa = a.shape[2] // tk
        a_spec = pl.BlockSpec((1, tm, tk), lambda i, j, kk: (kk // pa, i, kk % pa))
    elif ta:
        a_spec = pl.BlockSpec((tk, tm), lambda i, j, kk: (kk, i))
    else:
        a_spec = pl.BlockSpec((tm, tk), lambda i, j, kk: (i, kk))
    if b_gathered and tb:
        pr, pc = rh // tn, cw // tk
        b_spec = pl.BlockSpec((1, 1, tn, tk), lambda i, j, kk: (j // pr, kk // pc, j % pr, kk % pc))
    elif b_both:
        pc = cw // tn
        b_spec = pl.BlockSpec((2, 1, rh, tn), lambda i, j, kk: (0, j // pc, 0, j % pc))
    elif b_gathered:
        pr, pc = rh // tk, cw // tn
        b_spec = pl.BlockSpec((1, 1, tk, tn), lambda i, j, kk: (kk // pr, j // pc, kk % pr, j % pc))
    elif b_stacked:
        pb = b.shape[2] // tn
        b_spec = pl.BlockSpec((1, tk, tn), lambda i, j, kk: (j // pb, kk, j % pb))
    elif tb:
        b_spec = pl.BlockSpec((tn, tk), lambda i, j, kk: (j + jo, kk))
    else:
        b_spec = pl.BlockSpec((tk, tn), lambda i, j, kk: (kk, j + jo))
    o_spec = pl.BlockSpec((tm, tn), lambda i, j, kk: (i, j))
    if out_gathered:
        assert m == 2 * out_gathered[0] and n == 4 * out_gathered[1] and res is None
        qr, qc = out_gathered[0] // tm, out_gathered[1] // tn
        o_spec = pl.BlockSpec((1, 1, tm, tn), lambda i, j, kk: (i // qr, j // qc, i % qr, j % qc))
    dn = (((0 if ta else 1,), (1 if tb else 0,)), ((), ()))
    has_res = res is not None
    ns = len(scatter)
    grid = (m // tm, n // tn, nk)

    def body(*refs):
        a_ref, b_ref = refs[:2]
        r_ref = refs[2] if has_res else None
        base = 2 + int(has_res)
        p_refs = refs[base:base + ns]
        o_ref = refs[base + ns]
        got_refs = refs[base + ns + 1:base + 2 * ns + 1]
        acc = refs[base + 2 * ns + 1]
        sems = refs[base + 2 * ns + 2:]
        kk = pl.program_id(2)
        if ns:
            first = jnp.logical_and(jnp.logical_and(pl.program_id(0) == 0, pl.program_id(1) == 0), kk == 0)
            last = jnp.logical_and(jnp.logical_and(pl.program_id(0) == grid[0] - 1, pl.program_id(1) == grid[1] - 1),
                                   kk == nk - 1)

            @pl.when(first)
            def _():
                _scatter_phase(0, p_refs, got_refs, sems)

        av = a_ref[0] if a_stacked else a_ref[...]
        if b_both:
            p = (lax.dot_general(av[:, :rh].astype(BF16), b_ref[0, 0].astype(BF16), dn, preferred_element_type=F32)
                 + lax.dot_general(av[:, rh:].astype(BF16), b_ref[1, 0].astype(BF16), dn, preferred_element_type=F32))
        else:
            bv = b_ref[0, 0] if b_gathered else (b_ref[0] if b_stacked else b_ref[...])
            p = lax.dot_general(av.astype(BF16), bv.astype(BF16), dn, preferred_element_type=F32)

        def finish(total):
            if r_ref is not None:
                total = total + r_ref[...]
            if col_scale is not None:
                total = total * jnp.where(pl.program_id(1) < col_scale[0] // tn, col_scale[1], 1.0)
            if out_gathered:
                o_ref[0, 0] = total.astype(out_dtype)
            else:
                o_ref[...] = total.astype(out_dtype)

        if nk == 1:
            finish(p)
        else:
            @pl.when(kk == 0)
            def _():
                acc[...] = p

            @pl.when(jnp.logical_and(kk > 0, kk < nk - 1))
            def _():
                acc[...] += p

            @pl.when(kk == nk - 1)
            def _():
                finish(acc[...] + p)

        if ns:
            @pl.when(last)
            def _():
                _scatter_phase(1, p_refs, got_refs, sems)

    ins = [a, b] + ([res] if has_res else []) + list(scatter)
    in_specs = [a_spec, b_spec] + ([o_spec] if has_res else []) + [ANY] * ns
    o_shape = jax.ShapeDtypeStruct((2, 4) + tuple(out_gathered) if out_gathered else (m, n), out_dtype)
    out = _call(
        body, name=name, grid=grid, in_specs=in_specs, out_specs=[o_spec] + [ANY] * ns,
        out_shape=[o_shape] + [jax.ShapeDtypeStruct(p.shape, p.dtype) for p in scatter],
        scratch_shapes=[pltpu.VMEM((tm, tn) if nk > 1 else (8, 128), F32)] + (_scatter_sems(ns) if ns else []),
        compiler_params=_cp(("arbitrary",) * 3 if ns else ("parallel", "parallel", "arbitrary")),
    )(*ins)
    return out if ns else out[0]


def _rms_fwd(name, x, g):
    s, d = x.shape
    tm = min(ROW_BLOCK, s)

    def body(x_ref, g_ref, h_ref):
        xv = x_ref[...]
        r = lax.rsqrt(jnp.mean(xv * xv, axis=-1, keepdims=True) + EPS)
        h_ref[...] = (xv * r * g_ref[...]).astype(BF16)

    row = pl.BlockSpec((tm, d), lambda i: (i, 0))
    vec = pl.BlockSpec((1, d), lambda i: (0, 0))
    return _call(body, name=name, grid=(s // tm,), in_specs=[row, vec], out_specs=row,
                 out_shape=jax.ShapeDtypeStruct((s, d), BF16), compiler_params=_cp(("parallel",)))(x, g)


def _rms_bwd(name, x, g, dh, dres):
    s, d = x.shape
    tm = min(ROW_BLOCK, s)

    def body(x_ref, g_ref, dh_ref, dres_ref, dx_ref, dg_ref):
        @pl.when(pl.program_id(0) == 0)
        def _():
            dg_ref[...] = jnp.zeros_like(dg_ref)

        xv = x_ref[...]
        r = lax.rsqrt(jnp.mean(xv * xv, axis=-1, keepdims=True) + EPS)
        xn = xv * r
        dhv = dh_ref[...]
        dxn = dhv * g_ref[...]
        dx_ref[...] = dres_ref[...] + r * (dxn - xn * jnp.mean(dxn * xn, axis=-1, keepdims=True))
        dg_ref[...] += jnp.sum(dhv * xn, axis=0, keepdims=True)

    row = pl.BlockSpec((tm, d), lambda i: (i, 0))
    vec = pl.BlockSpec((1, d), lambda i: (0, 0))
    return _call(body, name=name, grid=(s // tm,), in_specs=[row, vec, row, row], out_specs=[row, vec],
                 out_shape=[jax.ShapeDtypeStruct((s, d), F32), jax.ShapeDtypeStruct((1, d), F32)],
                 compiler_params=_cp(("arbitrary",)))(x, g, dh, dres)


def _loss_head(name, x, tgt, g):
    s, d = x.shape
    tm = min(ROW_BLOCK, s)
    nb = s // tm

    def body(x_ref, t_ref, g_ref, dx_ref, dg_ref, loss_ref, lacc):
        i = pl.program_id(0)

        @pl.when(i == 0)
        def _():
            dg_ref[...] = jnp.zeros_like(dg_ref)
            lacc[...] = jnp.zeros_like(lacc)

        xv = x_ref[...]
        gv = g_ref[...]
        r = lax.rsqrt(jnp.mean(xv * xv, axis=-1, keepdims=True) + EPS)
        xn = xv * r
        err = xn * gv - t_ref[...]
        lacc[...] += jnp.sum(err * err, axis=0, keepdims=True)
        dout = err * (1.0 / d)
        dg_ref[...] += jnp.sum(dout * xn, axis=0, keepdims=True)
        dxn = dout * gv
        dx_ref[...] = r * (dxn - xn * jnp.mean(dxn * xn, axis=-1, keepdims=True))

        @pl.when(i == nb - 1)
        def _():
            total = jnp.sum(lacc[...], axis=1, keepdims=True) * (0.5 / d)
            loss_ref[...] = jnp.broadcast_to(total, loss_ref.shape)

    row = pl.BlockSpec((tm, d), lambda i: (i, 0))
    vec = pl.BlockSpec((1, d), lambda i: (0, 0))
    one = pl.BlockSpec((1, 128), lambda i: (0, 0))
    return _call(body, name=name, grid=(nb,), in_specs=[row, row, vec], out_specs=[row, vec, one],
                 out_shape=[jax.ShapeDtypeStruct((s, d), F32), jax.ShapeDtypeStruct((1, d), F32),
                            jax.ShapeDtypeStruct((1, 128), F32)],
                 scratch_shapes=[pltpu.VMEM((1, d), F32)], compiler_params=_cp(("arbitrary",)))(x, tgt, g)


def _fox_gate_fwd(name, fl, bf, w, rep_scale=1.0):
    s = fl.shape[0]
    tb = min(GATE_BLOCK, s)

    def body(fl_ref, bf_ref, c_ref, crep_ref, carry):
        @pl.when(pl.program_id(0) == 0)
        def _():
            carry[...] = jnp.zeros_like(carry)

        z = fl_ref[...] + bf_ref[...]
        lf = jnp.minimum(z, 0.0) - jnp.log(1.0 + jnp.exp(-jnp.abs(z)))
        rows = lax.broadcasted_iota(jnp.int32, (tb, tb), 0)
        cols = lax.broadcasted_iota(jnp.int32, (tb, tb), 1)
        tri = (rows >= cols).astype(BF16)
        cs = _dot_exact(tri, lf) + carry[...]
        c_ref[...] = cs
        carry[...] = c_ref[tb - 1:tb, :]
        sel_r = lax.broadcasted_iota(jnp.int32, (128, w), 0)
        sel_c = lax.broadcasted_iota(jnp.int32, (128, w), 1)
        sel = (sel_r == sel_c // HEAD_DIM).astype(BF16)
        crep_ref[...] = _dot_exact(sel, cs * rep_scale, right=True)

    blk = pl.BlockSpec((tb, 128), lambda i: (i, 0))
    return _call(body, name=name, grid=(s // tb,),
                 in_specs=[blk, pl.BlockSpec((1, 128), lambda i: (0, 0))],
                 out_specs=[blk, pl.BlockSpec((tb, w), lambda i: (i, 0))],
                 out_shape=[jax.ShapeDtypeStruct((s, 128), F32), jax.ShapeDtypeStruct((s, w), F32)],
                 scratch_shapes=[pltpu.VMEM((1, 128), F32)], compiler_params=_cp(("arbitrary",)))(fl, bf)


def _fox_gate_bwd(name, drow, dcol, fl, bf):
    s = fl.shape[0]
    tb = min(GATE_BLOCK, s)
    nb = s // tb

    def body(dr_ref, dc_ref, fl_ref, bf_ref, dfl_ref, dbf_ref, carry, tmp):
        @pl.when(pl.program_id(0) == 0)
        def _():
            carry[...] = jnp.zeros_like(carry)
            dbf_ref[...] = jnp.zeros_like(dbf_ref)

        rows = lax.broadcasted_iota(jnp.int32, (tb, tb), 0)
        cols = lax.broadcasted_iota(jnp.int32, (tb, tb), 1)
        triu = (rows <= cols).astype(BF16)
        dlf = _dot_exact(triu, dr_ref[...] + dc_ref[...]) + carry[...]
        tmp[...] = dlf
        carry[...] = tmp[0:1, :]
        z = fl_ref[...] + bf_ref[...]
        dfl = dlf * (1.0 / (1.0 + jnp.exp(z)))
        dfl_ref[...] = dfl
        dbf_ref[...] += jnp.sum(dfl, axis=0, keepdims=True)

    blk = pl.BlockSpec((tb, 128), lambda i: (nb - 1 - i, 0))
    vec = pl.BlockSpec((1, 128), lambda i: (0, 0))
    return _call(body, name=name, grid=(nb,), in_specs=[blk, blk, blk, vec], out_specs=[blk, vec],
                 out_shape=[jax.ShapeDtypeStruct((s, 128), F32), jax.ShapeDtypeStruct((1, 128), F32)],
                 scratch_shapes=[pltpu.VMEM((1, 128), F32), pltpu.VMEM((tb, 128), F32)],
                 compiler_params=_cp(("arbitrary",)))(drow, dcol, fl, bf)


LOG2E = 1.4426950408889634
ATT_SCALE = HEAD_DIM ** -0.5
Q_FOLD = ATT_SCALE * LOG2E


def _attn_fwd(name, qkv, c_rep, gate, h_count, gather=()):
    s = qkv.shape[0]
    w = h_count * HEAD_DIM
    t = min(ATT_BLOCK, s)
    hp = 2 if h_count % 2 == 0 else 1
    wb = hp * HEAD_DIM
    ng = len(gather)
    nh, nq = h_count // hp, s // t

    def body(*refs):
        q_ref, k_ref, v_ref, c_ref, gate_ref = refs[:5]
        h_refs = refs[5:5 + ng]
        o_ref, lse_ref, y_ref = refs[5 + ng:8 + ng]
        big_refs = refs[8 + ng:8 + 2 * ng]
        m_s, l_s, acc_s = refs[8 + 2 * ng:11 + 2 * ng]
        sems = refs[11 + 2 * ng:]
        hs = pl.program_id(0)
        qi = pl.program_id(1)
        if ng:
            @pl.when(jnp.logical_and(hs == 0, qi == 0))
            def _():
                _ag_phase(0, h_refs, big_refs, sems)

            @pl.when(jnp.logical_and(hs == nh // 2, qi == 0))
            def _():
                _ag_phase(1, h_refs, big_refs, sems)

        m_s[...] = jnp.full(m_s.shape, -jnp.inf, F32)
        l_s[...] = jnp.zeros_like(l_s)
        acc_s[...] = jnp.zeros_like(acc_s)

        def step(kb, masked):
            off = pl.multiple_of(kb * t, t)
            for a in range(hp):
                cols_a = slice(a * HEAD_DIM, (a + 1) * HEAD_DIM)
                kk = k_ref[pl.ds(off, t), cols_a]
                vv = v_ref[pl.ds(off, t), cols_a]
                cc = jnp.tile(c_ref[pl.ds(off, t), cols_a], (1, t // HEAD_DIM))
                st = lax.dot_general(kk, q_ref[:, cols_a], NT, preferred_element_type=F32) - cc
                if masked:
                    rows = lax.broadcasted_iota(jnp.int32, (t, t), 0)
                    cols = lax.broadcasted_iota(jnp.int32, (t, t), 1)
                    st = jnp.where(cols >= rows, st, -jnp.inf)
                m_prev = m_s[a]
                m_new = jnp.maximum(m_prev, jnp.max(st, axis=0, keepdims=True))
                pt = jnp.exp2(st - m_new)
                alpha = jnp.exp2(m_prev - m_new)
                l_s[a] = alpha * l_s[a] + jnp.sum(pt, axis=0, keepdims=True)
                acc_s[a] = alpha * acc_s[a] + lax.dot_general(vv, pt.astype(BF16), TN, preferred_element_type=F32)
                m_s[a] = m_new

        def loop_body(pair, carry):
            step(2 * pair, False)
            step(2 * pair + 1, False)
            return carry

        lax.fori_loop(0, qi // 2, loop_body, 0)

        @pl.when(lax.rem(qi, 2) == 1)
        def _():
            step(qi - 1, False)

        step(qi, True)
        for a in range(hp):
            l = l_s[a]
            cols_a = slice(a * HEAD_DIM, (a + 1) * HEAD_DIM)
            oa = (acc_s[a] / l).T
            o_ref[:, cols_a] = oa
            g = gate_ref[:, cols_a]
            y_ref[:, cols_a] = (oa * (g * _sigmoid(g))).astype(BF16)
            lse_ref[a] = m_s[a] + jnp.log2(l)

        if ng:
            @pl.when(jnp.logical_and(hs == nh - 1, qi == nq - 1))
            def _():
                _ag_phase(2, h_refs, big_refs, sems)

    blk = lambda off: pl.BlockSpec((t, wb), lambda h, i: (i, off + h))
    whole = lambda off: pl.BlockSpec((s, wb), lambda h, i: (0, off + h))
    rowv = pl.BlockSpec((hp, 1, t), lambda h, i: (h, 0, i))
    return _call(
        body, name=name, grid=(nh, nq),
        in_specs=[blk(0), whole(nh), whole(2 * nh), whole(0), blk(0)] + [ANY] * ng,
        out_specs=[blk(0), rowv, blk(0)] + [ANY] * ng,
        out_shape=[jax.ShapeDtypeStruct((s, w), F32), jax.ShapeDtypeStruct((h_count, 1, s), F32),
                   jax.ShapeDtypeStruct((s, w), BF16)] + _ag_out_shapes(gather),
        scratch_shapes=[pltpu.VMEM((hp, 1, t), F32), pltpu.VMEM((hp, 1, t), F32), pltpu.VMEM((hp, HEAD_DIM, t), F32)]
        + (_ag_sems(ng) if ng else []),
        compiler_params=_cp(("arbitrary", "arbitrary") if ng else ("parallel", "arbitrary")),
    )(qkv, qkv, qkv, c_rep, gate, *gather)


def _attn_bwd(name, qkv, do, dgate, lse_row, delta_row, c_rep, h_count, scatter=()):
    s = qkv.shape[0]
    w = h_count * HEAD_DIM
    t = min(ATT_BLOCK, s)
    nq = s // t
    scale = HEAD_DIM ** -0.5
    ns = len(scatter)

    def body(*refs):
        k_ref, v_ref, q_ref, do_ref, lse_ref, dl_ref, c_ref, dgate_ref = refs[:8]
        p_refs = refs[8:8 + ns]
        dp_ref, dc_ref, rs_ref = refs[8 + ns:11 + ns]
        got_refs = refs[11 + ns:11 + 2 * ns]
        dk_s, dv_s, dc_s, dq_s, rs_s = refs[11 + 2 * ns:16 + 2 * ns]
        sems = refs[16 + 2 * ns:]
        kj = pl.program_id(1)
        if ns:
            @pl.when(jnp.logical_and(pl.program_id(0) == 0, kj == 0))
            def _():
                _scatter_phase(0, p_refs, got_refs, sems)

        kk = k_ref[...]
        vv = v_ref[...]
        ccol = jnp.tile(c_ref[...], (1, t // HEAD_DIM))
        dk_s[...] = jnp.zeros_like(dk_s)
        dv_s[...] = jnp.zeros_like(dv_s)
        dc_s[...] = jnp.zeros_like(dc_s)

        @pl.when(kj == 0)
        def _():
            dq_s[...] = jnp.zeros_like(dq_s)
            rs_s[...] = jnp.zeros_like(rs_s)

        def step(qb, masked):
            off = pl.multiple_of(qb * t, t)
            q = q_ref[pl.ds(off, t), :]
            dov = do_ref[pl.ds(off, t), :]
            st = lax.dot_general(kk, q, NT, preferred_element_type=F32) - ccol
            pt = jnp.exp2(st - lse_ref[0, :, pl.ds(off, t)])
            if masked:
                rows = lax.broadcasted_iota(jnp.int32, (t, t), 0)
                cols = lax.broadcasted_iota(jnp.int32, (t, t), 1)
                pt = jnp.where(cols >= rows, pt, 0.0)
            dv_s[...] += lax.dot_general(pt.astype(BF16), dov, NN, preferred_element_type=F32)
            dpt = lax.dot_general(vv, dov, NT, preferred_element_type=F32)
            dst = pt * (dpt - dl_ref[0, :, pl.ds(off, t)])
            dstb = dst.astype(BF16)
            dk_s[...] += lax.dot_general(dstb, q, NN, preferred_element_type=F32)
            dc_s[...] += jnp.sum(dst, axis=1, keepdims=True)
            dq_s[:, pl.ds(off, t)] += lax.dot_general(kk, dstb, TN, preferred_element_type=F32)
            rs_s[:, pl.ds(off, t)] += jnp.sum(dst, axis=0, keepdims=True)

        step(kj, True)

        def loop_body(qb, carry):
            step(qb, False)
            return carry

        lax.fori_loop(kj + 1, nq, loop_body, 0)
        rows_j = pl.ds(pl.multiple_of(kj * t, t), t)
        dp_ref[1, rows_j, :] = (dk_s[...] * (1.0 / LOG2E)).astype(BF16)
        dp_ref[2, rows_j, :] = dv_s[...].astype(BF16)
        dc_ref[...] = jnp.broadcast_to(-dc_s[...], dc_ref.shape)

        @pl.when(kj == nq - 1)
        def _():
            for b in range(nq):
                dp_ref[0, b * t:(b + 1) * t, :] = (dq_s[:, b * t:(b + 1) * t] * scale).T.astype(BF16)
            dp_ref[3] = dgate_ref[...]
            rs_ref[0] = rs_s[...]

        if ns:
            @pl.when(jnp.logical_and(pl.program_id(0) == h_count - 1, kj == nq - 1))
            def _():
                _scatter_phase(1, p_refs, got_refs, sems)

    hh = h_count
    blk = lambda off: pl.BlockSpec((t, HEAD_DIM), lambda h, j: (j, off + h))
    whole = pl.BlockSpec((s, HEAD_DIM), lambda h, j: (0, h))
    rowv = pl.BlockSpec((1, 1, s), lambda h, j: (h, 0, 0))
    return _call(
        body, name=name, grid=(hh, nq),
        in_specs=[blk(hh), blk(2 * hh), whole, whole, rowv, rowv, blk(0), whole] + [ANY] * ns,
        out_specs=[pl.BlockSpec((4, s, HEAD_DIM), lambda h, j: (0, 0, h)), blk(0), rowv] + [ANY] * ns,
        out_shape=[jax.ShapeDtypeStruct((4, s, w), BF16), jax.ShapeDtypeStruct((s, w), F32),
                   jax.ShapeDtypeStruct((hh, 1, s), F32)] + [jax.ShapeDtypeStruct(p.shape, p.dtype) for p in scatter],
        scratch_shapes=[pltpu.VMEM((t, HEAD_DIM), F32), pltpu.VMEM((t, HEAD_DIM), F32), pltpu.VMEM((t, 1), F32),
                        pltpu.VMEM((HEAD_DIM, s), F32), pltpu.VMEM((1, s), F32)] + (_scatter_sems(ns) if ns else []),
        compiler_params=_cp(("arbitrary", "arbitrary") if ns else ("parallel", "arbitrary")),
    )(qkv, qkv, qkv, do, lse_row, delta_row, c_rep, dgate, *scatter)


def _fox_pre_bwd(name, dy, o, gate, exchange=()):
    s, w = o.shape
    tm = min(ROW_BLOCK, s)
    ne = len(exchange)
    nb = s // tm

    def body(*refs):
        dy_ref, o_ref, g_ref = refs[:3]
        x_refs = refs[3:3 + ne]
        do_ref, dg_ref, dl_ref = refs[3 + ne:6 + ne]
        got_refs = refs[6 + ne:6 + 2 * ne]
        sems = refs[6 + 2 * ne:]
        if ne:
            @pl.when(pl.program_id(0) == 0)
            def _():
                _sibling_phase(0, x_refs, got_refs, sems)

            @pl.when(pl.program_id(0) == nb - 1)
            def _():
                _sibling_phase(1, x_refs, got_refs, sems)

        g = g_ref[...]
        sg = _sigmoid(g)
        dyv = dy_ref[...]
        ov = o_ref[...]
        dov = dyv * (g * sg)
        do_ref[...] = dov.astype(BF16)
        dg_ref[...] = (dyv * ov * (sg * (1.0 + g * (1.0 - sg)))).astype(BF16)
        sel_r = lax.broadcasted_iota(jnp.int32, (w, 128), 0)
        sel_c = lax.broadcasted_iota(jnp.int32, (w, 128), 1)
        sel = (sel_r // HEAD_DIM == sel_c).astype(BF16)
        dl_ref[...] = _dot_exact(sel, dov * ov, right=True)

    row = pl.BlockSpec((tm, w), lambda i: (i, 0))
    lanes = pl.BlockSpec((tm, 128), lambda i: (i, 0))
    return _call(body, name=name, grid=(nb,), in_specs=[row, row, row] + [ANY] * ne,
                 out_specs=[row, row, lanes] + [ANY] * ne,
                 out_shape=[jax.ShapeDtypeStruct((s, w), BF16), jax.ShapeDtypeStruct((s, w), BF16),
                            jax.ShapeDtypeStruct((s, 128), F32)] + _sibling_out_shapes(exchange),
                 scratch_shapes=_sibling_sems(ne) if ne else [],
                 compiler_params=_cp(("arbitrary",) if ne else ("parallel",)))(dy, o, gate, *exchange)


def _hgrn_chunk_terms(q_c, k_c, b_c, b_s, base):
    nsub = HGRN_CHUNK // HGRN_SUB
    refs = [jnp.zeros((1, HEAD_DIM), F32)]
    for i in range(1, nsub):
        r0 = base + i * HGRN_SUB - 1
        refs.append(b_s[r0:r0 + 1, :])
    rfull = jnp.concatenate([jnp.broadcast_to(r, (HGRN_SUB, HEAD_DIM)) for r in refs], axis=0)
    eq = jnp.exp(b_c - rfull)
    qe = q_c * eq
    es = [jnp.exp(jnp.minimum(r - b_c, EXP_CLAMP)) for r in refs]
    kes = [(k_c * e).astype(BF16) for e in es]
    return eq, qe, es, kes


def _chunk_cumsum(x, reverse=False):
    cc = HGRN_CHUNK
    nc = x.shape[0] // cc
    rows = lax.broadcasted_iota(jnp.int32, (cc, cc), 0)
    cols = lax.broadcasted_iota(jnp.int32, (cc, cc), 1)
    tri = ((rows <= cols) if reverse else (rows >= cols)).astype(BF16)
    wide = jnp.concatenate([x[n * cc:(n + 1) * cc] for n in range(nc)], axis=1)
    res = _dot_exact(tri, wide)
    return jnp.concatenate([res[:, n * HEAD_DIM:(n + 1) * HEAD_DIM] for n in range(nc)], axis=0)


def _hgrn_block_pre(q_ref, f_ref, lbl_ref, b_s):
    lb = _sigmoid(lbl_ref[1:2, :] - lbl_ref[0:1, :])
    qr = q_ref[...]
    sq = _sigmoid(qr)
    q = qr * sq
    fz = f_ref[...]
    sg = _sigmoid(fz)
    f = lb + (1.0 - lb) * sg
    g = jnp.log(f)
    k = (1.0 - lb) * (1.0 / (1.0 + jnp.exp(fz)))
    b = _chunk_cumsum(g)
    b_s[...] = b
    return lb, qr, sq, q, sg, f, k, b


def _hgrn_fwd(name, p1, lbl, onorm, h_count):
    s = p1.shape[0]
    w = h_count * HEAD_DIM
    t = min(HGRN_BLOCK, s)
    nc = t // HGRN_CHUNK
    nsub = HGRN_CHUNK // HGRN_SUB
    cc = HGRN_CHUNK

    def body(q_ref, f_ref, i_ref, g_ref, lbl_ref, on_ref, y_ref, o_ref, st_ref, state_s, b_s):
        @pl.when(pl.program_id(1) == 0)
        def _():
            state_s[...] = jnp.zeros_like(state_s)

        lb, qr, sq, q, sg, f, k, b = _hgrn_block_pre(q_ref, f_ref, lbl_ref, b_s)
        v = i_ref[...]
        r64 = lax.broadcasted_iota(jnp.int32, (cc, cc), 0)
        c64 = lax.broadcasted_iota(jnp.int32, (cc, cc), 1)
        for n in range(nc):
            sl = slice(n * cc, (n + 1) * cc)
            q_c, k_c, v_c, b_c = q[sl], k[sl], v[sl], b[sl]
            bl = b_s[n * cc + cc - 1:n * cc + cc, :]
            eq, qe, es, kes = _hgrn_chunk_terms(q_c, k_c, b_c, b_s, n * cc)
            qeb = qe.astype(BF16)
            a = jnp.concatenate(
                [lax.dot_general(qeb[i * HGRN_SUB:(i + 1) * HGRN_SUB], kes[i], NT, preferred_element_type=F32)
                 for i in range(nsub)], axis=0)
            a = jnp.where(r64 >= c64, a, 0.0)
            st = state_s[...]
            st_ref[0, n] = st.astype(BF16)
            inter = _dot(q_c * jnp.exp(b_c), st, NT)
            intra = _dot(a, v_c)
            o_ref[sl, :] = inter + intra
            kb = k_c * jnp.exp(bl - b_c)
            state_s[...] = st * jnp.exp(bl) + _dot(v_c, kb, TN)
        o = o_ref[...]
        rr = lax.rsqrt(jnp.mean(o * o, axis=-1, keepdims=True) + EPS)
        gate = g_ref[...]
        y_ref[...] = ((o * rr) * on_ref[...] * (gate * _sigmoid(gate))).astype(BF16)

    hh = h_count
    blk = lambda off: pl.BlockSpec((t, HEAD_DIM), lambda h, i: (i, off + h))
    return _call(
        body, name=name, grid=(hh, s // t),
        in_specs=[blk(0), blk(hh), blk(2 * hh), blk(3 * hh),
                  pl.BlockSpec((2, HEAD_DIM), lambda h, i: (0, h)), pl.BlockSpec((1, HEAD_DIM), lambda h, i: (0, h))],
        out_specs=[blk(0), blk(0), pl.BlockSpec((1, nc, HEAD_DIM, HEAD_DIM), lambda h, i: (h, i, 0, 0))],
        out_shape=[jax.ShapeDtypeStruct((s, w), BF16), jax.ShapeDtypeStruct((s, w), F32),
                   jax.ShapeDtypeStruct((hh, s // cc, HEAD_DIM, HEAD_DIM), BF16)],
        scratch_shapes=[pltpu.VMEM((HEAD_DIM, HEAD_DIM), F32), pltpu.VMEM((t, HEAD_DIM), F32)],
        compiler_params=_cp(("parallel", "arbitrary")),
    )(p1, p1, p1, p1, lbl, onorm)


def _hgrn_bwd(name, p1, lbl, onorm, o, dy, states, h_count):
    s = p1.shape[0]
    w = h_count * HEAD_DIM
    t = min(HGRN_BLOCK, s)
    nb = s // t
    nc = t // HGRN_CHUNK
    nsub = HGRN_CHUNK // HGRN_SUB
    cc = HGRN_CHUNK

    def body(q_ref, f_ref, i_ref, g_ref, lbl_ref, on_ref, o_ref, dy_ref, st_ref,
             dp_ref, dlog_ref, don_ref,
             dst_s, b_s, do_s, dqs, dks, dbs, exs):
        @pl.when(pl.program_id(1) == 0)
        def _():
            dst_s[...] = jnp.zeros_like(dst_s)
            dlog_ref[...] = jnp.zeros_like(dlog_ref)
            don_ref[...] = jnp.zeros_like(don_ref)

        lb, qr, sq, q, sg, f, k, b = _hgrn_block_pre(q_ref, f_ref, lbl_ref, b_s)
        v = i_ref[...]

        ov = o_ref[...]
        rr = lax.rsqrt(jnp.mean(ov * ov, axis=-1, keepdims=True) + EPS)
        on = ov * rr
        gate = g_ref[...]
        sgt = _sigmoid(gate)
        silu = gate * sgt
        dyv = dy_ref[...]
        gain = on_ref[...]
        dp_ref[3] = (dyv * on * gain * (sgt * (1.0 + gate * (1.0 - sgt)))).astype(BF16)
        don_ref[...] += jnp.sum(dyv * on * silu, axis=0, keepdims=True)
        d_on = dyv * gain * silu
        do_s[...] = rr * (d_on - on * jnp.mean(d_on * on, axis=-1, keepdims=True))

        r64 = lax.broadcasted_iota(jnp.int32, (cc, cc), 0)
        c64 = lax.broadcasted_iota(jnp.int32, (cc, cc), 1)
        upper = r64 <= c64
        for n in reversed(range(nc)):
            sl = slice(n * cc, (n + 1) * cc)
            q_c, k_c, v_c, b_c = q[sl], k[sl], v[sl], b[sl]
            do_c = do_s[sl, :]
            bl = b_s[n * cc + cc - 1:n * cc + cc, :]
            ebl = jnp.exp(bl)
            eq, qe, es, kes = _hgrn_chunk_terms(q_c, k_c, b_c, b_s, n * cc)
            qeb = qe.astype(BF16)
            dob = do_c.astype(BF16)
            vb = v_c.astype(BF16)
            st = st_ref[0, n].astype(F32)
            dstn = dst_s[...]
            qb_ = q_c * jnp.exp(b_c)
            kb_ = k_c * jnp.exp(bl - b_c)
            at = jnp.zeros((cc, cc), F32)
            for i in range(nsub):
                blk_i = (c64 // HGRN_SUB) == i
                at = at + jnp.where(blk_i, lax.dot_general(kes[i], qeb, NT, preferred_element_type=F32), 0.0)
            at = jnp.where(upper, at, 0.0)
            dv = _dot(at, dob) + _dot(kb_, dstn, NT)
            dqb = _dot(dob, st)
            dkb = _dot(vb, dstn)
            da = jnp.where(r64 >= c64, lax.dot_general(dob, vb, NT, preferred_element_type=F32), 0.0)
            dat = jnp.where(upper, lax.dot_general(vb, dob, NT, preferred_element_type=F32), 0.0)
            dab = da.astype(BF16)
            dq_raw = jnp.concatenate(
                [lax.dot_general(dab[i * HGRN_SUB:(i + 1) * HGRN_SUB], kes[i], NN, preferred_element_type=F32)
                 for i in range(nsub)], axis=0)
            db = qb_.astype(BF16).astype(F32) * dqb + qeb.astype(F32) * dq_raw
            dkbk = dkb * kb_.astype(BF16).astype(F32)
            db = db - dkbk
            dk_in = jnp.zeros((cc, HEAD_DIM), F32)
            for i in range(nsub):
                blk_i = (c64 // HGRN_SUB) == i
                dk_raw = _dot(jnp.where(blk_i, dat, 0.0), qeb)
                dk_in = dk_in + dk_raw * es[i]
                db = db - kes[i].astype(F32) * dk_raw
            dqs[sl, :] = dqb * jnp.exp(b_c) + dq_raw * eq
            dks[sl, :] = dkb * jnp.exp(bl - b_c) + dk_in
            dbs[sl, :] = db
            extra = jnp.sum(dkbk, axis=0, keepdims=True) + jnp.sum(dstn * st, axis=0, keepdims=True) * ebl
            exs[sl, :] = jnp.broadcast_to(extra, (cc, HEAD_DIM))
            dp_ref[2, sl, :] = dv.astype(BF16)
            dst_s[...] = dstn * ebl + _dot(dob, qb_, TN)

        dg = _chunk_cumsum(dbs[...], reverse=True) + exs[...]
        df = dg / f - dks[...]
        dp_ref[1] = (df * (1.0 - lb) * sg * (1.0 - sg)).astype(BF16)
        dp_ref[0] = (dqs[...] * (sq * (1.0 + qr * (1.0 - sq)))).astype(BF16)
        dlb = jnp.sum(df * (1.0 - sg), axis=0, keepdims=True) * (lb * (1.0 - lb))
        dlog_ref[0:1, :] += -dlb
        dlog_ref[1:2, :] += dlb

    hh = h_count
    blk = lambda off: pl.BlockSpec((t, HEAD_DIM), lambda h, i: (nb - 1 - i, off + h))
    two = pl.BlockSpec((2, HEAD_DIM), lambda h, i: (0, h))
    one = pl.BlockSpec((1, HEAD_DIM), lambda h, i: (0, h))
    tile = pltpu.VMEM((t, HEAD_DIM), F32)
    return _call(
        body, name=name, grid=(hh, nb),
        in_specs=[blk(0), blk(hh), blk(2 * hh), blk(3 * hh), two, one, blk(0), blk(0),
                  pl.BlockSpec((1, nc, HEAD_DIM, HEAD_DIM), lambda h, i: (h, nb - 1 - i, 0, 0))],
        out_specs=[pl.BlockSpec((4, t, HEAD_DIM), lambda h, i: (0, nb - 1 - i, h)), two, one],
        out_shape=[jax.ShapeDtypeStruct((4, s, w), BF16), jax.ShapeDtypeStruct((2, w), F32),
                   jax.ShapeDtypeStruct((1, w), F32)],
        scratch_shapes=[pltpu.VMEM((HEAD_DIM, HEAD_DIM), F32), tile, tile, tile, tile, tile, tile],
        compiler_params=_cp(("parallel", "arbitrary")),
    )(p1, p1, p1, p1, lbl, onorm, o, dy, states)


def _adamw(name, w, g, m, v):
    r, c = w.shape
    tr = r if r <= 128 else _row_block(r, 128)
    c1 = 1.0 - ADAM_B1 ** ADAM_STEP
    c2 = 1.0 - ADAM_B2 ** ADAM_STEP

    def body(w_ref, g_ref, m_ref, v_ref, d_ref, nm_ref, nv_ref):
        gv = g_ref[...]
        nm = ADAM_B1 * m_ref[...] + (1.0 - ADAM_B1) * gv
        nv = ADAM_B2 * v_ref[...] + (1.0 - ADAM_B2) * (gv * gv)
        nm_ref[...] = nm
        nv_ref[...] = nv
        d_ref[...] = -ADAM_LR * ((nm / c1) / (jnp.sqrt(nv / c2) + ADAM_EPS) + ADAM_WD * w_ref[...])

    blk = pl.BlockSpec((tr, c), lambda i: (i, 0))
    sh = jax.ShapeDtypeStruct((r, c), F32)
    return _call(body, name=name, grid=(r // tr,), in_specs=[blk] * 4, out_specs=[blk] * 3,
                 out_shape=[sh, sh, sh], compiler_params=_cp(("parallel",)))(w, g, m, v)


SLAB_BLOCK_BYTES = 2 * 1024 * 1024


def _slab_rows(r, c):
    return _row_block(r, max(16, SLAB_BLOCK_BYTES // (4 * c) // 16 * 16))


def _pair_add(name, g2, recv, core):
    _, nch, r, c = g2.shape
    tr = _slab_rows(r, c)

    grid_spec = pltpu.PrefetchScalarGridSpec(
        num_scalar_prefetch=1, grid=(nch, r // tr),
        in_specs=[pl.BlockSpec((1, 1, tr, c), lambda j, i, cr: (cr[0], j, i, 0)),
                  pl.BlockSpec((1, tr, c), lambda j, i, cr: (j, i, 0))],
        out_specs=pl.BlockSpec((1, tr, c), lambda j, i, cr: (j, i, 0)))

    def body(core_ref, a_ref, b_ref, o_ref):
        o_ref[...] = (a_ref[0] + b_ref[...]).astype(BF16)

    return _call(body, name=name, grid_spec=grid_spec, out_shape=jax.ShapeDtypeStruct((nch, r, c), BF16),
                 compiler_params=_cp(("parallel", "parallel")))(core, g2, recv)


def _sum_slots(name, x, core=None):
    n, r, c = x.shape
    tr = _slab_rows(r, c)

    def total(x_ref):
        acc = x_ref[0].astype(F32)
        for j in range(1, n):
            acc = acc + x_ref[j].astype(F32)
        return acc

    if core is None:
        def body(x_ref, o_ref):
            o_ref[...] = total(x_ref)

        return _call(body, name=name, grid=(r // tr,),
                     in_specs=[pl.BlockSpec((n, tr, c), lambda i: (0, i, 0))],
                     out_specs=pl.BlockSpec((tr, c), lambda i: (i, 0)),
                     out_shape=jax.ShapeDtypeStruct((r, c), F32), compiler_params=_cp(("parallel",)))(x)

    def body_half(core_ref, x_ref, o_ref):
        o_ref[0] = total(x_ref)

    grid_spec = pltpu.PrefetchScalarGridSpec(
        num_scalar_prefetch=1, grid=(r // tr,),
        in_specs=[pl.BlockSpec((n, tr, c), lambda i, cr: (0, i, 0))],
        out_specs=pl.BlockSpec((1, tr, c), lambda i, cr: (cr[0], i, 0)))
    return _call(body_half, name=name, grid_spec=grid_spec, out_shape=jax.ShapeDtypeStruct((2, r, c), F32),
                 compiler_params=_cp(("parallel",)))(core, x)


def _pos():
    return lax.axis_index("x"), lax.axis_index("y"), lax.axis_index("c")


def _flip(v, f):
    return (1 - v) if f else v


CHIP_FLIPS = ((0, 1), (1, 0), (1, 1))
DEV_FLIPS = tuple((fx, fy, fc) for fx in (0, 1) for fy in (0, 1) for fc in (0, 1))[1:]


def _remote(src, dst, ssem, rsem, dev):
    return pltpu.make_async_remote_copy(src_ref=src, dst_ref=dst, send_sem=ssem, recv_sem=rsem,
                                        device_id=dev, device_id_type=MESH)


def _ag_weights(name, halves):
    n = len(halves)

    def body(*refs):
        h_refs, big_refs, sems = refs[:n], refs[n:2 * n], refs[2 * n:]
        _ag_phase(0, h_refs, big_refs, sems)
        _ag_phase(1, h_refs, big_refs, sems)
        _ag_phase(2, h_refs, big_refs, sems)

    return _call(body, name=name, in_specs=[ANY] * n, out_specs=[ANY] * n,
                 out_shape=_ag_out_shapes(halves), scratch_shapes=_ag_sems(n))(*halves)


def _ag_out_shapes(halves):
    return [jax.ShapeDtypeStruct((2, 4) + h.shape, h.dtype) for h in halves]


def _ag_sems(n):
    return [pltpu.SemaphoreType.DMA((n,)), pltpu.SemaphoreType.DMA((7 * n,)), pltpu.SemaphoreType.DMA((7 * n,))]


def _ag_phase(phase, h_refs, big_refs, sems):
    lsem, ssem, rsem = sems
    n = len(h_refs)
    x, y, cc = _pos()
    me = 2 * x + y
    sib = (x, y, 1 - cc)
    peers = [(_flip(x, fx), _flip(y, fy), cc) for fx, fy in CHIP_FLIPS]
    chips = [2 * px + py for px, py, _ in peers]
    for a in range(n):
        mine = big_refs[a].at[cc, me]
        first = [pltpu.make_async_copy(h_refs[a], mine, lsem.at[a])]
        first += [_remote(h_refs[a], mine, ssem.at[7 * a + k], rsem.at[7 * a + k], peers[k]) for k in range(3)]
        first += [_remote(h_refs[a], mine, ssem.at[7 * a + 3], rsem.at[7 * a + 3], sib)]
        passed = []
        for k in range(3):
            blk = big_refs[a].at[cc, chips[k]]
            passed.append(_remote(blk, blk, ssem.at[7 * a + 4 + k], rsem.at[7 * a + 4 + k], sib))
        if phase == 0:
            for cp in first:
                cp.start()
        elif phase == 1:
            for k in range(3):
                blk = big_refs[a].at[cc, chips[k]]
                _remote(blk, blk, ssem.at[7 * a + k], rsem.at[7 * a + k], peers[k]).wait_recv()
                passed[k].start()
        else:
            theirs = big_refs[a].at[1 - cc, me]
            _remote(theirs, theirs, ssem.at[7 * a + 3], rsem.at[7 * a + 3], sib).wait_recv()
            for k in range(3):
                blk = big_refs[a].at[1 - cc, chips[k]]
                _remote(blk, blk, ssem.at[7 * a + 4 + k], rsem.at[7 * a + 4 + k], sib).wait_recv()
            first[0].wait()
            for cp in first[1:] + passed:
                cp.wait_send()


def _ag_devices(name, v):
    r, c = v.shape

    def body(v_ref, out_ref, lsem, ssem, rsem):
        x, y, cc = _pos()
        me = 4 * x + 2 * y + cc
        loc = pltpu.make_async_copy(v_ref, out_ref.at[me], lsem)
        loc.start()
        started = []
        for k, (fx, fy, fc) in enumerate(DEV_FLIPS):
            cp = _remote(v_ref, out_ref.at[me], ssem.at[k], rsem.at[k], (_flip(x, fx), _flip(y, fy), _flip(cc, fc)))
            cp.start()
            started.append(cp)
        for k, (fx, fy, fc) in enumerate(DEV_FLIPS):
            px, py, pc = _flip(x, fx), _flip(y, fy), _flip(cc, fc)
            blk = out_ref.at[4 * px + 2 * py + pc]
            _remote(blk, blk, ssem.at[k], rsem.at[k], (px, py, pc)).wait_recv()
        for cp in started:
            cp.wait_send()
        loc.wait()

    return _call(body, name=name, in_specs=[ANY], out_specs=ANY,
                 out_shape=jax.ShapeDtypeStruct((8, r, c), v.dtype),
                 scratch_shapes=[pltpu.SemaphoreType.DMA, pltpu.SemaphoreType.DMA((7,)), pltpu.SemaphoreType.DMA((7,))])(v)


def _sibling_other_half(name, g2s):
    n = len(g2s)

    def body(*refs):
        g_refs, out_refs, sems = refs[:n], refs[n:2 * n], refs[2 * n:]
        _sibling_phase(0, g_refs, out_refs, sems)
        _sibling_phase(1, g_refs, out_refs, sems)

    return _call(body, name=name, in_specs=[ANY] * n, out_specs=[ANY] * n,
                 out_shape=_sibling_out_shapes(g2s), scratch_shapes=_sibling_sems(n))(*g2s)


def _sibling_out_shapes(g2s):
    return [jax.ShapeDtypeStruct(g.shape[1:], g.dtype) for g in g2s]


def _sibling_sems(n):
    return [pltpu.SemaphoreType.DMA((n,)), pltpu.SemaphoreType.DMA((n,))]


def _sibling_phase(phase, g_refs, out_refs, sems):
    ssem, rsem = sems
    x, y, cc = _pos()
    for a in range(len(g_refs)):
        cp = _remote(g_refs[a].at[1 - cc], out_refs[a], ssem.at[a], rsem.at[a], (x, y, 1 - cc))
        if phase == 0:
            cp.start()
        else:
            cp.wait()


def _scatter_sems(n):
    return [pltpu.SemaphoreType.DMA((n,)), pltpu.SemaphoreType.DMA((3 * n,)), pltpu.SemaphoreType.DMA((3 * n,))]


def _scatter_phase(phase, p_refs, out_refs, sems):
    lsem, ssem, rsem = sems
    x, y, cc = _pos()
    me = 2 * x + y
    peers = [(_flip(x, fx), _flip(y, fy), cc) for fx, fy in CHIP_FLIPS]
    for a in range(len(p_refs)):
        loc = pltpu.make_async_copy(p_refs[a].at[me], out_refs[a].at[me], lsem.at[a])
        sends = [_remote(p_refs[a].at[2 * px + py], out_refs[a].at[me], ssem.at[3 * a + k], rsem.at[3 * a + k], peers[k])
                 for k, (px, py, _) in enumerate(peers)]
        if phase == 0:
            loc.start()
            for cp in sends:
                cp.start()
        else:
            for k, (px, py, _) in enumerate(peers):
                blk = out_refs[a].at[2 * px + py]
                _remote(blk, blk, ssem.at[3 * a + k], rsem.at[3 * a + k], peers[k]).wait_recv()
            loc.wait()
            for cp in sends:
                cp.wait_send()


def _sibling_join(name, bufs):
    n = len(bufs)

    def body(*refs):
        out_refs, (ssem, rsem) = refs[n:2 * n], refs[2 * n:]
        x, y, cc = _pos()
        sib = (x, y, 1 - cc)
        cps = [_remote(out_refs[a].at[cc], out_refs[a].at[cc], ssem.at[a], rsem.at[a], sib) for a in range(n)]
        for cp in cps:
            cp.start()
        for a in range(n):
            theirs = out_refs[a].at[1 - cc]
            _remote(theirs, theirs, ssem.at[a], rsem.at[a], sib).wait_recv()
        for cp in cps:
            cp.wait_send()

    return _call(body, name=name, in_specs=[ANY] * n, out_specs=[ANY] * n,
                 out_shape=[jax.ShapeDtypeStruct(b.shape, b.dtype) for b in bufs],
                 input_output_aliases={a: a for a in range(n)},
                 scratch_shapes=[pltpu.SemaphoreType.DMA((n,)), pltpu.SemaphoreType.DMA((n,))])(*bufs)


def _pad_lanes(a, width=128):
    return jnp.pad(a, ((0, 0), (0, width - a.shape[1])))


def kernel(x, norm_gains, fox_w_in, fox_b_f, hgrn_w_in, hgrn_lb_logits, hgrn_onorm, w_out, final_gain, loss_target, m_norm_gains, m_fox_w_in, m_fox_b_f, m_hgrn_w_in, m_hgrn_lb_logits, m_hgrn_onorm, m_w_out, m_final_gain, v_norm_gains, v_fox_w_in, v_fox_b_f, v_hgrn_w_in, v_hgrn_lb_logits, v_hgrn_onorm, v_w_out, v_final_gain):
    s, d = x.shape[1], x.shape[2]
    wq = w_out.shape[1]
    w = 4 * wq
    hh = w // HEAD_DIM
    fox_cols = fox_w_in.shape[2]
    assert 4 * fox_cols == 4 * w + hh and hgrn_w_in.shape[2] == w
    core = lax.axis_index("c")

    x0 = x[0]
    tgt = loss_target[0]

    def my_half(a):
        return lax.dynamic_index_in_dim(a, core, 0, keepdims=False).astype(BF16)

    (big_fox,) = _ag_weights("ag_weights", [my_half(fox_w_in.reshape(2, d // 2, fox_cols))])
    later_halves = [my_half(hgrn_w_in.reshape(2, d // 2, w)), my_half(w_out),
                    lax.dynamic_index_in_dim(hgrn_onorm.reshape(2, 1, wq // 2), core, 0, keepdims=False)]
    fox_chip = [big_fox[:, j].reshape(d, fox_cols) for j in range(4)]
    def fox_columns(a, b):
        out = []
        for j in range(4):
            lo, hi = max(a, j * fox_cols), min(b, (j + 1) * fox_cols)
            if lo < hi:
                out.append(fox_chip[j][:, lo - j * fox_cols:hi - j * fox_cols])
        return out

    w_main = jnp.concatenate(fox_columns(0, 3 * w) + fox_columns(3 * w + hh, 4 * w + hh), axis=1)
    w_fl = _pad_lanes(jnp.concatenate(fox_columns(3 * w, 3 * w + hh), axis=1))

    bf_pad = _pad_lanes(fox_b_f)
    g0, g1 = norm_gains[0:1], norm_gains[1:2]
    gf = final_gain.reshape(1, d)

    h0 = _rms_fwd("rms0_fwd", x0, g0)
    qkv = _mm("fox_qkv", h0, w_main, out_dtype=BF16, n=3 * w, col_scale=(w, Q_FOLD))
    gate0 = _mm("fox_gate", h0, w_main, n=w, b_off=3 * w)
    fl = _mm("fox_flogit", h0, w_fl)
    _, c_rep = _fox_gate_fwd("fox_cumsum", fl, bf_pad, w, rep_scale=LOG2E)
    o0, lse_row, y0, big_hgrn, big_out, big_onorm = _attn_fwd("fox_attn_fwd", qkv, c_rep, gate0, hh, gather=later_halves)
    onorm_full = big_onorm.transpose(1, 0, 2, 3).reshape(1, w)
    w_o = big_out.reshape(2, w, d)
    x1 = _mm("fox_out", y0, w_o[0], res=x0)
    h1 = _rms_fwd("rms1_fwd", x1, g1)
    p1 = _mm("hgrn_in", h1, big_hgrn, b_gathered=True)
    y1, o1, states = _hgrn_fwd("hgrn_fwd", p1, hgrn_lb_logits, onorm_full, hh)
    x2 = _mm("hgrn_out", y1, w_o[1], res=x1)
    dx2, d_gf, loss_tile = _loss_head("loss_head", x2, tgt, gf)

    dy1 = _mm("hgrn_out_dy", dx2, w_o[1], tb=True)
    d_wo1 = _mm("hgrn_out_dw", y1, dx2, ta=True)
    dp1, d_lbl, d_onorm = _hgrn_bwd("hgrn_bwd", p1, hgrn_lb_logits, onorm_full, o1, dy1, states, hh)
    dh1 = _mm("hgrn_in_dh", dp1, big_hgrn, tb=True, b_gathered=True, a_stacked=True)
    g2_hgrn = _mm("hgrn_in_dw", h1, dp1, ta=True, out_gathered=(d // 2, w), b_stacked=True)
    dx1, d_g1 = _rms_bwd("rms1_bwd", x1, g1, dh1, dx2)
    dy0 = _mm("fox_out_dy", dx1, w_o[0], tb=True)
    d_wo0 = _mm("fox_out_dw", y0, dx1, ta=True)
    core_arr = core.reshape(1).astype(jnp.int32)
    g2_out = jnp.stack([d_wo0, d_wo1]).reshape(2, 4, wq, d)
    do0, dgate0, delta, sib_hgrn, sib_out = _fox_pre_bwd("fox_pre_bwd", dy0, o0, gate0, exchange=[g2_hgrn, g2_out])
    delta_row = delta[:, :hh].T.reshape(hh, 1, s)
    pairs_early = [_pair_add("rs_pair_add_hgrn", g2_hgrn, sib_hgrn, core_arr),
                   _pair_add("rs_pair_add_out", g2_out, sib_out, core_arr)]
    dp0, dc_rep, rowsum_row, got_hgrn, got_out = _attn_bwd(
        "fox_attn_bwd", qkv, do0, dgate0, lse_row, delta_row, c_rep, hh, scatter=pairs_early)
    dfl, d_bf = _fox_gate_bwd("fox_cumsum_bwd", _pad_lanes(rowsum_row.reshape(hh, s).T),
                              _pad_lanes(dc_rep[:, ::HEAD_DIM]), fl, bf_pad)
    d_wmain = _mm("fox_in_dw", h0, dp0, ta=True, b_stacked=True)
    d_wfl = _mm("fox_fl_dw", h0, dfl, ta=True)

    def grad_columns(a, b):
        out = []
        for lo, hi, src, shift in ((0, 3 * w, d_wmain, 0), (3 * w, 3 * w + hh, d_wfl, 3 * w),
                                   (3 * w + hh, 4 * w + hh, d_wmain, hh)):
            l2, h2 = max(a, lo), min(b, hi)
            if l2 < h2:
                out.append(src[:, l2 - shift:h2 - shift])
        return out

    g2_fox = jnp.stack([jnp.concatenate(grad_columns(j * fox_cols, (j + 1) * fox_cols), axis=1).reshape(2, d // 2, fox_cols)
                        for j in range(4)], axis=1)
    (sib_fox,) = _sibling_other_half("rs_sibling", [g2_fox])
    dh0, got_fox = _mm("fox_in_dh", dp0, w_main, tb=True, res=_mm("fox_fl_dh", dfl, w_fl, tb=True), a_stacked=True,
                       scatter=[_pair_add("rs_pair_add_fox", g2_fox, sib_fox, core_arr)])
    grad_x, d_g0 = _rms_bwd("rms0_bwd", x0, g0, dh0, dx1)
    halves_sum = [_sum_slots("rs_sum_" + nm, got, core_arr)
                  for nm, got in zip(("fox", "hgrn", "out"), (got_fox, got_hgrn, got_out))]
    r_fox, r_hgrn, r_out = _sibling_join("rs_join", halves_sum)
    g_fox = r_fox.reshape(d, fox_cols)
    g_hgrn = r_hgrn.reshape(d, w)
    g_out = r_out.reshape(2 * wq, d)

    small = jnp.concatenate([jnp.concatenate([d_g0, d_g1], axis=0).reshape(-1), d_lbl.reshape(-1), d_gf.reshape(-1),
                             d_onorm.reshape(-1), d_bf.reshape(-1)])
    n_small = small.shape[0]
    pad_to = -(-n_small // 1024) * 1024
    small = jnp.pad(small, (0, pad_to - n_small)).reshape(pad_to // 128, 128)
    small = _sum_slots("small_sum", _ag_devices("small_gather", small)).reshape(-1)
    g_norm = small[:2 * d].reshape(2, d)
    g_lbl = small[2 * d:2 * d + 2 * w].reshape(2, w)
    g_gf = small[2 * d + 2 * w:3 * d + 2 * w]
    g_onorm_full = small[3 * d + 2 * w:3 * d + 3 * w]
    g_bf = small[3 * d + 3 * w:3 * d + 3 * w + hh].reshape(1, hh)
    chip = 2 * lax.axis_index("x") + lax.axis_index("y")
    g_onorm = lax.dynamic_slice_in_dim(g_onorm_full, chip * wq, wq).reshape(1, wq)

    loss = lax.psum(loss_tile[0, 0], ("x", "y", "c"))

    def upd(name, wt, g, m, v):
        shp = wt.shape
        two = lambda a: a.reshape(-1, shp[-1])
        dl, nm, nv = _adamw(name, two(wt), two(g), two(m), two(v))
        return g.reshape(shp), dl.reshape(shp), nm.reshape(shp), nv.reshape(shp)

    res = [
        upd("adamw_norm_gains", norm_gains, g_norm, m_norm_gains, v_norm_gains),
        upd("adamw_fox_w_in", fox_w_in, g_fox, m_fox_w_in, v_fox_w_in),
        upd("adamw_fox_b_f", fox_b_f, g_bf, m_fox_b_f, v_fox_b_f),
        upd("adamw_hgrn_w_in", hgrn_w_in, g_hgrn, m_hgrn_w_in, v_hgrn_w_in),
        upd("adamw_lb_logits", hgrn_lb_logits, g_lbl, m_hgrn_lb_logits, v_hgrn_lb_logits),
        upd("adamw_onorm", hgrn_onorm, g_onorm, m_hgrn_onorm, v_hgrn_onorm),
        upd("adamw_w_out", w_out, g_out, m_w_out, v_w_out),
        upd("adamw_final_gain", final_gain.reshape(1, d), g_gf.reshape(1, d), m_final_gain.reshape(1, d),
            v_final_gain.reshape(1, d)),
    ]
    res[-1] = tuple(a.reshape(d) for a in res[-1])
    grads, deltas, new_m, new_v = zip(*res)
    return (loss, grad_x[None], *grads, *deltas, *new_m, *new_v)
```

```python
import functools

import jax
import jax.numpy as jnp
from jax import lax
from jax.experimental import pallas as pl
from jax.experimental.pallas import tpu as pltpu

F32 = jnp.float32
BF16 = jnp.bfloat16
MESH = pl.DeviceIdType.MESH
ANY = pl.BlockSpec(memory_space=pl.ANY)

EPS = 1e-6
HEAD_DIM = 128
HGRN_CHUNK = 64
HGRN_SUB = 32
EXP_CLAMP = 80.0
ATT_BLOCK = 1024
HGRN_BLOCK = 2048
GATE_BLOCK = 512
ROW_BLOCK = 512
MM_TM, MM_TN, MM_TK = 1024, 1024, 2048
VMEM_LIMIT_V7X = 56 * 1024 * 1024

ADAM_LR, ADAM_B1, ADAM_B2, ADAM_EPS, ADAM_WD, ADAM_STEP = 0.001, 0.9, 0.999, 1e-08, 0.01, 10

NT = (((1,), (1,)), ((), ()))
TN = (((0,), (0,)), ((), ()))
NN = (((1,), (0,)), ((), ()))


def _call(body, **kw):
    return pl.pallas_call(body, **kw)


def _cp(dims=None):
    kw = dict(vmem_limit_bytes=VMEM_LIMIT_V7X)
    if dims is not None:
        kw["dimension_semantics"] = dims
    return pltpu.CompilerParams(**kw)


def _sigmoid(x):
    return 1.0 / (1.0 + jnp.exp(-x))


def _dot(a, b, dn=NN):
    return lax.dot_general(a.astype(BF16), b.astype(BF16), dn, preferred_element_type=F32)


def _split3(x):
    hi = x.astype(BF16)
    r1 = x - hi.astype(F32)
    mid = r1.astype(BF16)
    lo = (r1 - mid.astype(F32)).astype(BF16)
    return hi, mid, lo


def _dot_exact(m01, x, right=False):
    hi, mid, lo = _split3(x)
    if right:
        dot = lambda p: lax.dot_general(p, m01, NN, preferred_element_type=F32)
    else:
        dot = lambda p: lax.dot_general(m01, p, NN, preferred_element_type=F32)
    return dot(hi) + dot(mid) + dot(lo)


def _row_block(rows, cap):
    if rows <= cap:
        return rows
    best = None
    for t in range(16, cap + 1, 16):
        if rows % t == 0:
            best = t
    assert best is not None, rows
    return best


def _mm(name, a, b, *, ta=False, tb=False, out_dtype=F32, res=None, n=None, b_off=0, b_gathered=False,
        out_gathered=None, scatter=(), a_stacked=False, b_stacked=False, col_scale=None):
    if a_stacked:
        assert not ta
        m, k = a.shape[1], a.shape[0] * a.shape[2]
    else:
        m, k = (a.shape[1], a.shape[0]) if ta else a.shape
    if b_gathered:
        rh, cw = b.shape[2], b.shape[3]
        n_full = 2 * rh if tb else 4 * cw
    elif b_stacked:
        assert not tb
        n_full = b.shape[0] * b.shape[2]
    else:
        n_full = b.shape[0] if tb else b.shape[1]
    n = n_full if n is None else n
    tm, tn, tk = min(MM_TM, m), min(MM_TN, n), min(MM_TK, k)
    b_both = b_gathered and not tb and tk == 2 * rh
    if b_gathered and not b_both:
        tn, tk = (min(tn, rh), min(tk, cw)) if tb else (min(tn, cw), min(tk, rh))
    if b_both:
        tn = min(tn, cw)
    if out_gathered:
        tm, tn = min(tm, out_gathered[0]), min(tn, out_gathered[1])
    if a_stacked:
        tk = min(tk, a.shape[2])
    if b_stacked:
        tn = min(tn, b.shape[2])
    assert m % tm == 0 and n % tn == 0 and k % tk == 0 and b_off % tn == 0
    nk = k // tk
    jo = b_off // tn
    if a_stacked:
        pa = a.shape[2] // tk
        a_spec = pl.BlockSpec((1, tm, tk), lambda i, j, kk: (kk // pa, i, kk % pa))
    elif ta:
        a_spec = pl.BlockSpec((tk, tm), lambda i, j, kk: (kk, i))
    else:
        a_spec = pl.BlockSpec((tm, tk), lambda i, j, kk: (i, kk))
    if b_gathered and tb:
        pr, pc = rh // tn, cw // tk
        b_spec = pl.BlockSpec((1, 1, tn, tk), lambda i, j, kk: (j // pr, kk // pc, j % pr, kk % pc))
    elif b_both:
        pc = cw // tn
        b_spec = pl.BlockSpec((2, 1, rh, tn), lambda i, j, kk: (0, j // pc, 0, j % pc))
    elif b_gathered:
        pr, pc = rh // tk, cw // tn
        b_spec = pl.BlockSpec((1, 1, tk, tn), lambda i, j, kk: (kk // pr, j // pc, kk % pr, j % pc))
    elif b_stacked:
        pb = b.shape[2] // tn
        b_spec = pl.BlockSpec((1, tk, tn), lambda i, j, kk: (j // pb, kk, j % pb))
    elif tb:
        b_spec = pl.BlockSpec((tn, tk), lambda i, j, kk: (j + jo, kk))
    else:
        b_spec = pl.BlockSpec((tk, tn), lambda i, j, kk: (kk, j + jo))
    o_spec = pl.BlockSpec((tm, tn), lambda i, j, kk: (i, j))
    if out_gathered:
        assert m == 2 * out_gathered[0] and n == 4 * out_gathered[1] and res is None
        qr, qc = out_gathered[0] // tm, out_gathered[1] // tn
        o_spec = pl.BlockSpec((1, 1, tm, tn), lambda i, j, kk: (i // qr, j // qc, i % qr, j % qc))
    dn = (((0 if ta else 1,), (1 if tb else 0,)), ((), ()))
    has_res = res is not None
    ns = len(scatter)
    grid = (m // tm, n // tn, nk)

    def body(*refs):
        a_ref, b_ref = refs[:2]
        r_ref = refs[2] if has_res else None
        base = 2 + int(has_res)
        p_refs = refs[base:base + ns]
        o_ref = refs[base + ns]
        got_refs = refs[base + ns + 1:base + 2 * ns + 1]
        acc = refs[base + 2 * ns + 1]
        sems = refs[base + 2 * ns + 2:]
        kk = pl.program_id(2)
        if ns:
            first = jnp.logical_and(jnp.logical_and(pl.program_id(0) == 0, pl.program_id(1) == 0), kk == 0)
            last = jnp.logical_and(jnp.logical_and(pl.program_id(0) == grid[0] - 1, pl.program_id(1) == grid[1] - 1),
                                   kk == nk - 1)

            @pl.when(first)
            def _():
                _scatter_phase(0, p_refs, got_refs, sems)

        av = a_ref[0] if a_stacked else a_ref[...]
        if b_both:
            p = (lax.dot_general(av[:, :rh].astype(BF16), b_ref[0, 0].astype(BF16), dn, preferred_element_type=F32)
                 + lax.dot_general(av[:, rh:].astype(BF16), b_ref[1, 0].astype(BF16), dn, preferred_element_type=F32))
        else:
            bv = b_ref[0, 0] if b_gathered else (b_ref[0] if b_stacked else b_ref[...])
            p = lax.dot_general(av.astype(BF16), bv.astype(BF16), dn, preferred_element_type=F32)

        def finish(total):
            if r_ref is not None:
                total = total + r_ref[...]
            if col_scale is not None:
                total = total * jnp.where(pl.program_id(1) < col_scale[0] // tn, col_scale[1], 1.0)
            if out_gathered:
                o_ref[0, 0] = total.astype(out_dtype)
            else:
                o_ref[...] = total.astype(out_dtype)

        if nk == 1:
            finish(p)
        else:
            @pl.when(kk == 0)
            def _():
                acc[...] = p

            @pl.when(jnp.logical_and(kk > 0, kk < nk - 1))
            def _():
                acc[...] += p

            @pl.when(kk == nk - 1)
            def _():
                finish(acc[...] + p)

        if ns:
            @pl.when(last)
            def _():
                _scatter_phase(1, p_refs, got_refs, sems)

    ins = [a, b] + ([res] if has_res else []) + list(scatter)
    in_specs = [a_spec, b_spec] + ([o_spec] if has_res else []) + [ANY] * ns
    o_shape = jax.ShapeDtypeStruct((2, 4) + tuple(out_gathered) if out_gathered else (m, n), out_dtype)
    out = _call(
        body, name=name, grid=grid, in_specs=in_specs, out_specs=[o_spec] + [ANY] * ns,
        out_shape=[o_shape] + [jax.ShapeDtypeStruct(p.shape, p.dtype) for p in scatter],
        scratch_shapes=[pltpu.VMEM((tm, tn) if nk > 1 else (8, 128), F32)] + (_scatter_sems(ns) if ns else []),
        compiler_params=_cp(("arbitrary",) * 3 if ns else ("parallel", "parallel", "arbitrary")),
    )(*ins)
    return out if ns else out[0]


def _rms_fwd(name, x, g):
    s, d = x.shape
    tm = min(ROW_BLOCK, s)

    def body(x_ref, g_ref, h_ref):
        xv = x_ref[...]
        r = lax.rsqrt(jnp.mean(xv * xv, axis=-1, keepdims=True) + EPS)
        h_ref[...] = (xv * r * g_ref[...]).astype(BF16)

    row = pl.BlockSpec((tm, d), lambda i: (i, 0))
    vec = pl.BlockSpec((1, d), lambda i: (0, 0))
    return _call(body, name=name, grid=(s // tm,), in_specs=[row, vec], out_specs=row,
                 out_shape=jax.ShapeDtypeStruct((s, d), BF16), compiler_params=_cp(("parallel",)))(x, g)


def _rms_bwd(name, x, g, dh, dres):
    s, d = x.shape
    tm = min(ROW_BLOCK, s)

    def body(x_ref, g_ref, dh_ref, dres_ref, dx_ref, dg_ref):
        @pl.when(pl.program_id(0) == 0)
        def _():
            dg_ref[...] = jnp.zeros_like(dg_ref)

        xv = x_ref[...]
        r = lax.rsqrt(jnp.mean(xv * xv, axis=-1, keepdims=True) + EPS)
        xn = xv * r
        dhv = dh_ref[...]
        dxn = dhv * g_ref[...]
        dx_ref[...] = dres_ref[...] + r * (dxn - xn * jnp.mean(dxn * xn, axis=-1, keepdims=True))
        dg_ref[...] += jnp.sum(dhv * xn, axis=0, keepdims=True)

    row = pl.BlockSpec((tm, d), lambda i: (i, 0))
    vec = pl.BlockSpec((1, d), lambda i: (0, 0))
    return _call(body, name=name, grid=(s // tm,), in_specs=[row, vec, row, row], out_specs=[row, vec],
                 out_shape=[jax.ShapeDtypeStruct((s, d), F32), jax.ShapeDtypeStruct((1, d), F32)],
                 compiler_params=_cp(("arbitrary",)))(x, g, dh, dres)


def _loss_head(name, x, tgt, g):
    s, d = x.shape
    tm = min(ROW_BLOCK, s)
    nb = s // tm

    def body(x_ref, t_ref, g_ref, dx_ref, dg_ref, loss_ref, lacc):
        i = pl.program_id(0)

        @pl.when(i == 0)
        def _():
            dg_ref[...] = jnp.zeros_like(dg_ref)
            lacc[...] = jnp.zeros_like(lacc)

        xv = x_ref[...]
        gv = g_ref[...]
        r = lax.rsqrt(jnp.mean(xv * xv, axis=-1, keepdims=True) + EPS)
        xn = xv * r
        err = xn * gv - t_ref[...]
        lacc[...] += jnp.sum(err * err, axis=0, keepdims=True)
        dout = err * (1.0 / d)
        dg_ref[...] += jnp.sum(dout * xn, axis=0, keepdims=True)
        dxn = dout * gv
        dx_ref[...] = r * (dxn - xn * jnp.mean(dxn * xn, axis=-1, keepdims=True))

        @pl.when(i == nb - 1)
        def _():
            total = jnp.sum(lacc[...], axis=1, keepdims=True) * (0.5 / d)
            loss_ref[...] = jnp.broadcast_to(total, loss_ref.shape)

    row = pl.BlockSpec((tm, d), lambda i: (i, 0))
    vec = pl.BlockSpec((1, d), lambda i: (0, 0))
    one = pl.BlockSpec((1, 128), lambda i: (0, 0))
    return _call(body, name=name, grid=(nb,), in_specs=[row, row, vec], out_specs=[row, vec, one],
                 out_shape=[jax.ShapeDtypeStruct((s, d), F32), jax.ShapeDtypeStruct((1, d), F32),
                            jax.ShapeDtypeStruct((1, 128), F32)],
                 scratch_shapes=[pltpu.VMEM((1, d), F32)], compiler_params=_cp(("arbitrary",)))(x, tgt, g)


def _fox_gate_fwd(name, fl, bf, w, rep_scale=1.0):
    s = fl.shape[0]
    tb = min(GATE_BLOCK, s)

    def body(fl_ref, bf_ref, c_ref, crep_ref, carry):
        @pl.when(pl.program_id(0) == 0)
        def _():
            carry[...] = jnp.zeros_like(carry)

        z = fl_ref[...] + bf_ref[...]
        lf = jnp.minimum(z, 0.0) - jnp.log(1.0 + jnp.exp(-jnp.abs(z)))
        rows = lax.broadcasted_iota(jnp.int32, (tb, tb), 0)
        cols = lax.broadcasted_iota(jnp.int32, (tb, tb), 1)
        tri = (rows >= cols).astype(BF16)
        cs = _dot_exact(tri, lf) + carry[...]
        c_ref[...] = cs
        carry[...] = c_ref[tb - 1:tb, :]
        sel_r = lax.broadcasted_iota(jnp.int32, (128, w), 0)
        sel_c = lax.broadcasted_iota(jnp.int32, (128, w), 1)
        sel = (sel_r == sel_c // HEAD_DIM).astype(BF16)
        crep_ref[...] = _dot_exact(sel, cs * rep_scale, right=True)

    blk = pl.BlockSpec((tb, 128), lambda i: (i, 0))
    return _call(body, name=name, grid=(s // tb,),
                 in_specs=[blk, pl.BlockSpec((1, 128), lambda i: (0, 0))],
                 out_specs=[blk, pl.BlockSpec((tb, w), lambda i: (i, 0))],
                 out_shape=[jax.ShapeDtypeStruct((s, 128), F32), jax.ShapeDtypeStruct((s, w), F32)],
                 scratch_shapes=[pltpu.VMEM((1, 128), F32)], compiler_params=_cp(("arbitrary",)))(fl, bf)


def _fox_gate_bwd(name, drow, dcol, fl, bf):
    s = fl.shape[0]
    tb = min(GATE_BLOCK, s)
    nb = s // tb

    def body(dr_ref, dc_ref, fl_ref, bf_ref, dfl_ref, dbf_ref, carry, tmp):
        @pl.when(pl.program_id(0) == 0)
        def _():
            carry[...] = jnp.zeros_like(carry)
            dbf_ref[...] = jnp.zeros_like(dbf_ref)

        rows = lax.broadcasted_iota(jnp.int32, (tb, tb), 0)
        cols = lax.broadcasted_iota(jnp.int32, (tb, tb), 1)
        triu = (rows <= cols).astype(BF16)
        dlf = _dot_exact(triu, dr_ref[...] + dc_ref[...]) + carry[...]
        tmp[...] = dlf
        carry[...] = tmp[0:1, :]
        z = fl_ref[...] + bf_ref[...]
        dfl = dlf * (1.0 / (1.0 + jnp.exp(z)))
        dfl_ref[...] = dfl
        dbf_ref[...] += jnp.sum(dfl, axis=0, keepdims=True)

    blk = pl.BlockSpec((tb, 128), lambda i: (nb - 1 - i, 0))
    vec = pl.BlockSpec((1, 128), lambda i: (0, 0))
    return _call(body, name=name, grid=(nb,), in_specs=[blk, blk, blk, vec], out_specs=[blk, vec],
                 out_shape=[jax.ShapeDtypeStruct((s, 128), F32), jax.ShapeDtypeStruct((1, 128), F32)],
                 scratch_shapes=[pltpu.VMEM((1, 128), F32), pltpu.VMEM((tb, 128), F32)],
                 compiler_params=_cp(("arbitrary",)))(drow, dcol, fl, bf)


LOG2E = 1.4426950408889634
ATT_SCALE = HEAD_DIM ** -0.5
Q_FOLD = ATT_SCALE * LOG2E


def _attn_fwd(name, qkv, c_rep, gate, h_count, gather=()):
    s = qkv.shape[0]
    w = h_count * HEAD_DIM
    t = min(ATT_BLOCK, s)
    hp = 2 if h_count % 2 == 0 else 1
    wb = hp * HEAD_DIM
    ng = len(gather)
    nh, nq = h_count // hp, s // t

    def body(*refs):
        q_ref, k_ref, v_ref, c_ref, gate_ref = refs[:5]
        h_refs = refs[5:5 + ng]
        o_ref, lse_ref, y_ref = refs[5 + ng:8 + ng]
        big_refs = refs[8 + ng:8 + 2 * ng]
        m_s, l_s, acc_s = refs[8 + 2 * ng:11 + 2 * ng]
        sems = refs[11 + 2 * ng:]
        hs = pl.program_id(0)
        qi = pl.program_id(1)
        if ng:
            @pl.when(jnp.logical_and(hs == 0, qi == 0))
            def _():
                _ag_phase(0, h_refs, big_refs, sems)

            @pl.when(jnp.logical_and(hs == nh // 2, qi == 0))
            def _():
                _ag_phase(1, h_refs, big_refs, sems)

        m_s[...] = jnp.full(m_s.shape, -jnp.inf, F32)
        l_s[...] = jnp.zeros_like(l_s)
        acc_s[...] = jnp.zeros_like(acc_s)

        def step(kb, masked):
            off = pl.multiple_of(kb * t, t)
            for a in range(hp):
                cols_a = slice(a * HEAD_DIM, (a + 1) * HEAD_DIM)
                kk = k_ref[pl.ds(off, t), cols_a]
                vv = v_ref[pl.ds(off, t), cols_a]
                cc = jnp.tile(c_ref[pl.ds(off, t), cols_a], (1, t // HEAD_DIM))
                st = lax.dot_general(kk, q_ref[:, cols_a], NT, preferred_element_type=F32) - cc
                if masked:
                    rows = lax.broadcasted_iota(jnp.int32, (t, t), 0)
                    cols = lax.broadcasted_iota(jnp.int32, (t, t), 1)
                    st = jnp.where(cols >= rows, st, -jnp.inf)
                m_prev = m_s[a]
                m_new = jnp.maximum(m_prev, jnp.max(st, axis=0, keepdims=True))
                pt = jnp.exp2(st - m_new)
                alpha = jnp.exp2(m_prev - m_new)
                l_s[a] = alpha * l_s[a] + jnp.sum(pt, axis=0, keepdims=True)
                acc_s[a] = alpha * acc_s[a] + lax.dot_general(vv, pt.astype(BF16), TN, preferred_element_type=F32)
                m_s[a] = m_new

        def loop_body(pair, carry):
            step(2 * pair, False)
            step(2 * pair + 1, False)
            return carry

        lax.fori_loop(0, qi // 2, loop_body, 0)

        @pl.when(lax.rem(qi, 2) == 1)
        def _():
            step(qi - 1, False)

        step(qi, True)
        for a in range(hp):
            l = l_s[a]
            cols_a = slice(a * HEAD_DIM, (a + 1) * HEAD_DIM)
            oa = (acc_s[a] / l).T
            o_ref[:, cols_a] = oa
            g = gate_ref[:, cols_a]
            y_ref[:, cols_a] = (oa * (g * _sigmoid(g))).astype(BF16)
            lse_ref[a] = m_s[a] + jnp.log2(l)

        if ng:
            @pl.when(jnp.logical_and(hs == nh - 1, qi == nq - 1))
            def _():
                _ag_phase(2, h_refs, big_refs, sems)

    blk = lambda off: pl.BlockSpec((t, wb), lambda h, i: (i, off + h))
    whole = lambda off: pl.BlockSpec((s, wb), lambda h, i: (0, off + h))
    rowv = pl.BlockSpec((hp, 1, t), lambda h, i: (h, 0, i))
    return _call(
        body, name=name, grid=(nh, nq),
        in_specs=[blk(0), whole(nh), whole(2 * nh), whole(0), blk(0)] + [ANY] * ng,
        out_specs=[blk(0), rowv, blk(0)] + [ANY] * ng,
        out_shape=[jax.ShapeDtypeStruct((s, w), F32), jax.ShapeDtypeStruct((h_count, 1, s), F32),
                   jax.ShapeDtypeStruct((s, w), BF16)] + _ag_out_shapes(gather),
        scratch_shapes=[pltpu.VMEM((hp, 1, t), F32), pltpu.VMEM((hp, 1, t), F32), pltpu.VMEM((hp, HEAD_DIM, t), F32)]
        + (_ag_sems(ng) if ng else []),
        compiler_params=_cp(("arbitrary", "arbitrary") if ng else ("parallel", "arbitrary")),
    )(qkv, qkv, qkv, c_rep, gate, *gather)


def _attn_bwd(name, qkv, do, dgate, lse_row, delta_row, c_rep, h_count, scatter=()):
    s = qkv.shape[0]
    w = h_count * HEAD_DIM
    t = min(ATT_BLOCK, s)
    nq = s // t
    scale = HEAD_DIM ** -0.5
    ns = len(scatter)

    def body(*refs):
        k_ref, v_ref, q_ref, do_ref, lse_ref, dl_ref, c_ref, dgate_ref = refs[:8]
        p_refs = refs[8:8 + ns]
        dp_ref, dc_ref, rs_ref = refs[8 + ns:11 + ns]
        got_refs = refs[11 + ns:11 + 2 * ns]
        dk_s, dv_s, dc_s, dq_s, rs_s = refs[11 + 2 * ns:16 + 2 * ns]
        sems = refs[16 + 2 * ns:]
        kj = pl.program_id(1)
        if ns:
            @pl.when(jnp.logical_and(pl.program_id(0) == 0, kj == 0))
            def _():
                _scatter_phase(0, p_refs, got_refs, sems)

        kk = k_ref[...]
        vv = v_ref[...]
        ccol = jnp.tile(c_ref[...], (1, t // HEAD_DIM))
        dk_s[...] = jnp.zeros_like(dk_s)
        dv_s[...] = jnp.zeros_like(dv_s)
        dc_s[...] = jnp.zeros_like(dc_s)

        @pl.when(kj == 0)
        def _():
            dq_s[...] = jnp.zeros_like(dq_s)
            rs_s[...] = jnp.zeros_like(rs_s)

        def step(qb, masked):
            off = pl.multiple_of(qb * t, t)
            q = q_ref[pl.ds(off, t), :]
            dov = do_ref[pl.ds(off, t), :]
            st = lax.dot_general(kk, q, NT, preferred_element_type=F32) - ccol
            pt = jnp.exp2(st - lse_ref[0, :, pl.ds(off, t)])
            if masked:
                rows = lax.broadcasted_iota(jnp.int32, (t, t), 0)
                cols = lax.broadcasted_iota(jnp.int32, (t, t), 1)
                pt = jnp.where(cols >= rows, pt, 0.0)
            dv_s[...] += lax.dot_general(pt.astype(BF16), dov, NN, preferred_element_type=F32)
            dpt = lax.dot_general(vv, dov, NT, preferred_element_type=F32)
            dst = pt * (dpt - dl_ref[0, :, pl.ds(off, t)])
            dstb = dst.astype(BF16)
            dk_s[...] += lax.dot_general(dstb, q, NN, preferred_element_type=F32)
            dc_s[...] += jnp.sum(dst, axis=1, keepdims=True)
            dq_s[:, pl.ds(off, t)] += lax.dot_general(kk, dstb, TN, preferred_element_type=F32)
            rs_s[:, pl.ds(off, t)] += jnp.sum(dst, axis=0, keepdims=True)

        step(kj, True)

        def loop_body(qb, carry):
            step(qb, False)
            return carry

        lax.fori_loop(kj + 1, nq, loop_body, 0)
        rows_j = pl.ds(pl.multiple_of(kj * t, t), t)
        dp_ref[1, rows_j, :] = (dk_s[...] * (1.0 / LOG2E)).astype(BF16)
        dp_ref[2, rows_j, :] = dv_s[...].astype(BF16)
        dc_ref[...] = jnp.broadcast_to(-dc_s[...], dc_ref.shape)

        @pl.when(kj == nq - 1)
        def _():
            for b in range(nq):
                dp_ref[0, b * t:(b + 1) * t, :] = (dq_s[:, b * t:(b + 1) * t] * scale).T.astype(BF16)
            dp_ref[3] = dgate_ref[...]
            rs_ref[0] = rs_s[...]

        if ns:
            @pl.when(jnp.logical_and(pl.program_id(0) == h_count - 1, kj == nq - 1))
            def _():
                _scatter_phase(1, p_refs, got_refs, sems)

    hh = h_count
    blk = lambda off: pl.BlockSpec((t, HEAD_DIM), lambda h, j: (j, off + h))
    whole = pl.BlockSpec((s, HEAD_DIM), lambda h, j: (0, h))
    rowv = pl.BlockSpec((1, 1, s), lambda h, j: (h, 0, 0))
    return _call(
        body, name=name, grid=(hh, nq),
        in_specs=[blk(hh), blk(2 * hh), whole, whole, rowv, rowv, blk(0), whole] + [ANY] * ns,
        out_specs=[pl.BlockSpec((4, s, HEAD_DIM), lambda h, j: (0, 0, h)), blk(0), rowv] + [ANY] * ns,
        out_shape=[jax.ShapeDtypeStruct((4, s, w), BF16), jax.ShapeDtypeStruct((s, w), F32),
                   jax.ShapeDtypeStruct((hh, 1, s), F32)] + [jax.ShapeDtypeStruct(p.shape, p.dtype) for p in scatter],
        scratch_shapes=[pltpu.VMEM((t, HEAD_DIM), F32), pltpu.VMEM((t, HEAD_DIM), F32), pltpu.VMEM((t, 1), F32),
                        pltpu.VMEM((HEAD_DIM, s), F32), pltpu.VMEM((1, s), F32)] + (_scatter_sems(ns) if ns else []),
        compiler_params=_cp(("arbitrary", "arbitrary") if ns else ("parallel", "arbitrary")),
    )(qkv, qkv, qkv, do, lse_row, delta_row, c_rep, dgate, *scatter)


def _fox_pre_bwd(name, dy, o, gate, exchange=()):
    s, w = o.shape
    tm = min(ROW_BLOCK, s)
    ne = len(exchange)
    nb = s // tm

    def body(*refs):
        dy_ref, o_ref, g_ref = refs[:3]
        x_refs = refs[3:3 + ne]
        do_ref, dg_ref, dl_ref = refs[3 + ne:6 + ne]
        got_refs = refs[6 + ne:6 + 2 * ne]
        sems = refs[6 + 2 * ne:]
        if ne:
            @pl.when(pl.program_id(0) == 0)
            def _():
                _sibling_phase(0, x_refs, got_refs, sems)

            @pl.when(pl.program_id(0) == nb - 1)
            def _():
                _sibling_phase(1, x_refs, got_refs, sems)

        g = g_ref[...]
        sg = _sigmoid(g)
        dyv = dy_ref[...]
        ov = o_ref[...]
        dov = dyv * (g * sg)
        do_ref[...] = dov.astype(BF16)
        dg_ref[...] = (dyv * ov * (sg * (1.0 + g * (1.0 - sg)))).astype(BF16)
        sel_r = lax.broadcasted_iota(jnp.int32, (w, 128), 0)
        sel_c = lax.broadcasted_iota(jnp.int32, (w, 128), 1)
        sel = (sel_r // HEAD_DIM == sel_c).astype(BF16)
        dl_ref[...] = _dot_exact(sel, dov * ov, right=True)

    row = pl.BlockSpec((tm, w), lambda i: (i, 0))
    lanes = pl.BlockSpec((tm, 128), lambda i: (i, 0))
    return _call(body, name=name, grid=(nb,), in_specs=[row, row, row] + [ANY] * ne,
                 out_specs=[row, row, lanes] + [ANY] * ne,
                 out_shape=[jax.ShapeDtypeStruct((s, w), BF16), jax.ShapeDtypeStruct((s, w), BF16),
                            jax.ShapeDtypeStruct((s, 128), F32)] + _sibling_out_shapes(exchange),
                 scratch_shapes=_sibling_sems(ne) if ne else [],
                 compiler_params=_cp(("arbitrary",) if ne else ("parallel",)))(dy, o, gate, *exchange)


def _hgrn_chunk_terms(q_c, k_c, b_c, b_s, base):
    nsub = HGRN_CHUNK // HGRN_SUB
    refs = [jnp.zeros((1, HEAD_DIM), F32)]
    for i in range(1, nsub):
        r0 = base + i * HGRN_SUB - 1
        refs.append(b_s[r0:r0 + 1, :])
    rfull = jnp.concatenate([jnp.broadcast_to(r, (HGRN_SUB, HEAD_DIM)) for r in refs], axis=0)
    eq = jnp.exp(b_c - rfull)
    qe = q_c * eq
    es = [jnp.exp(jnp.minimum(r - b_c, EXP_CLAMP)) for r in refs]
    kes = [(k_c * e).astype(BF16) for e in es]
    return eq, qe, es, kes


def _chunk_cumsum(x, reverse=False):
    cc = HGRN_CHUNK
    nc = x.shape[0] // cc
    rows = lax.broadcasted_iota(jnp.int32, (cc, cc), 0)
    cols = lax.broadcasted_iota(jnp.int32, (cc, cc), 1)
    tri = ((rows <= cols) if reverse else (rows >= cols)).astype(BF16)
    wide = jnp.concatenate([x[n * cc:(n + 1) * cc] for n in range(nc)], axis=1)
    res = _dot_exact(tri, wide)
    return jnp.concatenate([res[:, n * HEAD_DIM:(n + 1) * HEAD_DIM] for n in range(nc)], axis=0)


def _hgrn_block_pre(q_ref, f_ref, lbl_ref, b_s):
    lb = _sigmoid(lbl_ref[1:2, :] - lbl_ref[0:1, :])
    qr = q_ref[...]
    sq = _sigmoid(qr)
    q = qr * sq
    fz = f_ref[...]
    sg = _sigmoid(fz)
    f = lb + (1.0 - lb) * sg
    g = jnp.log(f)
    k = (1.0 - lb) * (1.0 / (1.0 + jnp.exp(fz)))
    b = _chunk_cumsum(g)
    b_s[...] = b
    return lb, qr, sq, q, sg, f, k, b


def _hgrn_fwd(name, p1, lbl, onorm, h_count):
    s = p1.shape[0]
    w = h_count * HEAD_DIM
    t = min(HGRN_BLOCK, s)
    nc = t // HGRN_CHUNK
    nsub = HGRN_CHUNK // HGRN_SUB
    cc = HGRN_CHUNK

    def body(q_ref, f_ref, i_ref, g_ref, lbl_ref, on_ref, y_ref, o_ref, st_ref, state_s, b_s):
        @pl.when(pl.program_id(1) == 0)
        def _():
            state_s[...] = jnp.zeros_like(state_s)

        lb, qr, sq, q, sg, f, k, b = _hgrn_block_pre(q_ref, f_ref, lbl_ref, b_s)
        v = i_ref[...]
        r64 = lax.broadcasted_iota(jnp.int32, (cc, cc), 0)
        c64 = lax.broadcasted_iota(jnp.int32, (cc, cc), 1)
        for n in range(nc):
            sl = slice(n * cc, (n + 1) * cc)
            q_c, k_c, v_c, b_c = q[sl], k[sl], v[sl], b[sl]
            bl = b_s[n * cc + cc - 1:n * cc + cc, :]
            eq, qe, es, kes = _hgrn_chunk_terms(q_c, k_c, b_c, b_s, n * cc)
            qeb = qe.astype(BF16)
            a = jnp.concatenate(
                [lax.dot_general(qeb[i * HGRN_SUB:(i + 1) * HGRN_SUB], kes[i], NT, preferred_element_type=F32)
                 for i in range(nsub)], axis=0)
            a = jnp.where(r64 >= c64, a, 0.0)
            st = state_s[...]
            st_ref[0, n] = st.astype(BF16)
            inter = _dot(q_c * jnp.exp(b_c), st, NT)
            intra = _dot(a, v_c)
            o_ref[sl, :] = inter + intra
            kb = k_c * jnp.exp(bl - b_c)
            state_s[...] = st * jnp.exp(bl) + _dot(v_c, kb, TN)
        o = o_ref[...]
        rr = lax.rsqrt(jnp.mean(o * o, axis=-1, keepdims=True) + EPS)
        gate = g_ref[...]
        y_ref[...] = ((o * rr) * on_ref[...] * (gate * _sigmoid(gate))).astype(BF16)

    hh = h_count
    blk = lambda off: pl.BlockSpec((t, HEAD_DIM), lambda h, i: (i, off + h))
    return _call(
        body, name=name, grid=(hh, s // t),
        in_specs=[blk(0), blk(hh), blk(2 * hh), blk(3 * hh),
                  pl.BlockSpec((2, HEAD_DIM), lambda h, i: (0, h)), pl.BlockSpec((1, HEAD_DIM), lambda h, i: (0, h))],
        out_specs=[blk(0), blk(0), pl.BlockSpec((1, nc, HEAD_DIM, HEAD_DIM), lambda h, i: (h, i, 0, 0))],
        out_shape=[jax.ShapeDtypeStruct((s, w), BF16), jax.ShapeDtypeStruct((s, w), F32),
                   jax.ShapeDtypeStruct((hh, s // cc, HEAD_DIM, HEAD_DIM), BF16)],
        scratch_shapes=[pltpu.VMEM((HEAD_DIM, HEAD_DIM), F32), pltpu.VMEM((t, HEAD_DIM), F32)],
        compiler_params=_cp(("parallel", "arbitrary")),
    )(p1, p1, p1, p1, lbl, onorm)


def _hgrn_bwd(name, p1, lbl, onorm, o, dy, states, h_count):
    s = p1.shape[0]
    w = h_count * HEAD_DIM
    t = min(HGRN_BLOCK, s)
    nb = s // t
    nc = t // HGRN_CHUNK
    nsub = HGRN_CHUNK // HGRN_SUB
    cc = HGRN_CHUNK

    def body(q_ref, f_ref, i_ref, g_ref, lbl_ref, on_ref, o_ref, dy_ref, st_ref,
             dp_ref, dlog_ref, don_ref,
             dst_s, b_s, do_s, dqs, dks, dbs, exs):
        @pl.when(pl.program_id(1) == 0)
        def _():
            dst_s[...] = jnp.zeros_like(dst_s)
            dlog_ref[...] = jnp.zeros_like(dlog_ref)
            don_ref[...] = jnp.zeros_like(don_ref)

        lb, qr, sq, q, sg, f, k, b = _hgrn_block_pre(q_ref, f_ref, lbl_ref, b_s)
        v = i_ref[...]

        ov = o_ref[...]
        rr = lax.rsqrt(jnp.mean(ov * ov, axis=-1, keepdims=True) + EPS)
        on = ov * rr
        gate = g_ref[...]
        sgt = _sigmoid(gate)
        silu = gate * sgt
        dyv = dy_ref[...]
        gain = on_ref[...]
        dp_ref[3] = (dyv * on * gain * (sgt * (1.0 + gate * (1.0 - sgt)))).astype(BF16)
        don_ref[...] += jnp.sum(dyv * on * silu, axis=0, keepdims=True)
        d_on = dyv * gain * silu
        do_s[...] = rr * (d_on - on * jnp.mean(d_on * on, axis=-1, keepdims=True))

        r64 = lax.broadcasted_iota(jnp.int32, (cc, cc), 0)
        c64 = lax.broadcasted_iota(jnp.int32, (cc, cc), 1)
        upper = r64 <= c64
        for n in reversed(range(nc)):
            sl = slice(n * cc, (n + 1) * cc)
            q_c, k_c, v_c, b_c = q[sl], k[sl], v[sl], b[sl]
            do_c = do_s[sl, :]
            bl = b_s[n * cc + cc - 1:n * cc + cc, :]
            ebl = jnp.exp(bl)
            eq, qe, es, kes = _hgrn_chunk_terms(q_c, k_c, b_c, b_s, n * cc)
            qeb = qe.astype(BF16)
            dob = do_c.astype(BF16)
            vb = v_c.astype(BF16)
            st = st_ref[0, n].astype(F32)
            dstn = dst_s[...]
            qb_ = q_c * jnp.exp(b_c)
            kb_ = k_c * jnp.exp(bl - b_c)
            at = jnp.zeros((cc, cc), F32)
            for i in range(nsub):
                blk_i = (c64 // HGRN_SUB) == i
                at = at + jnp.where(blk_i, lax.dot_general(kes[i], qeb, NT, preferred_element_type=F32), 0.0)
            at = jnp.where(upper, at, 0.0)
            dv = _dot(at, dob) + _dot(kb_, dstn, NT)
            dqb = _dot(dob, st)
            dkb = _dot(vb, dstn)
            da = jnp.where(r64 >= c64, lax.dot_general(dob, vb, NT, preferred_element_type=F32), 0.0)
            dat = jnp.where(upper, lax.dot_general(vb, dob, NT, preferred_element_type=F32), 0.0)
            dab = da.astype(BF16)
            dq_raw = jnp.concatenate(
                [lax.dot_general(dab[i * HGRN_SUB:(i + 1) * HGRN_SUB], kes[i], NN, preferred_element_type=F32)
                 for i in range(nsub)], axis=0)
            db = qb_.astype(BF16).astype(F32) * dqb + qeb.astype(F32) * dq_raw
            dkbk = dkb * kb_.astype(BF16).astype(F32)
            db = db - dkbk
            dk_in = jnp.zeros((cc, HEAD_DIM), F32)
            for i in range(nsub):
                blk_i = (c64 // HGRN_SUB) == i
                dk_raw = _dot(jnp.where(blk_i, dat, 0.0), qeb)
                dk_in = dk_in + dk_raw * es[i]
                db = db - kes[i].astype(F32) * dk_raw
            dqs[sl, :] = dqb * jnp.exp(b_c) + dq_raw * eq
            dks[sl, :] = dkb * jnp.exp(bl - b_c) + dk_in
            dbs[sl, :] = db
            extra = jnp.sum(dkbk, axis=0, keepdims=True) + jnp.sum(dstn * st, axis=0, keepdims=True) * ebl
            exs[sl, :] = jnp.broadcast_to(extra, (cc, HEAD_DIM))
            dp_ref[2, sl, :] = dv.astype(BF16)
            dst_s[...] = dstn * ebl + _dot(dob, qb_, TN)

        dg = _chunk_cumsum(dbs[...], reverse=True) + exs[...]
        df = dg / f - dks[...]
        dp_ref[1] = (df * (1.0 - lb) * sg * (1.0 - sg)).astype(BF16)
        dp_ref[0] = (dqs[...] * (sq * (1.0 + qr * (1.0 - sq)))).astype(BF16)
        dlb = jnp.sum(df * (1.0 - sg), axis=0, keepdims=True) * (lb * (1.0 - lb))
        dlog_ref[0:1, :] += -dlb
        dlog_ref[1:2, :] += dlb

    hh = h_count
    blk = lambda off: pl.BlockSpec((t, HEAD_DIM), lambda h, i: (nb - 1 - i, off + h))
    two = pl.BlockSpec((2, HEAD_DIM), lambda h, i: (0, h))
    one = pl.BlockSpec((1, HEAD_DIM), lambda h, i: (0, h))
    tile = pltpu.VMEM((t, HEAD_DIM), F32)
    return _call(
        body, name=name, grid=(hh, nb),
        in_specs=[blk(0), blk(hh), blk(2 * hh), blk(3 * hh), two, one, blk(0), blk(0),
                  pl.BlockSpec((1, nc, HEAD_DIM, HEAD_DIM), lambda h, i: (h, nb - 1 - i, 0, 0))],
        out_specs=[pl.BlockSpec((4, t, HEAD_DIM), lambda h, i: (0, nb - 1 - i, h)), two, one],
        out_shape=[jax.ShapeDtypeStruct((4, s, w), BF16), jax.ShapeDtypeStruct((2, w), F32),
                   jax.ShapeDtypeStruct((1, w), F32)],
        scratch_shapes=[pltpu.VMEM((HEAD_DIM, HEAD_DIM), F32), tile, tile, tile, tile, tile, tile],
        compiler_params=_cp(("parallel", "arbitrary")),
    )(p1, p1, p1, p1, lbl, onorm, o, dy, states)


def _adamw(name, w, g, m, v):
    r, c = w.shape
    tr = r if r <= 128 else _row_block(r, 128)
    c1 = 1.0 - ADAM_B1 ** ADAM_STEP
    c2 = 1.0 - ADAM_B2 ** ADAM_STEP

    def body(w_ref, g_ref, m_ref, v_ref, d_ref, nm_ref, nv_ref):
        gv = g_ref[...]
        nm = ADAM_B1 * m_ref[...] + (1.0 - ADAM_B1) * gv
        nv = ADAM_B2 * v_ref[...] + (1.0 - ADAM_B2) * (gv * gv)
        nm_ref[...] = nm
        nv_ref[...] = nv
        d_ref[...] = -ADAM_LR * ((nm / c1) / (jnp.sqrt(nv / c2) + ADAM_EPS) + ADAM_WD * w_ref[...])

    blk = pl.BlockSpec((tr, c), lambda i: (i, 0))
    sh = jax.ShapeDtypeStruct((r, c), F32)
    return _call(body, name=name, grid=(r // tr,), in_specs=[blk] * 4, out_specs=[blk] * 3,
                 out_shape=[sh, sh, sh], compiler_params=_cp(("parallel",)))(w, g, m, v)


SLAB_BLOCK_BYTES = 2 * 1024 * 1024


def _slab_rows(r, c):
    return _row_block(r, max(16, SLAB_BLOCK_BYTES // (4 * c) // 16 * 16))


def _pair_add(name, g2, recv, core):
    _, nch, r, c = g2.shape
    tr = _slab_rows(r, c)

    grid_spec = pltpu.PrefetchScalarGridSpec(
        num_scalar_prefetch=1, grid=(nch, r // tr),
        in_specs=[pl.BlockSpec((1, 1, tr, c), lambda j, i, cr: (cr[0], j, i, 0)),
                  pl.BlockSpec((1, tr, c), lambda j, i, cr: (j, i, 0))],
        out_specs=pl.BlockSpec((1, tr, c), lambda j, i, cr: (j, i, 0)))

    def body(core_ref, a_ref, b_ref, o_ref):
        o_ref[...] = (a_ref[0] + b_ref[...]).astype(BF16)

    return _call(body, name=name, grid_spec=grid_spec, out_shape=jax.ShapeDtypeStruct((nch, r, c), BF16),
                 compiler_params=_cp(("parallel", "parallel")))(core, g2, recv)


def _sum_slots(name, x, core=None):
    n, r, c = x.shape
    tr = _slab_rows(r, c)

    def total(x_ref):
        acc = x_ref[0].astype(F32)
        for j in range(1, n):
            acc = acc + x_ref[j].astype(F32)
        return acc

    if core is None:
        def body(x_ref, o_ref):
            o_ref[...] = total(x_ref)

        return _call(body, name=name, grid=(r // tr,),
                     in_specs=[pl.BlockSpec((n, tr, c), lambda i: (0, i, 0))],
                     out_specs=pl.BlockSpec((tr, c), lambda i: (i, 0)),
                     out_shape=jax.ShapeDtypeStruct((r, c), F32), compiler_params=_cp(("parallel",)))(x)

    def body_half(core_ref, x_ref, o_ref):
        o_ref[0] = total(x_ref)

    grid_spec = pltpu.PrefetchScalarGridSpec(
        num_scalar_prefetch=1, grid=(r // tr,),
        in_specs=[pl.BlockSpec((n, tr, c), lambda i, cr: (0, i, 0))],
        out_specs=pl.BlockSpec((1, tr, c), lambda i, cr: (cr[0], i, 0)))
    return _call(body_half, name=name, grid_spec=grid_spec, out_shape=jax.ShapeDtypeStruct((2, r, c), F32),
                 compiler_params=_cp(("parallel",)))(core, x)


def _pos():
    return lax.axis_index("x"), lax.axis_index("y"), lax.axis_index("c")


def _flip(v, f):
    return (1 - v) if f else v


CHIP_FLIPS = ((0, 1), (1, 0), (1, 1))
DEV_FLIPS = tuple((fx, fy, fc) for fx in (0, 1) for fy in (0, 1) for fc in (0, 1))[1:]


def _remote(src, dst, ssem, rsem, dev):
    return pltpu.make_async_remote_copy(src_ref=src, dst_ref=dst, send_sem=ssem, recv_sem=rsem,
                                        device_id=dev, device_id_type=MESH)


def _ag_weights(name, halves):
    n = len(halves)

    def body(*refs):
        h_refs, big_refs, sems = refs[:n], refs[n:2 * n], refs[2 * n:]
        _ag_phase(0, h_refs, big_refs, sems)
        _ag_phase(1, h_refs, big_refs, sems)
        _ag_phase(2, h_refs, big_refs, sems)

    return _call(body, name=name, in_specs=[ANY] * n, out_specs=[ANY] * n,
                 out_shape=_ag_out_shapes(halves), scratch_shapes=_ag_sems(n))(*halves)


def _ag_out_shapes(halves):
    return [jax.ShapeDtypeStruct((2, 4) + h.shape, h.dtype) for h in halves]


def _ag_sems(n):
    return [pltpu.SemaphoreType.DMA((n,)), pltpu.SemaphoreType.DMA((7 * n,)), pltpu.SemaphoreType.DMA((7 * n,))]


def _ag_phase(phase, h_refs, big_refs, sems):
    lsem, ssem, rsem = sems
    n = len(h_refs)
    x, y, cc = _pos()
    me = 2 * x + y
    sib = (x, y, 1 - cc)
    peers = [(_flip(x, fx), _flip(y, fy), cc) for fx, fy in CHIP_FLIPS]
    chips = [2 * px + py for px, py, _ in peers]
    for a in range(n):
        mine = big_refs[a].at[cc, me]
        first = [pltpu.make_async_copy(h_refs[a], mine, lsem.at[a])]
        first += [_remote(h_refs[a], mine, ssem.at[7 * a + k], rsem.at[7 * a + k], peers[k]) for k in range(3)]
        first += [_remote(h_refs[a], mine, ssem.at[7 * a + 3], rsem.at[7 * a + 3], sib)]
        passed = []
        for k in range(3):
            blk = big_refs[a].at[cc, chips[k]]
            passed.append(_remote(blk, blk, ssem.at[7 * a + 4 + k], rsem.at[7 * a + 4 + k], sib))
        if phase == 0:
            for cp in first:
                cp.start()
        elif phase == 1:
            for k in range(3):
                blk = big_refs[a].at[cc, chips[k]]
                _remote(blk, blk, ssem.at[7 * a + k], rsem.at[7 * a + k], peers[k]).wait_recv()
                passed[k].start()
        else:
            theirs = big_refs[a].at[1 - cc, me]
            _remote(theirs, theirs, ssem.at[7 * a + 3], rsem.at[7 * a + 3], sib).wait_recv()
            for k in range(3):
                blk = big_refs[a].at[1 - cc, chips[k]]
                _remote(blk, blk, ssem.at[7 * a + 4 + k], rsem.at[7 * a + 4 + k], sib).wait_recv()
            first[0].wait()
            for cp in first[1:] + passed:
                cp.wait_send()


def _ag_devices(name, v):
    r, c = v.shape

    def body(v_ref, out_ref, lsem, ssem, rsem):
        x, y, cc = _pos()
        me = 4 * x + 2 * y + cc
        loc = pltpu.make_async_copy(v_ref, out_ref.at[me], lsem)
        loc.start()
        started = []
        for k, (fx, fy, fc) in enumerate(DEV_FLIPS):
            cp = _remote(v_ref, out_ref.at[me], ssem.at[k], rsem.at[k], (_flip(x, fx), _flip(y, fy), _flip(cc, fc)))
            cp.start()
            started.append(cp)
        for k, (fx, fy, fc) in enumerate(DEV_FLIPS):
            px, py, pc = _flip(x, fx), _flip(y, fy), _flip(cc, fc)
            blk = out_ref.at[4 * px + 2 * py + pc]
            _remote(blk, blk, ssem.at[k], rsem.at[k], (px, py, pc)).wait_recv()
        for cp in started:
            cp.wait_send()
        loc.wait()

    return _call(body, name=name, in_specs=[ANY], out_specs=ANY,
                 out_shape=jax.ShapeDtypeStruct((8, r, c), v.dtype),
                 scratch_shapes=[pltpu.SemaphoreType.DMA, pltpu.SemaphoreType.DMA((7,)), pltpu.SemaphoreType.DMA((7,))])(v)


def _sibling_other_half(name, g2s):
    n = len(g2s)

    def body(*refs):
        g_refs, out_refs, sems = refs[:n], refs[n:2 * n], refs[2 * n:]
        _sibling_phase(0, g_refs, out_refs, sems)
        _sibling_phase(1, g_refs, out_refs, sems)

    return _call(body, name=name, in_specs=[ANY] * n, out_specs=[ANY] * n,
                 out_shape=_sibling_out_shapes(g2s), scratch_shapes=_sibling_sems(n))(*g2s)


def _sibling_out_shapes(g2s):
    return [jax.ShapeDtypeStruct(g.shape[1:], g.dtype) for g in g2s]


def _sibling_sems(n):
    return [pltpu.SemaphoreType.DMA((n,)), pltpu.SemaphoreType.DMA((n,))]


def _sibling_phase(phase, g_refs, out_refs, sems):
    ssem, rsem = sems
    x, y, cc = _pos()
    for a in range(len(g_refs)):
        cp = _remote(g_refs[a].at[1 - cc], out_refs[a], ssem.at[a], rsem.at[a], (x, y, 1 - cc))
        if phase == 0:
            cp.start()
        else:
            cp.wait()


def _scatter_sems(n):
    return [pltpu.SemaphoreType.DMA((n,)), pltpu.SemaphoreType.DMA((3 * n,)), pltpu.SemaphoreType.DMA((3 * n,))]


def _scatter_phase(phase, p_refs, out_refs, sems):
    lsem, ssem, rsem = sems
    x, y, cc = _pos()
    me = 2 * x + y
    peers = [(_flip(x, fx), _flip(y, fy), cc) for fx, fy in CHIP_FLIPS]
    for a in range(len(p_refs)):
        loc = pltpu.make_async_copy(p_refs[a].at[me], out_refs[a].at[me], lsem.at[a])
        sends = [_remote(p_refs[a].at[2 * px + py], out_refs[a].at[me], ssem.at[3 * a + k], rsem.at[3 * a + k], peers[k])
                 for k, (px, py, _) in enumerate(peers)]
        if phase == 0:
            loc.start()
            for cp in sends:
                cp.start()
        else:
            for k, (px, py, _) in enumerate(peers):
                blk = out_refs[a].at[2 * px + py]
                _remote(blk, blk, ssem.at[3 * a + k], rsem.at[3 * a + k], peers[k]).wait_recv()
            loc.wait()
            for cp in sends:
                cp.wait_send()


def _sibling_join(name, bufs):
    n = len(bufs)

    def body(*refs):
        out_refs, (ssem, rsem) = refs[n:2 * n], refs[2 * n:]
        x, y, cc = _pos()
        sib = (x, y, 1 - cc)
        cps = [_remote(out_refs[a].at[cc], out_refs[a].at[cc], ssem.at[a], rsem.at[a], sib) for a in range(n)]
        for cp in cps:
            cp.start()
        for a in range(n):
            theirs = out_refs[a].at[1 - cc]
            _remote(theirs, theirs, ssem.at[a], rsem.at[a], sib).wait_recv()
        for cp in cps:
            cp.wait_send()

    return _call(body, name=name, in_specs=[ANY] * n, out_specs=[ANY] * n,
                 out_shape=[jax.ShapeDtypeStruct(b.shape, b.dtype) for b in bufs],
                 input_output_aliases={a: a for a in range(n)},
                 scratch_shapes=[pltpu.SemaphoreType.DMA((n,)), pltpu.SemaphoreType.DMA((n,))])(*bufs)


def _pad_lanes(a, width=128):
    return jnp.pad(a, ((0, 0), (0, width - a.shape[1])))


def kernel(x, norm_gains, fox_w_in, fox_b_f, hgrn_w_in, hgrn_lb_logits, hgrn_onorm, w_out, final_gain, loss_target, m_norm_gains, m_fox_w_in, m_fox_b_f, m_hgrn_w_in, m_hgrn_lb_logits, m_hgrn_onorm, m_w_out, m_final_gain, v_norm_gains, v_fox_w_in, v_fox_b_f, v_hgrn_w_in, v_hgrn_lb_logits, v_hgrn_onorm, v_w_out, v_final_gain):
    s, d = x.shape[1], x.shape[2]
    wq = w_out.shape[1]
    w = 4 * wq
    hh = w // HEAD_DIM
    fox_cols = fox_w_in.shape[2]
    assert 4 * fox_cols == 4 * w + hh and hgrn_w_in.shape[2] == w
    core = lax.axis_index("c")

    x0 = x[0]
    tgt = loss_target[0]

    def my_half(a):
        return lax.dynamic_index_in_dim(a, core, 0, keepdims=False).astype(BF16)

    (big_fox,) = _ag_weights("ag_weights", [my_half(fox_w_in.reshape(2, d // 2, fox_cols))])
    later_halves = [my_half(hgrn_w_in.reshape(2, d // 2, w)), my_half(w_out),
                    lax.dynamic_index_in_dim(hgrn_onorm.reshape(2, 1, wq // 2), core, 0, keepdims=False)]
    fox_chip = [big_fox[:, j].reshape(d, fox_cols) for j in range(4)]
    def fox_columns(a, b):
        out = []
        for j in range(4):
            lo, hi = max(a, j * fox_cols), min(b, (j + 1) * fox_cols)
            if lo < hi:
                out.append(fox_chip[j][:, lo - j * fox_cols:hi - j * fox_cols])
        return out

    w_main = jnp.concatenate(fox_columns(0, 3 * w) + fox_columns(3 * w + hh, 4 * w + hh), axis=1)
    w_fl = _pad_lanes(jnp.concatenate(fox_columns(3 * w, 3 * w + hh), axis=1))

    bf_pad = _pad_lanes(fox_b_f)
    g0, g1 = norm_gains[0:1], norm_gains[1:2]
    gf = final_gain.reshape(1, d)

    h0 = _rms_fwd("rms0_fwd", x0, g0)
    qkv = _mm("fox_qkv", h0, w_main, out_dtype=BF16, n=3 * w, col_scale=(w, Q_FOLD))
    gate0 = _mm("fox_gate", h0, w_main, n=w, b_off=3 * w)
    fl = _mm("fox_flogit", h0, w_fl)
    _, c_rep = _fox_gate_fwd("fox_cumsum", fl, bf_pad, w, rep_scale=LOG2E)
    o0, lse_row, y0, big_hgrn, big_out, big_onorm = _attn_fwd("fox_attn_fwd", qkv, c_rep, gate0, hh, gather=later_halves)
    onorm_full = big_onorm.transpose(1, 0, 2, 3).reshape(1, w)
    w_o = big_out.reshape(2, w, d)
    x1 = _mm("fox_out", y0, w_o[0], res=x0)
    h1 = _rms_fwd("rms1_fwd", x1, g1)
    p1 = _mm("hgrn_in", h1, big_hgrn, b_gathered=True)
    y1, o1, states = _hgrn_fwd("hgrn_fwd", p1, hgrn_lb_logits, onorm_full, hh)
    x2 = _mm("hgrn_out", y1, w_o[1], res=x1)
    dx2, d_gf, loss_tile = _loss_head("loss_head", x2, tgt, gf)

    dy1 = _mm("hgrn_out_dy", dx2, w_o[1], tb=True)
    d_wo1 = _mm("hgrn_out_dw", y1, dx2, ta=True)
    dp1, d_lbl, d_onorm = _hgrn_bwd("hgrn_bwd", p1, hgrn_lb_logits, onorm_full, o1, dy1, states, hh)
    dh1 = _mm("hgrn_in_dh", dp1, big_hgrn, tb=True, b_gathered=True, a_stacked=True)
    g2_hgrn = _mm("hgrn_in_dw", h1, dp1, ta=True, out_gathered=(d // 2, w), b_stacked=True)
    dx1, d_g1 = _rms_bwd("rms1_bwd", x1, g1, dh1, dx2)
    dy0 = _mm("fox_out_dy", dx1, w_o[0], tb=True)
    d_wo0 = _mm("fox_out_dw", y0, dx1, ta=True)
    core_arr = core.reshape(1).astype(jnp.int32)
    g2_out = jnp.stack([d_wo0, d_wo1]).reshape(2, 4, wq, d)
    do0, dgate0, delta, sib_hgrn, sib_out = _fox_pre_bwd("fox_pre_bwd", dy0, o0, gate0, exchange=[g2_hgrn, g2_out])
    delta_row = delta[:, :hh].T.reshape(hh, 1, s)
    pairs_early = [_pair_add("rs_pair_add_hgrn", g2_hgrn, sib_hgrn, core_arr),
                   _pair_add("rs_pair_add_out", g2_out, sib_out, core_arr)]
    dp0, dc_rep, rowsum_row, got_hgrn, got_out = _attn_bwd(
        "fox_attn_bwd", qkv, do0, dgate0, lse_row, delta_row, c_rep, hh, scatter=pairs_early)
    dfl, d_bf = _fox_gate_bwd("fox_cumsum_bwd", _pad_lanes(rowsum_row.reshape(hh, s).T),
                              _pad_lanes(dc_rep[:, ::HEAD_DIM]), fl, bf_pad)
    d_wmain = _mm("fox_in_dw", h0, dp0, ta=True, b_stacked=True)
    d_wfl = _mm("fox_fl_dw", h0, dfl, ta=True)

    def grad_columns(a, b):
        out = []
        for lo, hi, src, shift in ((0, 3 * w, d_wmain, 0), (3 * w, 3 * w + hh, d_wfl, 3 * w),
                                   (3 * w + hh, 4 * w + hh, d_wmain, hh)):
            l2, h2 = max(a, lo), min(b, hi)
            if l2 < h2:
                out.append(src[:, l2 - shift:h2 - shift])
        return out

    g2_fox = jnp.stack([jnp.concatenate(grad_columns(j * fox_cols, (j + 1) * fox_cols), axis=1).reshape(2, d // 2, fox_cols)
                        for j in range(4)], axis=1)
    (sib_fox,) = _sibling_other_half("rs_sibling", [g2_fox])
    dh0, got_fox = _mm("fox_in_dh", dp0, w_main, tb=True, res=_mm("fox_fl_dh", dfl, w_fl, tb=True), a_stacked=True,
                       scatter=[_pair_add("rs_pair_add_fox", g2_fox, sib_fox, core_arr)])
    grad_x, d_g0 = _rms_bwd("rms0_bwd", x0, g0, dh0, dx1)
    halves_sum = [_sum_slots("rs_sum_" + nm, got, core_arr)
                  for nm, got in zip(("fox", "hgrn", "out"), (got_fox, got_hgrn, got_out))]
    r_fox, r_hgrn, r_out = _sibling_join("rs_join", halves_sum)
    g_fox = r_fox.reshape(d, fox_cols)
    g_hgrn = r_hgrn.reshape(d, w)
    g_out = r_out.reshape(2 * wq, d)

    small = jnp.concatenate([jnp.concatenate([d_g0, d_g1], axis=0).reshape(-1), d_lbl.reshape(-1), d_gf.reshape(-1),
                             d_onorm.reshape(-1), d_bf.reshape(-1)])
    n_small = small.shape[0]
    pad_to = -(-n_small // 1024) * 1024
    small = jnp.pad(small, (0, pad_to - n_small)).reshape(pad_to // 128, 128)
    small = _sum_slots("small_sum", _ag_devices("small_gather", small)).reshape(-1)
    g_norm = small[:2 * d].reshape(2, d)
    g_lbl = small[2 * d:2 * d + 2 * w].reshape(2, w)
    g_gf = small[2 * d + 2 * w:3 * d + 2 * w]
    g_onorm_full = small[3 * d + 2 * w:3 * d + 3 * w]
    g_bf = small[3 * d + 3 * w:3 * d + 3 * w + hh].reshape(1, hh)
    chip = 2 * lax.axis_index("x") + lax.axis_index("y")
    g_onorm = lax.dynamic_slice_in_dim(g_onorm_full, chip * wq, wq).reshape(1, wq)

    loss = lax.psum(loss_tile[0, 0], ("x", "y", "c"))

    def upd(name, wt, g, m, v):
        shp = wt.shape
        two = lambda a: a.reshape(-1, shp[-1])
        dl, nm, nv = _adamw(name, two(wt), two(g), two(m), two(v))
        return g.reshape(shp), dl.reshape(shp), nm.reshape(shp), nv.reshape(shp)

    res = [
        upd("adamw_norm_gains", norm_gains, g_norm, m_norm_gains, v_norm_gains),
        upd("adamw_fox_w_in", fox_w_in, g_fox, m_fox_w_in, v_fox_w_in),
        upd("adamw_fox_b_f", fox_b_f, g_bf, m_fox_b_f, v_fox_b_f),
        upd("adamw_hgrn_w_in", hgrn_w_in, g_hgrn, m_hgrn_w_in, v_hgrn_w_in),
        upd("adamw_lb_logits", hgrn_lb_logits, g_lbl, m_hgrn_lb_logits, v_hgrn_lb_logits),
        upd("adamw_onorm", hgrn_onorm, g_onorm, m_hgrn_onorm, v_hgrn_onorm),
        upd("adamw_w_out", w_out, g_out, m_w_out, v_w_out),
        upd("adamw_final_gain", final_gain.reshape(1, d), g_gf.reshape(1, d), m_final_gain.reshape(1, d),
            v_final_gain.reshape(1, d)),
    ]
    res[-1] = tuple(a.reshape(d) for a in res[-1])
    grads, deltas, new_m, new_v = zip(*res)
    return (loss, grad_x[None], *grads, *deltas, *new_m, *new_v)
```

```python
import functools

import jax
import jax.numpy as jnp
from jax import lax
from jax.experimental import pallas as pl
from jax.experimental.pallas import tpu as pltpu

F32 = jnp.float32
BF16 = jnp.bfloat16
MESH = pl.DeviceIdType.MESH
ANY = pl.BlockSpec(memory_space=pl.ANY)

EPS = 1e-6
HEAD_DIM = 128
HGRN_CHUNK = 64
HGRN_SUB = 32
EXP_CLAMP = 80.0
ATT_BLOCK = 1024
ATT_STRIP = 512
HGRN_BLOCK = 2048
GATE_BLOCK = 512
ROW_BLOCK = 512
MM_TM, MM_TN, MM_TK = 1024, 1024, 2048
VMEM_LIMIT_V7X = 56 * 1024 * 1024

ADAM_LR, ADAM_B1, ADAM_B2, ADAM_EPS, ADAM_WD, ADAM_STEP = 0.001, 0.9, 0.999, 1e-08, 0.01, 10

NT = (((1,), (1,)), ((), ()))
TN = (((0,), (0,)), ((), ()))
NN = (((1,), (0,)), ((), ()))


def _call(body, **kw):
    return pl.pallas_call(body, **kw)


def _cp(dims=None):
    kw = dict(vmem_limit_bytes=VMEM_LIMIT_V7X)
    if dims is not None:
        kw["dimension_semantics"] = dims
    return pltpu.CompilerParams(**kw)


def _sigmoid(x):
    return 1.0 / (1.0 + jnp.exp(-x))


def _dot(a, b, dn=NN):
    return lax.dot_general(a.astype(BF16), b.astype(BF16), dn, preferred_element_type=F32)


def _split3(x):
    hi = x.astype(BF16)
    r1 = x - hi.astype(F32)
    mid = r1.astype(BF16)
    lo = (r1 - mid.astype(F32)).astype(BF16)
    return hi, mid, lo


def _dot_exact(m01, x, right=False):
    hi, mid, lo = _split3(x)
    if right:
        dot = lambda p: lax.dot_general(p, m01, NN, preferred_element_type=F32)
    else:
        dot = lambda p: lax.dot_general(m01, p, NN, preferred_element_type=F32)
    return dot(hi) + dot(mid) + dot(lo)


def _row_block(rows, cap):
    if rows <= cap:
        return rows
    best = None
    for t in range(16, cap + 1, 16):
        if rows % t == 0:
            best = t
    assert best is not None, rows
    return best


def _mm(name, a, b, *, ta=False, tb=False, out_dtype=F32, res=None, n=None, b_off=0, b_gathered=False,
        out_gathered=None, scatter=(), a_stacked=False, b_stacked=False, col_scale=None):
    if a_stacked:
        assert not ta
        m, k = a.shape[1], a.shape[0] * a.shape[2]
    else:
        m, k = (a.shape[1], a.shape[0]) if ta else a.shape
    if b_gathered:
        rh, cw = b.shape[2], b.shape[3]
        n_full = 2 * rh if tb else 4 * cw
    elif b_stacked:
        assert not tb
        n_full = b.shape[0] * b.shape[2]
    else:
        n_full = b.shape[0] if tb else b.shape[1]
    n = n_full if n is None else n
    tm, tn, tk = min(MM_TM, m), min(MM_TN, n), min(MM_TK, k)
    b_both = b_gathered and not tb and tk == 2 * rh
    if b_gathered and not b_both:
        tn, tk = (min(tn, rh), min(tk, cw)) if tb else (min(tn, cw), min(tk, rh))
    if b_both:
        tn = min(tn, cw)
    if out_gathered:
        tm, tn = min(tm, out_gathered[0]), min(tn, out_gathered[1])
    if a_stacked:
        tk = min(tk, a.shape[2])
    if b_stacked:
        tn = min(tn, b.shape[2])
    assert m % tm == 0 and n % tn == 0 and k % tk == 0 and b_off % tn == 0
    nk = k // tk
    jo = b_off // tn
    if a_stacked:
        pa = a.shape[2] // tk
        a_spec = pl.BlockSpec((1, tm, tk), lambda i, j, kk: (kk // pa, i, kk % pa))
    elif ta:
        a_spec = pl.BlockSpec((tk, tm), lambda i, j, kk: (kk, i))
    else:
        a_spec = pl.BlockSpec((tm, tk), lambda i, j, kk: (i, kk))
    if b_gathered and tb:
        pr, pc = rh // tn, cw // tk
        b_spec = pl.BlockSpec((1, 1, tn, tk), lambda i, j, kk: (j // pr, kk // pc, j % pr, kk % pc))
    elif b_both:
        pc = cw // tn
        b_spec = pl.BlockSpec((2, 1, rh, tn), lambda i, j, kk: (0, j // pc, 0, j % pc))
    elif b_gathered:
        pr, pc = rh // tk, cw // tn
        b_spec = pl.BlockSpec((1, 1, tk, tn), lambda i, j, kk: (kk // pr, j // pc, kk % pr, j % pc))
    elif b_stacked:
        pb = b.shape[2] // tn
        b_spec = pl.BlockSpec((1, tk, tn), lambda i, j, kk: (j // pb, kk, j % pb))
    elif tb:
        b_spec = pl.BlockSpec((tn, tk), lambda i, j, kk: (j + jo, kk))
    else:
        b_spec = pl.BlockSpec((tk, tn), lambda i, j, kk: (kk, j + jo))
    o_spec = pl.BlockSpec((tm, tn), lambda i, j, kk: (i, j))
    if out_gathered:
        assert m == 2 * out_gathered[0] and n == 4 * out_gathered[1] and res is None
        qr, qc = out_gathered[0] // tm, out_gathered[1] // tn
        o_spec = pl.BlockSpec((1, 1, tm, tn), lambda i, j, kk: (i // qr, j // qc, i % qr, j % qc))
    dn = (((0 if ta else 1,), (1 if tb else 0,)), ((), ()))
    has_res = res is not None
    ns = len(scatter)
    grid = (m // tm, n // tn, nk)

    def body(*refs):
        a_ref, b_ref = refs[:2]
        r_ref = refs[2] if has_res else None
        base = 2 + int(has_res)
        p_refs = refs[base:base + ns]
        o_ref = refs[base + ns]
        got_refs = refs[base + ns + 1:base + 2 * ns + 1]
        acc = refs[base + 2 * ns + 1]
        sems = refs[base + 2 * ns + 2:]
        kk = pl.program_id(2)
        if ns:
            first = jnp.logical_and(jnp.logical_and(pl.program_id(0) == 0, pl.program_id(1) == 0), kk == 0)
            last = jnp.logical_and(jnp.logical_and(pl.program_id(0) == grid[0] - 1, pl.program_id(1) == grid[1] - 1),
                                   kk == nk - 1)

            @pl.when(first)
            def _():
                _scatter_phase(0, p_refs, got_refs, sems)

        av = a_ref[0] if a_stacked else a_ref[...]
        if b_both:
            p = (lax.dot_general(av[:, :rh].astype(BF16), b_ref[0, 0].astype(BF16), dn, preferred_element_type=F32)
                 + lax.dot_general(av[:, rh:].astype(BF16), b_ref[1, 0].astype(BF16), dn, preferred_element_type=F32))
        else:
            bv = b_ref[0, 0] if b_gathered else (b_ref[0] if b_stacked else b_ref[...])
            p = lax.dot_general(av.astype(BF16), bv.astype(BF16), dn, preferred_element_type=F32)

        def finish(total):
            if r_ref is not None:
                total = total + r_ref[...]
            if col_scale is not None:
                total = total * jnp.where(pl.program_id(1) < col_scale[0] // tn, col_scale[1], 1.0)
            if out_gathered:
                o_ref[0, 0] = total.astype(out_dtype)
            else:
                o_ref[...] = total.astype(out_dtype)

        if nk == 1:
            finish(p)
        else:
            @pl.when(kk == 0)
            def _():
                acc[...] = p

            @pl.when(jnp.logical_and(kk > 0, kk < nk - 1))
            def _():
                acc[...] += p

            @pl.when(kk == nk - 1)
            def _():
                finish(acc[...] + p)

        if ns:
            @pl.when(last)
            def _():
                _scatter_phase(1, p_refs, got_refs, sems)

    ins = [a, b] + ([res] if has_res else []) + list(scatter)
    in_specs = [a_spec, b_spec] + ([o_spec] if has_res else []) + [ANY] * ns
    o_shape = jax.ShapeDtypeStruct((2, 4) + tuple(out_gathered) if out_gathered else (m, n), out_dtype)
    out = _call(
        body, name=name, grid=grid, in_specs=in_specs, out_specs=[o_spec] + [ANY] * ns,
        out_shape=[o_shape] + [jax.ShapeDtypeStruct(p.shape, p.dtype) for p in scatter],
        scratch_shapes=[pltpu.VMEM((tm, tn) if nk > 1 else (8, 128), F32)] + (_scatter_sems(ns) if ns else []),
        compiler_params=_cp(("arbitrary",) * 3 if ns else ("parallel", "parallel", "arbitrary")),
    )(*ins)
    return out if ns else out[0]


def _rms_fwd(name, x, g):
    s, d = x.shape
    tm = min(ROW_BLOCK, s)

    def body(x_ref, g_ref, h_ref):
        xv = x_ref[...]
        r = lax.rsqrt(jnp.mean(xv * xv, axis=-1, keepdims=True) + EPS)
        h_ref[...] = (xv * r * g_ref[...]).astype(BF16)

    row = pl.BlockSpec((tm, d), lambda i: (i, 0))
    vec = pl.BlockSpec((1, d), lambda i: (0, 0))
    return _call(body, name=name, grid=(s // tm,), in_specs=[row, vec], out_specs=row,
                 out_shape=jax.ShapeDtypeStruct((s, d), BF16), compiler_params=_cp(("parallel",)))(x, g)


def _rms_bwd(name, x, g, dh, dres):
    s, d = x.shape
    tm = min(ROW_BLOCK, s)

    def body(x_ref, g_ref, dh_ref, dres_ref, dx_ref, dg_ref):
        @pl.when(pl.program_id(0) == 0)
        def _():
            dg_ref[...] = jnp.zeros_like(dg_ref)

        xv = x_ref[...]
        r = lax.rsqrt(jnp.mean(xv * xv, axis=-1, keepdims=True) + EPS)
        xn = xv * r
        dhv = dh_ref[...]
        dxn = dhv * g_ref[...]
        dx_ref[...] = dres_ref[...] + r * (dxn - xn * jnp.mean(dxn * xn, axis=-1, keepdims=True))
        dg_ref[...] += jnp.sum(dhv * xn, axis=0, keepdims=True)

    row = pl.BlockSpec((tm, d), lambda i: (i, 0))
    vec = pl.BlockSpec((1, d), lambda i: (0, 0))
    return _call(body, name=name, grid=(s // tm,), in_specs=[row, vec, row, row], out_specs=[row, vec],
                 out_shape=[jax.ShapeDtypeStruct((s, d), F32), jax.ShapeDtypeStruct((1, d), F32)],
                 compiler_params=_cp(("arbitrary",)))(x, g, dh, dres)


def _loss_head(name, x, tgt, g):
    s, d = x.shape
    tm = min(ROW_BLOCK, s)
    nb = s // tm

    def body(x_ref, t_ref, g_ref, dx_ref, dg_ref, loss_ref, lacc):
        i = pl.program_id(0)

        @pl.when(i == 0)
        def _():
            dg_ref[...] = jnp.zeros_like(dg_ref)
            lacc[...] = jnp.zeros_like(lacc)

        xv = x_ref[...]
        gv = g_ref[...]
        r = lax.rsqrt(jnp.mean(xv * xv, axis=-1, keepdims=True) + EPS)
        xn = xv * r
        err = xn * gv - t_ref[...]
        lacc[...] += jnp.sum(err * err, axis=0, keepdims=True)
        dout = err * (1.0 / d)
        dg_ref[...] += jnp.sum(dout * xn, axis=0, keepdims=True)
        dxn = dout * gv
        dx_ref[...] = r * (dxn - xn * jnp.mean(dxn * xn, axis=-1, keepdims=True))

        @pl.when(i == nb - 1)
        def _():
            total = jnp.sum(lacc[...], axis=1, keepdims=True) * (0.5 / d)
            loss_ref[...] = jnp.broadcast_to(total, loss_ref.shape)

    row = pl.BlockSpec((tm, d), lambda i: (i, 0))
    vec = pl.BlockSpec((1, d), lambda i: (0, 0))
    one = pl.BlockSpec((1, 128), lambda i: (0, 0))
    return _call(body, name=name, grid=(nb,), in_specs=[row, row, vec], out_specs=[row, vec, one],
                 out_shape=[jax.ShapeDtypeStruct((s, d), F32), jax.ShapeDtypeStruct((1, d), F32),
                            jax.ShapeDtypeStruct((1, 128), F32)],
                 scratch_shapes=[pltpu.VMEM((1, d), F32)], compiler_params=_cp(("arbitrary",)))(x, tgt, g)


def _fox_gate_fwd(name, fl, bf, w, rep_scale=1.0):
    s = fl.shape[0]
    tb = min(GATE_BLOCK, s)

    def body(fl_ref, bf_ref, c_ref, crep_ref, carry):
        @pl.when(pl.program_id(0) == 0)
        def _():
            carry[...] = jnp.zeros_like(carry)

        z = fl_ref[...] + bf_ref[...]
        lf = jnp.minimum(z, 0.0) - jnp.log(1.0 + jnp.exp(-jnp.abs(z)))
        rows = lax.broadcasted_iota(jnp.int32, (tb, tb), 0)
        cols = lax.broadcasted_iota(jnp.int32, (tb, tb), 1)
        tri = (rows >= cols).astype(BF16)
        cs = _dot_exact(tri, lf) + carry[...]
        c_ref[...] = cs
        carry[...] = c_ref[tb - 1:tb, :]
        sel_r = lax.broadcasted_iota(jnp.int32, (128, w), 0)
        sel_c = lax.broadcasted_iota(jnp.int32, (128, w), 1)
        sel = (sel_r == sel_c // HEAD_DIM).astype(BF16)
        crep_ref[...] = _dot_exact(sel, cs * rep_scale, right=True)

    blk = pl.BlockSpec((tb, 128), lambda i: (i, 0))
    return _call(body, name=name, grid=(s // tb,),
                 in_specs=[blk, pl.BlockSpec((1, 128), lambda i: (0, 0))],
                 out_specs=[blk, pl.BlockSpec((tb, w), lambda i: (i, 0))],
                 out_shape=[jax.ShapeDtypeStruct((s, 128), F32), jax.ShapeDtypeStruct((s, w), F32)],
                 scratch_shapes=[pltpu.VMEM((1, 128), F32)], compiler_params=_cp(("arbitrary",)))(fl, bf)


def _fox_gate_bwd(name, drow, dcol, fl, bf):
    s = fl.shape[0]
    tb = min(GATE_BLOCK, s)
    nb = s // tb

    def body(dr_ref, dc_ref, fl_ref, bf_ref, dfl_ref, dbf_ref, carry, tmp):
        @pl.when(pl.program_id(0) == 0)
        def _():
            carry[...] = jnp.zeros_like(carry)
            dbf_ref[...] = jnp.zeros_like(dbf_ref)

        rows = lax.broadcasted_iota(jnp.int32, (tb, tb), 0)
        cols = lax.broadcasted_iota(jnp.int32, (tb, tb), 1)
        triu = (rows <= cols).astype(BF16)
        dlf = _dot_exact(triu, dr_ref[...] + dc_ref[...]) + carry[...]
        tmp[...] = dlf
        carry[...] = tmp[0:1, :]
        z = fl_ref[...] + bf_ref[...]
        dfl = dlf * (1.0 / (1.0 + jnp.exp(z)))
        dfl_ref[...] = dfl
        dbf_ref[...] += jnp.sum(dfl, axis=0, keepdims=True)

    blk = pl.BlockSpec((tb, 128), lambda i: (nb - 1 - i, 0))
    vec = pl.BlockSpec((1, 128), lambda i: (0, 0))
    return _call(body, name=name, grid=(nb,), in_specs=[blk, blk, blk, vec], out_specs=[blk, vec],
                 out_shape=[jax.ShapeDtypeStruct((s, 128), F32), jax.ShapeDtypeStruct((1, 128), F32)],
                 scratch_shapes=[pltpu.VMEM((1, 128), F32), pltpu.VMEM((tb, 128), F32)],
                 compiler_params=_cp(("arbitrary",)))(drow, dcol, fl, bf)


LOG2E = 1.4426950408889634
ATT_SCALE = HEAD_DIM ** -0.5
Q_FOLD = ATT_SCALE * LOG2E


def _attn_fwd(name, qkv, c_rep, gate, h_count, gather=()):
    s = qkv.shape[0]
    w = h_count * HEAD_DIM
    t = min(ATT_BLOCK, s)
    hp = 2 if h_count % 2 == 0 else 1
    wb = hp * HEAD_DIM
    ng = len(gather)
    nh, nq = h_count // hp, s // t

    def body(*refs):
        q_ref, k_ref, v_ref, c_ref, gate_ref = refs[:5]
        h_refs = refs[5:5 + ng]
        o_ref, lse_ref, y_ref = refs[5 + ng:8 + ng]
        big_refs = refs[8 + ng:8 + 2 * ng]
        m_s, l_s, acc_s = refs[8 + 2 * ng:11 + 2 * ng]
        sems = refs[11 + 2 * ng:]
        hs = pl.program_id(0)
        qi = pl.program_id(1)
        if ng:
            @pl.when(jnp.logical_and(hs == 0, qi == 0))
            def _():
                _ag_phase(0, h_refs, big_refs, sems)

            @pl.when(jnp.logical_and(hs == nh // 2, qi == 0))
            def _():
                _ag_phase(1, h_refs, big_refs, sems)

        m_s[...] = jnp.full(m_s.shape, -jnp.inf, F32)
        l_s[...] = jnp.zeros_like(l_s)
        acc_s[...] = jnp.zeros_like(acc_s)

        tw = min(t, ATT_STRIP)

        def step(kb, masked):
            off = pl.multiple_of(kb * t, t)
            for a in range(hp):
                cols_a = slice(a * HEAD_DIM, (a + 1) * HEAD_DIM)
                kk = k_ref[pl.ds(off, t), cols_a]
                vv = v_ref[pl.ds(off, t), cols_a]
                cc = jnp.tile(c_ref[pl.ds(off, t), cols_a], (1, tw // HEAD_DIM))
                for u in range(t // tw):
                    strip = slice(u * tw, (u + 1) * tw)
                    st = lax.dot_general(kk, q_ref[strip, cols_a], NT, preferred_element_type=F32) - cc
                    if masked:
                        rows = lax.broadcasted_iota(jnp.int32, (t, tw), 0)
                        cols = lax.broadcasted_iota(jnp.int32, (t, tw), 1) + u * tw
                        st = jnp.where(cols >= rows, st, -jnp.inf)
                    m_prev = m_s[a, :, strip]
                    m_new = jnp.maximum(m_prev, jnp.max(st, axis=0, keepdims=True))
                    pt = jnp.exp2(st - m_new)
                    alpha = jnp.exp2(m_prev - m_new)
                    l_s[a, :, strip] = alpha * l_s[a, :, strip] + jnp.sum(pt, axis=0, keepdims=True)
                    acc_s[a, :, strip] = alpha * acc_s[a, :, strip] + lax.dot_general(
                        vv, pt.astype(BF16), TN, preferred_element_type=F32)
                    m_s[a, :, strip] = m_new

        def loop_body(pair, carry):
            step(2 * pair, False)
            step(2 * pair + 1, False)
            return carry

        lax.fori_loop(0, qi // 2, loop_body, 0)

        @pl.when(lax.rem(qi, 2) == 1)
        def _():
            step(qi - 1, False)

        step(qi, True)
        for a in range(hp):
            l = l_s[a]
            cols_a = slice(a * HEAD_DIM, (a + 1) * HEAD_DIM)
            oa = (acc_s[a] / l).T
            o_ref[:, cols_a] = oa
            g = gate_ref[:, cols_a]
            y_ref[:, cols_a] = (oa * (g * _sigmoid(g))).astype(BF16)
            lse_ref[a] = m_s[a] + jnp.log2(l)

        if ng:
            @pl.when(jnp.logical_and(hs == nh - 1, qi == nq - 1))
            def _():
                _ag_phase(2, h_refs, big_refs, sems)

    blk = lambda off: pl.BlockSpec((t, wb), lambda h, i: (i, off + h))
    whole = lambda off: pl.BlockSpec((s, wb), lambda h, i: (0, off + h))
    rowv = pl.BlockSpec((hp, 1, t), lambda h, i: (h, 0, i))
    return _call(
        body, name=name, grid=(nh, nq),
        in_specs=[blk(0), whole(nh), whole(2 * nh), whole(0), blk(0)] + [ANY] * ng,
        out_specs=[blk(0), rowv, blk(0)] + [ANY] * ng,
        out_shape=[jax.ShapeDtypeStruct((s, w), F32), jax.ShapeDtypeStruct((h_count, 1, s), F32),
                   jax.ShapeDtypeStruct((s, w), BF16)] + _ag_out_shapes(gather),
        scratch_shapes=[pltpu.VMEM((hp, 1, t), F32), pltpu.VMEM((hp, 1, t), F32), pltpu.VMEM((hp, HEAD_DIM, t), F32)]
        + (_ag_sems(ng) if ng else []),
        compiler_params=_cp(("arbitrary", "arbitrary") if ng else ("parallel", "arbitrary")),
    )(qkv, qkv, qkv, c_rep, gate, *gather)


def _attn_bwd(name, qkv, do, dgate, lse_row, delta_row, c_rep, h_count, scatter=()):
    s = qkv.shape[0]
    w = h_count * HEAD_DIM
    t = min(ATT_BLOCK, s)
    nq = s // t
    scale = HEAD_DIM ** -0.5
    ns = len(scatter)

    def body(*refs):
        k_ref, v_ref, q_ref, do_ref, lse_ref, dl_ref, c_ref, dgate_ref = refs[:8]
        p_refs = refs[8:8 + ns]
        dp_ref, dc_ref, rs_ref = refs[8 + ns:11 + ns]
        got_refs = refs[11 + ns:11 + 2 * ns]
        dk_s, dv_s, dc_s, dq_s, rs_s = refs[11 + 2 * ns:16 + 2 * ns]
        sems = refs[16 + 2 * ns:]
        kj = pl.program_id(1)
        if ns:
            @pl.when(jnp.logical_and(pl.program_id(0) == 0, kj == 0))
            def _():
                _scatter_phase(0, p_refs, got_refs, sems)

        kk = k_ref[...]
        vv = v_ref[...]
        ccol = jnp.tile(c_ref[...], (1, t // HEAD_DIM))
        dk_s[...] = jnp.zeros_like(dk_s)
        dv_s[...] = jnp.zeros_like(dv_s)
        dc_s[...] = jnp.zeros_like(dc_s)

        @pl.when(kj == 0)
        def _():
            dq_s[...] = jnp.zeros_like(dq_s)
            rs_s[...] = jnp.zeros_like(rs_s)

        def step(qb, masked):
            off = pl.multiple_of(qb * t, t)
            q = q_ref[pl.ds(off, t), :]
            dov = do_ref[pl.ds(off, t), :]
            st = lax.dot_general(kk, q, NT, preferred_element_type=F32) - ccol
            pt = jnp.exp2(st - lse_ref[0, :, pl.ds(off, t)])
            if masked:
                rows = lax.broadcasted_iota(jnp.int32, (t, t), 0)
                cols = lax.broadcasted_iota(jnp.int32, (t, t), 1)
                pt = jnp.where(cols >= rows, pt, 0.0)
            dv_s[...] += lax.dot_general(pt.astype(BF16), dov, NN, preferred_element_type=F32)
            dpt = lax.dot_general(vv, dov, NT, preferred_element_type=F32)
            dst = pt * (dpt - dl_ref[0, :, pl.ds(off, t)])
            dstb = dst.astype(BF16)
            dk_s[...] += lax.dot_general(dstb, q, NN, preferred_element_type=F32)
            dc_s[...] += jnp.sum(dst, axis=1, keepdims=True)
            dq_s[:, pl.ds(off, t)] += lax.dot_general(kk, dstb, TN, preferred_element_type=F32)
            rs_s[:, pl.ds(off, t)] += jnp.sum(dst, axis=0, keepdims=True)

        step(kj, True)

        def loop_body(qb, carry):
            step(qb, False)
            return carry

        lax.fori_loop(kj + 1, nq, loop_body, 0)
        rows_j = pl.ds(pl.multiple_of(kj * t, t), t)
        dp_ref[1, rows_j, :] = (dk_s[...] * (1.0 / LOG2E)).astype(BF16)
        dp_ref[2, rows_j, :] = dv_s[...].astype(BF16)
        dc_ref[...] = jnp.broadcast_to(-dc_s[...], dc_ref.shape)

        @pl.when(kj == nq - 1)
        def _():
            for b in range(nq):
                dp_ref[0, b * t:(b + 1) * t, :] = (dq_s[:, b * t:(b + 1) * t] * scale).T.astype(BF16)
            dp_ref[3] = dgate_ref[...]
            rs_ref[0] = rs_s[...]

        if ns:
            @pl.when(jnp.logical_and(pl.program_id(0) == h_count - 1, kj == nq - 1))
            def _():
                _scatter_phase(1, p_refs, got_refs, sems)

    hh = h_count
    blk = lambda off: pl.BlockSpec((t, HEAD_DIM), lambda h, j: (j, off + h))
    whole = pl.BlockSpec((s, HEAD_DIM), lambda h, j: (0, h))
    rowv = pl.BlockSpec((1, 1, s), lambda h, j: (h, 0, 0))
    return _call(
        body, name=name, grid=(hh, nq),
        in_specs=[blk(hh), blk(2 * hh), whole, whole, rowv, rowv, blk(0), whole] + [ANY] * ns,
        out_specs=[pl.BlockSpec((4, s, HEAD_DIM), lambda h, j: (0, 0, h)), blk(0), rowv] + [ANY] * ns,
        out_shape=[jax.ShapeDtypeStruct((4, s, w), BF16), jax.ShapeDtypeStruct((s, w), F32),
                   jax.ShapeDtypeStruct((hh, 1, s), F32)] + [jax.ShapeDtypeStruct(p.shape, p.dtype) for p in scatter],
        scratch_shapes=[pltpu.VMEM((t, HEAD_DIM), F32), pltpu.VMEM((t, HEAD_DIM), F32), pltpu.VMEM((t, 1), F32),
                        pltpu.VMEM((HEAD_DIM, s), F32), pltpu.VMEM((1, s), F32)] + (_scatter_sems(ns) if ns else []),
        compiler_params=_cp(("arbitrary", "arbitrary") if ns else ("parallel", "arbitrary")),
    )(qkv, qkv, qkv, do, lse_row, delta_row, c_rep, dgate, *scatter)


def _fox_pre_bwd(name, dy, o, gate, exchange=()):
    s, w = o.shape
    tm = min(ROW_BLOCK, s)
    ne = len(exchange)
    nb = s // tm

    def body(*refs):
        dy_ref, o_ref, g_ref = refs[:3]
        x_refs = refs[3:3 + ne]
        do_ref, dg_ref, dl_ref = refs[3 + ne:6 + ne]
        got_refs = refs[6 + ne:6 + 2 * ne]
        sems = refs[6 + 2 * ne:]
        if ne:
            @pl.when(pl.program_id(0) == 0)
            def _():
                _sibling_phase(0, x_refs, got_refs, sems)

            @pl.when(pl.program_id(0) == nb - 1)
            def _():
                _sibling_phase(1, x_refs, got_refs, sems)

        g = g_ref[...]
        sg = _sigmoid(g)
        dyv = dy_ref[...]
        ov = o_ref[...]
        dov = dyv * (g * sg)
        do_ref[...] = dov.astype(BF16)
        dg_ref[...] = (dyv * ov * (sg * (1.0 + g * (1.0 - sg)))).astype(BF16)
        sel_r = lax.broadcasted_iota(jnp.int32, (w, 128), 0)
        sel_c = lax.broadcasted_iota(jnp.int32, (w, 128), 1)
        sel = (sel_r // HEAD_DIM == sel_c).astype(BF16)
        dl_ref[...] = _dot_exact(sel, dov * ov, right=True)

    row = pl.BlockSpec((tm, w), lambda i: (i, 0))
    lanes = pl.BlockSpec((tm, 128), lambda i: (i, 0))
    return _call(body, name=name, grid=(nb,), in_specs=[row, row, row] + [ANY] * ne,
                 out_specs=[row, row, lanes] + [ANY] * ne,
                 out_shape=[jax.ShapeDtypeStruct((s, w), BF16), jax.ShapeDtypeStruct((s, w), BF16),
                            jax.ShapeDtypeStruct((s, 128), F32)] + _sibling_out_shapes(exchange),
                 scratch_shapes=_sibling_sems(ne) if ne else [],
                 compiler_params=_cp(("arbitrary",) if ne else ("parallel",)))(dy, o, gate, *exchange)


def _hgrn_chunk_terms(q_c, k_c, b_c, b_s, base):
    nsub = HGRN_CHUNK // HGRN_SUB
    refs = [jnp.zeros((1, HEAD_DIM), F32)]
    for i in range(1, nsub):
        r0 = base + i * HGRN_SUB - 1
        refs.append(b_s[r0:r0 + 1, :])
    rfull = jnp.concatenate([jnp.broadcast_to(r, (HGRN_SUB, HEAD_DIM)) for r in refs], axis=0)
    eq = jnp.exp(b_c - rfull)
    qe = q_c * eq
    es = [jnp.exp(jnp.minimum(r - b_c, EXP_CLAMP)) for r in refs]
    kes = [(k_c * e).astype(BF16) for e in es]
    return eq, qe, es, kes


def _chunk_cumsum(x, reverse=False):
    cc = HGRN_CHUNK
    nc = x.shape[0] // cc
    rows = lax.broadcasted_iota(jnp.int32, (cc, cc), 0)
    cols = lax.broadcasted_iota(jnp.int32, (cc, cc), 1)
    tri = ((rows <= cols) if reverse else (rows >= cols)).astype(BF16)
    wide = jnp.concatenate([x[n * cc:(n + 1) * cc] for n in range(nc)], axis=1)
    res = _dot_exact(tri, wide)
    return jnp.concatenate([res[:, n * HEAD_DIM:(n + 1) * HEAD_DIM] for n in range(nc)], axis=0)


def _hgrn_block_pre(q_ref, f_ref, lbl_ref, b_s):
    lb = _sigmoid(lbl_ref[1:2, :] - lbl_ref[0:1, :])
    qr = q_ref[...]
    sq = _sigmoid(qr)
    q = qr * sq
    fz = f_ref[...]
    sg = _sigmoid(fz)
    f = lb + (1.0 - lb) * sg
    g = jnp.log(f)
    k = (1.0 - lb) * (1.0 / (1.0 + jnp.exp(fz)))
    b = _chunk_cumsum(g)
    b_s[...] = b
    return lb, qr, sq, q, sg, f, k, b


def _hgrn_fwd(name, p1, lbl, onorm, h_count):
    s = p1.shape[0]
    w = h_count * HEAD_DIM
    t = min(HGRN_BLOCK, s)
    nc = t // HGRN_CHUNK
    nsub = HGRN_CHUNK // HGRN_SUB
    cc = HGRN_CHUNK

    def body(q_ref, f_ref, i_ref, g_ref, lbl_ref, on_ref, y_ref, o_ref, st_ref, state_s, b_s):
        @pl.when(pl.program_id(1) == 0)
        def _():
            state_s[...] = jnp.zeros_like(state_s)

        lb, qr, sq, q, sg, f, k, b = _hgrn_block_pre(q_ref, f_ref, lbl_ref, b_s)
        v = i_ref[...]
        r64 = lax.broadcasted_iota(jnp.int32, (cc, cc), 0)
        c64 = lax.broadcasted_iota(jnp.int32, (cc, cc), 1)
        for n in range(nc):
            sl = slice(n * cc, (n + 1) * cc)
            q_c, k_c, v_c, b_c = q[sl], k[sl], v[sl], b[sl]
            bl = b_s[n * cc + cc - 1:n * cc + cc, :]
            eq, qe, es, kes = _hgrn_chunk_terms(q_c, k_c, b_c, b_s, n * cc)
            qeb = qe.astype(BF16)
            a = jnp.concatenate(
                [lax.dot_general(qeb[i * HGRN_SUB:(i + 1) * HGRN_SUB], kes[i], NT, preferred_element_type=F32)
                 for i in range(nsub)], axis=0)
            a = jnp.where(r64 >= c64, a, 0.0)
            st = state_s[...]
            st_ref[0, n] = st.astype(BF16)
            inter = _dot(q_c * jnp.exp(b_c), st, NT)
            intra = _dot(a, v_c)
            o_ref[sl, :] = inter + intra
            kb = k_c * jnp.exp(bl - b_c)
            state_s[...] = st * jnp.exp(bl) + _dot(v_c, kb, TN)
        o = o_ref[...]
        rr = lax.rsqrt(jnp.mean(o * o, axis=-1, keepdims=True) + EPS)
        gate = g_ref[...]
        y_ref[...] = ((o * rr) * on_ref[...] * (gate * _sigmoid(gate))).astype(BF16)

    hh = h_count
    blk = lambda off: pl.BlockSpec((t, HEAD_DIM), lambda h, i: (i, off + h))
    return _call(
        body, name=name, grid=(hh, s // t),
        in_specs=[blk(0), blk(hh), blk(2 * hh), blk(3 * hh),
                  pl.BlockSpec((2, HEAD_DIM), lambda h, i: (0, h)), pl.BlockSpec((1, HEAD_DIM), lambda h, i: (0, h))],
        out_specs=[blk(0), blk(0), pl.BlockSpec((1, nc, HEAD_DIM, HEAD_DIM), lambda h, i: (h, i, 0, 0))],
        out_shape=[jax.ShapeDtypeStruct((s, w), BF16), jax.ShapeDtypeStruct((s, w), F32),
                   jax.ShapeDtypeStruct((hh, s // cc, HEAD_DIM, HEAD_DIM), BF16)],
        scratch_shapes=[pltpu.VMEM((HEAD_DIM, HEAD_DIM), F32), pltpu.VMEM((t, HEAD_DIM), F32)],
        compiler_params=_cp(("parallel", "arbitrary")),
    )(p1, p1, p1, p1, lbl, onorm)


def _hgrn_bwd(name, p1, lbl, onorm, o, dy, states, h_count):
    s = p1.shape[0]
    w = h_count * HEAD_DIM
    t = min(HGRN_BLOCK, s)
    nb = s // t
    nc = t // HGRN_CHUNK
    nsub = HGRN_CHUNK // HGRN_SUB
    cc = HGRN_CHUNK

    def body(q_ref, f_ref, i_ref, g_ref, lbl_ref, on_ref, o_ref, dy_ref, st_ref,
             dp_ref, dlog_ref, don_ref,
             dst_s, b_s, do_s, dqs, dks, dbs, exs):
        @pl.when(pl.program_id(1) == 0)
        def _():
            dst_s[...] = jnp.zeros_like(dst_s)
            dlog_ref[...] = jnp.zeros_like(dlog_ref)
            don_ref[...] = jnp.zeros_like(don_ref)

        lb, qr, sq, q, sg, f, k, b = _hgrn_block_pre(q_ref, f_ref, lbl_ref, b_s)
        v = i_ref[...]

        ov = o_ref[...]
        rr = lax.rsqrt(jnp.mean(ov * ov, axis=-1, keepdims=True) + EPS)
        on = ov * rr
        gate = g_ref[...]
        sgt = _sigmoid(gate)
        silu = gate * sgt
        dyv = dy_ref[...]
        gain = on_ref[...]
        dp_ref[3] = (dyv * on * gain * (sgt * (1.0 + gate * (1.0 - sgt)))).astype(BF16)
        don_ref[...] += jnp.sum(dyv * on * silu, axis=0, keepdims=True)
        d_on = dyv * gain * silu
        do_s[...] = rr * (d_on - on * jnp.mean(d_on * on, axis=-1, keepdims=True))

        r64 = lax.broadcasted_iota(jnp.int32, (cc, cc), 0)
        c64 = lax.broadcasted_iota(jnp.int32, (cc, cc), 1)
        upper = r64 <= c64
        for n in reversed(range(nc)):
            sl = slice(n * cc, (n + 1) * cc)
            q_c, k_c, v_c, b_c = q[sl], k[sl], v[sl], b[sl]
            do_c = do_s[sl, :]
            bl = b_s[n * cc + cc - 1:n * cc + cc, :]
            ebl = jnp.exp(bl)
            eq, qe, es, kes = _hgrn_chunk_terms(q_c, k_c, b_c, b_s, n * cc)
            qeb = qe.astype(BF16)
            dob = do_c.astype(BF16)
            vb = v_c.astype(BF16)
            st = st_ref[0, n].astype(F32)
            dstn = dst_s[...]
            qb_ = q_c * jnp.exp(b_c)
            kb_ = k_c * jnp.exp(bl - b_c)
            at = jnp.zeros((cc, cc), F32)
            for i in range(nsub):
                blk_i = (c64 // HGRN_SUB) == i
                at = at + jnp.where(blk_i, lax.dot_general(kes[i], qeb, NT, preferred_element_type=F32), 0.0)
            at = jnp.where(upper, at, 0.0)
            dv = _dot(at, dob) + _dot(kb_, dstn, NT)
            dqb = _dot(dob, st)
            dkb = _dot(vb, dstn)
            da = jnp.where(r64 >= c64, lax.dot_general(dob, vb, NT, preferred_element_type=F32), 0.0)
            dat = jnp.where(upper, lax.dot_general(vb, dob, NT, preferred_element_type=F32), 0.0)
            dab = da.astype(BF16)
            dq_raw = jnp.concatenate(
                [lax.dot_general(dab[i * HGRN_SUB:(i + 1) * HGRN_SUB], kes[i], NN, preferred_element_type=F32)
                 for i in range(nsub)], axis=0)
            db = qb_.astype(BF16).astype(F32) * dqb + qeb.astype(F32) * dq_raw
            dkbk = dkb * kb_.astype(BF16).astype(F32)
            db = db - dkbk
            dk_in = jnp.zeros((cc, HEAD_DIM), F32)
            for i in range(nsub):
                blk_i = (c64 // HGRN_SUB) == i
                dk_raw = _dot(jnp.where(blk_i, dat, 0.0), qeb)
                dk_in = dk_in + dk_raw * es[i]
                db = db - kes[i].astype(F32) * dk_raw
            dqs[sl, :] = dqb * jnp.exp(b_c) + dq_raw * eq
            dks[sl, :] = dkb * jnp.exp(bl - b_c) + dk_in
            dbs[sl, :] = db
            extra = jnp.sum(dkbk, axis=0, keepdims=True) + jnp.sum(dstn * st, axis=0, keepdims=True) * ebl
            exs[sl, :] = jnp.broadcast_to(extra, (cc, HEAD_DIM))
            dp_ref[2, sl, :] = dv.astype(BF16)
            dst_s[...] = dstn * ebl + _dot(dob, qb_, TN)

        dg = _chunk_cumsum(dbs[...], reverse=True) + exs[...]
        df = dg / f - dks[...]
        dp_ref[1] = (df * (1.0 - lb) * sg * (1.0 - sg)).astype(BF16)
        dp_ref[0] = (dqs[...] * (sq * (1.0 + qr * (1.0 - sq)))).astype(BF16)
        dlb = jnp.sum(df * (1.0 - sg), axis=0, keepdims=True) * (lb * (1.0 - lb))
        dlog_ref[0:1, :] += -dlb
        dlog_ref[1:2, :] += dlb

    hh = h_count
    blk = lambda off: pl.BlockSpec((t, HEAD_DIM), lambda h, i: (nb - 1 - i, off + h))
    two = pl.BlockSpec((2, HEAD_DIM), lambda h, i: (0, h))
    one = pl.BlockSpec((1, HEAD_DIM), lambda h, i: (0, h))
    tile = pltpu.VMEM((t, HEAD_DIM), F32)
    return _call(
        body, name=name, grid=(hh, nb),
        in_specs=[blk(0), blk(hh), blk(2 * hh), blk(3 * hh), two, one, blk(0), blk(0),
                  pl.BlockSpec((1, nc, HEAD_DIM, HEAD_DIM), lambda h, i: (h, nb - 1 - i, 0, 0))],
        out_specs=[pl.BlockSpec((4, t, HEAD_DIM), lambda h, i: (0, nb - 1 - i, h)), two, one],
        out_shape=[jax.ShapeDtypeStruct((4, s, w), BF16), jax.ShapeDtypeStruct((2, w), F32),
                   jax.ShapeDtypeStruct((1, w), F32)],
        scratch_shapes=[pltpu.VMEM((HEAD_DIM, HEAD_DIM), F32), tile, tile, tile, tile, tile, tile],
        compiler_params=_cp(("parallel", "arbitrary")),
    )(p1, p1, p1, p1, lbl, onorm, o, dy, states)


def _adamw(name, w, g, m, v):
    r, c = w.shape
    tr = r if r <= 128 else _row_block(r, 128)
    c1 = 1.0 - ADAM_B1 ** ADAM_STEP
    c2 = 1.0 - ADAM_B2 ** ADAM_STEP

    def body(w_ref, g_ref, m_ref, v_ref, d_ref, nm_ref, nv_ref):
        gv = g_ref[...]
        nm = ADAM_B1 * m_ref[...] + (1.0 - ADAM_B1) * gv
        nv = ADAM_B2 * v_ref[...] + (1.0 - ADAM_B2) * (gv * gv)
        nm_ref[...] = nm
        nv_ref[...] = nv
        d_ref[...] = -ADAM_LR * ((nm / c1) / (jnp.sqrt(nv / c2) + ADAM_EPS) + ADAM_WD * w_ref[...])

    blk = pl.BlockSpec((tr, c), lambda i: (i, 0))
    sh = jax.ShapeDtypeStruct((r, c), F32)
    return _call(body, name=name, grid=(r // tr,), in_specs=[blk] * 4, out_specs=[blk] * 3,
                 out_shape=[sh, sh, sh], compiler_params=_cp(("parallel",)))(w, g, m, v)


SLAB_BLOCK_BYTES = 2 * 1024 * 1024


def _slab_rows(r, c):
    return _row_block(r, max(16, SLAB_BLOCK_BYTES // (4 * c) // 16 * 16))


def _pair_add(name, g2, recv, core):
    _, nch, r, c = g2.shape
    tr = _slab_rows(r, c)

    grid_spec = pltpu.PrefetchScalarGridSpec(
        num_scalar_prefetch=1, grid=(nch, r // tr),
        in_specs=[pl.BlockSpec((1, 1, tr, c), lambda j, i, cr: (cr[0], j, i, 0)),
                  pl.BlockSpec((1, tr, c), lambda j, i, cr: (j, i, 0))],
        out_specs=pl.BlockSpec((1, tr, c), lambda j, i, cr: (j, i, 0)))

    def body(core_ref, a_ref, b_ref, o_ref):
        o_ref[...] = (a_ref[0] + b_ref[...]).astype(BF16)

    return _call(body, name=name, grid_spec=grid_spec, out_shape=jax.ShapeDtypeStruct((nch, r, c), BF16),
                 compiler_params=_cp(("parallel", "parallel")))(core, g2, recv)


def _sum_slots(name, x, core=None):
    n, r, c = x.shape
    tr = _slab_rows(r, c)

    def total(x_ref):
        acc = x_ref[0].astype(F32)
        for j in range(1, n):
            acc = acc + x_ref[j].astype(F32)
        return acc

    if core is None:
        def body(x_ref, o_ref):
            o_ref[...] = total(x_ref)

        return _call(body, name=name, grid=(r // tr,),
                     in_specs=[pl.BlockSpec((n, tr, c), lambda i: (0, i, 0))],
                     out_specs=pl.BlockSpec((tr, c), lambda i: (i, 0)),
                     out_shape=jax.ShapeDtypeStruct((r, c), F32), compiler_params=_cp(("parallel",)))(x)

    def body_half(core_ref, x_ref, o_ref):
        o_ref[0] = total(x_ref)

    grid_spec = pltpu.PrefetchScalarGridSpec(
        num_scalar_prefetch=1, grid=(r // tr,),
        in_specs=[pl.BlockSpec((n, tr, c), lambda i, cr: (0, i, 0))],
        out_specs=pl.BlockSpec((1, tr, c), lambda i, cr: (cr[0], i, 0)))
    return _call(body_half, name=name, grid_spec=grid_spec, out_shape=jax.ShapeDtypeStruct((2, r, c), F32),
                 compiler_params=_cp(("parallel",)))(core, x)


def _pos():
    return lax.axis_index("x"), lax.axis_index("y"), lax.axis_index("c")


def _flip(v, f):
    return (1 - v) if f else v


CHIP_FLIPS = ((0, 1), (1, 0), (1, 1))
DEV_FLIPS = tuple((fx, fy, fc) for fx in (0, 1) for fy in (0, 1) for fc in (0, 1))[1:]


def _remote(src, dst, ssem, rsem, dev):
    return pltpu.make_async_remote_copy(src_ref=src, dst_ref=dst, send_sem=ssem, recv_sem=rsem,
                                        device_id=dev, device_id_type=MESH)


def _ag_weights(name, halves):
    n = len(halves)

    def body(*refs):
        h_refs, big_refs, sems = refs[:n], refs[n:2 * n], refs[2 * n:]
        _ag_phase(0, h_refs, big_refs, sems)
        _ag_phase(1, h_refs, big_refs, sems)
        _ag_phase(2, h_refs, big_refs, sems)

    return _call(body, name=name, in_specs=[ANY] * n, out_specs=[ANY] * n,
                 out_shape=_ag_out_shapes(halves), scratch_shapes=_ag_sems(n))(*halves)


def _ag_out_shapes(halves):
    return [jax.ShapeDtypeStruct((2, 4) + h.shape, h.dtype) for h in halves]


def _ag_sems(n):
    return [pltpu.SemaphoreType.DMA((n,)), pltpu.SemaphoreType.DMA((7 * n,)), pltpu.SemaphoreType.DMA((7 * n,))]


def _ag_phase(phase, h_refs, big_refs, sems):
    lsem, ssem, rsem = sems
    n = len(h_refs)
    x, y, cc = _pos()
    me = 2 * x + y
    sib = (x, y, 1 - cc)
    peers = [(_flip(x, fx), _flip(y, fy), cc) for fx, fy in CHIP_FLIPS]
    chips = [2 * px + py for px, py, _ in peers]
    for a in range(n):
        mine = big_refs[a].at[cc, me]
        first = [pltpu.make_async_copy(h_refs[a], mine, lsem.at[a])]
        first += [_remote(h_refs[a], mine, ssem.at[7 * a + k], rsem.at[7 * a + k], peers[k]) for k in range(3)]
        first += [_remote(h_refs[a], mine, ssem.at[7 * a + 3], rsem.at[7 * a + 3], sib)]
        passed = []
        for k in range(3):
            blk = big_refs[a].at[cc, chips[k]]
            passed.append(_remote(blk, blk, ssem.at[7 * a + 4 + k], rsem.at[7 * a + 4 + k], sib))
        if phase == 0:
            for cp in first:
                cp.start()
        elif phase == 1:
            for k in range(3):
                blk = big_refs[a].at[cc, chips[k]]
                _remote(blk, blk, ssem.at[7 * a + k], rsem.at[7 * a + k], peers[k]).wait_recv()
                passed[k].start()
        else:
            theirs = big_refs[a].at[1 - cc, me]
            _remote(theirs, theirs, ssem.at[7 * a + 3], rsem.at[7 * a + 3], sib).wait_recv()
            for k in range(3):
                blk = big_refs[a].at[1 - cc, chips[k]]
                _remote(blk, blk, ssem.at[7 * a + 4 + k], rsem.at[7 * a + 4 + k], sib).wait_recv()
            first[0].wait()
            for cp in first[1:] + passed:
                cp.wait_send()


def _ag_devices(name, v):
    r, c = v.shape

    def body(v_ref, out_ref, lsem, ssem, rsem):
        x, y, cc = _pos()
        me = 4 * x + 2 * y + cc
        loc = pltpu.make_async_copy(v_ref, out_ref.at[me], lsem)
        loc.start()
        started = []
        for k, (fx, fy, fc) in enumerate(DEV_FLIPS):
            cp = _remote(v_ref, out_ref.at[me], ssem.at[k], rsem.at[k], (_flip(x, fx), _flip(y, fy), _flip(cc, fc)))
            cp.start()
            started.append(cp)
        for k, (fx, fy, fc) in enumerate(DEV_FLIPS):
            px, py, pc = _flip(x, fx), _flip(y, fy), _flip(cc, fc)
            blk = out_ref.at[4 * px + 2 * py + pc]
            _remote(blk, blk, ssem.at[k], rsem.at[k], (px, py, pc)).wait_recv()
        for cp in started:
            cp.wait_send()
        loc.wait()

    return _call(body, name=name, in_specs=[ANY], out_specs=ANY,
                 out_shape=jax.ShapeDtypeStruct((8, r, c), v.dtype),
                 scratch_shapes=[pltpu.SemaphoreType.DMA, pltpu.SemaphoreType.DMA((7,)), pltpu.SemaphoreType.DMA((7,))])(v)


def _sibling_other_half(name, g2s):
    n = len(g2s)

    def body(*refs):
        g_refs, out_refs, sems = refs[:n], refs[n:2 * n], refs[2 * n:]
        _sibling_phase(0, g_refs, out_refs, sems)
        _sibling_phase(1, g_refs, out_refs, sems)

    return _call(body, name=name, in_specs=[ANY] * n, out_specs=[ANY] * n,
                 out_shape=_sibling_out_shapes(g2s), scratch_shapes=_sibling_sems(n))(*g2s)


def _sibling_out_shapes(g2s):
    return [jax.ShapeDtypeStruct(g.shape[1:], g.dtype) for g in g2s]


def _sibling_sems(n):
    return [pltpu.SemaphoreType.DMA((n,)), pltpu.SemaphoreType.DMA((n,))]


def _sibling_phase(phase, g_refs, out_refs, sems):
    ssem, rsem = sems
    x, y, cc = _pos()
    for a in range(len(g_refs)):
        cp = _remote(g_refs[a].at[1 - cc], out_refs[a], ssem.at[a], rsem.at[a], (x, y, 1 - cc))
        if phase == 0:
            cp.start()
        else:
            cp.wait()


def _scatter_sems(n):
    return [pltpu.SemaphoreType.DMA((n,)), pltpu.SemaphoreType.DMA((3 * n,)), pltpu.SemaphoreType.DMA((3 * n,))]


def _scatter_phase(phase, p_refs, out_refs, sems):
    lsem, ssem, rsem = sems
    x, y, cc = _pos()
    me = 2 * x + y
    peers = [(_flip(x, fx), _flip(y, fy), cc) for fx, fy in CHIP_FLIPS]
    for a in range(len(p_refs)):
        loc = pltpu.make_async_copy(p_refs[a].at[me], out_refs[a].at[me], lsem.at[a])
        sends = [_remote(p_refs[a].at[2 * px + py], out_refs[a].at[me], ssem.at[3 * a + k], rsem.at[3 * a + k], peers[k])
                 for k, (px, py, _) in enumerate(peers)]
        if phase == 0:
            loc.start()
            for cp in sends:
                cp.start()
        else:
            for k, (px, py, _) in enumerate(peers):
                blk = out_refs[a].at[2 * px + py]
                _remote(blk, blk, ssem.at[3 * a + k], rsem.at[3 * a + k], peers[k]).wait_recv()
            loc.wait()
            for cp in sends:
                cp.wait_send()


def _sibling_join(name, bufs):
    n = len(bufs)

    def body(*refs):
        out_refs, (ssem, rsem) = refs[n:2 * n], refs[2 * n:]
        x, y, cc = _pos()
        sib = (x, y, 1 - cc)
        cps = [_remote(out_refs[a].at[cc], out_refs[a].at[cc], ssem.at[a], rsem.at[a], sib) for a in range(n)]
        for cp in cps:
            cp.start()
        for a in range(n):
            theirs = out_refs[a].at[1 - cc]
            _remote(theirs, theirs, ssem.at[a], rsem.at[a], sib).wait_recv()
        for cp in cps:
            cp.wait_send()

    return _call(body, name=name, in_specs=[ANY] * n, out_specs=[ANY] * n,
                 out_shape=[jax.ShapeDtypeStruct(b.shape, b.dtype) for b in bufs],
                 input_output_aliases={a: a for a in range(n)},
                 scratch_shapes=[pltpu.SemaphoreType.DMA((n,)), pltpu.SemaphoreType.DMA((n,))])(*bufs)


def _pad_lanes(a, width=128):
    return jnp.pad(a, ((0, 0), (0, width - a.shape[1])))


def kernel(x, norm_gains, fox_w_in, fox_b_f, hgrn_w_in, hgrn_lb_logits, hgrn_onorm, w_out, final_gain, loss_target, m_norm_gains, m_fox_w_in, m_fox_b_f, m_hgrn_w_in, m_hgrn_lb_logits, m_hgrn_onorm, m_w_out, m_final_gain, v_norm_gains, v_fox_w_in, v_fox_b_f, v_hgrn_w_in, v_hgrn_lb_logits, v_hgrn_onorm, v_w_out, v_final_gain):
    s, d = x.shape[1], x.shape[2]
    wq = w_out.shape[1]
    w = 4 * wq
    hh = w // HEAD_DIM
    fox_cols = fox_w_in.shape[2]
    assert 4 * fox_cols == 4 * w + hh and hgrn_w_in.shape[2] == w
    core = lax.axis_index("c")

    x0 = x[0]
    tgt = loss_target[0]

    def my_half(a):
        return lax.dynamic_index_in_dim(a, core, 0, keepdims=False).astype(BF16)

    (big_fox,) = _ag_weights("ag_weights", [my_half(fox_w_in.reshape(2, d // 2, fox_cols))])
    later_halves = [my_half(hgrn_w_in.reshape(2, d // 2, w)), my_half(w_out),
                    lax.dynamic_index_in_dim(hgrn_onorm.reshape(2, 1, wq // 2), core, 0, keepdims=False)]
    fox_chip = [big_fox[:, j].reshape(d, fox_cols) for j in range(4)]
    def fox_columns(a, b):
        out = []
        for j in range(4):
            lo, hi = max(a, j * fox_cols), min(b, (j + 1) * fox_cols)
            if lo < hi:
                out.append(fox_chip[j][:, lo - j * fox_cols:hi - j * fox_cols])
        return out

    w_main = jnp.concatenate(fox_columns(0, 3 * w) + fox_columns(3 * w + hh, 4 * w + hh), axis=1)
    w_fl = _pad_lanes(jnp.concatenate(fox_columns(3 * w, 3 * w + hh), axis=1))

    bf_pad = _pad_lanes(fox_b_f)
    g0, g1 = norm_gains[0:1], norm_gains[1:2]
    gf = final_gain.reshape(1, d)

    h0 = _rms_fwd("rms0_fwd", x0, g0)
    qkv = _mm("fox_qkv", h0, w_main, out_dtype=BF16, n=3 * w, col_scale=(w, Q_FOLD))
    gate0 = _mm("fox_gate", h0, w_main, n=w, b_off=3 * w)
    fl = _mm("fox_flogit", h0, w_fl)
    _, c_rep = _fox_gate_fwd("fox_cumsum", fl, bf_pad, w, rep_scale=LOG2E)
    o0, lse_row, y0, big_hgrn, big_out, big_onorm = _attn_fwd("fox_attn_fwd", qkv, c_rep, gate0, hh, gather=later_halves)
    onorm_full = big_onorm.transpose(1, 0, 2, 3).reshape(1, w)
    w_o = big_out.reshape(2, w, d)
    x1 = _mm("fox_out", y0, w_o[0], res=x0)
    h1 = _rms_fwd("rms1_fwd", x1, g1)
    p1 = _mm("hgrn_in", h1, big_hgrn, b_gathered=True)
    y1, o1, states = _hgrn_fwd("hgrn_fwd", p1, hgrn_lb_logits, onorm_full, hh)
    x2 = _mm("hgrn_out", y1, w_o[1], res=x1)
    dx2, d_gf, loss_tile = _loss_head("loss_head", x2, tgt, gf)

    dy1 = _mm("hgrn_out_dy", dx2, w_o[1], tb=True)
    d_wo1 = _mm("hgrn_out_dw", y1, dx2, ta=True)
    dp1, d_lbl, d_onorm = _hgrn_bwd("hgrn_bwd", p1, hgrn_lb_logits, onorm_full, o1, dy1, states, hh)
    dh1 = _mm("hgrn_in_dh", dp1, big_hgrn, tb=True, b_gathered=True, a_stacked=True)
    g2_hgrn = _mm("hgrn_in_dw", h1, dp1, ta=True, out_gathered=(d // 2, w), b_stacked=True)
    dx1, d_g1 = _rms_bwd("rms1_bwd", x1, g1, dh1, dx2)
    dy0 = _mm("fox_out_dy", dx1, w_o[0], tb=True)
    d_wo0 = _mm("fox_out_dw", y0, dx1, ta=True)
    core_arr = core.reshape(1).astype(jnp.int32)
    g2_out = jnp.stack([d_wo0, d_wo1]).reshape(2, 4, wq, d)
    do0, dgate0, delta, sib_hgrn, sib_out = _fox_pre_bwd("fox_pre_bwd", dy0, o0, gate0, exchange=[g2_hgrn, g2_out])
    delta_row = delta[:, :hh].T.reshape(hh, 1, s)
    pairs_early = [_pair_add("rs_pair_add_hgrn", g2_hgrn, sib_hgrn, core_arr),
                   _pair_add("rs_pair_add_out", g2_out, sib_out, core_arr)]
    dp0, dc_rep, rowsum_row, got_hgrn, got_out = _attn_bwd(
        "fox_attn_bwd", qkv, do0, dgate0, lse_row, delta_row, c_rep, hh, scatter=pairs_early)
    dfl, d_bf = _fox_gate_bwd("fox_cumsum_bwd", _pad_lanes(rowsum_row.reshape(hh, s).T),
                              _pad_lanes(dc_rep[:, ::HEAD_DIM]), fl, bf_pad)
    d_wmain = _mm("fox_in_dw", h0, dp0, ta=True, b_stacked=True)
    d_wfl = _mm("fox_fl_dw", h0, dfl, ta=True)

    def grad_columns(a, b):
        out = []
        for lo, hi, src, shift in ((0, 3 * w, d_wmain, 0), (3 * w, 3 * w + hh, d_wfl, 3 * w),
                                   (3 * w + hh, 4 * w + hh, d_wmain, hh)):
            l2, h2 = max(a, lo), min(b, hi)
            if l2 < h2:
                out.append(src[:, l2 - shift:h2 - shift])
        return out

    g2_fox = jnp.stack([jnp.concatenate(grad_columns(j * fox_cols, (j + 1) * fox_cols), axis=1).reshape(2, d // 2, fox_cols)
                        for j in range(4)], axis=1)
    (sib_fox,) = _sibling_other_half("rs_sibling", [g2_fox])
    dh0, got_fox = _mm("fox_in_dh", dp0, w_main, tb=True, res=_mm("fox_fl_dh", dfl, w_fl, tb=True), a_stacked=True,
                       scatter=[_pair_add("rs_pair_add_fox", g2_fox, sib_fox, core_arr)])
    grad_x, d_g0 = _rms_bwd("rms0_bwd", x0, g0, dh0, dx1)
    halves_sum = [_sum_slots("rs_sum_" + nm, got, core_arr)
                  for nm, got in zip(("fox", "hgrn", "out"), (got_fox, got_hgrn, got_out))]
    r_fox, r_hgrn, r_out = _sibling_join("rs_join", halves_sum)
    g_fox = r_fox.reshape(d, fox_cols)
    g_hgrn = r_hgrn.reshape(d, w)
    g_out = r_out.reshape(2 * wq, d)

    small = jnp.concatenate([jnp.concatenate([d_g0, d_g1], axis=0).reshape(-1), d_lbl.reshape(-1), d_gf.reshape(-1),
                             d_onorm.reshape(-1), d_bf.reshape(-1)])
    n_small = small.shape[0]
    pad_to = -(-n_small // 1024) * 1024
    small = jnp.pad(small, (0, pad_to - n_small)).reshape(pad_to // 128, 128)
    small = _sum_slots("small_sum", _ag_devices("small_gather", small)).reshape(-1)
    g_norm = small[:2 * d].reshape(2, d)
    g_lbl = small[2 * d:2 * d + 2 * w].reshape(2, w)
    g_gf = small[2 * d + 2 * w:3 * d + 2 * w]
    g_onorm_full = small[3 * d + 2 * w:3 * d + 3 * w]
    g_bf = small[3 * d + 3 * w:3 * d + 3 * w + hh].reshape(1, hh)
    chip = 2 * lax.axis_index("x") + lax.axis_index("y")
    g_onorm = lax.dynamic_slice_in_dim(g_onorm_full, chip * wq, wq).reshape(1, wq)

    loss = lax.psum(loss_tile[0, 0], ("x", "y", "c"))

    def upd(name, wt, g, m, v):
        shp = wt.shape
        two = lambda a: a.reshape(-1, shp[-1])
        dl, nm, nv = _adamw(name, two(wt), two(g), two(m), two(v))
        return g.reshape(shp), dl.reshape(shp), nm.reshape(shp), nv.reshape(shp)

    res = [
        upd("adamw_norm_gains", norm_gains, g_norm, m_norm_gains, v_norm_gains),
        upd("adamw_fox_w_in", fox_w_in, g_fox, m_fox_w_in, v_fox_w_in),
        upd("adamw_fox_b_f", fox_b_f, g_bf, m_fox_b_f, v_fox_b_f),
        upd("adamw_hgrn_w_in", hgrn_w_in, g_hgrn, m_hgrn_w_in, v_hgrn_w_in),
        upd("adamw_lb_logits", hgrn_lb_logits, g_lbl, m_hgrn_lb_logits, v_hgrn_lb_logits),
        upd("adamw_onorm", hgrn_onorm, g_onorm, m_hgrn_onorm, v_hgrn_onorm),
        upd("adamw_w_out", w_out, g_out, m_w_out, v_w_out),
        upd("adamw_final_gain", final_gain.reshape(1, d), g_gf.reshape(1, d), m_final_gain.reshape(1, d),
            v_final_gain.reshape(1, d)),
    ]
    res[-1] = tuple(a.reshape(d) for a in res[-1])
    grads, deltas, new_m, new_v = zip(*res)
    return (loss, grad_x[None], *grads, *deltas, *new_m, *new_v)
```
